```python
import jax, jax.numpy as jnp
from jax import lax
import numpy as np

D_MODEL = 1024
BATCH = 32
SEQ = 2048
DEPTH = 1

CTX_LEN = 256
GRID_W = 64
N_HEADS = 8
HEAD_DIM = 64
D_ATT = N_HEADS * HEAD_DIM
WIN_ROWS_MAX = 8
WIN_COLS = 16
D_CONV = 512
CONV_WIDTH = 31
N_GROUPS = 4
EXPERTS_PER_GROUP = 8
N_EXPERTS = N_GROUPS * EXPERTS_PER_GROUP
TOP_K_IN_GROUP = 2
D_EXPERT = 512
MOE_BLOCK = 128
NORM_EPS = 1e-6
NEG_INF = -1e30

Q0 = 0
K0 = Q0 + D_ATT
V0 = K0 + D_ATT
GLU0 = V0 + D_ATT
GA0 = GLU0 + 2 * D_CONV
GB0 = GA0 + D_MODEL
D_IN = GB0 + D_MODEL

kernel_name = "hybrid_natten_conformer_hmoe_dit"


def rmsnorm(x, g):
    xf = x.astype(jnp.float32)
    y = xf * lax.rsqrt(jnp.mean(xf * xf, axis=-1, keepdims=True) + NORM_EPS)
    return (y * g.astype(jnp.float32)).astype(x.dtype)


def layernorm(x, g, b):
    xf = x.astype(jnp.float32)
    mu = jnp.mean(xf, axis=-1, keepdims=True)
    var = jnp.mean(jnp.square(xf - mu), axis=-1, keepdims=True)
    y = (xf - mu) * lax.rsqrt(var + NORM_EPS)
    return (y * g.astype(jnp.float32) + b.astype(jnp.float32)).astype(x.dtype)


def modulate(h, shift, scale):
    return h * (1 + scale) + shift


def split_heads(t):
    return t.reshape(t.shape[0], t.shape[1], N_HEADS, HEAD_DIM)


def latent_neighbourhood_attention(q, k, v, k_ctx, v_ctx, rpb):
    B, N, H, hd = q.shape
    rows = N // GRID_W
    kh = min(WIN_ROWS_MAX, rows)
    scale = hd ** -0.5
    q_rows = q.reshape(B, rows, GRID_W, H, hd).transpose(1, 0, 3, 2, 4)
    k_grid = k.reshape(B, rows, GRID_W, H, hd).transpose(0, 3, 1, 2, 4)
    v_grid = v.reshape(B, rows, GRID_W, H, hd).transpose(0, 3, 1, 2, 4)
    cq = jnp.arange(GRID_W)[:, None]
    ck = jnp.arange(GRID_W)[None, :]
    c_start = jnp.clip(cq - WIN_COLS // 2, 0, GRID_W - WIN_COLS)
    col_mask = (ck >= c_start) & (ck < c_start + WIN_COLS)
    band_mask = jnp.broadcast_to(col_mask[:, None, :], (GRID_W, kh, GRID_W)).reshape(GRID_W, kh * GRID_W)
    dc_idx = jnp.clip(ck - cq + WIN_COLS - 1, 0, 2 * WIN_COLS - 2)
    rpb_cols = rpb[:, :, dc_idx]

    def row_block(args):
        q_r, r = args
        r_start = jnp.clip(r - kh // 2, 0, rows - kh)
        k_blk = lax.dynamic_slice_in_dim(k_grid, r_start, kh, axis=2).reshape(B, H, kh * GRID_W, hd)
        v_blk = lax.dynamic_slice_in_dim(v_grid, r_start, kh, axis=2).reshape(B, H, kh * GRID_W, hd)
        dr_idx = r_start + jnp.arange(kh) - r + WIN_ROWS_MAX - 1
        bias = jnp.take(rpb_cols, dr_idx, axis=1)
        bias = bias.transpose(0, 2, 1, 3).reshape(H, GRID_W, kh * GRID_W).astype(jnp.float32)
        s_loc = jnp.einsum('bhqd,bhkd->bhqk', q_r, k_blk, preferred_element_type=jnp.float32) * scale + bias
        s_loc = jnp.where(band_mask, s_loc, NEG_INF)
        s_ctx = jnp.einsum('bhqd,bhkd->bhqk', q_r, k_ctx, preferred_element_type=jnp.float32) * scale
        p = jax.nn.softmax(jnp.concatenate([s_loc, s_ctx], axis=-1), axis=-1).astype(v.dtype)
        n_loc = kh * GRID_W
        return (jnp.einsum('bhqk,bhkd->bhqd', p[..., :n_loc], v_blk)
                + jnp.einsum('bhqk,bhkd->bhqd', p[..., n_loc:], v_ctx))

    o = lax.map(row_block, (q_rows, jnp.arange(rows)))
    return o.transpose(1, 0, 3, 2, 4).reshape(B, N, H * hd)


def context_attention(q_c, k_c, v_c):
    s = jnp.einsum('bhqd,bhkd->bhqk', q_c, k_c, preferred_element_type=jnp.float32) * (HEAD_DIM ** -0.5)
    p = jax.nn.softmax(s, axis=-1).astype(v_c.dtype)
    o = jnp.einsum('bhqk,bhkd->bhqd', p, v_c)
    return o.transpose(0, 2, 1, 3).reshape(q_c.shape[0], q_c.shape[2], D_ATT)


def conformer_conv(glu_in, conv_w, conv_b, ln_g, ln_b, w_out):
    a, g = jnp.split(glu_in, 2, axis=-1)
    u = a * jax.nn.sigmoid(g)
    y = lax.conv_general_dilated(u, conv_w[:, None, :].astype(u.dtype), window_strides=(1,),
                                 padding=[(CONV_WIDTH // 2, CONV_WIDTH // 2)],
                                 dimension_numbers=('NWC', 'WIO', 'NWC'),
                                 feature_group_count=D_CONV) + conv_b
    y = jax.nn.silu(layernorm(y, ln_g, ln_b))
    return y @ w_out


def merge_branches(proj, att, conv_w, conv_b, ln_g, ln_b, w_att_out, w_conv_out, w_o):
    y_att = att @ w_att_out
    y_conv = conformer_conv(proj[..., GLU0:GA0], conv_w, conv_b, ln_g, ln_b, w_conv_out)
    gate_att = jax.nn.sigmoid(proj[..., GA0:GB0])
    gate_conv = jax.nn.sigmoid(proj[..., GB0:D_IN])
    return (gate_att * y_att + gate_conv * y_conv) @ w_o


def hierarchical_moe(h, w_rg, b_rg, w_re, b_re, w_g, w_u, w_d):
    T, D = h.shape
    hf = h.astype(jnp.float32)
    g_logits = hf @ w_rg.astype(jnp.float32) + b_rg.astype(jnp.float32)
    g_idx = jnp.argmax(g_logits, axis=-1)
    p_group = jnp.take_along_axis(jax.nn.softmax(g_logits, axis=-1), g_idx[:, None], axis=-1)
    e_logits = (hf @ w_re.astype(jnp.float32) + b_re.astype(jnp.float32)).reshape(T, N_GROUPS, EXPERTS_PER_GROUP)
    e_sel = jnp.take_along_axis(e_logits, g_idx[:, None, None], axis=1)[:, 0]
    top_v, top_i = lax.top_k(e_sel, TOP_K_IN_GROUP)
    gate = p_group * jax.nn.softmax(top_v, axis=-1)
    flat_e = (g_idx[:, None] * EXPERTS_PER_GROUP + top_i).reshape(-1)
    flat_tok = jnp.repeat(jnp.arange(T), TOP_K_IN_GROUP)
    flat_w = gate.reshape(-1)
    A = T * TOP_K_IN_GROUP
    order = jnp.argsort(flat_e)
    e_s, tok_s, w_s = flat_e[order], flat_tok[order], flat_w[order]
    counts = jnp.bincount(flat_e, length=N_EXPERTS)
    padded = (counts + MOE_BLOCK - 1) // MOE_BLOCK * MOE_BLOCK
    pad_end = jnp.cumsum(padded)
    pad_start = pad_end - padded
    start = jnp.cumsum(counts) - counts
    dest = pad_start[e_s] + (jnp.arange(A) - start[e_s])
    n_blocks = -(-A // MOE_BLOCK) + N_EXPERTS
    P = n_blocks * MOE_BLOCK
    row_tok = jnp.zeros((P,), jnp.int32).at[dest].set(tok_s.astype(jnp.int32))
    row_w = jnp.zeros((P,), jnp.float32).at[dest].set(w_s)
    block_e = jnp.clip(jnp.searchsorted(pad_end, jnp.arange(n_blocks) * MOE_BLOCK, side='right'), 0, N_EXPERTS - 1)

    def run_block(args):
        tok, e, wt = args
        xb = h[tok]
        y = (jax.nn.silu(xb @ w_g[e]) * (xb @ w_u[e])) @ w_d[e]
        return y.astype(jnp.float32) * wt[:, None]

    ys = lax.map(run_block, (row_tok.reshape(n_blocks, MOE_BLOCK), block_e, row_w.reshape(n_blocks, MOE_BLOCK)))
    out = jnp.zeros((T, D), jnp.float32).at[row_tok].add(ys.reshape(P, D))
    return out.astype(h.dtype)


def setup_inputs(seed: int = 0) -> dict:
    key = jax.random.key(seed)
    ks = jax.random.split(key, 25)
    L, D = DEPTH, D_MODEL

    def nrm(k, shape, s):
        return jax.random.normal(k, shape, jnp.float32) * s

    return {
        "x": nrm(ks[0], (BATCH, SEQ, D), 1.0),
        "c": nrm(ks[1], (BATCH, D), 1.0),
        "ctx": nrm(ks[2], (BATCH, CTX_LEN, D), 1.0),
        "c_ctx": nrm(ks[3], (D,), 1.0),
        "w_mod": nrm(ks[4], (L, D, 6 * D), 0.5 * D ** -0.5),
        "b_mod": nrm(ks[5], (L, 6 * D), 0.01),
        "norm_mix": 1.0 + nrm(ks[6], (L, D), 0.02),
        "w_in": nrm(ks[7], (L, D, D_IN), D ** -0.5),
        "rpb": nrm(ks[8], (L, N_HEADS, 2 * WIN_ROWS_MAX - 1, 2 * WIN_COLS - 1), 0.1),
        "w_att_out": nrm(ks[9], (L, D_ATT, D), D_ATT ** -0.5),
        "conv_w": nrm(ks[10], (L, CONV_WIDTH, D_CONV), CONV_WIDTH ** -0.5),
        "conv_b": nrm(ks[11], (L, D_CONV), 0.01),
        "conv_ln_g": 1.0 + nrm(ks[12], (L, D_CONV), 0.02),
        "conv_ln_b": nrm(ks[13], (L, D_CONV), 0.01),
        "w_conv_out": nrm(ks[14], (L, D_CONV, D), D_CONV ** -0.5),
        "w_o": nrm(ks[15], (L, D, D), D ** -0.5),
        "norm_ffn": 1.0 + nrm(ks[16], (L, D), 0.02),
        "w_router_group": nrm(ks[17], (L, D, N_GROUPS), D ** -0.5),
        "b_router_group": nrm(ks[18], (L, N_GROUPS), 0.01),
        "w_router_expert": nrm(ks[19], (L, D, N_EXPERTS), D ** -0.5),
        "b_router_expert": nrm(ks[20], (L, N_EXPERTS), 0.01),
        "w_exp_gate": nrm(ks[21], (L, N_EXPERTS, D, D_EXPERT), D ** -0.5),
        "w_exp_up": nrm(ks[22], (L, N_EXPERTS, D, D_EXPERT), D ** -0.5),
        "w_exp_down": nrm(ks[23], (L, N_EXPERTS, D_EXPERT, D), D_EXPERT ** -0.5),
        "final_norm": 1.0 + nrm(ks[24], (D,), 0.02),
    }


def reference(x, c, ctx, c_ctx, w_mod, b_mod, norm_mix, w_in, rpb, w_att_out, conv_w, conv_b,
              conv_ln_g, conv_ln_b, w_conv_out, w_o, norm_ffn, w_router_group, b_router_group,
              w_router_expert, b_router_expert, w_exp_gate, w_exp_up, w_exp_down, final_norm):
    B, N, D = x.shape
    n_ctx = ctx.shape[1]
    for l in range(DEPTH):
        last = l == DEPTH - 1
        m_lat = jax.nn.silu(c) @ w_mod[l] + b_mod[l]
        sh1, sc1, g1, sh2, sc2, g2 = jnp.split(m_lat[:, None, :], 6, axis=-1)
        m_ctx = jax.nn.silu(c_ctx) @ w_mod[l] + b_mod[l]
        sh1c, sc1c, g1c, sh2c, sc2c, g2c = jnp.split(m_ctx, 6, axis=-1)

        h = modulate(rmsnorm(x, norm_mix[l]), sh1, sc1)
        hc = modulate(rmsnorm(ctx, norm_mix[l]), sh1c, sc1c)
        proj = h @ w_in[l]
        q = split_heads(proj[..., Q0:K0])
        k = split_heads(proj[..., K0:V0])
        v = split_heads(proj[..., V0:GLU0])
        if last:
            kv_c = hc @ w_in[l][:, K0:GLU0]
            proj_c = None
        else:
            proj_c = hc @ w_in[l]
            kv_c = proj_c[..., K0:GLU0]
        k_c = split_heads(kv_c[..., :D_ATT]).transpose(0, 2, 1, 3)
        v_c = split_heads(kv_c[..., D_ATT:]).transpose(0, 2, 1, 3)
        att = latent_neighbourhood_attention(q, k, v, k_c, v_c, rpb[l])
        y = merge_branches(proj, att, conv_w[l], conv_b[l], conv_ln_g[l], conv_ln_b[l],
                           w_att_out[l], w_conv_out[l], w_o[l])
        x_mid = x + g1 * y
        h2 = modulate(rmsnorm(x_mid, norm_ffn[l]), sh2, sc2)

        if last:
            tokens = h2.reshape(B * N, D)
        else:
            q_c = split_heads(proj_c[..., Q0:K0]).transpose(0, 2, 1, 3)
            att_c = context_attention(q_c, k_c, v_c)
            y_c = merge_branches(proj_c, att_c, conv_w[l], conv_b[l], conv_ln_g[l], conv_ln_b[l],
                                 w_att_out[l], w_conv_out[l], w_o[l])
            ctx_mid = ctx + g1c * y_c
            h2c = modulate(rmsnorm(ctx_mid, norm_ffn[l]), sh2c, sc2c)
            tokens = jnp.concatenate([h2.reshape(B * N, D), h2c.reshape(B * n_ctx, D)], axis=0)
        f = hierarchical_moe(tokens, w_router_group[l], b_router_group[l], w_router_expert[l],
                             b_router_expert[l], w_exp_gate[l], w_exp_up[l], w_exp_down[l])
        x = x_mid + g2 * f[:B * N].reshape(B, N, D)
        if not last:
            ctx = ctx_mid + g2c * f[B * N:].reshape(B, n_ctx, D)
    return rmsnorm(x, final_norm)
```

```python
import functools

import numpy as np
import jax
import jax.numpy as jnp
from jax import lax
from jax.experimental import pallas as pl
from jax.experimental.pallas import tpu as pltpu

F32 = jnp.float32
BF16 = jnp.bfloat16
U32 = jnp.uint32

GRID_W = 64
N_HEADS = 8
HEAD_DIM = 64
D_ATT = N_HEADS * HEAD_DIM
WIN_ROWS = 8
WIN_COLS = 16
D_CONV = 512
CONV_WIDTH = 31
N_GROUPS = 4
EXPERTS_PER_GROUP = 8
N_EXPERTS = N_GROUPS * EXPERTS_PER_GROUP
PAIRS_PER_GROUP = EXPERTS_PER_GROUP * (EXPERTS_PER_GROUP - 1) // 2
N_CLASSES = N_GROUPS * PAIRS_PER_GROUP
NORM_EPS = 1e-6
NEG_INF = -1e30

LANES = 128
SUBLANES = 8
ROW_TILE = SUBLANES
HALO = 16
VMEM_LIMIT = 56 * 1024 * 1024

TM_PROJ = 512
TM_MERGE = 512
TM_ROWS = 1024
MOE_BLOCK = 128
CONV_CHUNK = 64

HIGHEST = lax.Precision.HIGHEST


def _norm_mod(x, g, shift, scale):
    ms = jnp.mean(x * x, axis=-1, keepdims=True)
    y = x * lax.rsqrt(ms + NORM_EPS) * g
    return y * (1.0 + scale) + shift


def _sigmoid(x):
    return jax.nn.sigmoid(x)


def _mod_kernel(c_ref, w_ref, b_ref, o_ref):
    c = c_ref[...]
    s = c * _sigmoid(c)
    o_ref[...] = jnp.dot(s, w_ref[...], precision=HIGHEST, preferred_element_type=F32) + b_ref[...]


def _mod_call(cc, w_mod, b_mod):
    rows, d = cc.shape
    n = w_mod.shape[1]
    tn = 1024
    return pl.pallas_call(
        _mod_kernel,
        grid=(n // tn,),
        in_specs=[
            pl.BlockSpec((rows, d), lambda j: (0, 0)),
            pl.BlockSpec((d, tn), lambda j: (0, j)),
            pl.BlockSpec((1, tn), lambda j: (0, j)),
        ],
        out_specs=pl.BlockSpec((rows, tn), lambda j: (0, j)),
        out_shape=jax.ShapeDtypeStruct((rows, n), F32),
        compiler_params=pltpu.CompilerParams(dimension_semantics=("arbitrary",), vmem_limit_bytes=VMEM_LIMIT),
        name="mod",
    )(cc, w_mod, b_mod)


Q0, K0, V0, GLU0 = 0, D_ATT, 2 * D_ATT, 3 * D_ATT
GA0 = GLU0 + 2 * D_CONV


def _proj_kernel(x_ref, mod_ref, g_ref, w_ref, q_ref, k_ref, v_ref, u_ref, ga_ref, gb_ref):
    d = x_ref.shape[1]
    mod = mod_ref[0]
    h = _norm_mod(x_ref[...], g_ref[...], mod[0:1], mod[1:2]).astype(BF16)

    def seg(lo, hi):
        return jnp.dot(h, w_ref[:, lo:hi], preferred_element_type=F32)

    q_ref[...] = (seg(Q0, K0) * (HEAD_DIM ** -0.5)).astype(BF16)
    k_ref[...] = seg(K0, V0).astype(BF16)
    v_ref[...] = seg(V0, GLU0).astype(BF16)
    a = seg(GLU0, GLU0 + D_CONV)
    g = seg(GLU0 + D_CONV, GA0)
    u_ref[...] = (a * _sigmoid(g)).astype(BF16)
    ga_ref[...] = _sigmoid(seg(GA0, GA0 + d)).astype(BF16)
    gb_ref[...] = _sigmoid(seg(GA0 + d, GA0 + 2 * d)).astype(BF16)


def _proj_call(x2, mod3, g, w_in, seq):
    t, d = x2.shape
    tm = TM_PROJ
    tpb = seq // tm
    d_in = w_in.shape[1]
    outs = [
        jax.ShapeDtypeStruct((t, D_ATT), BF16),
        jax.ShapeDtypeStruct((t, D_ATT), BF16),
        jax.ShapeDtypeStruct((t, D_ATT), BF16),
        jax.ShapeDtypeStruct((t, D_CONV), BF16),
        jax.ShapeDtypeStruct((t, d), BF16),
        jax.ShapeDtypeStruct((t, d), BF16),
    ]
    row = lambda i: (i, 0)
    return pl.pallas_call(
        _proj_kernel,
        grid=(t // tm,),
        in_specs=[
            pl.BlockSpec((tm, d), row),
            pl.BlockSpec((1, 6, d), lambda i: (i // tpb, 0, 0)),
            pl.BlockSpec((1, d), lambda i: (0, 0)),
            pl.BlockSpec((d, d_in), lambda i: (0, 0)),
        ],
        out_specs=[
            pl.BlockSpec((tm, D_ATT), row),
            pl.BlockSpec((tm, D_ATT), row),
            pl.BlockSpec((tm, D_ATT), row),
            pl.BlockSpec((tm, D_CONV), row),
            pl.BlockSpec((tm, d), row),
            pl.BlockSpec((tm, d), row),
        ],
        out_shape=outs,
        compiler_params=pltpu.CompilerParams(dimension_semantics=("arbitrary",), vmem_limit_bytes=VMEM_LIMIT),
        name="proj",
    )(x2, mod3, g, w_in)


def _ctx_kv_kernel(x_ref, mod_ref, g_ref, w_ref, k_ref, v_ref):
    mod = mod_ref[0]
    h = _norm_mod(x_ref[...], g_ref[...], mod[0:1], mod[1:2]).astype(BF16)
    k_ref[...] = jnp.dot(h, w_ref[:, 0:D_ATT], preferred_element_type=F32).astype(BF16)
    v_ref[...] = jnp.dot(h, w_ref[:, D_ATT:2 * D_ATT], preferred_element_type=F32).astype(BF16)


def _ctx_kv_call(c2, mod3, g, w_kv):
    t, d = c2.shape
    tm = 512
    row = lambda i: (i, 0)
    return pl.pallas_call(
        _ctx_kv_kernel,
        grid=(t // tm,),
        in_specs=[
            pl.BlockSpec((tm, d), row),
            pl.BlockSpec((1, 6, d), lambda i: (0, 0, 0)),
            pl.BlockSpec((1, d), lambda i: (0, 0)),
            pl.BlockSpec((d, 2 * D_ATT), lambda i: (0, 0)),
        ],
        out_specs=[pl.BlockSpec((tm, D_ATT), row), pl.BlockSpec((tm, D_ATT), row)],
        out_shape=[jax.ShapeDtypeStruct((t, D_ATT), BF16)] * 2,
        compiler_params=pltpu.CompilerParams(dimension_semantics=("arbitrary",), vmem_limit_bytes=VMEM_LIMIT),
        name="ctx_kv",
    )(c2, mod3, g, w_kv)


def _attn_kernel(q_ref, k_ref, v_ref, kc_ref, vc_ref, bias_ref, o_ref):
    rows = q_ref.shape[0] // GRID_W
    n_loc = WIN_ROWS * GRID_W
    nt = (((1,), (1,)), ((), ()))
    kc = kc_ref[...]
    vc = vc_ref[...]

    def body(r, carry):
        r_start = jnp.clip(r - WIN_ROWS // 2, 0, rows - WIN_ROWS)
        pat = r - r_start
        q_off = pl.multiple_of(r * GRID_W, GRID_W)
        w_off = pl.multiple_of(r_start * GRID_W, GRID_W)
        qr = q_ref[pl.ds(q_off, GRID_W), :]
        kw = k_ref[pl.ds(w_off, n_loc), :]
        vw = v_ref[pl.ds(w_off, n_loc), :]
        outs = []
        for h in range(N_HEADS):
            hs = slice(h * HEAD_DIM, (h + 1) * HEAD_DIM)
            qh = qr[:, hs]
            s_loc = lax.dot_general(qh, kw[:, hs], nt, preferred_element_type=F32) + bias_ref[pat, h]
            s_ctx = lax.dot_general(qh, kc[:, hs], nt, preferred_element_type=F32)
            m = jnp.maximum(jnp.max(s_loc, axis=-1, keepdims=True), jnp.max(s_ctx, axis=-1, keepdims=True))
            p_loc = jnp.exp(s_loc - m)
            p_ctx = jnp.exp(s_ctx - m)
            l = jnp.sum(p_loc, axis=-1, keepdims=True) + jnp.sum(p_ctx, axis=-1, keepdims=True)
            o = (jnp.dot(p_loc.astype(BF16), vw[:, hs], preferred_element_type=F32)
                 + jnp.dot(p_ctx.astype(BF16), vc[:, hs], preferred_element_type=F32))
            outs.append(o / l)
        o_ref[pl.ds(q_off, GRID_W), :] = jnp.concatenate(outs, axis=-1).astype(o_ref.dtype)
        return carry

    lax.fori_loop(0, rows, body, 0)


def _attn_call(q, k, v, kc, vc, bias, seq, n_ctx):
    t = q.shape[0]
    b = t // seq
    lat = pl.BlockSpec((seq, D_ATT), lambda i: (i, 0))
    ctx = pl.BlockSpec((n_ctx, D_ATT), lambda i: (i, 0))
    return pl.pallas_call(
        _attn_kernel,
        grid=(b,),
        in_specs=[lat, lat, lat, ctx, ctx,
                  pl.BlockSpec(bias.shape, lambda i: (0, 0, 0, 0))],
        out_specs=lat,
        out_shape=jax.ShapeDtypeStruct((t, D_ATT), BF16),
        compiler_params=pltpu.CompilerParams(dimension_semantics=("arbitrary",), vmem_limit_bytes=VMEM_LIMIT),
        name="attn",
    )(q, k, v, kc, vc, bias)


def _bias_table(rpb):
    cq = np.arange(GRID_W)[:, None]
    ck = np.arange(GRID_W)[None, :]
    c_start = np.clip(cq - WIN_COLS // 2, 0, GRID_W - WIN_COLS)
    col_mask = (ck >= c_start) & (ck < c_start + WIN_COLS)
    dc_idx = np.clip(ck - cq + WIN_COLS - 1, 0, 2 * WIN_COLS - 2)
    pat = np.arange(WIN_ROWS)[:, None]
    kr = np.arange(WIN_ROWS)[None, :]
    dr_idx = kr - pat + WIN_ROWS - 1
    tab = rpb[:, dr_idx][:, :, :, dc_idx]
    tab = jnp.where(col_mask[None, None, None], tab, NEG_INF)
    tab = tab.transpose(1, 0, 3, 2, 4)
    return tab.reshape(WIN_ROWS, N_HEADS, GRID_W, WIN_ROWS * GRID_W).astype(F32)


def _merge_kernel(att_ref, u_ref, up_ref, un_ref, ga_ref, gb_ref, x_ref, mod_ref,
                  cw_ref, cb_ref, lg_ref, lb_ref, wa_ref, wc_ref, wo_ref, nf_ref, wr_ref, br_ref, tri_ref,
                  xmid_ref, hp_ref, info_ref, cnt_ref,
                  ubuf, ybuf, run_ref, *, tiles_per_seq):
    i = pl.program_id(0)
    tm = x_ref.shape[0]
    d = x_ref.shape[1]
    mod = mod_ref[0]
    first = (i % tiles_per_seq) == 0
    last = (i % tiles_per_seq) == tiles_per_seq - 1

    ubuf[0:HALO, :] = jnp.where(first, 0.0, up_ref[...].astype(F32))
    ubuf[HALO:HALO + tm, :] = u_ref[...].astype(F32)
    ubuf[HALO + tm:, :] = jnp.where(last, 0.0, un_ref[...].astype(F32))
    base = HALO - CONV_WIDTH // 2
    for tc in range(tm // CONV_CHUNK):
        for lc in range(D_CONV // LANES):
            ls = slice(lc * LANES, (lc + 1) * LANES)
            acc = jnp.zeros((CONV_CHUNK, LANES), F32)
            for j in range(CONV_WIDTH):
                lo = tc * CONV_CHUNK + base + j
                acc = acc + ubuf[lo:lo + CONV_CHUNK, ls] * cw_ref[j:j + 1, ls]
            ybuf[tc * CONV_CHUNK:(tc + 1) * CONV_CHUNK, ls] = acc
    yc = ybuf[...] + cb_ref[...]
    mu = jnp.mean(yc, axis=-1, keepdims=True)
    var = jnp.mean(jnp.square(yc - mu), axis=-1, keepdims=True)
    yn = (yc - mu) * lax.rsqrt(var + NORM_EPS) * lg_ref[...] + lb_ref[...]
    yn = yn * _sigmoid(yn)
    y_conv = jnp.dot(yn.astype(BF16), wc_ref[...], preferred_element_type=F32)

    y_att = jnp.dot(att_ref[...], wa_ref[...], preferred_element_type=F32)
    mix = ga_ref[...].astype(F32) * y_att + gb_ref[...].astype(F32) * y_conv
    y = jnp.dot(mix.astype(BF16), wo_ref[...], preferred_element_type=F32)
    x_mid = x_ref[...] + mod[2:3] * y
    xmid_ref[...] = x_mid
    h2 = _norm_mod(x_mid, nf_ref[...], mod[3:4], mod[4:5])

    logits = jnp.dot(h2, wr_ref[...], precision=HIGHEST, preferred_element_type=F32) + br_ref[...]
    lane = lax.broadcasted_iota(jnp.int32, logits.shape, 1).astype(F32)
    big = float(LANES)
    is_g = lane < N_GROUPS
    gl = jnp.where(is_g, logits, -jnp.inf)
    gmax = jnp.max(gl, axis=-1, keepdims=True)
    g_idx = jnp.min(jnp.where(gl == gmax, lane, big), axis=-1, keepdims=True)
    p_group = 1.0 / jnp.sum(jnp.where(is_g, jnp.exp(gl - gmax), 0.0), axis=-1, keepdims=True)
    e_lo = N_GROUPS + EXPERTS_PER_GROUP * g_idx
    in_grp = (lane >= e_lo) & (lane < e_lo + EXPERTS_PER_GROUP)
    el = jnp.where(in_grp, logits, -jnp.inf)
    v1 = jnp.max(el, axis=-1, keepdims=True)
    i1 = jnp.min(jnp.where(el == v1, lane, big), axis=-1, keepdims=True)
    el2 = jnp.where(lane == i1, -jnp.inf, el)
    v2 = jnp.max(el2, axis=-1, keepdims=True)
    i2 = jnp.min(jnp.where(el2 == v2, lane, big), axis=-1, keepdims=True)
    e2 = jnp.exp(v2 - v1)
    gate1 = p_group / (1.0 + e2)
    gate2 = p_group * e2 / (1.0 + e2)
    j1 = i1 - e_lo
    j2 = i2 - e_lo
    ja = jnp.minimum(j1, j2)
    jb = jnp.maximum(j1, j2)
    gate_a = jnp.where(j1 < j2, gate1, gate2)
    gate_b = jnp.where(j1 < j2, gate2, gate1)
    pair = ja * (2 * EXPERTS_PER_GROUP - 1 - ja) * 0.5 + (jb - ja - 1.0)
    cls = g_idx * PAIRS_PER_GROUP + pair

    @pl.when(i == 0)
    def _():
        run_ref[...] = jnp.zeros_like(run_ref)

    onehot = lane == cls
    oh_f = jnp.where(onehot, 1.0, 0.0)
    before = jnp.dot(tri_ref[...], oh_f.astype(BF16), preferred_element_type=F32)
    run = run_ref[0:1, :]
    rank = jnp.sum(oh_f * (before + run), axis=-1, keepdims=True)
    new_run = run + jnp.sum(oh_f, axis=0, keepdims=True)
    run_ref[...] = jnp.broadcast_to(new_run, run_ref.shape)
    cnt_ref[...] = jnp.broadcast_to(new_run, cnt_ref.shape)

    info = jnp.where(lane == 0.0, cls, jnp.where(lane == 1.0, rank, 0.0))
    info_ref[...] = info.T[0:SUBLANES, :]

    half = d // 2
    hi = lax.bitcast_convert_type(h2[:, :half].astype(BF16).astype(F32), U32)
    lo = lax.bitcast_convert_type(h2[:, half:].astype(BF16).astype(F32), U32)
    words = hi | (lo >> 16)
    n_words = half // LANES
    for c in range(n_words):
        hp_ref[pl.ds(c, tm, stride=ROW_TILE), :] = words[:, c * LANES:(c + 1) * LANES]
    gates = jnp.where(lane == 0.0, gate_a, jnp.where(lane == 1.0, gate_b, 0.0))
    hp_ref[pl.ds(n_words, tm, stride=ROW_TILE), :] = lax.bitcast_convert_type(gates, U32)
    zero = jnp.zeros((tm, LANES), U32)
    for c in range(n_words + 1, ROW_TILE):
        hp_ref[pl.ds(c, tm, stride=ROW_TILE), :] = zero


def _merge_call(att, u, ga, gb, x2, mod3, cw, cb, lg, lb, wa, wc, wo, nf, wr, br, seq):
    t, d = x2.shape
    tm = TM_MERGE
    tpb = seq // tm
    hb = tm // HALO
    n_halo = t // HALO
    tri = jnp.asarray(np.tril(np.ones((tm, tm), np.float32), -1), BF16)
    row = lambda i: (i, 0)
    const = lambda i: (0, 0)
    kern = functools.partial(_merge_kernel, tiles_per_seq=tpb)
    return pl.pallas_call(
        kern,
        grid=(t // tm,),
        in_specs=[
            pl.BlockSpec((tm, D_ATT), row),
            pl.BlockSpec((tm, D_CONV), row),
            pl.BlockSpec((HALO, D_CONV), lambda i: (jnp.maximum(i * hb - 1, 0), 0)),
            pl.BlockSpec((HALO, D_CONV), lambda i: (jnp.minimum((i + 1) * hb, n_halo - 1), 0)),
            pl.BlockSpec((tm, d), row),
            pl.BlockSpec((tm, d), row),
            pl.BlockSpec((tm, d), row),
            pl.BlockSpec((1, 6, d), lambda i: (i // tpb, 0, 0)),
            pl.BlockSpec(cw.shape, const),
            pl.BlockSpec((1, D_CONV), const),
            pl.BlockSpec((1, D_CONV), const),
            pl.BlockSpec((1, D_CONV), const),
            pl.BlockSpec(wa.shape, const),
            pl.BlockSpec(wc.shape, const),
            pl.BlockSpec(wo.shape, const),
            pl.BlockSpec((1, d), const),
            pl.BlockSpec(wr.shape, const),
            pl.BlockSpec((1, LANES), const),
            pl.BlockSpec((tm, tm), const),
        ],
        out_specs=[
            pl.BlockSpec((tm, d), row),
            pl.BlockSpec((tm * ROW_TILE, LANES), row),
            pl.BlockSpec((SUBLANES, tm), lambda i: (0, i)),
            pl.BlockSpec((SUBLANES, LANES), const),
        ],
        out_shape=[
            jax.ShapeDtypeStruct((t, d), F32),
            jax.ShapeDtypeStruct((t * ROW_TILE, LANES), U32),
            jax.ShapeDtypeStruct((SUBLANES, t), F32),
            jax.ShapeDtypeStruct((SUBLANES, LANES), F32),
        ],
        scratch_shapes=[
            pltpu.VMEM((tm + 2 * HALO, D_CONV), F32),
            pltpu.VMEM((tm, D_CONV), F32),
            pltpu.VMEM((SUBLANES, LANES), F32),
        ],
        compiler_params=pltpu.CompilerParams(dimension_semantics=("arbitrary",), vmem_limit_bytes=VMEM_LIMIT),
        name="merge",
    )(att, u, u, u, ga, gb, x2, mod3, cw, cb, lg, lb, wa, wc, wo, nf, wr, br, tri)


def _dispatch_kernel(dest_ref, hp_hbm, zeros_hbm, hs_hbm, sem):
    del zeros_hbm
    n = dest_ref.shape[0]
    base = pl.program_id(0) * n

    def row_copy(t):
        src = pl.multiple_of((base + t) * ROW_TILE, ROW_TILE)
        dst = pl.multiple_of(dest_ref[t] * ROW_TILE, ROW_TILE)
        return pltpu.make_async_copy(hp_hbm.at[pl.ds(src, ROW_TILE)], hs_hbm.at[pl.ds(dst, ROW_TILE)], sem)

    def body(t, carry):
        row_copy(t).start()
        return carry

    lax.fori_loop(0, n, body, 0, unroll=8)
    pltpu.make_async_copy(hp_hbm.at[pl.ds(0, n * ROW_TILE)], hs_hbm.at[pl.ds(0, n * ROW_TILE)], sem).wait()


def _dispatch_call(dest, hp, p_rows):
    t = dest.shape[0]
    zeros = jnp.zeros((p_rows * ROW_TILE, LANES), U32)
    return pl.pallas_call(
        _dispatch_kernel,
        grid=(t // TM_ROWS,),
        in_specs=[
            pl.BlockSpec((TM_ROWS,), lambda i: (i,), memory_space=pltpu.SMEM),
            pl.BlockSpec(memory_space=pl.ANY),
            pl.BlockSpec(memory_space=pl.ANY),
        ],
        out_specs=pl.BlockSpec(memory_space=pl.ANY),
        out_shape=jax.ShapeDtypeStruct((p_rows * ROW_TILE, LANES), U32),
        scratch_shapes=[pltpu.SemaphoreType.DMA(())],
        input_output_aliases={2: 0},
        compiler_params=pltpu.CompilerParams(dimension_semantics=("arbitrary",), has_side_effects=True),
        name="dispatch",
    )(dest, hp, zeros)


def _moe_kernel(ea_ref, eb_ref, valid_ref, hs_ref, wga_ref, wua_ref, wda_ref, wgb_ref, wub_ref, wdb_ref, y_ref):
    i = pl.program_id(0)
    blk = hs_ref.shape[0] // ROW_TILE
    d = wga_ref.shape[1]
    n_words = d // 2 // LANES

    @pl.when(valid_ref[i] == 1)
    def _():
        his, los = [], []
        for c in range(n_words):
            w = hs_ref[pl.ds(c, blk, stride=ROW_TILE), :]
            his.append(lax.bitcast_convert_type(w & jnp.uint32(0xFFFF0000), F32).astype(BF16))
            los.append(lax.bitcast_convert_type(w << 16, F32).astype(BF16))
        x = jnp.concatenate(his + los, axis=-1)
        gates = lax.bitcast_convert_type(hs_ref[pl.ds(n_words, blk, stride=ROW_TILE), :], F32)
        gate_a = gates[:, 0:1]
        gate_b = gates[:, 1:2]

        def mlp(wg, wu, wd):
            g = jnp.dot(x, wg[0], preferred_element_type=F32)
            u = jnp.dot(x, wu[0], preferred_element_type=F32)
            a = (g * _sigmoid(g) * u).astype(BF16)
            return jnp.dot(a, wd[0], preferred_element_type=F32)

        y = gate_a * mlp(wga_ref, wua_ref, wda_ref) + gate_b * mlp(wgb_ref, wub_ref, wdb_ref)
        for c in range(d // LANES):
            y_ref[pl.ds(c, blk, stride=ROW_TILE), :] = y[:, c * LANES:(c + 1) * LANES]

    @pl.when(valid_ref[i] != 1)
    def _():
        y_ref[...] = jnp.zeros_like(y_ref)


def _moe_call(blk_ea, blk_eb, blk_valid, hs, wg, wu, wd):
    nb = blk_ea.shape[0]
    d = wg.shape[1]
    de = wg.shape[2]
    rows = MOE_BLOCK * ROW_TILE
    tok = pl.BlockSpec((rows, LANES), lambda i, ea, eb, va: (i, 0))
    w_a = lambda shape: pl.BlockSpec((1,) + shape, lambda i, ea, eb, va: (ea[i], 0, 0))
    w_b = lambda shape: pl.BlockSpec((1,) + shape, lambda i, ea, eb, va: (eb[i], 0, 0))
    grid_spec = pltpu.PrefetchScalarGridSpec(
        num_scalar_prefetch=3,
        grid=(nb,),
        in_specs=[tok, w_a((d, de)), w_a((d, de)), w_a((de, d)), w_b((d, de)), w_b((d, de)), w_b((de, d))],
        out_specs=tok,
    )
    return pl.pallas_call(
        _moe_kernel,
        grid_spec=grid_spec,
        out_shape=jax.ShapeDtypeStruct((nb * rows, LANES), F32),
        compiler_params=pltpu.CompilerParams(dimension_semantics=("arbitrary",), vmem_limit_bytes=VMEM_LIMIT),
        name="moe",
    )(blk_ea, blk_eb, blk_valid, hs, wg, wu, wd, wg, wu, wd)


def _final_kernel(dest_ref, xmid_ref, mod_ref, fn_ref, y_hbm, o_ref, fbuf, sem):
    n = dest_ref.shape[0]
    d = xmid_ref.shape[1]

    def body(t, carry):
        src = pl.multiple_of(dest_ref[t] * ROW_TILE, ROW_TILE)
        dst = pl.multiple_of(t * ROW_TILE, ROW_TILE)
        pltpu.make_async_copy(y_hbm.at[pl.ds(src, ROW_TILE)], fbuf.at[pl.ds(dst, ROW_TILE)], sem).start()
        return carry

    lax.fori_loop(0, n, body, 0, unroll=8)
    pltpu.make_async_copy(y_hbm.at[pl.ds(0, n * ROW_TILE)], fbuf, sem).wait()
    f = jnp.concatenate([fbuf[pl.ds(c, n, stride=ROW_TILE), :] for c in range(d // LANES)], axis=-1)
    x = xmid_ref[...] + mod_ref[0][5:6] * f
    ms = jnp.mean(x * x, axis=-1, keepdims=True)
    o_ref[...] = x * lax.rsqrt(ms + NORM_EPS) * fn_ref[...]


def _final_call(dest, xmid, mod3, fn, y, seq):
    t, d = xmid.shape
    tm = TM_ROWS
    tpb = seq // tm
    return pl.pallas_call(
        _final_kernel,
        grid=(t // tm,),
        in_specs=[
            pl.BlockSpec((tm,), lambda i: (i,), memory_space=pltpu.SMEM),
            pl.BlockSpec((tm, d), lambda i: (i, 0)),
            pl.BlockSpec((1, 6, d), lambda i: (i // tpb, 0, 0)),
            pl.BlockSpec((1, d), lambda i: (0, 0)),
            pl.BlockSpec(memory_space=pl.ANY),
        ],
        out_specs=pl.BlockSpec((tm, d), lambda i: (i, 0)),
        out_shape=jax.ShapeDtypeStruct((t, d), F32),
        scratch_shapes=[pltpu.VMEM((tm * ROW_TILE, LANES), F32), pltpu.SemaphoreType.DMA(())],
        compiler_params=pltpu.CompilerParams(dimension_semantics=("arbitrary",), vmem_limit_bytes=VMEM_LIMIT),
        name="final",
    )(dest, xmid, mod3, fn, y)


def _pair_tables():
    ea, eb = [], []
    for g in range(N_GROUPS):
        for a in range(EXPERTS_PER_GROUP):
            for b in range(a + 1, EXPERTS_PER_GROUP):
                ea.append(g * EXPERTS_PER_GROUP + a)
                eb.append(g * EXPERTS_PER_GROUP + b)
    return np.asarray(ea, np.int32), np.asarray(eb, np.int32)


def _routing_plan(info, counts, t):
    cls = info[0].astype(jnp.int32)
    rank = info[1].astype(jnp.int32)
    cnt = counts[0, :N_CLASSES].astype(jnp.int32)
    padded = (cnt + MOE_BLOCK - 1) // MOE_BLOCK * MOE_BLOCK
    pad_end = jnp.cumsum(padded)
    pad_start = pad_end - padded
    dest = pad_start[cls] + rank
    nb = t // MOE_BLOCK + N_CLASSES
    nb_used = pad_end[-1] // MOE_BLOCK
    blk = jnp.arange(nb, dtype=jnp.int32)
    blk_cls = jnp.clip(jnp.searchsorted(pad_end, blk * MOE_BLOCK, side="right"), 0, N_CLASSES - 1)
    valid = blk < nb_used
    last_cls = blk_cls[jnp.maximum(nb_used - 1, 0)]
    blk_cls = jnp.where(valid, blk_cls, last_cls).astype(jnp.int32)
    tab_a, tab_b = _pair_tables()
    blk_ea = jnp.asarray(tab_a)[blk_cls]
    blk_eb = jnp.asarray(tab_b)[blk_cls]
    return dest, blk_ea, blk_eb, valid.astype(jnp.int32), nb


def kernel(x, c, ctx, c_ctx, w_mod, b_mod, norm_mix, w_in, rpb, w_att_out, conv_w, conv_b, conv_ln_g, conv_ln_b,
           w_conv_out, w_o, norm_ffn, w_router_group, b_router_group, w_router_expert, b_router_expert,
           w_exp_gate, w_exp_up, w_exp_down, final_norm):
    b, seq, d = x.shape
    n_ctx = ctx.shape[1]
    t = b * seq
    assert w_mod.shape[0] == 1, "single layer"
    assert seq % TM_ROWS == 0 and seq // GRID_W >= WIN_ROWS and n_ctx % 8 == 0 and (b * n_ctx) % 512 == 0
    assert d == 1024

    mod_rows = -(-(b + 1) // SUBLANES) * SUBLANES
    cc = jnp.zeros((mod_rows, d), F32).at[:b].set(c).at[b].set(c_ctx)
    m_all = _mod_call(cc, w_mod[0], b_mod[0][None, :])
    mod_lat = m_all[:b].reshape(b, 6, d)
    mod_ctx = m_all[b:b + 1].reshape(1, 6, d)

    x2 = x.reshape(t, d)
    g_mix = norm_mix[0][None, :]
    w_in_b = w_in[0].astype(BF16)
    q, k, v, u, ga, gb = _proj_call(x2, mod_lat, g_mix, w_in_b, seq)
    kc, vc = _ctx_kv_call(ctx.reshape(b * n_ctx, d), mod_ctx, g_mix, w_in_b[:, K0:GLU0])

    att = _attn_call(q, k, v, kc, vc, _bias_table(rpb[0]), seq, n_ctx)

    cw = jnp.zeros((32, D_CONV), F32).at[:CONV_WIDTH].set(conv_w[0])
    wr = jnp.zeros((d, LANES), F32).at[:, :N_GROUPS].set(w_router_group[0])
    wr = wr.at[:, N_GROUPS:N_GROUPS + N_EXPERTS].set(w_router_expert[0])
    br = jnp.zeros((1, LANES), F32).at[0, :N_GROUPS].set(b_router_group[0])
    br = br.at[0, N_GROUPS:N_GROUPS + N_EXPERTS].set(b_router_expert[0])
    x_mid, hp, info, counts = _merge_call(
        att, u, ga, gb, x2, mod_lat, cw, conv_b[0][None, :], conv_ln_g[0][None, :], conv_ln_b[0][None, :],
        w_att_out[0].astype(BF16), w_conv_out[0].astype(BF16), w_o[0].astype(BF16), norm_ffn[0][None, :],
        wr, br, seq)

    dest, blk_ea, blk_eb, blk_valid, nb = _routing_plan(info, counts, t)
    hs = _dispatch_call(dest, hp, nb * MOE_BLOCK)
    y = _moe_call(blk_ea, blk_eb, blk_valid, hs,
                  w_exp_gate[0].astype(BF16), w_exp_up[0].astype(BF16), w_exp_down[0].astype(BF16))
    out = _final_call(dest, x_mid, mod_lat, final_norm[None, :], y, seq)
    return out.reshape(b, seq, d)
```

```python
import functools

import numpy as np
import jax
import jax.numpy as jnp
from jax import lax
from jax.experimental import pallas as pl
from jax.experimental.pallas import tpu as pltpu

F32 = jnp.float32
BF16 = jnp.bfloat16
U32 = jnp.uint32

GRID_W = 64
N_HEADS = 8
HEAD_DIM = 64
D_ATT = N_HEADS * HEAD_DIM
WIN_ROWS = 8
WIN_COLS = 16
D_CONV = 512
CONV_WIDTH = 31
N_GROUPS = 4
EXPERTS_PER_GROUP = 8
N_EXPERTS = N_GROUPS * EXPERTS_PER_GROUP
PAIRS_PER_GROUP = EXPERTS_PER_GROUP * (EXPERTS_PER_GROUP - 1) // 2
N_CLASSES = N_GROUPS * PAIRS_PER_GROUP
NORM_EPS = 1e-6
NEG_INF = -1e30

LANES = 128
SUBLANES = 8
ROW_TILE = SUBLANES
HALO = 16
VMEM_LIMIT = 56 * 1024 * 1024

TM_PROJ = 512
TM_MERGE = 512
TM_ROWS = 1024
MOE_BLOCK = 128
CONV_CHUNK = 64

HIGHEST = lax.Precision.HIGHEST


def _norm_mod(x, g, shift, scale):
    ms = jnp.mean(x * x, axis=-1, keepdims=True)
    y = x * lax.rsqrt(ms + NORM_EPS) * g
    return y * (1.0 + scale) + shift


def _sigmoid(x):
    return jax.nn.sigmoid(x)


def _mod_kernel(c_ref, w_ref, b_ref, o_ref):
    c = c_ref[...]
    s = c * _sigmoid(c)
    o_ref[...] = jnp.dot(s, w_ref[...], precision=HIGHEST, preferred_element_type=F32) + b_ref[...]


def _mod_call(cc, w_mod, b_mod):
    rows, d = cc.shape
    n = w_mod.shape[1]
    tn = 1024
    return pl.pallas_call(
        _mod_kernel,
        grid=(n // tn,),
        in_specs=[
            pl.BlockSpec((rows, d), lambda j: (0, 0)),
            pl.BlockSpec((d, tn), lambda j: (0, j)),
            pl.BlockSpec((1, tn), lambda j: (0, j)),
        ],
        out_specs=pl.BlockSpec((rows, tn), lambda j: (0, j)),
        out_shape=jax.ShapeDtypeStruct((rows, n), F32),
        compiler_params=pltpu.CompilerParams(dimension_semantics=("arbitrary",), vmem_limit_bytes=VMEM_LIMIT),
        name="mod",
    )(cc, w_mod, b_mod)


Q0, K0, V0, GLU0 = 0, D_ATT, 2 * D_ATT, 3 * D_ATT
GA0 = GLU0 + 2 * D_CONV


def _proj_kernel(x_ref, mod_ref, g_ref, w_ref, q_ref, k_ref, v_ref, u_ref, ga_ref, gb_ref):
    d = x_ref.shape[1]
    mod = mod_ref[0]
    h = _norm_mod(x_ref[...], g_ref[...], mod[0:1], mod[1:2]).astype(BF16)

    def seg(lo, hi):
        return jnp.dot(h, w_ref[:, lo:hi], preferred_element_type=F32)

    q_ref[...] = (seg(Q0, K0) * (HEAD_DIM ** -0.5)).astype(BF16)
    k_ref[...] = seg(K0, V0).astype(BF16)
    v_ref[...] = seg(V0, GLU0).astype(BF16)
    a = seg(GLU0, GLU0 + D_CONV)
    g = seg(GLU0 + D_CONV, GA0)
    u_ref[...] = (a * _sigmoid(g)).astype(BF16)
    ga_ref[...] = _sigmoid(seg(GA0, GA0 + d)).astype(BF16)
    gb_ref[...] = _sigmoid(seg(GA0 + d, GA0 + 2 * d)).astype(BF16)


def _proj_call(x2, mod3, g, w_in, seq):
    t, d = x2.shape
    tm = TM_PROJ
    tpb = seq // tm
    d_in = w_in.shape[1]
    outs = [
        jax.ShapeDtypeStruct((t, D_ATT), BF16),
        jax.ShapeDtypeStruct((t, D_ATT), BF16),
        jax.ShapeDtypeStruct((t, D_ATT), BF16),
        jax.ShapeDtypeStruct((t, D_CONV), BF16),
        jax.ShapeDtypeStruct((t, d), BF16),
        jax.ShapeDtypeStruct((t, d), BF16),
    ]
    row = lambda i: (i, 0)
    return pl.pallas_call(
        _proj_kernel,
        grid=(t // tm,),
        in_specs=[
            pl.BlockSpec((tm, d), row),
            pl.BlockSpec((1, 6, d), lambda i: (i // tpb, 0, 0)),
            pl.BlockSpec((1, d), lambda i: (0, 0)),
            pl.BlockSpec((d, d_in), lambda i: (0, 0)),
        ],
        out_specs=[
            pl.BlockSpec((tm, D_ATT), row),
            pl.BlockSpec((tm, D_ATT), row),
            pl.BlockSpec((tm, D_ATT), row),
            pl.BlockSpec((tm, D_CONV), row),
            pl.BlockSpec((tm, d), row),
            pl.BlockSpec((tm, d), row),
        ],
        out_shape=outs,
        compiler_params=pltpu.CompilerParams(dimension_semantics=("arbitrary",), vmem_limit_bytes=VMEM_LIMIT),
        name="proj",
    )(x2, mod3, g, w_in)


def _ctx_kv_kernel(x_ref, mod_ref, g_ref, w_ref, k_ref, v_ref):
    mod = mod_ref[0]
    h = _norm_mod(x_ref[...], g_ref[...], mod[0:1], mod[1:2]).astype(BF16)
    k_ref[...] = jnp.dot(h, w_ref[:, 0:D_ATT], preferred_element_type=F32).astype(BF16)
    v_ref[...] = jnp.dot(h, w_ref[:, D_ATT:2 * D_ATT], preferred_element_type=F32).astype(BF16)


def _ctx_kv_call(c2, mod3, g, w_kv):
    t, d = c2.shape
    tm = 512
    row = lambda i: (i, 0)
    return pl.pallas_call(
        _ctx_kv_kernel,
        grid=(t // tm,),
        in_specs=[
            pl.BlockSpec((tm, d), row),
            pl.BlockSpec((1, 6, d), lambda i: (0, 0, 0)),
            pl.BlockSpec((1, d), lambda i: (0, 0)),
            pl.BlockSpec((d, 2 * D_ATT), lambda i: (0, 0)),
        ],
        out_specs=[pl.BlockSpec((tm, D_ATT), row), pl.BlockSpec((tm, D_ATT), row)],
        out_shape=[jax.ShapeDtypeStruct((t, D_ATT), BF16)] * 2,
        compiler_params=pltpu.CompilerParams(dimension_semantics=("arbitrary",), vmem_limit_bytes=VMEM_LIMIT),
        name="ctx_kv",
    )(c2, mod3, g, w_kv)


def _attn_kernel(q_ref, k_ref, v_ref, kc_ref, vc_ref, bias_ref, o_ref):
    rows = q_ref.shape[0] // GRID_W
    n_loc = WIN_ROWS * GRID_W
    nt = (((1,), (1,)), ((), ()))
    kc = kc_ref[...]
    vc = vc_ref[...]

    def body(r, carry):
        r_start = jnp.clip(r - WIN_ROWS // 2, 0, rows - WIN_ROWS)
        pat = r - r_start
        q_off = pl.multiple_of(r * GRID_W, GRID_W)
        w_off = pl.multiple_of(r_start * GRID_W, GRID_W)
        qr = q_ref[pl.ds(q_off, GRID_W), :]
        kw = k_ref[pl.ds(w_off, n_loc), :]
        vw = v_ref[pl.ds(w_off, n_loc), :]
        outs = []
        for h in range(N_HEADS):
            hs = slice(h * HEAD_DIM, (h + 1) * HEAD_DIM)
            qh = qr[:, hs]
            s_loc = lax.dot_general(qh, kw[:, hs], nt, preferred_element_type=F32) + bias_ref[pat, h]
            s_ctx = lax.dot_general(qh, kc[:, hs], nt, preferred_element_type=F32)
            m = jnp.maximum(jnp.max(s_loc, axis=-1, keepdims=True), jnp.max(s_ctx, axis=-1, keepdims=True))
            p_loc = jnp.exp(s_loc - m)
            p_ctx = jnp.exp(s_ctx - m)
            l = jnp.sum(p_loc, axis=-1, keepdims=True) + jnp.sum(p_ctx, axis=-1, keepdims=True)
            o = (jnp.dot(p_loc.astype(BF16), vw[:, hs], preferred_element_type=F32)
                 + jnp.dot(p_ctx.astype(BF16), vc[:, hs], preferred_element_type=F32))
            outs.append(o / l)
        o_ref[pl.ds(q_off, GRID_W), :] = jnp.concatenate(outs, axis=-1).astype(o_ref.dtype)
        return carry

    lax.fori_loop(0, rows, body, 0)


def _attn_call(q, k, v, kc, vc, bias, seq, n_ctx):
    t = q.shape[0]
    b = t // seq
    lat = pl.BlockSpec((seq, D_ATT), lambda i: (i, 0))
    ctx = pl.BlockSpec((n_ctx, D_ATT), lambda i: (i, 0))
    return pl.pallas_call(
        _attn_kernel,
        grid=(b,),
        in_specs=[lat, lat, lat, ctx, ctx,
                  pl.BlockSpec(bias.shape, lambda i: (0, 0, 0, 0))],
        out_specs=lat,
        out_shape=jax.ShapeDtypeStruct((t, D_ATT), BF16),
        compiler_params=pltpu.CompilerParams(dimension_semantics=("arbitrary",), vmem_limit_bytes=VMEM_LIMIT),
        name="attn",
    )(q, k, v, kc, vc, bias)


def _bias_table(rpb):
    cq = np.arange(GRID_W)[:, None]
    ck = np.arange(GRID_W)[None, :]
    c_start = np.clip(cq - WIN_COLS // 2, 0, GRID_W - WIN_COLS)
    col_mask = (ck >= c_start) & (ck < c_start + WIN_COLS)
    dc_idx = np.clip(ck - cq + WIN_COLS - 1, 0, 2 * WIN_COLS - 2)
    pat = np.arange(WIN_ROWS)[:, None]
    kr = np.arange(WIN_ROWS)[None, :]
    dr_idx = kr - pat + WIN_ROWS - 1
    tab = rpb[:, dr_idx][:, :, :, dc_idx]
    tab = jnp.where(col_mask[None, None, None], tab, NEG_INF)
    tab = tab.transpose(1, 0, 3, 2, 4)
    return tab.reshape(WIN_ROWS, N_HEADS, GRID_W, WIN_ROWS * GRID_W).astype(F32)


def _merge_kernel(att_ref, u_ref, up_ref, un_ref, ga_ref, gb_ref, x_ref, mod_ref,
                  cw_ref, cb_ref, lg_ref, lb_ref, wa_ref, wc_ref, wo_ref, nf_ref, wr_ref, br_ref, tri_ref,
                  xmid_ref, hp_ref, info_ref, cnt_ref,
                  ubuf, ybuf, run_ref, *, tiles_per_seq):
    i = pl.program_id(0)
    tm = x_ref.shape[0]
    d = x_ref.shape[1]
    mod = mod_ref[0]
    first = (i % tiles_per_seq) == 0
    last = (i % tiles_per_seq) == tiles_per_seq - 1

    ubuf[0:HALO, :] = jnp.where(first, 0.0, up_ref[...].astype(F32))
    ubuf[HALO:HALO + tm, :] = u_ref[...].astype(F32)
    ubuf[HALO + tm:, :] = jnp.where(last, 0.0, un_ref[...].astype(F32))
    base = HALO - CONV_WIDTH // 2
    for tc in range(tm // CONV_CHUNK):
        for lc in range(D_CONV // LANES):
            ls = slice(lc * LANES, (lc + 1) * LANES)
            acc = jnp.zeros((CONV_CHUNK, LANES), F32)
            for j in range(CONV_WIDTH):
                lo = tc * CONV_CHUNK + base + j
                acc = acc + ubuf[lo:lo + CONV_CHUNK, ls] * cw_ref[j:j + 1, ls]
            ybuf[tc * CONV_CHUNK:(tc + 1) * CONV_CHUNK, ls] = acc
    yc = ybuf[...] + cb_ref[...]
    mu = jnp.mean(yc, axis=-1, keepdims=True)
    var = jnp.mean(jnp.square(yc - mu), axis=-1, keepdims=True)
    yn = (yc - mu) * lax.rsqrt(var + NORM_EPS) * lg_ref[...] + lb_ref[...]
    yn = yn * _sigmoid(yn)
    y_conv = jnp.dot(yn.astype(BF16), wc_ref[...], preferred_element_type=F32)

    y_att = jnp.dot(att_ref[...], wa_ref[...], preferred_element_type=F32)
    mix = ga_ref[...].astype(F32) * y_att + gb_ref[...].astype(F32) * y_conv
    y = jnp.dot(mix.astype(BF16), wo_ref[...], preferred_element_type=F32)
    x_mid = x_ref[...] + mod[2:3] * y
    xmid_ref[...] = x_mid
    h2 = _norm_mod(x_mid, nf_ref[...], mod[3:4], mod[4:5])

    logits = jnp.dot(h2, wr_ref[...], precision=HIGHEST, preferred_element_type=F32) + br_ref[...]
    lane = lax.broadcasted_iota(jnp.int32, logits.shape, 1).astype(F32)
    big = float(LANES)
    is_g = lane < N_GROUPS
    gl = jnp.where(is_g, logits, -jnp.inf)
    gmax = jnp.max(gl, axis=-1, keepdims=True)
    g_idx = jnp.min(jnp.where(gl == gmax, lane, big), axis=-1, keepdims=True)
    p_group = 1.0 / jnp.sum(jnp.where(is_g, jnp.exp(gl - gmax), 0.0), axis=-1, keepdims=True)
    e_lo = N_GROUPS + EXPERTS_PER_GROUP * g_idx
    in_grp = (lane >= e_lo) & (lane < e_lo + EXPERTS_PER_GROUP)
    el = jnp.where(in_grp, logits, -jnp.inf)
    v1 = jnp.max(el, axis=-1, keepdims=True)
    i1 = jnp.min(jnp.where(el == v1, lane, big), axis=-1, keepdims=True)
    el2 = jnp.where(lane == i1, -jnp.inf, el)
    v2 = jnp.max(el2, axis=-1, keepdims=True)
    i2 = jnp.min(jnp.where(el2 == v2, lane, big), axis=-1, keepdims=True)
    e2 = jnp.exp(v2 - v1)
    gate1 = p_group / (1.0 + e2)
    gate2 = p_group * e2 / (1.0 + e2)
    j1 = i1 - e_lo
    j2 = i2 - e_lo
    ja = jnp.minimum(j1, j2)
    jb = jnp.maximum(j1, j2)
    gate_a = jnp.where(j1 < j2, gate1, gate2)
    gate_b = jnp.where(j1 < j2, gate2, gate1)
    pair = ja * (2 * EXPERTS_PER_GROUP - 1 - ja) * 0.5 + (jb - ja - 1.0)
    cls = g_idx * PAIRS_PER_GROUP + pair

    @pl.when(i == 0)
    def _():
        run_ref[...] = jnp.zeros_like(run_ref)

    onehot = lane == cls
    oh_f = jnp.where(onehot, 1.0, 0.0)
    before = jnp.dot(tri_ref[...], oh_f.astype(BF16), preferred_element_type=F32)
    run = run_ref[0:1, :]
    rank = jnp.sum(oh_f * (before + run), axis=-1, keepdims=True)
    new_run = run + jnp.sum(oh_f, axis=0, keepdims=True)
    run_ref[...] = jnp.broadcast_to(new_run, run_ref.shape)
    cnt_ref[...] = jnp.broadcast_to(new_run, cnt_ref.shape)

    info = jnp.where(lane == 0.0, cls, jnp.where(lane == 1.0, rank, 0.0))
    info_ref[...] = info.T[0:SUBLANES, :]

    half = d // 2
    hi = lax.bitcast_convert_type(h2[:, :half].astype(BF16).astype(F32), U32)
    lo = lax.bitcast_convert_type(h2[:, half:].astype(BF16).astype(F32), U32)
    words = hi | (lo >> 16)
    n_words = half // LANES
    for c in range(n_words):
        hp_ref[pl.ds(c, tm, stride=ROW_TILE), :] = words[:, c * LANES:(c + 1) * LANES]
    gates = jnp.where(lane == 0.0, gate_a, jnp.where(lane == 1.0, gate_b, 0.0))
    hp_ref[pl.ds(n_words, tm, stride=ROW_TILE), :] = lax.bitcast_convert_type(gates, U32)
    zero = jnp.zeros((tm, LANES), U32)
    for c in range(n_words + 1, ROW_TILE):
        hp_ref[pl.ds(c, tm, stride=ROW_TILE), :] = zero


def _merge_call(att, u, ga, gb, x2, mod3, cw, cb, lg, lb, wa, wc, wo, nf, wr, br, seq):
    t, d = x2.shape
    tm = TM_MERGE
    tpb = seq // tm
    hb = tm // HALO
    n_halo = t // HALO
    tri = jnp.asarray(np.tril(np.ones((tm, tm), np.float32), -1), BF16)
    row = lambda i: (i, 0)
    const = lambda i: (0, 0)
    kern = functools.partial(_merge_kernel, tiles_per_seq=tpb)
    return pl.pallas_call(
        kern,
        grid=(t // tm,),
        in_specs=[
            pl.BlockSpec((tm, D_ATT), row),
            pl.BlockSpec((tm, D_CONV), row),
            pl.BlockSpec((HALO, D_CONV), lambda i: (jnp.maximum(i * hb - 1, 0), 0)),
            pl.BlockSpec((HALO, D_CONV), lambda i: (jnp.minimum((i + 1) * hb, n_halo - 1), 0)),
            pl.BlockSpec((tm, d), row),
            pl.BlockSpec((tm, d), row),
            pl.BlockSpec((tm, d), row),
            pl.BlockSpec((1, 6, d), lambda i: (i // tpb, 0, 0)),
            pl.BlockSpec(cw.shape, const),
            pl.BlockSpec((1, D_CONV), const),
            pl.BlockSpec((1, D_CONV), const),
            pl.BlockSpec((1, D_CONV), const),
            pl.BlockSpec(wa.shape, const),
            pl.BlockSpec(wc.shape, const),
            pl.BlockSpec(wo.shape, const),
            pl.BlockSpec((1, d), const),
            pl.BlockSpec(wr.shape, const),
            pl.BlockSpec((1, LANES), const),
            pl.BlockSpec((tm, tm), const),
        ],
        out_specs=[
            pl.BlockSpec((tm, d), row),
            pl.BlockSpec((tm * ROW_TILE, LANES), row),
            pl.BlockSpec((SUBLANES, tm), lambda i: (0, i)),
            pl.BlockSpec((SUBLANES, LANES), const),
        ],
        out_shape=[
            jax.ShapeDtypeStruct((t, d), F32),
            jax.ShapeDtypeStruct((t * ROW_TILE, LANES), U32),
            jax.ShapeDtypeStruct((SUBLANES, t), F32),
            jax.ShapeDtypeStruct((SUBLANES, LANES), F32),
        ],
        scratch_shapes=[
            pltpu.VMEM((tm + 2 * HALO, D_CONV), F32),
            pltpu.VMEM((tm, D_CONV), F32),
            pltpu.VMEM((SUBLANES, LANES), F32),
        ],
        compiler_params=pltpu.CompilerParams(dimension_semantics=("arbitrary",), vmem_limit_bytes=VMEM_LIMIT),
        name="merge",
    )(att, u, u, u, ga, gb, x2, mod3, cw, cb, lg, lb, wa, wc, wo, nf, wr, br, tri)


def _dispatch_kernel(dest_ref, hp_ref, zeros_hbm, hs_hbm, sem):
    del zeros_hbm
    n = dest_ref.shape[0]

    def body(t, carry):
        src = pl.multiple_of(t * ROW_TILE, ROW_TILE)
        dst = pl.multiple_of(dest_ref[t] * ROW_TILE, ROW_TILE)
        pltpu.make_async_copy(hp_ref.at[pl.ds(src, ROW_TILE)], hs_hbm.at[pl.ds(dst, ROW_TILE)], sem).start()
        return carry

    lax.fori_loop(0, n, body, 0, unroll=8)
    pltpu.make_async_copy(hp_ref, hs_hbm.at[pl.ds(0, n * ROW_TILE)], sem).wait()


def _dispatch_call(dest, hp, p_rows):
    t = dest.shape[0]
    zeros = jnp.zeros((p_rows * ROW_TILE, LANES), U32)
    return pl.pallas_call(
        _dispatch_kernel,
        grid=(t // TM_ROWS,),
        in_specs=[
            pl.BlockSpec((TM_ROWS,), lambda i: (i,), memory_space=pltpu.SMEM),
            pl.BlockSpec((TM_ROWS * ROW_TILE, LANES), lambda i: (i, 0)),
            pl.BlockSpec(memory_space=pl.ANY),
        ],
        out_specs=pl.BlockSpec(memory_space=pl.ANY),
        out_shape=jax.ShapeDtypeStruct((p_rows * ROW_TILE, LANES), U32),
        scratch_shapes=[pltpu.SemaphoreType.DMA(())],
        input_output_aliases={2: 0},
        compiler_params=pltpu.CompilerParams(dimension_semantics=("arbitrary",), has_side_effects=True,
                                             vmem_limit_bytes=VMEM_LIMIT),
        name="dispatch",
    )(dest, hp, zeros)


def _moe_kernel(ea_ref, eb_ref, valid_ref, hs_ref, wga_ref, wua_ref, wda_ref, wgb_ref, wub_ref, wdb_ref, y_ref):
    i = pl.program_id(0)
    blk = hs_ref.shape[0] // ROW_TILE
    d = wga_ref.shape[1]
    n_words = d // 2 // LANES

    @pl.when(valid_ref[i] == 1)
    def _():
        his, los = [], []
        for c in range(n_words):
            w = hs_ref[pl.ds(c, blk, stride=ROW_TILE), :]
            his.append(lax.bitcast_convert_type(w & jnp.uint32(0xFFFF0000), F32).astype(BF16))
            los.append(lax.bitcast_convert_type(w << 16, F32).astype(BF16))
        x = jnp.concatenate(his + los, axis=-1)
        gates = lax.bitcast_convert_type(hs_ref[pl.ds(n_words, blk, stride=ROW_TILE), :], F32)
        gate_a = gates[:, 0:1]
        gate_b = gates[:, 1:2]

        def mlp(wg, wu, wd):
            g = jnp.dot(x, wg[0], preferred_element_type=F32)
            u = jnp.dot(x, wu[0], preferred_element_type=F32)
            a = (g * _sigmoid(g) * u).astype(BF16)
            return jnp.dot(a, wd[0], preferred_element_type=F32)

        y = gate_a * mlp(wga_ref, wua_ref, wda_ref) + gate_b * mlp(wgb_ref, wub_ref, wdb_ref)
        for c in range(d // LANES):
            y_ref[pl.ds(c, blk, stride=ROW_TILE), :] = y[:, c * LANES:(c + 1) * LANES]

    @pl.when(valid_ref[i] != 1)
    def _():
        y_ref[...] = jnp.zeros_like(y_ref)


def _moe_call(blk_ea, blk_eb, blk_valid, hs, wg, wu, wd):
    nb = blk_ea.shape[0]
    d = wg.shape[1]
    de = wg.shape[2]
    rows = MOE_BLOCK * ROW_TILE
    tok = pl.BlockSpec((rows, LANES), lambda i, ea, eb, va: (i, 0))
    w_a = lambda shape: pl.BlockSpec((1,) + shape, lambda i, ea, eb, va: (ea[i], 0, 0))
    w_b = lambda shape: pl.BlockSpec((1,) + shape, lambda i, ea, eb, va: (eb[i], 0, 0))
    grid_spec = pltpu.PrefetchScalarGridSpec(
        num_scalar_prefetch=3,
        grid=(nb,),
        in_specs=[tok, w_a((d, de)), w_a((d, de)), w_a((de, d)), w_b((d, de)), w_b((d, de)), w_b((de, d))],
        out_specs=tok,
    )
    return pl.pallas_call(
        _moe_kernel,
        grid_spec=grid_spec,
        out_shape=jax.ShapeDtypeStruct((nb * rows, LANES), F32),
        compiler_params=pltpu.CompilerParams(dimension_semantics=("arbitrary",), vmem_limit_bytes=VMEM_LIMIT),
        name="moe",
    )(blk_ea, blk_eb, blk_valid, hs, wg, wu, wd, wg, wu, wd)


def _final_kernel(dest_ref, xmid_ref, mod_ref, fn_ref, y_hbm, o_ref, fbuf, sem):
    n = dest_ref.shape[0]
    d = xmid_ref.shape[1]

    def body(t, carry):
        src = pl.multiple_of(dest_ref[t] * ROW_TILE, ROW_TILE)
        dst = pl.multiple_of(t * ROW_TILE, ROW_TILE)
        pltpu.make_async_copy(y_hbm.at[pl.ds(src, ROW_TILE)], fbuf.at[pl.ds(dst, ROW_TILE)], sem).start()
        return carry

    lax.fori_loop(0, n, body, 0, unroll=8)
    pltpu.make_async_copy(y_hbm.at[pl.ds(0, n * ROW_TILE)], fbuf, sem).wait()
    f = jnp.concatenate([fbuf[pl.ds(c, n, stride=ROW_TILE), :] for c in range(d // LANES)], axis=-1)
    x = xmid_ref[...] + mod_ref[0][5:6] * f
    ms = jnp.mean(x * x, axis=-1, keepdims=True)
    o_ref[...] = x * lax.rsqrt(ms + NORM_EPS) * fn_ref[...]


def _final_call(dest, xmid, mod3, fn, y, seq):
    t, d = xmid.shape
    tm = TM_ROWS
    tpb = seq // tm
    return pl.pallas_call(
        _final_kernel,
        grid=(t // tm,),
        in_specs=[
            pl.BlockSpec((tm,), lambda i: (i,), memory_space=pltpu.SMEM),
            pl.BlockSpec((tm, d), lambda i: (i, 0)),
            pl.BlockSpec((1, 6, d), lambda i: (i // tpb, 0, 0)),
            pl.BlockSpec((1, d), lambda i: (0, 0)),
            pl.BlockSpec(memory_space=pl.ANY),
        ],
        out_specs=pl.BlockSpec((tm, d), lambda i: (i, 0)),
        out_shape=jax.ShapeDtypeStruct((t, d), F32),
        scratch_shapes=[pltpu.VMEM((tm * ROW_TILE, LANES), F32), pltpu.SemaphoreType.DMA(())],
        compiler_params=pltpu.CompilerParams(dimension_semantics=("arbitrary",), vmem_limit_bytes=VMEM_LIMIT),
        name="final",
    )(dest, xmid, mod3, fn, y)


def _pair_tables():
    ea, eb = [], []
    for g in range(N_GROUPS):
        for a in range(EXPERTS_PER_GROUP):
            for b in range(a + 1, EXPERTS_PER_GROUP):
                ea.append(g * EXPERTS_PER_GROUP + a)
                eb.append(g * EXPERTS_PER_GROUP + b)
    return np.asarray(ea, np.int32), np.asarray(eb, np.int32)


def _routing_plan(info, counts, t):
    cls = info[0].astype(jnp.int32)
    rank = info[1].astype(jnp.int32)
    cnt = counts[0, :N_CLASSES].astype(jnp.int32)
    padded = (cnt + MOE_BLOCK - 1) // MOE_BLOCK * MOE_BLOCK
    pad_end = jnp.cumsum(padded)
    pad_start = pad_end - padded
    cls_ids = jnp.arange(N_CLASSES, dtype=jnp.int32)
    dest = rank + jnp.sum(jnp.where(cls[:, None] == cls_ids[None, :], pad_start[None, :], 0), axis=1)
    nb = t // MOE_BLOCK + N_CLASSES
    nb_used = pad_end[-1] // MOE_BLOCK
    blk = jnp.arange(nb, dtype=jnp.int32)
    blk_cls = jnp.clip(jnp.searchsorted(pad_end, blk * MOE_BLOCK, side="right"), 0, N_CLASSES - 1)
    valid = blk < nb_used
    last_cls = blk_cls[jnp.maximum(nb_used - 1, 0)]
    blk_cls = jnp.where(valid, blk_cls, last_cls).astype(jnp.int32)
    tab_a, tab_b = _pair_tables()
    blk_ea = jnp.asarray(tab_a)[blk_cls]
    blk_eb = jnp.asarray(tab_b)[blk_cls]
    return dest, blk_ea, blk_eb, valid.astype(jnp.int32), nb


def kernel(x, c, ctx, c_ctx, w_mod, b_mod, norm_mix, w_in, rpb, w_att_out, conv_w, conv_b, conv_ln_g, conv_ln_b,
           w_conv_out, w_o, norm_ffn, w_router_group, b_router_group, w_router_expert, b_router_expert,
           w_exp_gate, w_exp_up, w_exp_down, final_norm):
    b, seq, d = x.shape
    n_ctx = ctx.shape[1]
    t = b * seq
    assert w_mod.shape[0] == 1, "single layer"
    assert seq % TM_ROWS == 0 and seq // GRID_W >= WIN_ROWS and n_ctx % 8 == 0 and (b * n_ctx) % 512 == 0
    assert d == 1024

    mod_rows = -(-(b + 1) // SUBLANES) * SUBLANES
    cc = jnp.zeros((mod_rows, d), F32).at[:b].set(c).at[b].set(c_ctx)
    m_all = _mod_call(cc, w_mod[0], b_mod[0][None, :])
    mod_lat = m_all[:b].reshape(b, 6, d)
    mod_ctx = m_all[b:b + 1].reshape(1, 6, d)

    x2 = x.reshape(t, d)
    g_mix = norm_mix[0][None, :]
    w_in_b = w_in[0].astype(BF16)
    q, k, v, u, ga, gb = _proj_call(x2, mod_lat, g_mix, w_in_b, seq)
    kc, vc = _ctx_kv_call(ctx.reshape(b * n_ctx, d), mod_ctx, g_mix, w_in_b[:, K0:GLU0])

    att = _attn_call(q, k, v, kc, vc, _bias_table(rpb[0]), seq, n_ctx)

    cw = jnp.zeros((32, D_CONV), F32).at[:CONV_WIDTH].set(conv_w[0])
    wr = jnp.zeros((d, LANES), F32).at[:, :N_GROUPS].set(w_router_group[0])
    wr = wr.at[:, N_GROUPS:N_GROUPS + N_EXPERTS].set(w_router_expert[0])
    br = jnp.zeros((1, LANES), F32).at[0, :N_GROUPS].set(b_router_group[0])
    br = br.at[0, N_GROUPS:N_GROUPS + N_EXPERTS].set(b_router_expert[0])
    x_mid, hp, info, counts = _merge_call(
        att, u, ga, gb, x2, mod_lat, cw, conv_b[0][None, :], conv_ln_g[0][None, :], conv_ln_b[0][None, :],
        w_att_out[0].astype(BF16), w_conv_out[0].astype(BF16), w_o[0].astype(BF16), norm_ffn[0][None, :],
        wr, br, seq)

    dest, blk_ea, blk_eb, blk_valid, nb = _routing_plan(info, counts, t)
    hs = _dispatch_call(dest, hp, nb * MOE_BLOCK)
    y = _moe_call(blk_ea, blk_eb, blk_valid, hs,
                  w_exp_gate[0].astype(BF16), w_exp_up[0].astype(BF16), w_exp_down[0].astype(BF16))
    out = _final_call(dest, x_mid, mod_lat, final_norm[None, :], y, seq)
    return out.reshape(b, seq, d)
```

```python
import functools

import numpy as np
import jax
import jax.numpy as jnp
from jax import lax
from jax.experimental import pallas as pl
from jax.experimental.pallas import tpu as pltpu

F32 = jnp.float32
BF16 = jnp.bfloat16
U32 = jnp.uint32

GRID_W = 64
N_HEADS = 8
HEAD_DIM = 64
D_ATT = N_HEADS * HEAD_DIM
WIN_ROWS = 8
WIN_COLS = 16
D_CONV = 512
CONV_WIDTH = 31
N_GROUPS = 4
EXPERTS_PER_GROUP = 8
N_EXPERTS = N_GROUPS * EXPERTS_PER_GROUP
PAIRS_PER_GROUP = EXPERTS_PER_GROUP * (EXPERTS_PER_GROUP - 1) // 2
N_CLASSES = N_GROUPS * PAIRS_PER_GROUP
NORM_EPS = 1e-6
NEG_INF = -1e30

LANES = 128
SUBLANES = 8
ROW_TILE = SUBLANES
HALO = 16
VMEM_LIMIT = 56 * 1024 * 1024

TM_PROJ = 512
TM_MERGE = 512
TM_ROWS = 1024
MOE_BLOCK = 128
CONV_CHUNK = 64

HIGHEST = lax.Precision.HIGHEST


def _norm_mod(x, g, shift, scale):
    ms = jnp.mean(x * x, axis=-1, keepdims=True)
    y = x * lax.rsqrt(ms + NORM_EPS) * g
    return y * (1.0 + scale) + shift


def _sigmoid(x):
    return jax.nn.sigmoid(x)


def _mod_kernel(c_ref, w_ref, b_ref, o_ref):
    c = c_ref[...]
    s = c * _sigmoid(c)
    o_ref[...] = jnp.dot(s, w_ref[...], precision=HIGHEST, preferred_element_type=F32) + b_ref[...]


def _mod_call(cc, w_mod, b_mod):
    rows, d = cc.shape
    n = w_mod.shape[1]
    tn = 1024
    return pl.pallas_call(
        _mod_kernel,
        grid=(n // tn,),
        in_specs=[
            pl.BlockSpec((rows, d), lambda j: (0, 0)),
            pl.BlockSpec((d, tn), lambda j: (0, j)),
            pl.BlockSpec((1, tn), lambda j: (0, j)),
        ],
        out_specs=pl.BlockSpec((rows, tn), lambda j: (0, j)),
        out_shape=jax.ShapeDtypeStruct((rows, n), F32),
        compiler_params=pltpu.CompilerParams(dimension_semantics=("arbitrary",), vmem_limit_bytes=VMEM_LIMIT),
        name="mod",
    )(cc, w_mod, b_mod)


Q0, K0, V0, GLU0 = 0, D_ATT, 2 * D_ATT, 3 * D_ATT
GA0 = GLU0 + 2 * D_CONV


NT_DIMS = (((1,), (1,)), ((), ()))


def _store_transposed(res_t, ref, scale=None):
    for j in range(ref.shape[0]):
        blk = res_t[:, j * LANES:(j + 1) * LANES]
        if scale is not None:
            blk = blk * scale
        ref[j] = blk.astype(ref.dtype)


def _proj_kernel(x_ref, mod_ref, g_ref, wqv_ref, w_ref, qt_ref, k_ref, vt_ref, u_ref, ga_ref, gb_ref):
    d = x_ref.shape[1]
    mod = mod_ref[0]
    h = _norm_mod(x_ref[...], g_ref[...], mod[0:1], mod[1:2]).astype(BF16)

    qv_t = lax.dot_general(wqv_ref[...], h, NT_DIMS, preferred_element_type=F32)
    _store_transposed(qv_t[:D_ATT], qt_ref, HEAD_DIM ** -0.5)
    _store_transposed(qv_t[D_ATT:], vt_ref)

    def seg(lo, hi):
        return jnp.dot(h, w_ref[:, lo:hi], preferred_element_type=F32)

    k_ref[...] = seg(0, D_ATT).astype(BF16)
    a = seg(D_ATT, D_ATT + D_CONV)
    g = seg(D_ATT + D_CONV, D_ATT + 2 * D_CONV)
    u_ref[...] = (a * _sigmoid(g)).astype(BF16)
    g0 = D_ATT + 2 * D_CONV
    ga_ref[...] = _sigmoid(seg(g0, g0 + d)).astype(BF16)
    gb_ref[...] = _sigmoid(seg(g0 + d, g0 + 2 * d)).astype(BF16)


def _proj_call(x2, mod3, g, wqv_t, w_rest, seq):
    t, d = x2.shape
    tm = TM_PROJ
    tpb = seq // tm
    nblk = tm // LANES
    outs = [
        jax.ShapeDtypeStruct((t // LANES, D_ATT, LANES), BF16),
        jax.ShapeDtypeStruct((t, D_ATT), BF16),
        jax.ShapeDtypeStruct((t // LANES, D_ATT, LANES), BF16),
        jax.ShapeDtypeStruct((t, D_CONV), BF16),
        jax.ShapeDtypeStruct((t, d), BF16),
        jax.ShapeDtypeStruct((t, d), BF16),
    ]
    row = lambda i: (i, 0)
    blk3 = lambda i: (i, 0, 0)
    return pl.pallas_call(
        _proj_kernel,
        grid=(t // tm,),
        in_specs=[
            pl.BlockSpec((tm, d), row),
            pl.BlockSpec((1, 6, d), lambda i: (i // tpb, 0, 0)),
            pl.BlockSpec((1, d), lambda i: (0, 0)),
            pl.BlockSpec(wqv_t.shape, lambda i: (0, 0)),
            pl.BlockSpec(w_rest.shape, lambda i: (0, 0)),
        ],
        out_specs=[
            pl.BlockSpec((nblk, D_ATT, LANES), blk3),
            pl.BlockSpec((tm, D_ATT), row),
            pl.BlockSpec((nblk, D_ATT, LANES), blk3),
            pl.BlockSpec((tm, D_CONV), row),
            pl.BlockSpec((tm, d), row),
            pl.BlockSpec((tm, d), row),
        ],
        out_shape=outs,
        compiler_params=pltpu.CompilerParams(dimension_semantics=("arbitrary",), vmem_limit_bytes=VMEM_LIMIT),
        name="proj",
    )(x2, mod3, g, wqv_t, w_rest)


def _ctx_kv_kernel(x_ref, mod_ref, g_ref, wk_ref, wvt_ref, k_ref, vt_ref):
    mod = mod_ref[0]
    h = _norm_mod(x_ref[...], g_ref[...], mod[0:1], mod[1:2]).astype(BF16)
    k_ref[...] = jnp.dot(h, wk_ref[...], preferred_element_type=F32).astype(BF16)
    _store_transposed(lax.dot_general(wvt_ref[...], h, NT_DIMS, preferred_element_type=F32), vt_ref)


def _ctx_kv_call(c2, mod3, g, w_k, w_vt):
    t, d = c2.shape
    tm = 512
    nblk = tm // LANES
    row = lambda i: (i, 0)
    return pl.pallas_call(
        _ctx_kv_kernel,
        grid=(t // tm,),
        in_specs=[
            pl.BlockSpec((tm, d), row),
            pl.BlockSpec((1, 6, d), lambda i: (0, 0, 0)),
            pl.BlockSpec((1, d), lambda i: (0, 0)),
            pl.BlockSpec(w_k.shape, lambda i: (0, 0)),
            pl.BlockSpec(w_vt.shape, lambda i: (0, 0)),
        ],
        out_specs=[pl.BlockSpec((tm, D_ATT), row), pl.BlockSpec((nblk, D_ATT, LANES), lambda i: (i, 0, 0))],
        out_shape=[jax.ShapeDtypeStruct((t, D_ATT), BF16), jax.ShapeDtypeStruct((t // LANES, D_ATT, LANES), BF16)],
        compiler_params=pltpu.CompilerParams(dimension_semantics=("arbitrary",), vmem_limit_bytes=VMEM_LIMIT),
        name="ctx_kv",
    )(c2, mod3, g, w_k, w_vt)


HEADS_PER_GROUP = 4
GROUP_W = HEADS_PER_GROUP * HEAD_DIM
N_HEAD_GROUPS = N_HEADS // HEADS_PER_GROUP


def _attn_kernel(qt_ref, k_ref, vt_ref, kc_ref, vct_ref, bias_ref, o_ref, vboth):
    nblk = qt_ref.shape[0]
    rows = 2 * nblk
    n_loc = WIN_ROWS * GRID_W
    n_ctx_blk = vct_ref.shape[0]
    half = GRID_W

    for j in range(nblk):
        vboth[0, j] = vt_ref[j]
    for j in range(nblk - 1):
        vboth[1, j] = jnp.concatenate([vt_ref[j][:, half:], vt_ref[j + 1][:, :half]], axis=1)
    vboth[1, nblk - 1] = jnp.zeros_like(vt_ref[0])

    lane = lax.broadcasted_iota(jnp.int32, (D_ATT, LANES), 1)
    low = lane < half
    rblk = lax.broadcasted_iota(jnp.int32, (GROUP_W, GROUP_W), 0) // HEAD_DIM
    cblk = lax.broadcasted_iota(jnp.int32, (GROUP_W, GROUP_W), 1) // HEAD_DIM
    diag = rblk == cblk
    low64 = lax.broadcasted_iota(jnp.int32, (HEAD_DIM, LANES), 1) < half
    kc = kc_ref[...]

    def one_row(row, tiled, side):
        r_start = jnp.clip(row - WIN_ROWS // 2, 0, rows - WIN_ROWS)
        pat = row - r_start
        kw = k_ref[pl.ds(pl.multiple_of(r_start * GRID_W, GRID_W), n_loc), :]
        vwin = vboth[r_start % 2, pl.ds(r_start // 2, n_loc // LANES)]
        parts = []
        for g in range(N_HEAD_GROUPS):
            fs = slice(g * GROUP_W, (g + 1) * GROUP_W)
            tg = tiled[fs, :]
            w = jnp.where(diag, jnp.concatenate([tg, tg], axis=1), jnp.zeros((), BF16))
            s_loc = jnp.dot(kw[:, fs], w, preferred_element_type=F32) + bias_ref[pat, g]
            s_ctx = jnp.dot(kc[:, fs], w, preferred_element_type=F32)
            m = jnp.maximum(jnp.max(s_loc, axis=0, keepdims=True), jnp.max(s_ctx, axis=0, keepdims=True))
            p_loc = jnp.exp(s_loc - m)
            p_ctx = jnp.exp(s_ctx - m)
            l = jnp.sum(p_loc, axis=0, keepdims=True) + jnp.sum(p_ctx, axis=0, keepdims=True)
            p = jnp.concatenate([p_loc.astype(BF16), p_ctx.astype(BF16)], axis=0)
            vt_g = jnp.concatenate([vwin[j][fs, :] for j in range(n_loc // LANES)]
                                   + [vct_ref[j][fs, :] for j in range(n_ctx_blk)], axis=1)
            o_t = jnp.dot(vt_g, p, preferred_element_type=F32) * (1.0 / l)
            for j in range(HEADS_PER_GROUP):
                blk = o_t[j * HEAD_DIM:(j + 1) * HEAD_DIM, (j // 2) * LANES:(j // 2 + 1) * LANES]
                if j % 2 != side:
                    blk = pltpu.roll(blk, half, axis=1)
                parts.append(blk)
        return parts

    def body(i, carry):
        xq = qt_ref[i]
        rolled = jnp.concatenate([xq[:, half:], xq[:, :half]], axis=1)
        parts_a = one_row(2 * i, jnp.where(low, xq, rolled), 0)
        parts_b = one_row(2 * i + 1, jnp.where(low, rolled, xq), 1)
        out = jnp.concatenate([jnp.where(low64, a, b) for a, b in zip(parts_a, parts_b)], axis=0)
        o_ref[i] = out.astype(o_ref.dtype)
        return carry

    lax.fori_loop(0, nblk, body, 0)


def _attn_call(qt, k, vt, kc, vct, bias, seq, n_ctx):
    t = k.shape[0]
    b = t // seq
    nblk = seq // LANES
    lat3 = pl.BlockSpec((nblk, D_ATT, LANES), lambda i: (i, 0, 0))
    return pl.pallas_call(
        _attn_kernel,
        grid=(b,),
        in_specs=[lat3,
                  pl.BlockSpec((seq, D_ATT), lambda i: (i, 0)),
                  lat3,
                  pl.BlockSpec((n_ctx, D_ATT), lambda i: (i, 0)),
                  pl.BlockSpec((n_ctx // LANES, D_ATT, LANES), lambda i: (i, 0, 0)),
                  pl.BlockSpec(bias.shape, lambda i: (0, 0, 0, 0))],
        out_specs=lat3,
        out_shape=jax.ShapeDtypeStruct((t // LANES, D_ATT, LANES), BF16),
        scratch_shapes=[pltpu.VMEM((2, nblk, D_ATT, LANES), BF16)],
        compiler_params=pltpu.CompilerParams(dimension_semantics=("arbitrary",), vmem_limit_bytes=VMEM_LIMIT),
        name="attn",
    )(qt, k, vt, kc, vct, bias)


def _bias_table(rpb):
    cq = np.arange(GRID_W)[:, None]
    ck = np.arange(GRID_W)[None, :]
    c_start = np.clip(cq - WIN_COLS // 2, 0, GRID_W - WIN_COLS)
    col_mask = (ck >= c_start) & (ck < c_start + WIN_COLS)
    dc_idx = np.clip(ck - cq + WIN_COLS - 1, 0, 2 * WIN_COLS - 2)
    pat = np.arange(WIN_ROWS)[:, None]
    kr = np.arange(WIN_ROWS)[None, :]
    dr_idx = kr - pat + WIN_ROWS - 1
    tab = rpb[:, dr_idx][:, :, :, dc_idx]
    tab = jnp.where(col_mask[None, None, None], tab, NEG_INF)
    tab = tab.reshape(N_HEAD_GROUPS, HEADS_PER_GROUP, WIN_ROWS, WIN_ROWS, GRID_W, GRID_W)
    tab = tab.transpose(2, 0, 3, 5, 1, 4)
    return tab.reshape(WIN_ROWS, N_HEAD_GROUPS, WIN_ROWS * GRID_W, GROUP_W).astype(F32)


def _merge_kernel(att_ref, u_ref, up_ref, un_ref, ga_ref, gb_ref, x_ref, mod_ref,
                  cw_ref, cb_ref, lg_ref, lb_ref, wa_ref, wc_ref, wo_ref, nf_ref, wr_ref, br_ref, tri_ref,
                  xmid_ref, hp_ref, info_ref, cnt_ref,
                  ubuf, ybuf, run_ref, *, tiles_per_seq):
    i = pl.program_id(0)
    tm = x_ref.shape[0]
    d = x_ref.shape[1]
    mod = mod_ref[0]
    first = (i % tiles_per_seq) == 0
    last = (i % tiles_per_seq) == tiles_per_seq - 1

    ubuf[0:HALO, :] = jnp.where(first, 0.0, up_ref[...].astype(F32))
    ubuf[HALO:HALO + tm, :] = u_ref[...].astype(F32)
    ubuf[HALO + tm:, :] = jnp.where(last, 0.0, un_ref[...].astype(F32))
    base = HALO - CONV_WIDTH // 2
    for tc in range(tm // CONV_CHUNK):
        for lc in range(D_CONV // LANES):
            ls = slice(lc * LANES, (lc + 1) * LANES)
            acc = jnp.zeros((CONV_CHUNK, LANES), F32)
            for j in range(CONV_WIDTH):
                lo = tc * CONV_CHUNK + base + j
                acc = acc + ubuf[lo:lo + CONV_CHUNK, ls] * cw_ref[j:j + 1, ls]
            ybuf[tc * CONV_CHUNK:(tc + 1) * CONV_CHUNK, ls] = acc
    yc = ybuf[...] + cb_ref[...]
    mu = jnp.mean(yc, axis=-1, keepdims=True)
    var = jnp.mean(jnp.square(yc - mu), axis=-1, keepdims=True)
    yn = (yc - mu) * lax.rsqrt(var + NORM_EPS) * lg_ref[...] + lb_ref[...]
    yn = yn * _sigmoid(yn)
    y_conv = jnp.dot(yn.astype(BF16), wc_ref[...], preferred_element_type=F32)

    att = jnp.concatenate([att_ref[j].astype(F32).T for j in range(tm // LANES)], axis=0)
    y_att = jnp.dot(att.astype(BF16), wa_ref[...], preferred_element_type=F32)
    mix = ga_ref[...].astype(F32) * y_att + gb_ref[...].astype(F32) * y_conv
    y = jnp.dot(mix.astype(BF16), wo_ref[...], preferred_element_type=F32)
    x_mid = x_ref[...] + mod[2:3] * y
    xmid_ref[...] = x_mid
    h2 = _norm_mod(x_mid, nf_ref[...], mod[3:4], mod[4:5])

    logits = jnp.dot(h2, wr_ref[...], precision=HIGHEST, preferred_element_type=F32) + br_ref[...]
    lane = lax.broadcasted_iota(jnp.int32, logits.shape, 1).astype(F32)
    big = float(LANES)
    is_g = lane < N_GROUPS
    gl = jnp.where(is_g, logits, -jnp.inf)
    gmax = jnp.max(gl, axis=-1, keepdims=True)
    g_idx = jnp.min(jnp.where(gl == gmax, lane, big), axis=-1, keepdims=True)
    p_group = 1.0 / jnp.sum(jnp.where(is_g, jnp.exp(gl - gmax), 0.0), axis=-1, keepdims=True)
    e_lo = N_GROUPS + EXPERTS_PER_GROUP * g_idx
    in_grp = (lane >= e_lo) & (lane < e_lo + EXPERTS_PER_GROUP)
    el = jnp.where(in_grp, logits, -jnp.inf)
    v1 = jnp.max(el, axis=-1, keepdims=True)
    i1 = jnp.min(jnp.where(el == v1, lane, big), axis=-1, keepdims=True)
    el2 = jnp.where(lane == i1, -jnp.inf, el)
    v2 = jnp.max(el2, axis=-1, keepdims=True)
    i2 = jnp.min(jnp.where(el2 == v2, lane, big), axis=-1, keepdims=True)
    e2 = jnp.exp(v2 - v1)
    gate1 = p_group / (1.0 + e2)
    gate2 = p_group * e2 / (1.0 + e2)
    j1 = i1 - e_lo
    j2 = i2 - e_lo
    ja = jnp.minimum(j1, j2)
    jb = jnp.maximum(j1, j2)
    gate_a = jnp.where(j1 < j2, gate1, gate2)
    gate_b = jnp.where(j1 < j2, gate2, gate1)
    pair = ja * (2 * EXPERTS_PER_GROUP - 1 - ja) * 0.5 + (jb - ja - 1.0)
    cls = g_idx * PAIRS_PER_GROUP + pair

    @pl.when(i == 0)
    def _():
        run_ref[...] = jnp.zeros_like(run_ref)

    onehot = lane == cls
    oh_f = jnp.where(onehot, 1.0, 0.0)
    before = jnp.dot(tri_ref[...], oh_f.astype(BF16), preferred_element_type=F32)
    run = run_ref[0:1, :]
    rank = jnp.sum(oh_f * (before + run), axis=-1, keepdims=True)
    new_run = run + jnp.sum(oh_f, axis=0, keepdims=True)
    run_ref[...] = jnp.broadcast_to(new_run, run_ref.shape)
    cnt_ref[...] = jnp.broadcast_to(new_run, cnt_ref.shape)

    info = jnp.where(lane == 0.0, cls, jnp.where(lane == 1.0, rank, 0.0))
    info_ref[...] = info.T[0:SUBLANES, :]

    half = d // 2
    hi = lax.bitcast_convert_type(h2[:, :half].astype(BF16).astype(F32), U32)
    lo = lax.bitcast_convert_type(h2[:, half:].astype(BF16).astype(F32), U32)
    words = hi | (lo >> 16)
    n_words = half // LANES
    for c in range(n_words):
        hp_ref[pl.ds(c, tm, stride=ROW_TILE), :] = words[:, c * LANES:(c + 1) * LANES]
    gates = jnp.where(lane == 0.0, gate_a, jnp.where(lane == 1.0, gate_b, 0.0))
    hp_ref[pl.ds(n_words, tm, stride=ROW_TILE), :] = lax.bitcast_convert_type(gates, U32)
    zero = jnp.zeros((tm, LANES), U32)
    for c in range(n_words + 1, ROW_TILE):
        hp_ref[pl.ds(c, tm, stride=ROW_TILE), :] = zero


def _merge_call(att, u, ga, gb, x2, mod3, cw, cb, lg, lb, wa, wc, wo, nf, wr, br, seq):
    t, d = x2.shape
    tm = TM_MERGE
    tpb = seq // tm
    hb = tm // HALO
    n_halo = t // HALO
    tri = jnp.asarray(np.tril(np.ones((tm, tm), np.float32), -1), BF16)
    row = lambda i: (i, 0)
    const = lambda i: (0, 0)
    kern = functools.partial(_merge_kernel, tiles_per_seq=tpb)
    return pl.pallas_call(
        kern,
        grid=(t // tm,),
        in_specs=[
            pl.BlockSpec((tm // LANES, D_ATT, LANES), lambda i: (i, 0, 0)),
            pl.BlockSpec((tm, D_CONV), row),
            pl.BlockSpec((HALO, D_CONV), lambda i: (jnp.maximum(i * hb - 1, 0), 0)),
            pl.BlockSpec((HALO, D_CONV), lambda i: (jnp.minimum((i + 1) * hb, n_halo - 1), 0)),
            pl.BlockSpec((tm, d), row),
            pl.BlockSpec((tm, d), row),
            pl.BlockSpec((tm, d), row),
            pl.BlockSpec((1, 6, d), lambda i: (i // tpb, 0, 0)),
            pl.BlockSpec(cw.shape, const),
            pl.BlockSpec((1, D_CONV), const),
            pl.BlockSpec((1, D_CONV), const),
            pl.BlockSpec((1, D_CONV), const),
            pl.BlockSpec(wa.shape, const),
            pl.BlockSpec(wc.shape, const),
            pl.BlockSpec(wo.shape, const),
            pl.BlockSpec((1, d), const),
            pl.BlockSpec(wr.shape, const),
            pl.BlockSpec((1, LANES), const),
            pl.BlockSpec((tm, tm), const),
        ],
        out_specs=[
            pl.BlockSpec((tm, d), row),
            pl.BlockSpec((tm * ROW_TILE, LANES), row),
            pl.BlockSpec((SUBLANES, tm), lambda i: (0, i)),
            pl.BlockSpec((SUBLANES, LANES), const),
        ],
        out_shape=[
            jax.ShapeDtypeStruct((t, d), F32),
            jax.ShapeDtypeStruct((t * ROW_TILE, LANES), U32),
            jax.ShapeDtypeStruct((SUBLANES, t), F32),
            jax.ShapeDtypeStruct((SUBLANES, LANES), F32),
        ],
        scratch_shapes=[
            pltpu.VMEM((tm + 2 * HALO, D_CONV), F32),
            pltpu.VMEM((tm, D_CONV), F32),
            pltpu.VMEM((SUBLANES, LANES), F32),
        ],
        compiler_params=pltpu.CompilerParams(dimension_semantics=("arbitrary",), vmem_limit_bytes=VMEM_LIMIT),
        name="merge",
    )(att, u, u, u, ga, gb, x2, mod3, cw, cb, lg, lb, wa, wc, wo, nf, wr, br, tri)


def _dispatch_kernel(dest_ref, hp_ref, zeros_hbm, hs_hbm, sem):
    del zeros_hbm
    n = dest_ref.shape[0]

    def body(t, carry):
        src = pl.multiple_of(t * ROW_TILE, ROW_TILE)
        dst = pl.multiple_of(dest_ref[t] * ROW_TILE, ROW_TILE)
        pltpu.make_async_copy(hp_ref.at[pl.ds(src, ROW_TILE)], hs_hbm.at[pl.ds(dst, ROW_TILE)], sem).start()
        return carry

    lax.fori_loop(0, n, body, 0, unroll=8)
    pltpu.make_async_copy(hp_ref, hs_hbm.at[pl.ds(0, n * ROW_TILE)], sem).wait()


def _dispatch_call(dest, hp, p_rows):
    t = dest.shape[0]
    zeros = jnp.zeros((p_rows * ROW_TILE, LANES), U32)
    return pl.pallas_call(
        _dispatch_kernel,
        grid=(t // TM_ROWS,),
        in_specs=[
            pl.BlockSpec((TM_ROWS,), lambda i: (i,), memory_space=pltpu.SMEM),
            pl.BlockSpec((TM_ROWS * ROW_TILE, LANES), lambda i: (i, 0)),
            pl.BlockSpec(memory_space=pl.ANY),
        ],
        out_specs=pl.BlockSpec(memory_space=pl.ANY),
        out_shape=jax.ShapeDtypeStruct((p_rows * ROW_TILE, LANES), U32),
        scratch_shapes=[pltpu.SemaphoreType.DMA(())],
        input_output_aliases={2: 0},
        compiler_params=pltpu.CompilerParams(dimension_semantics=("arbitrary",), has_side_effects=True,
                                             vmem_limit_bytes=VMEM_LIMIT),
        name="dispatch",
    )(dest, hp, zeros)


def _moe_kernel(ea_ref, eb_ref, valid_ref, hs_ref, wga_ref, wua_ref, wda_ref, wgb_ref, wub_ref, wdb_ref, y_ref):
    i = pl.program_id(0)
    blk = hs_ref.shape[0] // ROW_TILE
    d = wga_ref.shape[1]
    n_words = d // 2 // LANES

    @pl.when(valid_ref[i] == 1)
    def _():
        his, los = [], []
        for c in range(n_words):
            w = hs_ref[pl.ds(c, blk, stride=ROW_TILE), :]
            his.append(lax.bitcast_convert_type(w & jnp.uint32(0xFFFF0000), F32).astype(BF16))
            los.append(lax.bitcast_convert_type(w << 16, F32).astype(BF16))
        x = jnp.concatenate(his + los, axis=-1)
        gates = lax.bitcast_convert_type(hs_ref[pl.ds(n_words, blk, stride=ROW_TILE), :], F32)
        gate_a = gates[:, 0:1]
        gate_b = gates[:, 1:2]

        def mlp(wg, wu, wd):
            g = jnp.dot(x, wg[0], preferred_element_type=F32)
            u = jnp.dot(x, wu[0], preferred_element_type=F32)
            a = (g * _sigmoid(g) * u).astype(BF16)
            return jnp.dot(a, wd[0], preferred_element_type=F32)

        y = gate_a * mlp(wga_ref, wua_ref, wda_ref) + gate_b * mlp(wgb_ref, wub_ref, wdb_ref)
        for c in range(d // LANES):
            y_ref[pl.ds(c, blk, stride=ROW_TILE), :] = y[:, c * LANES:(c + 1) * LANES]

    @pl.when(valid_ref[i] != 1)
    def _():
        y_ref[...] = jnp.zeros_like(y_ref)


def _moe_call(blk_ea, blk_eb, blk_valid, hs, wg, wu, wd):
    nb = blk_ea.shape[0]
    d = wg.shape[1]
    de = wg.shape[2]
    rows = MOE_BLOCK * ROW_TILE
    tok = pl.BlockSpec((rows, LANES), lambda i, ea, eb, va: (i, 0))
    w_a = lambda shape: pl.BlockSpec((1,) + shape, lambda i, ea, eb, va: (ea[i], 0, 0))
    w_b = lambda shape: pl.BlockSpec((1,) + shape, lambda i, ea, eb, va: (eb[i], 0, 0))
    grid_spec = pltpu.PrefetchScalarGridSpec(
        num_scalar_prefetch=3,
        grid=(nb,),
        in_specs=[tok, w_a((d, de)), w_a((d, de)), w_a((de, d)), w_b((d, de)), w_b((d, de)), w_b((de, d))],
        out_specs=tok,
    )
    return pl.pallas_call(
        _moe_kernel,
        grid_spec=grid_spec,
        out_shape=jax.ShapeDtypeStruct((nb * rows, LANES), F32),
        compiler_params=pltpu.CompilerParams(dimension_semantics=("arbitrary",), vmem_limit_bytes=VMEM_LIMIT),
        name="moe",
    )(blk_ea, blk_eb, blk_valid, hs, wg, wu, wd, wg, wu, wd)


def _final_kernel(dest_ref, xmid_ref, mod_ref, fn_ref, y_hbm, o_ref, fbuf, sem):
    n = dest_ref.shape[0]
    d = xmid_ref.shape[1]

    def body(t, carry):
        src = pl.multiple_of(dest_ref[t] * ROW_TILE, ROW_TILE)
        dst = pl.multiple_of(t * ROW_TILE, ROW_TILE)
        pltpu.make_async_copy(y_hbm.at[pl.ds(src, ROW_TILE)], fbuf.at[pl.ds(dst, ROW_TILE)], sem).start()
        return carry

    lax.fori_loop(0, n, body, 0, unroll=8)
    pltpu.make_async_copy(y_hbm.at[pl.ds(0, n * ROW_TILE)], fbuf, sem).wait()
    f = jnp.concatenate([fbuf[pl.ds(c, n, stride=ROW_TILE), :] for c in range(d // LANES)], axis=-1)
    x = xmid_ref[...] + mod_ref[0][5:6] * f
    ms = jnp.mean(x * x, axis=-1, keepdims=True)
    o_ref[...] = x * lax.rsqrt(ms + NORM_EPS) * fn_ref[...]


def _final_call(dest, xmid, mod3, fn, y, seq):
    t, d = xmid.shape
    tm = TM_ROWS
    tpb = seq // tm
    return pl.pallas_call(
        _final_kernel,
        grid=(t // tm,),
        in_specs=[
            pl.BlockSpec((tm,), lambda i: (i,), memory_space=pltpu.SMEM),
            pl.BlockSpec((tm, d), lambda i: (i, 0)),
            pl.BlockSpec((1, 6, d), lambda i: (i // tpb, 0, 0)),
            pl.BlockSpec((1, d), lambda i: (0, 0)),
            pl.BlockSpec(memory_space=pl.ANY),
        ],
        out_specs=pl.BlockSpec((tm, d), lambda i: (i, 0)),
        out_shape=jax.ShapeDtypeStruct((t, d), F32),
        scratch_shapes=[pltpu.VMEM((tm * ROW_TILE, LANES), F32), pltpu.SemaphoreType.DMA(())],
        compiler_params=pltpu.CompilerParams(dimension_semantics=("arbitrary",), vmem_limit_bytes=VMEM_LIMIT),
        name="final",
    )(dest, xmid, mod3, fn, y)


def _pair_tables():
    ea, eb = [], []
    for g in range(N_GROUPS):
        for a in range(EXPERTS_PER_GROUP):
            for b in range(a + 1, EXPERTS_PER_GROUP):
                ea.append(g * EXPERTS_PER_GROUP + a)
                eb.append(g * EXPERTS_PER_GROUP + b)
    return np.asarray(ea, np.int32), np.asarray(eb, np.int32)


def _routing_plan(info, counts, t):
    cls = info[0].astype(jnp.int32)
    rank = info[1].astype(jnp.int32)
    cnt = counts[0, :N_CLASSES].astype(jnp.int32)
    padded = (cnt + MOE_BLOCK - 1) // MOE_BLOCK * MOE_BLOCK
    pad_end = jnp.cumsum(padded)
    pad_start = pad_end - padded
    cls_ids = jnp.arange(N_CLASSES, dtype=jnp.int32)
    dest = rank + jnp.sum(jnp.where(cls[:, None] == cls_ids[None, :], pad_start[None, :], 0), axis=1)
    nb = t // MOE_BLOCK + N_CLASSES
    nb_used = pad_end[-1] // MOE_BLOCK
    blk = jnp.arange(nb, dtype=jnp.int32)
    blk_cls = jnp.clip(jnp.searchsorted(pad_end, blk * MOE_BLOCK, side="right"), 0, N_CLASSES - 1)
    valid = blk < nb_used
    last_cls = blk_cls[jnp.maximum(nb_used - 1, 0)]
    blk_cls = jnp.where(valid, blk_cls, last_cls).astype(jnp.int32)
    tab_a, tab_b = _pair_tables()
    blk_ea = jnp.asarray(tab_a)[blk_cls]
    blk_eb = jnp.asarray(tab_b)[blk_cls]
    return dest, blk_ea, blk_eb, valid.astype(jnp.int32), nb


def kernel(x, c, ctx, c_ctx, w_mod, b_mod, norm_mix, w_in, rpb, w_att_out, conv_w, conv_b, conv_ln_g, conv_ln_b,
           w_conv_out, w_o, norm_ffn, w_router_group, b_router_group, w_router_expert, b_router_expert,
           w_exp_gate, w_exp_up, w_exp_down, final_norm):
    b, seq, d = x.shape
    n_ctx = ctx.shape[1]
    t = b * seq
    assert w_mod.shape[0] == 1, "single layer"
    assert seq % TM_ROWS == 0 and seq // GRID_W >= WIN_ROWS and n_ctx % LANES == 0 and (b * n_ctx) % 512 == 0
    assert d == 1024

    mod_rows = -(-(b + 1) // SUBLANES) * SUBLANES
    cc = jnp.zeros((mod_rows, d), F32).at[:b].set(c).at[b].set(c_ctx)
    m_all = _mod_call(cc, w_mod[0], b_mod[0][None, :])
    mod_lat = m_all[:b].reshape(b, 6, d)
    mod_ctx = m_all[b:b + 1].reshape(1, 6, d)

    x2 = x.reshape(t, d)
    g_mix = norm_mix[0][None, :]
    w_in_b = w_in[0].astype(BF16)
    wqv_t = jnp.concatenate([w_in_b[:, Q0:K0], w_in_b[:, V0:GLU0]], axis=1).T
    w_rest = jnp.concatenate([w_in_b[:, K0:V0], w_in_b[:, GLU0:]], axis=1)
    qt, k, vt, u, ga, gb = _proj_call(x2, mod_lat, g_mix, wqv_t, w_rest, seq)
    kc, vct = _ctx_kv_call(ctx.reshape(b * n_ctx, d), mod_ctx, g_mix, w_in_b[:, K0:V0], w_in_b[:, V0:GLU0].T)

    att = _attn_call(qt, k, vt, kc, vct, _bias_table(rpb[0]), seq, n_ctx)

    cw = jnp.zeros((32, D_CONV), F32).at[:CONV_WIDTH].set(conv_w[0])
    wr = jnp.zeros((d, LANES), F32).at[:, :N_GROUPS].set(w_router_group[0])
    wr = wr.at[:, N_GROUPS:N_GROUPS + N_EXPERTS].set(w_router_expert[0])
    br = jnp.zeros((1, LANES), F32).at[0, :N_GROUPS].set(b_router_group[0])
    br = br.at[0, N_GROUPS:N_GROUPS + N_EXPERTS].set(b_router_expert[0])
    x_mid, hp, info, counts = _merge_call(
        att, u, ga, gb, x2, mod_lat, cw, conv_b[0][None, :], conv_ln_g[0][None, :], conv_ln_b[0][None, :],
        w_att_out[0].astype(BF16), w_conv_out[0].astype(BF16), w_o[0].astype(BF16), norm_ffn[0][None, :],
        wr, br, seq)

    dest, blk_ea, blk_eb, blk_valid, nb = _routing_plan(info, counts, t)
    hs = _dispatch_call(dest, hp, nb * MOE_BLOCK)
    y = _moe_call(blk_ea, blk_eb, blk_valid, hs,
                  w_exp_gate[0].astype(BF16), w_exp_up[0].astype(BF16), w_exp_down[0].astype(BF16))
    out = _final_call(dest, x_mid, mod_lat, final_norm[None, :], y, seq)
    return out.reshape(b, seq, d)
```

```python
import functools

import numpy as np
import jax
import jax.numpy as jnp
from jax import lax
from jax.experimental import pallas as pl
from jax.experimental.pallas import tpu as pltpu

F32 = jnp.float32
BF16 = jnp.bfloat16
U32 = jnp.uint32

GRID_W = 64
N_HEADS = 8
HEAD_DIM = 64
D_ATT = N_HEADS * HEAD_DIM
WIN_ROWS = 8
WIN_COLS = 16
D_CONV = 512
CONV_WIDTH = 31
N_GROUPS = 4
EXPERTS_PER_GROUP = 8
N_EXPERTS = N_GROUPS * EXPERTS_PER_GROUP
PAIRS_PER_GROUP = EXPERTS_PER_GROUP * (EXPERTS_PER_GROUP - 1) // 2
N_CLASSES = N_GROUPS * PAIRS_PER_GROUP
NORM_EPS = 1e-6
NEG_INF = -1e30

LANES = 128
SUBLANES = 8
ROW_TILE = SUBLANES
HALO = 16
VMEM_LIMIT = 56 * 1024 * 1024

TM_PROJ = 512
TM_MERGE = 512
TM_ROWS = 1024
MOE_BLOCK = 256
CONV_CHUNK = 64

HIGHEST = lax.Precision.HIGHEST


def _norm_mod(x, g, shift, scale):
    ms = jnp.mean(x * x, axis=-1, keepdims=True)
    y = x * lax.rsqrt(ms + NORM_EPS) * g
    return y * (1.0 + scale) + shift


def _sigmoid(x):
    return jax.nn.sigmoid(x)


def _mod_kernel(c_ref, w_ref, b_ref, o_ref):
    c = c_ref[...]
    s = c * _sigmoid(c)
    o_ref[...] = jnp.dot(s, w_ref[...], precision=HIGHEST, preferred_element_type=F32) + b_ref[...]


def _mod_call(cc, w_mod, b_mod):
    rows, d = cc.shape
    n = w_mod.shape[1]
    tn = 1024
    return pl.pallas_call(
        _mod_kernel,
        grid=(n // tn,),
        in_specs=[
            pl.BlockSpec((rows, d), lambda j: (0, 0)),
            pl.BlockSpec((d, tn), lambda j: (0, j)),
            pl.BlockSpec((1, tn), lambda j: (0, j)),
        ],
        out_specs=pl.BlockSpec((rows, tn), lambda j: (0, j)),
        out_shape=jax.ShapeDtypeStruct((rows, n), F32),
        compiler_params=pltpu.CompilerParams(dimension_semantics=("arbitrary",), vmem_limit_bytes=VMEM_LIMIT),
        name="mod",
    )(cc, w_mod, b_mod)


Q0, K0, V0, GLU0 = 0, D_ATT, 2 * D_ATT, 3 * D_ATT
GA0 = GLU0 + 2 * D_CONV


NT_DIMS = (((1,), (1,)), ((), ()))


def _store_transposed(res_t, ref, scale=None):
    for j in range(ref.shape[0]):
        blk = res_t[:, j * LANES:(j + 1) * LANES]
        if scale is not None:
            blk = blk * scale
        ref[j] = blk.astype(ref.dtype)


def _proj_kernel(x_ref, mod_ref, g_ref, wqv_ref, w_ref, qt_ref, k_ref, vt_ref, u_ref, ga_ref, gb_ref):
    d = x_ref.shape[1]
    mod = mod_ref[0]
    h = _norm_mod(x_ref[...], g_ref[...], mod[0:1], mod[1:2]).astype(BF16)

    qv_t = lax.dot_general(wqv_ref[...], h, NT_DIMS, preferred_element_type=F32)
    _store_transposed(qv_t[:D_ATT], qt_ref, HEAD_DIM ** -0.5)
    _store_transposed(qv_t[D_ATT:], vt_ref)

    def seg(lo, hi):
        return jnp.dot(h, w_ref[:, lo:hi], preferred_element_type=F32)

    k_ref[...] = seg(0, D_ATT).astype(BF16)
    a = seg(D_ATT, D_ATT + D_CONV)
    g = seg(D_ATT + D_CONV, D_ATT + 2 * D_CONV)
    u_ref[...] = (a * _sigmoid(g)).astype(BF16)
    g0 = D_ATT + 2 * D_CONV
    ga_ref[...] = _sigmoid(seg(g0, g0 + d)).astype(BF16)
    gb_ref[...] = _sigmoid(seg(g0 + d, g0 + 2 * d)).astype(BF16)


def _proj_call(x2, mod3, g, wqv_t, w_rest, seq):
    t, d = x2.shape
    tm = TM_PROJ
    tpb = seq // tm
    nblk = tm // LANES
    outs = [
        jax.ShapeDtypeStruct((t // LANES, D_ATT, LANES), BF16),
        jax.ShapeDtypeStruct((t, D_ATT), BF16),
        jax.ShapeDtypeStruct((t // LANES, D_ATT, LANES), BF16),
        jax.ShapeDtypeStruct((t, D_CONV), BF16),
        jax.ShapeDtypeStruct((t, d), BF16),
        jax.ShapeDtypeStruct((t, d), BF16),
    ]
    row = lambda i: (i, 0)
    blk3 = lambda i: (i, 0, 0)
    return pl.pallas_call(
        _proj_kernel,
        grid=(t // tm,),
        in_specs=[
            pl.BlockSpec((tm, d), row),
            pl.BlockSpec((1, 6, d), lambda i: (i // tpb, 0, 0)),
            pl.BlockSpec((1, d), lambda i: (0, 0)),
            pl.BlockSpec(wqv_t.shape, lambda i: (0, 0)),
            pl.BlockSpec(w_rest.shape, lambda i: (0, 0)),
        ],
        out_specs=[
            pl.BlockSpec((nblk, D_ATT, LANES), blk3),
            pl.BlockSpec((tm, D_ATT), row),
            pl.BlockSpec((nblk, D_ATT, LANES), blk3),
            pl.BlockSpec((tm, D_CONV), row),
            pl.BlockSpec((tm, d), row),
            pl.BlockSpec((tm, d), row),
        ],
        out_shape=outs,
        compiler_params=pltpu.CompilerParams(dimension_semantics=("arbitrary",), vmem_limit_bytes=VMEM_LIMIT),
        name="proj",
    )(x2, mod3, g, wqv_t, w_rest)


def _ctx_kv_kernel(x_ref, mod_ref, g_ref, wk_ref, wvt_ref, k_ref, vt_ref):
    mod = mod_ref[0]
    h = _norm_mod(x_ref[...], g_ref[...], mod[0:1], mod[1:2]).astype(BF16)
    k_ref[...] = jnp.dot(h, wk_ref[...], preferred_element_type=F32).astype(BF16)
    _store_transposed(lax.dot_general(wvt_ref[...], h, NT_DIMS, preferred_element_type=F32), vt_ref)


def _ctx_kv_call(c2, mod3, g, w_k, w_vt):
    t, d = c2.shape
    tm = 512
    nblk = tm // LANES
    row = lambda i: (i, 0)
    return pl.pallas_call(
        _ctx_kv_kernel,
        grid=(t // tm,),
        in_specs=[
            pl.BlockSpec((tm, d), row),
            pl.BlockSpec((1, 6, d), lambda i: (0, 0, 0)),
            pl.BlockSpec((1, d), lambda i: (0, 0)),
            pl.BlockSpec(w_k.shape, lambda i: (0, 0)),
            pl.BlockSpec(w_vt.shape, lambda i: (0, 0)),
        ],
        out_specs=[pl.BlockSpec((tm, D_ATT), row), pl.BlockSpec((nblk, D_ATT, LANES), lambda i: (i, 0, 0))],
        out_shape=[jax.ShapeDtypeStruct((t, D_ATT), BF16), jax.ShapeDtypeStruct((t // LANES, D_ATT, LANES), BF16)],
        compiler_params=pltpu.CompilerParams(dimension_semantics=("arbitrary",), vmem_limit_bytes=VMEM_LIMIT),
        name="ctx_kv",
    )(c2, mod3, g, w_k, w_vt)


HEADS_PER_GROUP = 4
GROUP_W = HEADS_PER_GROUP * HEAD_DIM
N_HEAD_GROUPS = N_HEADS // HEADS_PER_GROUP


def _attn_kernel(qt_ref, k_ref, vt_ref, kc_ref, vct_ref, bias_ref, o_ref, vboth):
    nblk = qt_ref.shape[0]
    rows = 2 * nblk
    n_loc = WIN_ROWS * GRID_W
    n_ctx_blk = vct_ref.shape[0]
    half = GRID_W

    for j in range(nblk):
        vboth[0, j] = vt_ref[j]
    for j in range(nblk - 1):
        vboth[1, j] = jnp.concatenate([vt_ref[j][:, half:], vt_ref[j + 1][:, :half]], axis=1)
    vboth[1, nblk - 1] = jnp.zeros_like(vt_ref[0])

    lane = lax.broadcasted_iota(jnp.int32, (D_ATT, LANES), 1)
    low = lane < half
    rblk = lax.broadcasted_iota(jnp.int32, (GROUP_W, GROUP_W), 0) // HEAD_DIM
    cblk = lax.broadcasted_iota(jnp.int32, (GROUP_W, GROUP_W), 1) // HEAD_DIM
    diag = rblk == cblk
    low64 = lax.broadcasted_iota(jnp.int32, (HEAD_DIM, LANES), 1) < half
    kc = kc_ref[...]

    def one_row(row, tiled, side):
        r_start = jnp.clip(row - WIN_ROWS // 2, 0, rows - WIN_ROWS)
        pat = row - r_start
        kw = k_ref[pl.ds(pl.multiple_of(r_start * GRID_W, GRID_W), n_loc), :]
        vwin = vboth[r_start % 2, pl.ds(r_start // 2, n_loc // LANES)]
        parts = []
        for g in range(N_HEAD_GROUPS):
            fs = slice(g * GROUP_W, (g + 1) * GROUP_W)
            tg = tiled[fs, :]
            w = jnp.where(diag, jnp.concatenate([tg, tg], axis=1), jnp.zeros((), BF16))
            s_loc = jnp.dot(kw[:, fs], w, preferred_element_type=F32) + bias_ref[pat, g]
            s_ctx = jnp.dot(kc[:, fs], w, preferred_element_type=F32)
            m = jnp.maximum(jnp.max(s_loc, axis=0, keepdims=True), jnp.max(s_ctx, axis=0, keepdims=True))
            p_loc = jnp.exp(s_loc - m)
            p_ctx = jnp.exp(s_ctx - m)
            l = jnp.sum(p_loc, axis=0, keepdims=True) + jnp.sum(p_ctx, axis=0, keepdims=True)
            p = jnp.concatenate([p_loc.astype(BF16), p_ctx.astype(BF16)], axis=0)
            vt_g = jnp.concatenate([vwin[j][fs, :] for j in range(n_loc // LANES)]
                                   + [vct_ref[j][fs, :] for j in range(n_ctx_blk)], axis=1)
            o_t = jnp.dot(vt_g, p, preferred_element_type=F32) * (1.0 / l)
            for j in range(HEADS_PER_GROUP):
                blk = o_t[j * HEAD_DIM:(j + 1) * HEAD_DIM, (j // 2) * LANES:(j // 2 + 1) * LANES]
                if j % 2 != side:
                    blk = pltpu.roll(blk, half, axis=1)
                parts.append(blk)
        return parts

    def body(i, carry):
        xq = qt_ref[i]
        rolled = jnp.concatenate([xq[:, half:], xq[:, :half]], axis=1)
        parts_a = one_row(2 * i, jnp.where(low, xq, rolled), 0)
        parts_b = one_row(2 * i + 1, jnp.where(low, rolled, xq), 1)
        out = jnp.concatenate([jnp.where(low64, a, b) for a, b in zip(parts_a, parts_b)], axis=0)
        o_ref[i] = out.astype(o_ref.dtype)
        return carry

    lax.fori_loop(0, nblk, body, 0)


def _attn_call(qt, k, vt, kc, vct, bias, seq, n_ctx):
    t = k.shape[0]
    b = t // seq
    nblk = seq // LANES
    lat3 = pl.BlockSpec((nblk, D_ATT, LANES), lambda i: (i, 0, 0))
    return pl.pallas_call(
        _attn_kernel,
        grid=(b,),
        in_specs=[lat3,
                  pl.BlockSpec((seq, D_ATT), lambda i: (i, 0)),
                  lat3,
                  pl.BlockSpec((n_ctx, D_ATT), lambda i: (i, 0)),
                  pl.BlockSpec((n_ctx // LANES, D_ATT, LANES), lambda i: (i, 0, 0)),
                  pl.BlockSpec(bias.shape, lambda i: (0, 0, 0, 0))],
        out_specs=lat3,
        out_shape=jax.ShapeDtypeStruct((t // LANES, D_ATT, LANES), BF16),
        scratch_shapes=[pltpu.VMEM((2, nblk, D_ATT, LANES), BF16)],
        compiler_params=pltpu.CompilerParams(dimension_semantics=("arbitrary",), vmem_limit_bytes=VMEM_LIMIT),
        name="attn",
    )(qt, k, vt, kc, vct, bias)


def _bias_table(rpb):
    cq = np.arange(GRID_W)[:, None]
    ck = np.arange(GRID_W)[None, :]
    c_start = np.clip(cq - WIN_COLS // 2, 0, GRID_W - WIN_COLS)
    col_mask = (ck >= c_start) & (ck < c_start + WIN_COLS)
    dc_idx = np.clip(ck - cq + WIN_COLS - 1, 0, 2 * WIN_COLS - 2)
    pat = np.arange(WIN_ROWS)[:, None]
    kr = np.arange(WIN_ROWS)[None, :]
    dr_idx = kr - pat + WIN_ROWS - 1
    tab = rpb[:, dr_idx][:, :, :, dc_idx]
    tab = jnp.where(col_mask[None, None, None], tab, NEG_INF)
    tab = tab.reshape(N_HEAD_GROUPS, HEADS_PER_GROUP, WIN_ROWS, WIN_ROWS, GRID_W, GRID_W)
    tab = tab.transpose(2, 0, 3, 5, 1, 4)
    return tab.reshape(WIN_ROWS, N_HEAD_GROUPS, WIN_ROWS * GRID_W, GROUP_W).astype(F32)


def _merge_kernel(att_ref, u_ref, up_ref, un_ref, ga_ref, gb_ref, x_ref, mod_ref,
                  cw_ref, cb_ref, lg_ref, lb_ref, wa_ref, wc_ref, wo_ref, nf_ref, wr_ref, br_ref, tri_ref,
                  xmid_ref, hp_ref, info_ref, cnt_ref,
                  ubuf, shifted, ybuf, run_ref, *, tiles_per_seq):
    i = pl.program_id(0)
    tm = x_ref.shape[0]
    d = x_ref.shape[1]
    mod = mod_ref[0]
    first = (i % tiles_per_seq) == 0
    last = (i % tiles_per_seq) == tiles_per_seq - 1

    ubuf[0:HALO, :] = jnp.where(first, 0.0, up_ref[...].astype(F32))
    ubuf[HALO:HALO + tm, :] = u_ref[...].astype(F32)
    ubuf[HALO + tm:, :] = jnp.where(last, 0.0, un_ref[...].astype(F32))
    span = shifted.shape[1]
    for ph in range(SUBLANES):
        shifted[ph] = ubuf[ph:ph + span, :]
    base = HALO - CONV_WIDTH // 2
    for lc in range(D_CONV // LANES):
        ls = slice(lc * LANES, (lc + 1) * LANES)
        for tc in range(tm // CONV_CHUNK):
            acc = jnp.zeros((CONV_CHUNK, LANES), F32)
            for j in range(CONV_WIDTH):
                tiles, ph = divmod(base + j, SUBLANES)
                lo = tc * CONV_CHUNK + tiles * SUBLANES
                acc = acc + shifted[ph, lo:lo + CONV_CHUNK, ls] * cw_ref[j:j + 1, ls]
            ybuf[tc * CONV_CHUNK:(tc + 1) * CONV_CHUNK, ls] = acc
    yc = ybuf[...] + cb_ref[...]
    mu = jnp.mean(yc, axis=-1, keepdims=True)
    var = jnp.mean(jnp.square(yc - mu), axis=-1, keepdims=True)
    yn = (yc - mu) * lax.rsqrt(var + NORM_EPS) * lg_ref[...] + lb_ref[...]
    yn = yn * _sigmoid(yn)
    y_conv = jnp.dot(yn.astype(BF16), wc_ref[...], preferred_element_type=F32)

    att = jnp.concatenate([att_ref[j].astype(F32).T for j in range(tm // LANES)], axis=0)
    y_att = jnp.dot(att.astype(BF16), wa_ref[...], preferred_element_type=F32)
    mix = ga_ref[...].astype(F32) * y_att + gb_ref[...].astype(F32) * y_conv
    y = jnp.dot(mix.astype(BF16), wo_ref[...], preferred_element_type=F32)
    x_mid = x_ref[...] + mod[2:3] * y
    xmid_ref[...] = x_mid
    h2 = _norm_mod(x_mid, nf_ref[...], mod[3:4], mod[4:5])

    h2_hi = h2.astype(BF16)
    h2_lo = (h2 - h2_hi.astype(F32)).astype(BF16)
    t_hi = jnp.dot(h2_hi, wr_ref[...], preferred_element_type=F32)
    t_lo = jnp.dot(h2_lo, wr_ref[:, :LANES], preferred_element_type=F32)
    logits = t_hi[:, :LANES] + t_hi[:, LANES:] + t_lo + br_ref[...]
    lane = lax.broadcasted_iota(jnp.int32, logits.shape, 1).astype(F32)
    big = float(LANES)
    is_g = lane < N_GROUPS
    gl = jnp.where(is_g, logits, -jnp.inf)
    gmax = jnp.max(gl, axis=-1, keepdims=True)
    g_idx = jnp.min(jnp.where(gl == gmax, lane, big), axis=-1, keepdims=True)
    p_group = 1.0 / jnp.sum(jnp.where(is_g, jnp.exp(gl - gmax), 0.0), axis=-1, keepdims=True)
    e_lo = N_GROUPS + EXPERTS_PER_GROUP * g_idx
    in_grp = (lane >= e_lo) & (lane < e_lo + EXPERTS_PER_GROUP)
    el = jnp.where(in_grp, logits, -jnp.inf)
    v1 = jnp.max(el, axis=-1, keepdims=True)
    i1 = jnp.min(jnp.where(el == v1, lane, big), axis=-1, keepdims=True)
    el2 = jnp.where(lane == i1, -jnp.inf, el)
    v2 = jnp.max(el2, axis=-1, keepdims=True)
    i2 = jnp.min(jnp.where(el2 == v2, lane, big), axis=-1, keepdims=True)
    e2 = jnp.exp(v2 - v1)
    gate1 = p_group / (1.0 + e2)
    gate2 = p_group * e2 / (1.0 + e2)
    j1 = i1 - e_lo
    j2 = i2 - e_lo
    ja = jnp.minimum(j1, j2)
    jb = jnp.maximum(j1, j2)
    gate_a = jnp.where(j1 < j2, gate1, gate2)
    gate_b = jnp.where(j1 < j2, gate2, gate1)
    pair = ja * (2 * EXPERTS_PER_GROUP - 1 - ja) * 0.5 + (jb - ja - 1.0)
    cls = g_idx * PAIRS_PER_GROUP + pair

    @pl.when(i == 0)
    def _():
        run_ref[...] = jnp.zeros_like(run_ref)

    onehot = lane == cls
    oh_f = jnp.where(onehot, 1.0, 0.0)
    before = jnp.dot(tri_ref[...], oh_f.astype(BF16), preferred_element_type=F32)
    run = run_ref[0:1, :]
    rank = jnp.sum(oh_f * (before + run), axis=-1, keepdims=True)
    new_run = run + jnp.sum(oh_f, axis=0, keepdims=True)
    run_ref[...] = jnp.broadcast_to(new_run, run_ref.shape)
    cnt_ref[...] = jnp.broadcast_to(new_run, cnt_ref.shape)

    info = jnp.where(lane == 0.0, cls, jnp.where(lane == 1.0, rank, 0.0))
    info_ref[...] = info.T[0:SUBLANES, :]

    half = d // 2
    hi = lax.bitcast_convert_type(h2[:, :half].astype(BF16).astype(F32), U32)
    lo = lax.bitcast_convert_type(h2[:, half:].astype(BF16).astype(F32), U32)
    words = hi | (lo >> 16)
    n_words = half // LANES
    for c in range(n_words):
        hp_ref[pl.ds(c, tm, stride=ROW_TILE), :] = words[:, c * LANES:(c + 1) * LANES]
    gates = jnp.where(lane == 0.0, gate_a, jnp.where(lane == 1.0, gate_b, 0.0))
    hp_ref[pl.ds(n_words, tm, stride=ROW_TILE), :] = lax.bitcast_convert_type(gates, U32)
    zero = jnp.zeros((tm, LANES), U32)
    for c in range(n_words + 1, ROW_TILE):
        hp_ref[pl.ds(c, tm, stride=ROW_TILE), :] = zero


def _merge_call(att, u, ga, gb, x2, mod3, cw, cb, lg, lb, wa, wc, wo, nf, wr, br, seq):
    t, d = x2.shape
    tm = TM_MERGE
    tpb = seq // tm
    hb = tm // HALO
    n_halo = t // HALO
    tri = jnp.asarray(np.tril(np.ones((tm, tm), np.float32), -1), BF16)
    row = lambda i: (i, 0)
    const = lambda i: (0, 0)
    kern = functools.partial(_merge_kernel, tiles_per_seq=tpb)
    return pl.pallas_call(
        kern,
        grid=(t // tm,),
        in_specs=[
            pl.BlockSpec((tm // LANES, D_ATT, LANES), lambda i: (i, 0, 0)),
            pl.BlockSpec((tm, D_CONV), row),
            pl.BlockSpec((HALO, D_CONV), lambda i: (jnp.maximum(i * hb - 1, 0), 0)),
            pl.BlockSpec((HALO, D_CONV), lambda i: (jnp.minimum((i + 1) * hb, n_halo - 1), 0)),
            pl.BlockSpec((tm, d), row),
            pl.BlockSpec((tm, d), row),
            pl.BlockSpec((tm, d), row),
            pl.BlockSpec((1, 6, d), lambda i: (i // tpb, 0, 0)),
            pl.BlockSpec(cw.shape, const),
            pl.BlockSpec((1, D_CONV), const),
            pl.BlockSpec((1, D_CONV), const),
            pl.BlockSpec((1, D_CONV), const),
            pl.BlockSpec(wa.shape, const),
            pl.BlockSpec(wc.shape, const),
            pl.BlockSpec(wo.shape, const),
            pl.BlockSpec((1, d), const),
            pl.BlockSpec(wr.shape, const),
            pl.BlockSpec((1, LANES), const),
            pl.BlockSpec((tm, tm), const),
        ],
        out_specs=[
            pl.BlockSpec((tm, d), row),
            pl.BlockSpec((tm * ROW_TILE, LANES), row),
            pl.BlockSpec((SUBLANES, tm), lambda i: (0, i)),
            pl.BlockSpec((SUBLANES, LANES), const),
        ],
        out_shape=[
            jax.ShapeDtypeStruct((t, d), F32),
            jax.ShapeDtypeStruct((t * ROW_TILE, LANES), U32),
            jax.ShapeDtypeStruct((SUBLANES, t), F32),
            jax.ShapeDtypeStruct((SUBLANES, LANES), F32),
        ],
        scratch_shapes=[
            pltpu.VMEM((tm + 2 * HALO, D_CONV), F32),
            pltpu.VMEM((SUBLANES, tm + 2 * HALO - SUBLANES, D_CONV), F32),
            pltpu.VMEM((tm, D_CONV), F32),
            pltpu.VMEM((SUBLANES, LANES), F32),
        ],
        compiler_params=pltpu.CompilerParams(dimension_semantics=("arbitrary",), vmem_limit_bytes=VMEM_LIMIT),
        name="merge",
    )(att, u, u, u, ga, gb, x2, mod3, cw, cb, lg, lb, wa, wc, wo, nf, wr, br, tri)


def _dispatch_kernel(dest_ref, hp_ref, zeros_hbm, hs_hbm, sem):
    del zeros_hbm
    n = dest_ref.shape[0]

    def body(t, carry):
        src = pl.multiple_of(t * ROW_TILE, ROW_TILE)
        dst = pl.multiple_of(dest_ref[t] * ROW_TILE, ROW_TILE)
        pltpu.make_async_copy(hp_ref.at[pl.ds(src, ROW_TILE)], hs_hbm.at[pl.ds(dst, ROW_TILE)], sem).start()
        return carry

    lax.fori_loop(0, n, body, 0, unroll=8)
    pltpu.make_async_copy(hp_ref, hs_hbm.at[pl.ds(0, n * ROW_TILE)], sem).wait()


def _dispatch_call(dest, hp, p_rows):
    t = dest.shape[0]
    zeros = jnp.zeros((p_rows * ROW_TILE, LANES), U32)
    return pl.pallas_call(
        _dispatch_kernel,
        grid=(t // TM_ROWS,),
        in_specs=[
            pl.BlockSpec((TM_ROWS,), lambda i: (i,), memory_space=pltpu.SMEM),
            pl.BlockSpec((TM_ROWS * ROW_TILE, LANES), lambda i: (i, 0)),
            pl.BlockSpec(memory_space=pl.ANY),
        ],
        out_specs=pl.BlockSpec(memory_space=pl.ANY),
        out_shape=jax.ShapeDtypeStruct((p_rows * ROW_TILE, LANES), U32),
        scratch_shapes=[pltpu.SemaphoreType.DMA(())],
        input_output_aliases={2: 0},
        compiler_params=pltpu.CompilerParams(dimension_semantics=("arbitrary",), has_side_effects=True,
                                             vmem_limit_bytes=VMEM_LIMIT),
        name="dispatch",
    )(dest, hp, zeros)


def _moe_kernel(ea_ref, eb_ref, valid_ref, hs_ref, wga_ref, wua_ref, wda_ref, wgb_ref, wub_ref, wdb_ref, y_ref):
    i = pl.program_id(0)
    blk = hs_ref.shape[0] // ROW_TILE
    d = wga_ref.shape[1]
    n_words = d // 2 // LANES

    @pl.when(valid_ref[i] == 1)
    def _():
        his, los = [], []
        for c in range(n_words):
            w = hs_ref[pl.ds(c, blk, stride=ROW_TILE), :]
            his.append(lax.bitcast_convert_type(w & jnp.uint32(0xFFFF0000), F32).astype(BF16))
            los.append(lax.bitcast_convert_type(w << 16, F32).astype(BF16))
        x = jnp.concatenate(his + los, axis=-1)
        gates = lax.bitcast_convert_type(hs_ref[pl.ds(n_words, blk, stride=ROW_TILE), :], F32)
        gate_a = gates[:, 0:1]
        gate_b = gates[:, 1:2]

        def mlp(wg, wu, wd):
            g = jnp.dot(x, wg[0], preferred_element_type=F32)
            u = jnp.dot(x, wu[0], preferred_element_type=F32)
            a = (g * _sigmoid(g) * u).astype(BF16)
            return jnp.dot(a, wd[0], preferred_element_type=F32)

        y = gate_a * mlp(wga_ref, wua_ref, wda_ref) + gate_b * mlp(wgb_ref, wub_ref, wdb_ref)
        for c in range(d // LANES):
            y_ref[pl.ds(c, blk, stride=ROW_TILE), :] = y[:, c * LANES:(c + 1) * LANES]

    @pl.when(valid_ref[i] != 1)
    def _():
        y_ref[...] = jnp.zeros_like(y_ref)


def _moe_call(blk_ea, blk_eb, blk_valid, hs, wg, wu, wd):
    nb = blk_ea.shape[0]
    d = wg.shape[1]
    de = wg.shape[2]
    rows = MOE_BLOCK * ROW_TILE
    tok = pl.BlockSpec((rows, LANES), lambda i, ea, eb, va: (i, 0))
    w_a = lambda shape: pl.BlockSpec((1,) + shape, lambda i, ea, eb, va: (ea[i], 0, 0))
    w_b = lambda shape: pl.BlockSpec((1,) + shape, lambda i, ea, eb, va: (eb[i], 0, 0))
    grid_spec = pltpu.PrefetchScalarGridSpec(
        num_scalar_prefetch=3,
        grid=(nb,),
        in_specs=[tok, w_a((d, de)), w_a((d, de)), w_a((de, d)), w_b((d, de)), w_b((d, de)), w_b((de, d))],
        out_specs=tok,
    )
    return pl.pallas_call(
        _moe_kernel,
        grid_spec=grid_spec,
        out_shape=jax.ShapeDtypeStruct((nb * rows, LANES), F32),
        compiler_params=pltpu.CompilerParams(dimension_semantics=("arbitrary",), vmem_limit_bytes=VMEM_LIMIT),
        name="moe",
    )(blk_ea, blk_eb, blk_valid, hs, wg, wu, wd, wg, wu, wd)


def _final_kernel(dest_ref, xmid_ref, mod_ref, fn_ref, y_hbm, o_ref, fbuf, sem):
    n = dest_ref.shape[0]
    d = xmid_ref.shape[1]

    def body(t, carry):
        src = pl.multiple_of(dest_ref[t] * ROW_TILE, ROW_TILE)
        dst = pl.multiple_of(t * ROW_TILE, ROW_TILE)
        pltpu.make_async_copy(y_hbm.at[pl.ds(src, ROW_TILE)], fbuf.at[pl.ds(dst, ROW_TILE)], sem).start()
        return carry

    lax.fori_loop(0, n, body, 0, unroll=8)
    pltpu.make_async_copy(y_hbm.at[pl.ds(0, n * ROW_TILE)], fbuf, sem).wait()
    f = jnp.concatenate([fbuf[pl.ds(c, n, stride=ROW_TILE), :] for c in range(d // LANES)], axis=-1)
    x = xmid_ref[...] + mod_ref[0][5:6] * f
    ms = jnp.mean(x * x, axis=-1, keepdims=True)
    o_ref[...] = x * lax.rsqrt(ms + NORM_EPS) * fn_ref[...]


def _final_call(dest, xmid, mod3, fn, y, seq):
    t, d = xmid.shape
    tm = TM_ROWS
    tpb = seq // tm
    return pl.pallas_call(
        _final_kernel,
        grid=(t // tm,),
        in_specs=[
            pl.BlockSpec((tm,), lambda i: (i,), memory_space=pltpu.SMEM),
            pl.BlockSpec((tm, d), lambda i: (i, 0)),
            pl.BlockSpec((1, 6, d), lambda i: (i // tpb, 0, 0)),
            pl.BlockSpec((1, d), lambda i: (0, 0)),
            pl.BlockSpec(memory_space=pl.ANY),
        ],
        out_specs=pl.BlockSpec((tm, d), lambda i: (i, 0)),
        out_shape=jax.ShapeDtypeStruct((t, d), F32),
        scratch_shapes=[pltpu.VMEM((tm * ROW_TILE, LANES), F32), pltpu.SemaphoreType.DMA(())],
        compiler_params=pltpu.CompilerParams(dimension_semantics=("arbitrary",), vmem_limit_bytes=VMEM_LIMIT),
        name="final",
    )(dest, xmid, mod3, fn, y)


def _pair_tables():
    ea, eb = [], []
    for g in range(N_GROUPS):
        for a in range(EXPERTS_PER_GROUP):
            for b in range(a + 1, EXPERTS_PER_GROUP):
                ea.append(g * EXPERTS_PER_GROUP + a)
                eb.append(g * EXPERTS_PER_GROUP + b)
    return np.asarray(ea, np.int32), np.asarray(eb, np.int32)


def _routing_plan(info, counts, t):
    cls = info[0].astype(jnp.int32)
    rank = info[1].astype(jnp.int32)
    cnt = counts[0, :N_CLASSES].astype(jnp.int32)
    padded = (cnt + MOE_BLOCK - 1) // MOE_BLOCK * MOE_BLOCK
    pad_end = jnp.cumsum(padded)
    pad_start = pad_end - padded
    cls_ids = jnp.arange(N_CLASSES, dtype=jnp.int32)
    dest = rank + jnp.sum(jnp.where(cls[:, None] == cls_ids[None, :], pad_start[None, :], 0), axis=1)
    nb = t // MOE_BLOCK + N_CLASSES
    nb_used = pad_end[-1] // MOE_BLOCK
    blk = jnp.arange(nb, dtype=jnp.int32)
    blk_cls = jnp.clip(jnp.searchsorted(pad_end, blk * MOE_BLOCK, side="right"), 0, N_CLASSES - 1)
    valid = blk < nb_used
    last_cls = blk_cls[jnp.maximum(nb_used - 1, 0)]
    blk_cls = jnp.where(valid, blk_cls, last_cls).astype(jnp.int32)
    tab_a, tab_b = _pair_tables()
    blk_ea = jnp.asarray(tab_a)[blk_cls]
    blk_eb = jnp.asarray(tab_b)[blk_cls]
    return dest, blk_ea, blk_eb, valid.astype(jnp.int32), nb


def kernel(x, c, ctx, c_ctx, w_mod, b_mod, norm_mix, w_in, rpb, w_att_out, conv_w, conv_b, conv_ln_g, conv_ln_b,
           w_conv_out, w_o, norm_ffn, w_router_group, b_router_group, w_router_expert, b_router_expert,
           w_exp_gate, w_exp_up, w_exp_down, final_norm):
    b, seq, d = x.shape
    n_ctx = ctx.shape[1]
    t = b * seq
    assert w_mod.shape[0] == 1, "single layer"
    assert seq % TM_ROWS == 0 and seq // GRID_W >= WIN_ROWS and n_ctx % LANES == 0 and (b * n_ctx) % 512 == 0
    assert d == 1024

    mod_rows = -(-(b + 1) // SUBLANES) * SUBLANES
    cc = jnp.zeros((mod_rows, d), F32).at[:b].set(c).at[b].set(c_ctx)
    m_all = _mod_call(cc, w_mod[0], b_mod[0][None, :])
    mod_lat = m_all[:b].reshape(b, 6, d)
    mod_ctx = m_all[b:b + 1].reshape(1, 6, d)

    x2 = x.reshape(t, d)
    g_mix = norm_mix[0][None, :]
    w_in_b = w_in[0].astype(BF16)
    wqv_t = jnp.concatenate([w_in_b[:, Q0:K0], w_in_b[:, V0:GLU0]], axis=1).T
    w_rest = jnp.concatenate([w_in_b[:, K0:V0], w_in_b[:, GLU0:]], axis=1)
    qt, k, vt, u, ga, gb = _proj_call(x2, mod_lat, g_mix, wqv_t, w_rest, seq)
    kc, vct = _ctx_kv_call(ctx.reshape(b * n_ctx, d), mod_ctx, g_mix, w_in_b[:, K0:V0], w_in_b[:, V0:GLU0].T)

    att = _attn_call(qt, k, vt, kc, vct, _bias_table(rpb[0]), seq, n_ctx)

    cw = jnp.zeros((32, D_CONV), F32).at[:CONV_WIDTH].set(conv_w[0])
    wr = jnp.zeros((d, LANES), F32).at[:, :N_GROUPS].set(w_router_group[0])
    wr = wr.at[:, N_GROUPS:N_GROUPS + N_EXPERTS].set(w_router_expert[0])
    wr_hi = wr.astype(BF16)
    wr = jnp.concatenate([wr_hi, (wr - wr_hi.astype(F32)).astype(BF16)], axis=1)
    br = jnp.zeros((1, LANES), F32).at[0, :N_GROUPS].set(b_router_group[0])
    br = br.at[0, N_GROUPS:N_GROUPS + N_EXPERTS].set(b_router_expert[0])
    x_mid, hp, info, counts = _merge_call(
        att, u, ga, gb, x2, mod_lat, cw, conv_b[0][None, :], conv_ln_g[0][None, :], conv_ln_b[0][None, :],
        w_att_out[0].astype(BF16), w_conv_out[0].astype(BF16), w_o[0].astype(BF16), norm_ffn[0][None, :],
        wr, br, seq)

    dest, blk_ea, blk_eb, blk_valid, nb = _routing_plan(info, counts, t)
    hs = _dispatch_call(dest, hp, nb * MOE_BLOCK)
    y = _moe_call(blk_ea, blk_eb, blk_valid, hs,
                  w_exp_gate[0].astype(BF16), w_exp_up[0].astype(BF16), w_exp_down[0].astype(BF16))
    out = _final_call(dest, x_mid, mod_lat, final_norm[None, :], y, seq)
    return out.reshape(b, seq, d)
```

```python
import functools

import numpy as np
import jax
import jax.numpy as jnp
from jax import lax
from jax.experimental import pallas as pl
from jax.experimental.pallas import tpu as pltpu

F32 = jnp.float32
BF16 = jnp.bfloat16
U32 = jnp.uint32

GRID_W = 64
N_HEADS = 8
HEAD_DIM = 64
D_ATT = N_HEADS * HEAD_DIM
WIN_ROWS = 8
WIN_COLS = 16
D_CONV = 512
CONV_WIDTH = 31
N_GROUPS = 4
EXPERTS_PER_GROUP = 8
N_EXPERTS = N_GROUPS * EXPERTS_PER_GROUP
PAIRS_PER_GROUP = EXPERTS_PER_GROUP * (EXPERTS_PER_GROUP - 1) // 2
N_CLASSES = N_GROUPS * PAIRS_PER_GROUP
NORM_EPS = 1e-6
NEG_INF = -1e30

LANES = 128
SUBLANES = 8
ROW_TILE = SUBLANES
HALO = 16
VMEM_LIMIT = 56 * 1024 * 1024

TM_PROJ = 512
TM_MERGE = 512
TM_ROWS = 1024
MOE_BLOCK = 256
CONV_CHUNK = 64
DMA_UNROLL = 8

HIGHEST = lax.Precision.HIGHEST


def _norm_mod(x, g, shift, scale):
    ms = jnp.mean(x * x, axis=-1, keepdims=True)
    y = x * lax.rsqrt(ms + NORM_EPS) * g
    return y * (1.0 + scale) + shift


def _sigmoid(x):
    return jax.nn.sigmoid(x)


def _mod_kernel(c_ref, w_ref, b_ref, o_ref):
    c = c_ref[...]
    s = c * _sigmoid(c)
    o_ref[...] = jnp.dot(s, w_ref[...], precision=HIGHEST, preferred_element_type=F32) + b_ref[...]


def _mod_call(cc, w_mod, b_mod):
    rows, d = cc.shape
    n = w_mod.shape[1]
    tn = 1024
    return pl.pallas_call(
        _mod_kernel,
        grid=(n // tn,),
        in_specs=[
            pl.BlockSpec((rows, d), lambda j: (0, 0)),
            pl.BlockSpec((d, tn), lambda j: (0, j)),
            pl.BlockSpec((1, tn), lambda j: (0, j)),
        ],
        out_specs=pl.BlockSpec((rows, tn), lambda j: (0, j)),
        out_shape=jax.ShapeDtypeStruct((rows, n), F32),
        compiler_params=pltpu.CompilerParams(dimension_semantics=("arbitrary",), vmem_limit_bytes=VMEM_LIMIT),
        name="mod",
    )(cc, w_mod, b_mod)


Q0, K0, V0, GLU0 = 0, D_ATT, 2 * D_ATT, 3 * D_ATT
GA0 = GLU0 + 2 * D_CONV


NT_DIMS = (((1,), (1,)), ((), ()))


def _store_transposed(res_t, ref, scale=None):
    for j in range(ref.shape[0]):
        blk = res_t[:, j * LANES:(j + 1) * LANES]
        if scale is not None:
            blk = blk * scale
        ref[j] = blk.astype(ref.dtype)


def _proj_kernel(x_ref, mod_ref, g_ref, wqv_ref, w_ref, qt_ref, k_ref, vt_ref, u_ref, ga_ref, gb_ref):
    d = x_ref.shape[1]
    mod = mod_ref[0]
    h = _norm_mod(x_ref[...], g_ref[...], mod[0:1], mod[1:2]).astype(BF16)

    qv_t = lax.dot_general(wqv_ref[...], h, NT_DIMS, preferred_element_type=F32)
    _store_transposed(qv_t[:D_ATT], qt_ref, HEAD_DIM ** -0.5)
    _store_transposed(qv_t[D_ATT:], vt_ref)

    def seg(lo, hi):
        return jnp.dot(h, w_ref[:, lo:hi], preferred_element_type=F32)

    k_ref[...] = seg(0, D_ATT).astype(BF16)
    a = seg(D_ATT, D_ATT + D_CONV)
    g = seg(D_ATT + D_CONV, D_ATT + 2 * D_CONV)
    u_ref[...] = (a * _sigmoid(g)).astype(BF16)
    g0 = D_ATT + 2 * D_CONV
    ga_ref[...] = _sigmoid(seg(g0, g0 + d)).astype(BF16)
    gb_ref[...] = _sigmoid(seg(g0 + d, g0 + 2 * d)).astype(BF16)


def _proj_call(x2, mod3, g, wqv_t, w_rest, seq):
    t, d = x2.shape
    tm = TM_PROJ
    tpb = seq // tm
    nblk = tm // LANES
    outs = [
        jax.ShapeDtypeStruct((t // LANES, D_ATT, LANES), BF16),
        jax.ShapeDtypeStruct((t, D_ATT), BF16),
        jax.ShapeDtypeStruct((t // LANES, D_ATT, LANES), BF16),
        jax.ShapeDtypeStruct((t, D_CONV), BF16),
        jax.ShapeDtypeStruct((t, d), BF16),
        jax.ShapeDtypeStruct((t, d), BF16),
    ]
    row = lambda i: (i, 0)
    blk3 = lambda i: (i, 0, 0)
    return pl.pallas_call(
        _proj_kernel,
        grid=(t // tm,),
        in_specs=[
            pl.BlockSpec((tm, d), row),
            pl.BlockSpec((1, 6, d), lambda i: (i // tpb, 0, 0)),
            pl.BlockSpec((1, d), lambda i: (0, 0)),
            pl.BlockSpec(wqv_t.shape, lambda i: (0, 0)),
            pl.BlockSpec(w_rest.shape, lambda i: (0, 0)),
        ],
        out_specs=[
            pl.BlockSpec((nblk, D_ATT, LANES), blk3),
            pl.BlockSpec((tm, D_ATT), row),
            pl.BlockSpec((nblk, D_ATT, LANES), blk3),
            pl.BlockSpec((tm, D_CONV), row),
            pl.BlockSpec((tm, d), row),
            pl.BlockSpec((tm, d), row),
        ],
        out_shape=outs,
        compiler_params=pltpu.CompilerParams(dimension_semantics=("arbitrary",), vmem_limit_bytes=VMEM_LIMIT),
        name="proj",
    )(x2, mod3, g, wqv_t, w_rest)


def _ctx_kv_kernel(x_ref, mod_ref, g_ref, wk_ref, wvt_ref, k_ref, vt_ref):
    mod = mod_ref[0]
    h = _norm_mod(x_ref[...], g_ref[...], mod[0:1], mod[1:2]).astype(BF16)
    k_ref[...] = jnp.dot(h, wk_ref[...], preferred_element_type=F32).astype(BF16)
    _store_transposed(lax.dot_general(wvt_ref[...], h, NT_DIMS, preferred_element_type=F32), vt_ref)


def _ctx_kv_call(c2, mod3, g, w_k, w_vt):
    t, d = c2.shape
    tm = 512
    nblk = tm // LANES
    row = lambda i: (i, 0)
    return pl.pallas_call(
        _ctx_kv_kernel,
        grid=(t // tm,),
        in_specs=[
            pl.BlockSpec((tm, d), row),
            pl.BlockSpec((1, 6, d), lambda i: (0, 0, 0)),
            pl.BlockSpec((1, d), lambda i: (0, 0)),
            pl.BlockSpec(w_k.shape, lambda i: (0, 0)),
            pl.BlockSpec(w_vt.shape, lambda i: (0, 0)),
        ],
        out_specs=[pl.BlockSpec((tm, D_ATT), row), pl.BlockSpec((nblk, D_ATT, LANES), lambda i: (i, 0, 0))],
        out_shape=[jax.ShapeDtypeStruct((t, D_ATT), BF16), jax.ShapeDtypeStruct((t // LANES, D_ATT, LANES), BF16)],
        compiler_params=pltpu.CompilerParams(dimension_semantics=("arbitrary",), vmem_limit_bytes=VMEM_LIMIT),
        name="ctx_kv",
    )(c2, mod3, g, w_k, w_vt)


HEADS_PER_GROUP = 4
GROUP_W = HEADS_PER_GROUP * HEAD_DIM
N_HEAD_GROUPS = N_HEADS // HEADS_PER_GROUP


def _attn_kernel(qt_ref, k_ref, vt_ref, kc_ref, vct_ref, bias_ref, o_ref, vboth):
    nblk = qt_ref.shape[0]
    rows = 2 * nblk
    n_loc = WIN_ROWS * GRID_W
    n_ctx_blk = vct_ref.shape[0]
    half = GRID_W

    for j in range(nblk):
        vboth[0, j] = vt_ref[j]
    for j in range(nblk - 1):
        vboth[1, j] = jnp.concatenate([vt_ref[j][:, half:], vt_ref[j + 1][:, :half]], axis=1)
    vboth[1, nblk - 1] = jnp.zeros_like(vt_ref[0])

    lane = lax.broadcasted_iota(jnp.int32, (D_ATT, LANES), 1)
    low = lane < half
    rblk = lax.broadcasted_iota(jnp.int32, (GROUP_W, GROUP_W), 0) // HEAD_DIM
    cblk = lax.broadcasted_iota(jnp.int32, (GROUP_W, GROUP_W), 1) // HEAD_DIM
    diag = rblk == cblk
    low64 = lax.broadcasted_iota(jnp.int32, (HEAD_DIM, LANES), 1) < half
    kc = kc_ref[...]

    def one_row(row, tiled, side):
        r_start = jnp.clip(row - WIN_ROWS // 2, 0, rows - WIN_ROWS)
        pat = row - r_start
        kw = k_ref[pl.ds(pl.multiple_of(r_start * GRID_W, GRID_W), n_loc), :]
        vwin = vboth[r_start % 2, pl.ds(r_start // 2, n_loc // LANES)]
        parts = []
        for g in range(N_HEAD_GROUPS):
            fs = slice(g * GROUP_W, (g + 1) * GROUP_W)
            tg = tiled[fs, :]
            w = jnp.where(diag, jnp.concatenate([tg, tg], axis=1), jnp.zeros((), BF16))
            s_loc = jnp.dot(kw[:, fs], w, preferred_element_type=F32) + bias_ref[pat, g]
            s_ctx = jnp.dot(kc[:, fs], w, preferred_element_type=F32)
            m = jnp.maximum(jnp.max(s_loc, axis=0, keepdims=True), jnp.max(s_ctx, axis=0, keepdims=True))
            p_loc = jnp.exp(s_loc - m)
            p_ctx = jnp.exp(s_ctx - m)
            l = jnp.sum(p_loc, axis=0, keepdims=True) + jnp.sum(p_ctx, axis=0, keepdims=True)
            p = jnp.concatenate([p_loc.astype(BF16), p_ctx.astype(BF16)], axis=0)
            vt_g = jnp.concatenate([vwin[j][fs, :] for j in range(n_loc // LANES)]
                                   + [vct_ref[j][fs, :] for j in range(n_ctx_blk)], axis=1)
            o_t = jnp.dot(vt_g, p, preferred_element_type=F32) * (1.0 / l)
            for j in range(HEADS_PER_GROUP):
                blk = o_t[j * HEAD_DIM:(j + 1) * HEAD_DIM, (j // 2) * LANES:(j // 2 + 1) * LANES]
                if j % 2 != side:
                    blk = pltpu.roll(blk, half, axis=1)
                parts.append(blk)
        return parts

    def body(i, carry):
        xq = qt_ref[i]
        rolled = jnp.concatenate([xq[:, half:], xq[:, :half]], axis=1)
        parts_a = one_row(2 * i, jnp.where(low, xq, rolled), 0)
        parts_b = one_row(2 * i + 1, jnp.where(low, rolled, xq), 1)
        out = jnp.concatenate([jnp.where(low64, a, b) for a, b in zip(parts_a, parts_b)], axis=0)
        o_ref[i] = out.astype(o_ref.dtype)
        return carry

    lax.fori_loop(0, nblk, body, 0)


def _attn_call(qt, k, vt, kc, vct, bias, seq, n_ctx):
    t = k.shape[0]
    b = t // seq
    nblk = seq // LANES
    lat3 = pl.BlockSpec((nblk, D_ATT, LANES), lambda i: (i, 0, 0))
    return pl.pallas_call(
        _attn_kernel,
        grid=(b,),
        in_specs=[lat3,
                  pl.BlockSpec((seq, D_ATT), lambda i: (i, 0)),
                  lat3,
                  pl.BlockSpec((n_ctx, D_ATT), lambda i: (i, 0)),
                  pl.BlockSpec((n_ctx // LANES, D_ATT, LANES), lambda i: (i, 0, 0)),
                  pl.BlockSpec(bias.shape, lambda i: (0, 0, 0, 0))],
        out_specs=lat3,
        out_shape=jax.ShapeDtypeStruct((t // LANES, D_ATT, LANES), BF16),
        scratch_shapes=[pltpu.VMEM((2, nblk, D_ATT, LANES), BF16)],
        compiler_params=pltpu.CompilerParams(dimension_semantics=("arbitrary",), vmem_limit_bytes=VMEM_LIMIT),
        name="attn",
    )(qt, k, vt, kc, vct, bias)


def _bias_table(rpb):
    cq = np.arange(GRID_W)[:, None]
    ck = np.arange(GRID_W)[None, :]
    c_start = np.clip(cq - WIN_COLS // 2, 0, GRID_W - WIN_COLS)
    col_mask = (ck >= c_start) & (ck < c_start + WIN_COLS)
    dc_idx = np.clip(ck - cq + WIN_COLS - 1, 0, 2 * WIN_COLS - 2)
    pat = np.arange(WIN_ROWS)[:, None]
    kr = np.arange(WIN_ROWS)[None, :]
    dr_idx = kr - pat + WIN_ROWS - 1
    tab = rpb[:, dr_idx][:, :, :, dc_idx]
    tab = jnp.where(col_mask[None, None, None], tab, NEG_INF)
    tab = tab.reshape(N_HEAD_GROUPS, HEADS_PER_GROUP, WIN_ROWS, WIN_ROWS, GRID_W, GRID_W)
    tab = tab.transpose(2, 0, 3, 5, 1, 4)
    return tab.reshape(WIN_ROWS, N_HEAD_GROUPS, WIN_ROWS * GRID_W, GROUP_W).astype(F32)


def _merge_kernel(att_ref, u_ref, up_ref, un_ref, ga_ref, gb_ref, x_ref, mod_ref,
                  cw_ref, cb_ref, lg_ref, lb_ref, wa_ref, wc_ref, wo_ref, nf_ref, wr_ref, br_ref, tri_ref,
                  xmid_ref, hp_ref, info_ref, cnt_ref,
                  ubuf, shifted, ybuf, run_ref, *, tiles_per_seq):
    i = pl.program_id(0)
    tm = x_ref.shape[0]
    d = x_ref.shape[1]
    mod = mod_ref[0]
    first = (i % tiles_per_seq) == 0
    last = (i % tiles_per_seq) == tiles_per_seq - 1

    ubuf[0:HALO, :] = jnp.where(first, 0.0, up_ref[...].astype(F32))
    ubuf[HALO:HALO + tm, :] = u_ref[...].astype(F32)
    ubuf[HALO + tm:, :] = jnp.where(last, 0.0, un_ref[...].astype(F32))
    span = shifted.shape[1]
    for ph in range(SUBLANES):
        shifted[ph] = ubuf[ph:ph + span, :]
    base = HALO - CONV_WIDTH // 2
    for lc in range(D_CONV // LANES):
        ls = slice(lc * LANES, (lc + 1) * LANES)
        for tc in range(tm // CONV_CHUNK):
            acc = jnp.zeros((CONV_CHUNK, LANES), F32)
            for j in range(CONV_WIDTH):
                tiles, ph = divmod(base + j, SUBLANES)
                lo = tc * CONV_CHUNK + tiles * SUBLANES
                acc = acc + shifted[ph, lo:lo + CONV_CHUNK, ls] * cw_ref[j:j + 1, ls]
            ybuf[tc * CONV_CHUNK:(tc + 1) * CONV_CHUNK, ls] = acc
    yc = ybuf[...] + cb_ref[...]
    mu = jnp.mean(yc, axis=-1, keepdims=True)
    var = jnp.mean(jnp.square(yc - mu), axis=-1, keepdims=True)
    yn = (yc - mu) * lax.rsqrt(var + NORM_EPS) * lg_ref[...] + lb_ref[...]
    yn = yn * _sigmoid(yn)
    y_conv = jnp.dot(yn.astype(BF16), wc_ref[...], preferred_element_type=F32)

    att = jnp.concatenate([att_ref[j].astype(F32).T for j in range(tm // LANES)], axis=0)
    y_att = jnp.dot(att.astype(BF16), wa_ref[...], preferred_element_type=F32)
    mix = ga_ref[...].astype(F32) * y_att + gb_ref[...].astype(F32) * y_conv
    y = jnp.dot(mix.astype(BF16), wo_ref[...], preferred_element_type=F32)
    x_mid = x_ref[...] + mod[2:3] * y
    xmid_ref[...] = x_mid
    h2 = _norm_mod(x_mid, nf_ref[...], mod[3:4], mod[4:5])

    h2_hi = h2.astype(BF16)
    h2_lo = (h2 - h2_hi.astype(F32)).astype(BF16)
    t_hi = jnp.dot(h2_hi, wr_ref[...], preferred_element_type=F32)
    t_lo = jnp.dot(h2_lo, wr_ref[:, :LANES], preferred_element_type=F32)
    logits = t_hi[:, :LANES] + t_hi[:, LANES:] + t_lo + br_ref[...]
    lane = lax.broadcasted_iota(jnp.int32, logits.shape, 1).astype(F32)
    big = float(LANES)
    is_g = lane < N_GROUPS
    gl = jnp.where(is_g, logits, -jnp.inf)
    gmax = jnp.max(gl, axis=-1, keepdims=True)
    g_idx = jnp.min(jnp.where(gl == gmax, lane, big), axis=-1, keepdims=True)
    p_group = 1.0 / jnp.sum(jnp.where(is_g, jnp.exp(gl - gmax), 0.0), axis=-1, keepdims=True)
    e_lo = N_GROUPS + EXPERTS_PER_GROUP * g_idx
    in_grp = (lane >= e_lo) & (lane < e_lo + EXPERTS_PER_GROUP)
    el = jnp.where(in_grp, logits, -jnp.inf)
    v1 = jnp.max(el, axis=-1, keepdims=True)
    i1 = jnp.min(jnp.where(el == v1, lane, big), axis=-1, keepdims=True)
    el2 = jnp.where(lane == i1, -jnp.inf, el)
    v2 = jnp.max(el2, axis=-1, keepdims=True)
    i2 = jnp.min(jnp.where(el2 == v2, lane, big), axis=-1, keepdims=True)
    e2 = jnp.exp(v2 - v1)
    gate1 = p_group / (1.0 + e2)
    gate2 = p_group * e2 / (1.0 + e2)
    j1 = i1 - e_lo
    j2 = i2 - e_lo
    ja = jnp.minimum(j1, j2)
    jb = jnp.maximum(j1, j2)
    gate_a = jnp.where(j1 < j2, gate1, gate2)
    gate_b = jnp.where(j1 < j2, gate2, gate1)
    pair = ja * (2 * EXPERTS_PER_GROUP - 1 - ja) * 0.5 + (jb - ja - 1.0)
    cls = g_idx * PAIRS_PER_GROUP + pair

    @pl.when(i == 0)
    def _():
        run_ref[...] = jnp.zeros_like(run_ref)

    onehot = lane == cls
    oh_f = jnp.where(onehot, 1.0, 0.0)
    before = jnp.dot(tri_ref[...], oh_f.astype(BF16), preferred_element_type=F32)
    run = run_ref[0:1, :]
    rank = jnp.sum(oh_f * (before + run), axis=-1, keepdims=True)
    new_run = run + jnp.sum(oh_f, axis=0, keepdims=True)
    run_ref[...] = jnp.broadcast_to(new_run, run_ref.shape)
    cnt_ref[...] = jnp.broadcast_to(new_run, cnt_ref.shape)

    info = jnp.where(lane == 0.0, cls, jnp.where(lane == 1.0, rank, 0.0))
    info_ref[...] = info.T[0:SUBLANES, :]

    half = d // 2
    hi = lax.bitcast_convert_type(h2[:, :half].astype(BF16).astype(F32), U32)
    lo = lax.bitcast_convert_type(h2[:, half:].astype(BF16).astype(F32), U32)
    words = hi | (lo >> 16)
    n_words = half // LANES
    for c in range(n_words):
        hp_ref[pl.ds(c, tm, stride=ROW_TILE), :] = words[:, c * LANES:(c + 1) * LANES]
    gates = jnp.where(lane == 0.0, gate_a, jnp.where(lane == 1.0, gate_b, 0.0))
    hp_ref[pl.ds(n_words, tm, stride=ROW_TILE), :] = lax.bitcast_convert_type(gates, U32)
    zero = jnp.zeros((tm, LANES), U32)
    for c in range(n_words + 1, ROW_TILE):
        hp_ref[pl.ds(c, tm, stride=ROW_TILE), :] = zero


def _merge_call(att, u, ga, gb, x2, mod3, cw, cb, lg, lb, wa, wc, wo, nf, wr, br, seq):
    t, d = x2.shape
    tm = TM_MERGE
    tpb = seq // tm
    hb = tm // HALO
    n_halo = t // HALO
    tri = jnp.asarray(np.tril(np.ones((tm, tm), np.float32), -1), BF16)
    row = lambda i: (i, 0)
    const = lambda i: (0, 0)
    kern = functools.partial(_merge_kernel, tiles_per_seq=tpb)
    return pl.pallas_call(
        kern,
        grid=(t // tm,),
        in_specs=[
            pl.BlockSpec((tm // LANES, D_ATT, LANES), lambda i: (i, 0, 0)),
            pl.BlockSpec((tm, D_CONV), row),
            pl.BlockSpec((HALO, D_CONV), lambda i: (jnp.maximum(i * hb - 1, 0), 0)),
            pl.BlockSpec((HALO, D_CONV), lambda i: (jnp.minimum((i + 1) * hb, n_halo - 1), 0)),
            pl.BlockSpec((tm, d), row),
            pl.BlockSpec((tm, d), row),
            pl.BlockSpec((tm, d), row),
            pl.BlockSpec((1, 6, d), lambda i: (i // tpb, 0, 0)),
            pl.BlockSpec(cw.shape, const),
            pl.BlockSpec((1, D_CONV), const),
            pl.BlockSpec((1, D_CONV), const),
            pl.BlockSpec((1, D_CONV), const),
            pl.BlockSpec(wa.shape, const),
            pl.BlockSpec(wc.shape, const),
            pl.BlockSpec(wo.shape, const),
            pl.BlockSpec((1, d), const),
            pl.BlockSpec(wr.shape, const),
            pl.BlockSpec((1, LANES), const),
            pl.BlockSpec((tm, tm), const),
        ],
        out_specs=[
            pl.BlockSpec((tm, d), row),
            pl.BlockSpec((tm * ROW_TILE, LANES), row),
            pl.BlockSpec((SUBLANES, tm), lambda i: (0, i)),
            pl.BlockSpec((SUBLANES, LANES), const),
        ],
        out_shape=[
            jax.ShapeDtypeStruct((t, d), F32),
            jax.ShapeDtypeStruct((t * ROW_TILE, LANES), U32),
            jax.ShapeDtypeStruct((SUBLANES, t), F32),
            jax.ShapeDtypeStruct((SUBLANES, LANES), F32),
        ],
        scratch_shapes=[
            pltpu.VMEM((tm + 2 * HALO, D_CONV), F32),
            pltpu.VMEM((SUBLANES, tm + 2 * HALO - SUBLANES, D_CONV), F32),
            pltpu.VMEM((tm, D_CONV), F32),
            pltpu.VMEM((SUBLANES, LANES), F32),
        ],
        compiler_params=pltpu.CompilerParams(dimension_semantics=("arbitrary",), vmem_limit_bytes=VMEM_LIMIT),
        name="merge",
    )(att, u, u, u, ga, gb, x2, mod3, cw, cb, lg, lb, wa, wc, wo, nf, wr, br, tri)


def _dispatch_kernel(dest_ref, hp_ref, zeros_hbm, hs_hbm, sem):
    del zeros_hbm
    n = dest_ref.shape[0]

    def body(o, carry):
        for k in range(DMA_UNROLL):
            t = o * DMA_UNROLL + k
            src = pl.multiple_of(t * ROW_TILE, ROW_TILE)
            dst = pl.multiple_of(dest_ref[t] * ROW_TILE, ROW_TILE)
            pltpu.make_async_copy(hp_ref.at[pl.ds(src, ROW_TILE)], hs_hbm.at[pl.ds(dst, ROW_TILE)],
                                  sem).start(priority=k % 2)
        return carry

    lax.fori_loop(0, n // DMA_UNROLL, body, 0)
    pltpu.make_async_copy(hp_ref, hs_hbm.at[pl.ds(0, n * ROW_TILE)], sem).wait()


def _dispatch_call(dest, hp, p_rows):
    t = dest.shape[0]
    zeros = jnp.zeros((p_rows * ROW_TILE, LANES), U32)
    return pl.pallas_call(
        _dispatch_kernel,
        grid=(t // TM_ROWS,),
        in_specs=[
            pl.BlockSpec((TM_ROWS,), lambda i: (i,), memory_space=pltpu.SMEM),
            pl.BlockSpec((TM_ROWS * ROW_TILE, LANES), lambda i: (i, 0)),
            pl.BlockSpec(memory_space=pl.ANY),
        ],
        out_specs=pl.BlockSpec(memory_space=pl.ANY),
        out_shape=jax.ShapeDtypeStruct((p_rows * ROW_TILE, LANES), U32),
        scratch_shapes=[pltpu.SemaphoreType.DMA(())],
        input_output_aliases={2: 0},
        compiler_params=pltpu.CompilerParams(dimension_semantics=("arbitrary",), has_side_effects=True,
                                             vmem_limit_bytes=VMEM_LIMIT),
        name="dispatch",
    )(dest, hp, zeros)


def _moe_kernel(ea_ref, eb_ref, valid_ref, hs_ref, wga_ref, wua_ref, wda_ref, wgb_ref, wub_ref, wdb_ref, y_ref):
    i = pl.program_id(0)
    blk = hs_ref.shape[0] // ROW_TILE
    d = wga_ref.shape[1]
    n_words = d // 2 // LANES

    @pl.when(valid_ref[i] == 1)
    def _():
        his, los = [], []
        for c in range(n_words):
            w = hs_ref[pl.ds(c, blk, stride=ROW_TILE), :]
            his.append(lax.bitcast_convert_type(w & jnp.uint32(0xFFFF0000), F32).astype(BF16))
            los.append(lax.bitcast_convert_type(w << 16, F32).astype(BF16))
        x = jnp.concatenate(his + los, axis=-1)
        gates = lax.bitcast_convert_type(hs_ref[pl.ds(n_words, blk, stride=ROW_TILE), :], F32)
        gate_a = gates[:, 0:1]
        gate_b = gates[:, 1:2]

        def mlp(wg, wu, wd):
            g = jnp.dot(x, wg[0], preferred_element_type=F32)
            u = jnp.dot(x, wu[0], preferred_element_type=F32)
            a = (g * _sigmoid(g) * u).astype(BF16)
            return jnp.dot(a, wd[0], preferred_element_type=F32)

        y = gate_a * mlp(wga_ref, wua_ref, wda_ref) + gate_b * mlp(wgb_ref, wub_ref, wdb_ref)
        for c in range(d // LANES):
            y_ref[pl.ds(c, blk, stride=ROW_TILE), :] = y[:, c * LANES:(c + 1) * LANES]

    @pl.when(valid_ref[i] != 1)
    def _():
        y_ref[...] = jnp.zeros_like(y_ref)


def _moe_call(blk_ea, blk_eb, blk_valid, hs, wg, wu, wd):
    nb = blk_ea.shape[0]
    d = wg.shape[1]
    de = wg.shape[2]
    rows = MOE_BLOCK * ROW_TILE
    tok = pl.BlockSpec((rows, LANES), lambda i, ea, eb, va: (i, 0))
    w_a = lambda shape: pl.BlockSpec((1,) + shape, lambda i, ea, eb, va: (ea[i], 0, 0))
    w_b = lambda shape: pl.BlockSpec((1,) + shape, lambda i, ea, eb, va: (eb[i], 0, 0))
    grid_spec = pltpu.PrefetchScalarGridSpec(
        num_scalar_prefetch=3,
        grid=(nb,),
        in_specs=[tok, w_a((d, de)), w_a((d, de)), w_a((de, d)), w_b((d, de)), w_b((d, de)), w_b((de, d))],
        out_specs=tok,
    )
    return pl.pallas_call(
        _moe_kernel,
        grid_spec=grid_spec,
        out_shape=jax.ShapeDtypeStruct((nb * rows, LANES), F32),
        compiler_params=pltpu.CompilerParams(dimension_semantics=("arbitrary",), vmem_limit_bytes=VMEM_LIMIT),
        name="moe",
    )(blk_ea, blk_eb, blk_valid, hs, wg, wu, wd, wg, wu, wd)


def _final_kernel(dest_ref, xmid_ref, mod_ref, fn_ref, y_hbm, o_ref, fbuf, sem):
    n = dest_ref.shape[0]
    d = xmid_ref.shape[1]

    def body(o, carry):
        for k in range(DMA_UNROLL):
            t = o * DMA_UNROLL + k
            src = pl.multiple_of(dest_ref[t] * ROW_TILE, ROW_TILE)
            dst = pl.multiple_of(t * ROW_TILE, ROW_TILE)
            pltpu.make_async_copy(y_hbm.at[pl.ds(src, ROW_TILE)], fbuf.at[pl.ds(dst, ROW_TILE)],
                                  sem).start(priority=k % 2)
        return carry

    lax.fori_loop(0, n // DMA_UNROLL, body, 0)
    pltpu.make_async_copy(y_hbm.at[pl.ds(0, n * ROW_TILE)], fbuf, sem).wait()
    f = jnp.concatenate([fbuf[pl.ds(c, n, stride=ROW_TILE), :] for c in range(d // LANES)], axis=-1)
    x = xmid_ref[...] + mod_ref[0][5:6] * f
    ms = jnp.mean(x * x, axis=-1, keepdims=True)
    o_ref[...] = x * lax.rsqrt(ms + NORM_EPS) * fn_ref[...]


def _final_call(dest, xmid, mod3, fn, y, seq):
    t, d = xmid.shape
    tm = TM_ROWS
    tpb = seq // tm
    return pl.pallas_call(
        _final_kernel,
        grid=(t // tm,),
        in_specs=[
            pl.BlockSpec((tm,), lambda i: (i,), memory_space=pltpu.SMEM),
            pl.BlockSpec((tm, d), lambda i: (i, 0)),
            pl.BlockSpec((1, 6, d), lambda i: (i // tpb, 0, 0)),
            pl.BlockSpec((1, d), lambda i: (0, 0)),
            pl.BlockSpec(memory_space=pl.ANY),
        ],
        out_specs=pl.BlockSpec((tm, d), lambda i: (i, 0)),
        out_shape=jax.ShapeDtypeStruct((t, d), F32),
        scratch_shapes=[pltpu.VMEM((tm * ROW_TILE, LANES), F32), pltpu.SemaphoreType.DMA(())],
        compiler_params=pltpu.CompilerParams(dimension_semantics=("arbitrary",), vmem_limit_bytes=VMEM_LIMIT),
        name="final",
    )(dest, xmid, mod3, fn, y)


def _pair_tables():
    ea, eb = [], []
    for g in range(N_GROUPS):
        for a in range(EXPERTS_PER_GROUP):
            for b in range(a + 1, EXPERTS_PER_GROUP):
                ea.append(g * EXPERTS_PER_GROUP + a)
                eb.append(g * EXPERTS_PER_GROUP + b)
    return np.asarray(ea, np.int32), np.asarray(eb, np.int32)


def _routing_plan(info, counts, t):
    cls = info[0].astype(jnp.int32)
    rank = info[1].astype(jnp.int32)
    cnt = counts[0, :N_CLASSES].astype(jnp.int32)
    padded = (cnt + MOE_BLOCK - 1) // MOE_BLOCK * MOE_BLOCK
    pad_end = jnp.cumsum(padded)
    pad_start = pad_end - padded
    cls_ids = jnp.arange(N_CLASSES, dtype=jnp.int32)
    dest = rank + jnp.sum(jnp.where(cls[:, None] == cls_ids[None, :], pad_start[None, :], 0), axis=1)
    nb = t // MOE_BLOCK + N_CLASSES
    nb_used = pad_end[-1] // MOE_BLOCK
    blk = jnp.arange(nb, dtype=jnp.int32)
    blk_cls = jnp.clip(jnp.searchsorted(pad_end, blk * MOE_BLOCK, side="right"), 0, N_CLASSES - 1)
    valid = blk < nb_used
    last_cls = blk_cls[jnp.maximum(nb_used - 1, 0)]
    blk_cls = jnp.where(valid, blk_cls, last_cls).astype(jnp.int32)
    tab_a, tab_b = _pair_tables()
    blk_ea = jnp.asarray(tab_a)[blk_cls]
    blk_eb = jnp.asarray(tab_b)[blk_cls]
    return dest, blk_ea, blk_eb, valid.astype(jnp.int32), nb


def kernel(x, c, ctx, c_ctx, w_mod, b_mod, norm_mix, w_in, rpb, w_att_out, conv_w, conv_b, conv_ln_g, conv_ln_b,
           w_conv_out, w_o, norm_ffn, w_router_group, b_router_group, w_router_expert, b_router_expert,
           w_exp_gate, w_exp_up, w_exp_down, final_norm):
    b, seq, d = x.shape
    n_ctx = ctx.shape[1]
    t = b * seq
    assert w_mod.shape[0] == 1, "single layer"
    assert seq % TM_ROWS == 0 and seq // GRID_W >= WIN_ROWS and n_ctx % LANES == 0 and (b * n_ctx) % 512 == 0
    assert d == 1024

    mod_rows = -(-(b + 1) // SUBLANES) * SUBLANES
    cc = jnp.zeros((mod_rows, d), F32).at[:b].set(c).at[b].set(c_ctx)
    m_all = _mod_call(cc, w_mod[0], b_mod[0][None, :])
    mod_lat = m_all[:b].reshape(b, 6, d)
    mod_ctx = m_all[b:b + 1].reshape(1, 6, d)

    x2 = x.reshape(t, d)
    g_mix = norm_mix[0][None, :]
    w_in_b = w_in[0].astype(BF16)
    wqv_t = jnp.concatenate([w_in_b[:, Q0:K0], w_in_b[:, V0:GLU0]], axis=1).T
    w_rest = jnp.concatenate([w_in_b[:, K0:V0], w_in_b[:, GLU0:]], axis=1)
    qt, k, vt, u, ga, gb = _proj_call(x2, mod_lat, g_mix, wqv_t, w_rest, seq)
    kc, vct = _ctx_kv_call(ctx.reshape(b * n_ctx, d), mod_ctx, g_mix, w_in_b[:, K0:V0], w_in_b[:, V0:GLU0].T)

    att = _attn_call(qt, k, vt, kc, vct, _bias_table(rpb[0]), seq, n_ctx)

    cw = jnp.zeros((32, D_CONV), F32).at[:CONV_WIDTH].set(conv_w[0])
    wr = jnp.zeros((d, LANES), F32).at[:, :N_GROUPS].set(w_router_group[0])
    wr = wr.at[:, N_GROUPS:N_GROUPS + N_EXPERTS].set(w_router_expert[0])
    wr_hi = wr.astype(BF16)
    wr = jnp.concatenate([wr_hi, (wr - wr_hi.astype(F32)).astype(BF16)], axis=1)
    br = jnp.zeros((1, LANES), F32).at[0, :N_GROUPS].set(b_router_group[0])
    br = br.at[0, N_GROUPS:N_GROUPS + N_EXPERTS].set(b_router_expert[0])
    x_mid, hp, info, counts = _merge_call(
        att, u, ga, gb, x2, mod_lat, cw, conv_b[0][None, :], conv_ln_g[0][None, :], conv_ln_b[0][None, :],
        w_att_out[0].astype(BF16), w_conv_out[0].astype(BF16), w_o[0].astype(BF16), norm_ffn[0][None, :],
        wr, br, seq)

    dest, blk_ea, blk_eb, blk_valid, nb = _routing_plan(info, counts, t)
    hs = _dispatch_call(dest, hp, nb * MOE_BLOCK)
    y = _moe_call(blk_ea, blk_eb, blk_valid, hs,
                  w_exp_gate[0].astype(BF16), w_exp_up[0].astype(BF16), w_exp_down[0].astype(BF16))
    out = _final_call(dest, x_mid, mod_lat, final_norm[None, :], y, seq)
    return out.reshape(b, seq, d)
```

```python
import functools

import numpy as np
import jax
import jax.numpy as jnp
from jax import lax
from jax.experimental import pallas as pl
from jax.experimental.pallas import tpu as pltpu

F32 = jnp.float32
BF16 = jnp.bfloat16
U32 = jnp.uint32

GRID_W = 64
N_HEADS = 8
HEAD_DIM = 64
D_ATT = N_HEADS * HEAD_DIM
WIN_ROWS = 8
WIN_COLS = 16
D_CONV = 512
CONV_WIDTH = 31
N_GROUPS = 4
EXPERTS_PER_GROUP = 8
N_EXPERTS = N_GROUPS * EXPERTS_PER_GROUP
PAIRS_PER_GROUP = EXPERTS_PER_GROUP * (EXPERTS_PER_GROUP - 1) // 2
N_CLASSES = N_GROUPS * PAIRS_PER_GROUP
NORM_EPS = 1e-6
NEG_INF = -1e30

LANES = 128
SUBLANES = 8
ROW_TILE = SUBLANES
HALO = 16
VMEM_LIMIT = 56 * 1024 * 1024

TM_PROJ = 512
TM_MERGE = 512
MERGE_PARTS = 1
TM_ROWS = 1024
MOE_BLOCK = 256
CONV_CHUNK = 64
DMA_UNROLL = 8

HIGHEST = lax.Precision.HIGHEST


def _norm_mod(x, g, shift, scale):
    ms = jnp.mean(x * x, axis=-1, keepdims=True)
    y = x * lax.rsqrt(ms + NORM_EPS) * g
    return y * (1.0 + scale) + shift


def _sigmoid(x):
    return jax.nn.sigmoid(x)


def _mod_kernel(c_ref, w_ref, b_ref, o_ref):
    c = c_ref[...]
    s = c * _sigmoid(c)
    o_ref[...] = jnp.dot(s, w_ref[...], precision=HIGHEST, preferred_element_type=F32) + b_ref[...]


def _mod_call(cc, w_mod, b_mod):
    rows, d = cc.shape
    n = w_mod.shape[1]
    tn = 1024
    return pl.pallas_call(
        _mod_kernel,
        grid=(n // tn,),
        in_specs=[
            pl.BlockSpec((rows, d), lambda j: (0, 0)),
            pl.BlockSpec((d, tn), lambda j: (0, j)),
            pl.BlockSpec((1, tn), lambda j: (0, j)),
        ],
        out_specs=pl.BlockSpec((rows, tn), lambda j: (0, j)),
        out_shape=jax.ShapeDtypeStruct((rows, n), F32),
        compiler_params=pltpu.CompilerParams(dimension_semantics=("arbitrary",), vmem_limit_bytes=VMEM_LIMIT),
        name="mod",
    )(cc, w_mod, b_mod)


Q0, K0, V0, GLU0 = 0, D_ATT, 2 * D_ATT, 3 * D_ATT
GA0 = GLU0 + 2 * D_CONV


NT_DIMS = (((1,), (1,)), ((), ()))


def _store_transposed(res_t, ref, scale=None):
    for j in range(ref.shape[0]):
        blk = res_t[:, j * LANES:(j + 1) * LANES]
        if scale is not None:
            blk = blk * scale
        ref[j] = blk.astype(ref.dtype)


def _conv_kernel(u_ref, up_ref, un_ref, cw_ref, cb_ref, lg_ref, lb_ref, yn_ref, ubuf, shifted,
                 *, tiles_per_seq):
    i = pl.program_id(0)
    tm = u_ref.shape[0]
    first = (i % tiles_per_seq) == 0
    last = (i % tiles_per_seq) == tiles_per_seq - 1
    ubuf[0:HALO, :] = jnp.where(first, 0.0, up_ref[...].astype(F32))
    for r0 in range(0, tm, CONV_CHUNK):
        ubuf[HALO + r0:HALO + r0 + CONV_CHUNK, :] = u_ref[r0:r0 + CONV_CHUNK, :].astype(F32)
    ubuf[HALO + tm:, :] = jnp.where(last, 0.0, un_ref[...].astype(F32))

    span = shifted.shape[1]
    for ph in range(SUBLANES):
        for r0 in range(0, span, CONV_CHUNK):
            n = min(CONV_CHUNK, span - r0)
            shifted[ph, r0:r0 + n, :] = ubuf[ph + r0:ph + r0 + n, :]

    base = HALO - CONV_WIDTH // 2
    inv_c = 1.0 / D_CONV
    for tc in range(tm // CONV_CHUNK):
        accs = []
        for lc in range(D_CONV // LANES):
            ls = slice(lc * LANES, (lc + 1) * LANES)
            acc = jnp.zeros((CONV_CHUNK, LANES), F32)
            for j in range(CONV_WIDTH):
                tiles, ph = divmod(base + j, SUBLANES)
                lo = tc * CONV_CHUNK + tiles * SUBLANES
                acc = acc + shifted[ph, lo:lo + CONV_CHUNK, ls] * cw_ref[j:j + 1, ls]
            accs.append(acc + cb_ref[:, ls])
        mu = sum(jnp.sum(a, axis=-1, keepdims=True) for a in accs) * inv_c
        cen = [a - mu for a in accs]
        var = sum(jnp.sum(c * c, axis=-1, keepdims=True) for c in cen) * inv_c
        rstd = lax.rsqrt(var + NORM_EPS)
        for lc, c in enumerate(cen):
            ls = slice(lc * LANES, (lc + 1) * LANES)
            yn = c * rstd * lg_ref[:, ls] + lb_ref[:, ls]
            yn_ref[tc * CONV_CHUNK:(tc + 1) * CONV_CHUNK, ls] = (yn * _sigmoid(yn)).astype(yn_ref.dtype)


def _conv_call(u, cw, cb, lg, lb, seq):
    t = u.shape[0]
    tm = TM_MERGE
    tpb = seq // tm
    hb = tm // HALO
    n_halo = t // HALO
    row = lambda i: (i, 0)
    const = lambda i: (0, 0)
    return pl.pallas_call(
        functools.partial(_conv_kernel, tiles_per_seq=tpb),
        grid=(t // tm,),
        in_specs=[
            pl.BlockSpec((tm, D_CONV), row),
            pl.BlockSpec((HALO, D_CONV), lambda i: (jnp.maximum(i * hb - 1, 0), 0)),
            pl.BlockSpec((HALO, D_CONV), lambda i: (jnp.minimum((i + 1) * hb, n_halo - 1), 0)),
            pl.BlockSpec(cw.shape, const),
            pl.BlockSpec((1, D_CONV), const),
            pl.BlockSpec((1, D_CONV), const),
            pl.BlockSpec((1, D_CONV), const),
        ],
        out_specs=pl.BlockSpec((tm, D_CONV), row),
        out_shape=jax.ShapeDtypeStruct((t, D_CONV), BF16),
        scratch_shapes=[
            pltpu.VMEM((tm + 2 * HALO, D_CONV), F32),
            pltpu.VMEM((SUBLANES, tm + 2 * HALO - SUBLANES, D_CONV), F32),
        ],
        compiler_params=pltpu.CompilerParams(dimension_semantics=("arbitrary",), vmem_limit_bytes=VMEM_LIMIT),
        name="conv",
    )(u, u, u, cw, cb, lg, lb)


def _proj_kernel(x_ref, mod_ref, g_ref, wqv_ref, w_ref, qt_ref, k_ref, vt_ref, u_ref, ga_ref, gb_ref):
    d = x_ref.shape[1]
    mod = mod_ref[0]
    h = _norm_mod(x_ref[...], g_ref[...], mod[0:1], mod[1:2]).astype(BF16)

    qv_t = lax.dot_general(wqv_ref[...], h, NT_DIMS, preferred_element_type=F32)
    _store_transposed(qv_t[:D_ATT], qt_ref, HEAD_DIM ** -0.5)
    _store_transposed(qv_t[D_ATT:], vt_ref)

    def seg(lo, hi):
        return jnp.dot(h, w_ref[:, lo:hi], preferred_element_type=F32)

    k_ref[...] = seg(0, D_ATT).astype(BF16)
    a = seg(D_ATT, D_ATT + D_CONV)
    g = seg(D_ATT + D_CONV, D_ATT + 2 * D_CONV)
    u_ref[...] = (a * _sigmoid(g)).astype(BF16)
    g0 = D_ATT + 2 * D_CONV
    ga_ref[...] = _sigmoid(seg(g0, g0 + d)).astype(BF16)
    gb_ref[...] = _sigmoid(seg(g0 + d, g0 + 2 * d)).astype(BF16)


def _proj_call(x2, mod3, g, wqv_t, w_rest, seq):
    t, d = x2.shape
    tm = TM_PROJ
    tpb = seq // tm
    nblk = tm // LANES
    outs = [
        jax.ShapeDtypeStruct((t // LANES, D_ATT, LANES), BF16),
        jax.ShapeDtypeStruct((t, D_ATT), BF16),
        jax.ShapeDtypeStruct((t // LANES, D_ATT, LANES), BF16),
        jax.ShapeDtypeStruct((t, D_CONV), BF16),
        jax.ShapeDtypeStruct((t, d), BF16),
        jax.ShapeDtypeStruct((t, d), BF16),
    ]
    row = lambda i: (i, 0)
    blk3 = lambda i: (i, 0, 0)
    return pl.pallas_call(
        _proj_kernel,
        grid=(t // tm,),
        in_specs=[
            pl.BlockSpec((tm, d), row),
            pl.BlockSpec((1, 6, d), lambda i: (i // tpb, 0, 0)),
            pl.BlockSpec((1, d), lambda i: (0, 0)),
            pl.BlockSpec(wqv_t.shape, lambda i: (0, 0)),
            pl.BlockSpec(w_rest.shape, lambda i: (0, 0)),
        ],
        out_specs=[
            pl.BlockSpec((nblk, D_ATT, LANES), blk3),
            pl.BlockSpec((tm, D_ATT), row),
            pl.BlockSpec((nblk, D_ATT, LANES), blk3),
            pl.BlockSpec((tm, D_CONV), row),
            pl.BlockSpec((tm, d), row),
            pl.BlockSpec((tm, d), row),
        ],
        out_shape=outs,
        compiler_params=pltpu.CompilerParams(dimension_semantics=("arbitrary",), vmem_limit_bytes=VMEM_LIMIT),
        name="proj",
    )(x2, mod3, g, wqv_t, w_rest)


def _ctx_kv_kernel(x_ref, mod_ref, g_ref, wk_ref, wvt_ref, k_ref, vt_ref):
    mod = mod_ref[0]
    h = _norm_mod(x_ref[...], g_ref[...], mod[0:1], mod[1:2]).astype(BF16)
    k_ref[...] = jnp.dot(h, wk_ref[...], preferred_element_type=F32).astype(BF16)
    _store_transposed(lax.dot_general(wvt_ref[...], h, NT_DIMS, preferred_element_type=F32), vt_ref)


def _ctx_kv_call(c2, mod3, g, w_k, w_vt):
    t, d = c2.shape
    tm = 512
    nblk = tm // LANES
    row = lambda i: (i, 0)
    return pl.pallas_call(
        _ctx_kv_kernel,
        grid=(t // tm,),
        in_specs=[
            pl.BlockSpec((tm, d), row),
            pl.BlockSpec((1, 6, d), lambda i: (0, 0, 0)),
            pl.BlockSpec((1, d), lambda i: (0, 0)),
            pl.BlockSpec(w_k.shape, lambda i: (0, 0)),
            pl.BlockSpec(w_vt.shape, lambda i: (0, 0)),
        ],
        out_specs=[pl.BlockSpec((tm, D_ATT), row), pl.BlockSpec((nblk, D_ATT, LANES), lambda i: (i, 0, 0))],
        out_shape=[jax.ShapeDtypeStruct((t, D_ATT), BF16), jax.ShapeDtypeStruct((t // LANES, D_ATT, LANES), BF16)],
        compiler_params=pltpu.CompilerParams(dimension_semantics=("arbitrary",), vmem_limit_bytes=VMEM_LIMIT),
        name="ctx_kv",
    )(c2, mod3, g, w_k, w_vt)


HEADS_PER_GROUP = 4
GROUP_W = HEADS_PER_GROUP * HEAD_DIM
N_HEAD_GROUPS = N_HEADS // HEADS_PER_GROUP


def _attn_kernel(qt_ref, k_ref, vt_ref, kc_ref, vct_ref, bias_ref, o_ref, vboth):
    nblk = qt_ref.shape[0]
    rows = 2 * nblk
    n_loc = WIN_ROWS * GRID_W
    n_ctx_blk = vct_ref.shape[0]
    half = GRID_W

    for j in range(nblk):
        vboth[0, j] = vt_ref[j]
    for j in range(nblk - 1):
        vboth[1, j] = jnp.concatenate([vt_ref[j][:, half:], vt_ref[j + 1][:, :half]], axis=1)
    vboth[1, nblk - 1] = jnp.zeros_like(vt_ref[0])

    lane = lax.broadcasted_iota(jnp.int32, (D_ATT, LANES), 1)
    low = lane < half
    rblk = lax.broadcasted_iota(jnp.int32, (GROUP_W, GROUP_W), 0) // HEAD_DIM
    cblk = lax.broadcasted_iota(jnp.int32, (GROUP_W, GROUP_W), 1) // HEAD_DIM
    diag = rblk == cblk
    low64 = lax.broadcasted_iota(jnp.int32, (HEAD_DIM, LANES), 1) < half
    kc = kc_ref[...]

    def one_row(row, tiled, side):
        r_start = jnp.clip(row - WIN_ROWS // 2, 0, rows - WIN_ROWS)
        pat = row - r_start
        kw = k_ref[pl.ds(pl.multiple_of(r_start * GRID_W, GRID_W), n_loc), :]
        vwin = vboth[r_start % 2, pl.ds(r_start // 2, n_loc // LANES)]
        parts = []
        for g in range(N_HEAD_GROUPS):
            fs = slice(g * GROUP_W, (g + 1) * GROUP_W)
            tg = tiled[fs, :]
            w = jnp.where(diag, jnp.concatenate([tg, tg], axis=1), jnp.zeros((), BF16))
            s_loc = jnp.dot(kw[:, fs], w, preferred_element_type=F32) + bias_ref[pat, g]
            s_ctx = jnp.dot(kc[:, fs], w, preferred_element_type=F32)
            m = jnp.maximum(jnp.max(s_loc, axis=0, keepdims=True), jnp.max(s_ctx, axis=0, keepdims=True))
            p_loc = jnp.exp(s_loc - m)
            p_ctx = jnp.exp(s_ctx - m)
            l = jnp.sum(p_loc, axis=0, keepdims=True) + jnp.sum(p_ctx, axis=0, keepdims=True)
            p = jnp.concatenate([p_loc.astype(BF16), p_ctx.astype(BF16)], axis=0)
            vt_g = jnp.concatenate([vwin[j][fs, :] for j in range(n_loc // LANES)]
                                   + [vct_ref[j][fs, :] for j in range(n_ctx_blk)], axis=1)
            o_t = jnp.dot(vt_g, p, preferred_element_type=F32) * (1.0 / l)
            for j in range(HEADS_PER_GROUP):
                blk = o_t[j * HEAD_DIM:(j + 1) * HEAD_DIM, (j // 2) * LANES:(j // 2 + 1) * LANES]
                if j % 2 != side:
                    blk = pltpu.roll(blk, half, axis=1)
                parts.append(blk)
        return parts

    def body(i, carry):
        xq = qt_ref[i]
        rolled = jnp.concatenate([xq[:, half:], xq[:, :half]], axis=1)
        parts_a = one_row(2 * i, jnp.where(low, xq, rolled), 0)
        parts_b = one_row(2 * i + 1, jnp.where(low, rolled, xq), 1)
        out = jnp.concatenate([jnp.where(low64, a, b) for a, b in zip(parts_a, parts_b)], axis=0)
        o_ref[i] = out.astype(o_ref.dtype)
        return carry

    lax.fori_loop(0, nblk, body, 0, unroll=2)


def _attn_call(qt, k, vt, kc, vct, bias, seq, n_ctx):
    t = k.shape[0]
    b = t // seq
    nblk = seq // LANES
    lat3 = pl.BlockSpec((nblk, D_ATT, LANES), lambda i: (i, 0, 0))
    return pl.pallas_call(
        _attn_kernel,
        grid=(b,),
        in_specs=[lat3,
                  pl.BlockSpec((seq, D_ATT), lambda i: (i, 0)),
                  lat3,
                  pl.BlockSpec((n_ctx, D_ATT), lambda i: (i, 0)),
                  pl.BlockSpec((n_ctx // LANES, D_ATT, LANES), lambda i: (i, 0, 0)),
                  pl.BlockSpec(bias.shape, lambda i: (0, 0, 0, 0))],
        out_specs=lat3,
        out_shape=jax.ShapeDtypeStruct((t // LANES, D_ATT, LANES), BF16),
        scratch_shapes=[pltpu.VMEM((2, nblk, D_ATT, LANES), BF16)],
        compiler_params=pltpu.CompilerParams(dimension_semantics=("arbitrary",), vmem_limit_bytes=VMEM_LIMIT),
        name="attn",
    )(qt, k, vt, kc, vct, bias)


def _bias_table(rpb):
    cq = np.arange(GRID_W)[:, None]
    ck = np.arange(GRID_W)[None, :]
    c_start = np.clip(cq - WIN_COLS // 2, 0, GRID_W - WIN_COLS)
    col_mask = (ck >= c_start) & (ck < c_start + WIN_COLS)
    dc_idx = np.clip(ck - cq + WIN_COLS - 1, 0, 2 * WIN_COLS - 2)
    pat = np.arange(WIN_ROWS)[:, None]
    kr = np.arange(WIN_ROWS)[None, :]
    dr_idx = kr - pat + WIN_ROWS - 1
    tab = rpb[:, dr_idx][:, :, :, dc_idx]
    tab = jnp.where(col_mask[None, None, None], tab, NEG_INF)
    tab = tab.reshape(N_HEAD_GROUPS, HEADS_PER_GROUP, WIN_ROWS, WIN_ROWS, GRID_W, GRID_W)
    tab = tab.transpose(2, 0, 3, 5, 1, 4)
    return tab.reshape(WIN_ROWS, N_HEAD_GROUPS, WIN_ROWS * GRID_W, GROUP_W).astype(F32)


def _merge_rows(r0, n, att_ref, yn_ref, ga_ref, gb_ref, x_ref, mod,
                wa_ref, wc_ref, wo_ref, nf_ref, wr_ref, br_ref, xmid_ref, hp_ref):
    d = x_ref.shape[1]
    rs = slice(r0, r0 + n)
    y_conv = jnp.dot(yn_ref[rs, :], wc_ref[...], preferred_element_type=F32)

    att = jnp.concatenate([att_ref[j].astype(F32).T for j in range(r0 // LANES, (r0 + n) // LANES)], axis=0)
    y_att = jnp.dot(att.astype(BF16), wa_ref[...], preferred_element_type=F32)
    mix = ga_ref[rs, :].astype(F32) * y_att + gb_ref[rs, :].astype(F32) * y_conv
    y = jnp.dot(mix.astype(BF16), wo_ref[...], preferred_element_type=F32)
    x_mid = x_ref[rs, :] + mod[2:3] * y
    xmid_ref[rs, :] = x_mid
    h2 = _norm_mod(x_mid, nf_ref[...], mod[3:4], mod[4:5])

    h2_hi = h2.astype(BF16)
    h2_lo = (h2 - h2_hi.astype(F32)).astype(BF16)
    t_hi = jnp.dot(h2_hi, wr_ref[...], preferred_element_type=F32)
    t_lo = jnp.dot(h2_lo, wr_ref[:, :LANES], preferred_element_type=F32)
    logits = t_hi[:, :LANES] + t_hi[:, LANES:] + t_lo + br_ref[...]
    lane = lax.broadcasted_iota(jnp.int32, logits.shape, 1).astype(F32)
    big = float(LANES)
    is_g = lane < N_GROUPS
    gl = jnp.where(is_g, logits, -jnp.inf)
    gmax = jnp.max(gl, axis=-1, keepdims=True)
    g_idx = jnp.min(jnp.where(gl == gmax, lane, big), axis=-1, keepdims=True)
    p_group = 1.0 / jnp.sum(jnp.where(is_g, jnp.exp(gl - gmax), 0.0), axis=-1, keepdims=True)
    e_lo = N_GROUPS + EXPERTS_PER_GROUP * g_idx
    in_grp = (lane >= e_lo) & (lane < e_lo + EXPERTS_PER_GROUP)
    el = jnp.where(in_grp, logits, -jnp.inf)
    v1 = jnp.max(el, axis=-1, keepdims=True)
    i1 = jnp.min(jnp.where(el == v1, lane, big), axis=-1, keepdims=True)
    el2 = jnp.where(lane == i1, -jnp.inf, el)
    v2 = jnp.max(el2, axis=-1, keepdims=True)
    i2 = jnp.min(jnp.where(el2 == v2, lane, big), axis=-1, keepdims=True)
    e2 = jnp.exp(v2 - v1)
    gate1 = p_group / (1.0 + e2)
    gate2 = p_group * e2 / (1.0 + e2)
    j1 = i1 - e_lo
    j2 = i2 - e_lo
    ja = jnp.minimum(j1, j2)
    jb = jnp.maximum(j1, j2)
    gate_a = jnp.where(j1 < j2, gate1, gate2)
    gate_b = jnp.where(j1 < j2, gate2, gate1)
    pair = ja * (2 * EXPERTS_PER_GROUP - 1 - ja) * 0.5 + (jb - ja - 1.0)
    cls = g_idx * PAIRS_PER_GROUP + pair

    half = d // 2
    hi = lax.bitcast_convert_type(h2[:, :half].astype(BF16).astype(F32), U32)
    lo = lax.bitcast_convert_type(h2[:, half:].astype(BF16).astype(F32), U32)
    words = hi | (lo >> 16)
    n_words = half // LANES
    base = r0 * ROW_TILE
    for c in range(n_words):
        hp_ref[pl.ds(base + c, n, stride=ROW_TILE), :] = words[:, c * LANES:(c + 1) * LANES]
    gates = jnp.where(lane == 0.0, gate_a, jnp.where(lane == 1.0, gate_b, 0.0))
    hp_ref[pl.ds(base + n_words, n, stride=ROW_TILE), :] = lax.bitcast_convert_type(gates, U32)
    zero = jnp.zeros((n, LANES), U32)
    for c in range(n_words + 1, ROW_TILE):
        hp_ref[pl.ds(base + c, n, stride=ROW_TILE), :] = zero
    return cls


def _merge_kernel(att_ref, yn_ref, ga_ref, gb_ref, x_ref, mod_ref,
                  wa_ref, wc_ref, wo_ref, nf_ref, wr_ref, br_ref, tri_ref,
                  xmid_ref, hp_ref, info_ref, cnt_ref, run_ref):
    i = pl.program_id(0)
    tm = x_ref.shape[0]

    @pl.when(i == 0)
    def _():
        run_ref[...] = jnp.zeros_like(run_ref)

    mod = mod_ref[0]
    n = tm // MERGE_PARTS
    cls = jnp.concatenate(
        [_merge_rows(p * n, n, att_ref, yn_ref, ga_ref, gb_ref, x_ref, mod, wa_ref, wc_ref, wo_ref, nf_ref,
                     wr_ref, br_ref, xmid_ref, hp_ref) for p in range(MERGE_PARTS)], axis=0)

    lane = lax.broadcasted_iota(jnp.int32, (tm, LANES), 1).astype(F32)
    onehot = lane == cls
    oh_f = jnp.where(onehot, 1.0, 0.0)
    before = jnp.dot(tri_ref[...], oh_f.astype(BF16), preferred_element_type=F32)
    run = run_ref[0:1, :]
    rank = jnp.sum(oh_f * (before + run), axis=-1, keepdims=True)
    new_run = run + jnp.sum(oh_f, axis=0, keepdims=True)
    run_ref[...] = jnp.broadcast_to(new_run, run_ref.shape)
    cnt_ref[...] = jnp.broadcast_to(new_run, cnt_ref.shape)

    info = jnp.where(lane == 0.0, cls, jnp.where(lane == 1.0, rank, 0.0))
    info_ref[...] = info.T[0:SUBLANES, :]


def _merge_call(att, yn, ga, gb, x2, mod3, wa, wc, wo, nf, wr, br, seq):
    t, d = x2.shape
    tm = TM_MERGE
    tpb = seq // tm
    tri = jnp.asarray(np.tril(np.ones((tm, tm), np.float32), -1), BF16)
    row = lambda i: (i, 0)
    const = lambda i: (0, 0)
    return pl.pallas_call(
        _merge_kernel,
        grid=(t // tm,),
        in_specs=[
            pl.BlockSpec((tm // LANES, D_ATT, LANES), lambda i: (i, 0, 0)),
            pl.BlockSpec((tm, D_CONV), row),
            pl.BlockSpec((tm, d), row),
            pl.BlockSpec((tm, d), row),
            pl.BlockSpec((tm, d), row),
            pl.BlockSpec((1, 6, d), lambda i: (i // tpb, 0, 0)),
            pl.BlockSpec(wa.shape, const),
            pl.BlockSpec(wc.shape, const),
            pl.BlockSpec(wo.shape, const),
            pl.BlockSpec((1, d), const),
            pl.BlockSpec(wr.shape, const),
            pl.BlockSpec((1, LANES), const),
            pl.BlockSpec((tm, tm), const),
        ],
        out_specs=[
            pl.BlockSpec((tm, d), row),
            pl.BlockSpec((tm * ROW_TILE, LANES), row),
            pl.BlockSpec((SUBLANES, tm), lambda i: (0, i)),
            pl.BlockSpec((SUBLANES, LANES), const),
        ],
        out_shape=[
            jax.ShapeDtypeStruct((t, d), F32),
            jax.ShapeDtypeStruct((t * ROW_TILE, LANES), U32),
            jax.ShapeDtypeStruct((SUBLANES, t), F32),
            jax.ShapeDtypeStruct((SUBLANES, LANES), F32),
        ],
        scratch_shapes=[pltpu.VMEM((SUBLANES, LANES), F32)],
        compiler_params=pltpu.CompilerParams(dimension_semantics=("arbitrary",), vmem_limit_bytes=VMEM_LIMIT),
        name="merge",
    )(att, yn, ga, gb, x2, mod3, wa, wc, wo, nf, wr, br, tri)


def _dispatch_kernel(dest_ref, hp_ref, zeros_hbm, hs_hbm, sem):
    del zeros_hbm
    n = dest_ref.shape[0]

    def body(o, carry):
        for k in range(DMA_UNROLL):
            t = o * DMA_UNROLL + k
            src = pl.multiple_of(t * ROW_TILE, ROW_TILE)
            dst = pl.multiple_of(dest_ref[t] * ROW_TILE, ROW_TILE)
            pltpu.make_async_copy(hp_ref.at[pl.ds(src, ROW_TILE)], hs_hbm.at[pl.ds(dst, ROW_TILE)],
                                  sem).start(priority=k % 2)
        return carry

    lax.fori_loop(0, n // DMA_UNROLL, body, 0)
    pltpu.make_async_copy(hp_ref, hs_hbm.at[pl.ds(0, n * ROW_TILE)], sem).wait()


def _dispatch_call(dest, hp, p_rows):
    t = dest.shape[0]
    zeros = jnp.zeros((p_rows * ROW_TILE, LANES), U32)
    return pl.pallas_call(
        _dispatch_kernel,
        grid=(t // TM_ROWS,),
        in_specs=[
            pl.BlockSpec((TM_ROWS,), lambda i: (i,), memory_space=pltpu.SMEM),
            pl.BlockSpec((TM_ROWS * ROW_TILE, LANES), lambda i: (i, 0)),
            pl.BlockSpec(memory_space=pl.ANY),
        ],
        out_specs=pl.BlockSpec(memory_space=pl.ANY),
        out_shape=jax.ShapeDtypeStruct((p_rows * ROW_TILE, LANES), U32),
        scratch_shapes=[pltpu.SemaphoreType.DMA(())],
        input_output_aliases={2: 0},
        compiler_params=pltpu.CompilerParams(dimension_semantics=("arbitrary",), has_side_effects=True,
                                             vmem_limit_bytes=VMEM_LIMIT),
        name="dispatch",
    )(dest, hp, zeros)


def _moe_kernel(ea_ref, eb_ref, valid_ref, hs_ref, wga_ref, wua_ref, wda_ref, wgb_ref, wub_ref, wdb_ref, y_ref):
    i = pl.program_id(0)
    blk = hs_ref.shape[0] // ROW_TILE
    d = wga_ref.shape[1]
    n_words = d // 2 // LANES

    @pl.when(valid_ref[i] == 1)
    def _():
        his, los = [], []
        for c in range(n_words):
            w = hs_ref[pl.ds(c, blk, stride=ROW_TILE), :]
            his.append(lax.bitcast_convert_type(w & jnp.uint32(0xFFFF0000), F32).astype(BF16))
            los.append(lax.bitcast_convert_type(w << 16, F32).astype(BF16))
        x = jnp.concatenate(his + los, axis=-1)
        gates = lax.bitcast_convert_type(hs_ref[pl.ds(n_words, blk, stride=ROW_TILE), :], F32)
        gate_a = gates[:, 0:1]
        gate_b = gates[:, 1:2]

        def mlp(wg, wu, wd):
            g = jnp.dot(x, wg[0], preferred_element_type=F32)
            u = jnp.dot(x, wu[0], preferred_element_type=F32)
            a = (g * _sigmoid(g) * u).astype(BF16)
            return jnp.dot(a, wd[0], preferred_element_type=F32)

        y = gate_a * mlp(wga_ref, wua_ref, wda_ref) + gate_b * mlp(wgb_ref, wub_ref, wdb_ref)
        for c in range(d // LANES):
            y_ref[pl.ds(c, blk, stride=ROW_TILE), :] = y[:, c * LANES:(c + 1) * LANES]

    @pl.when(valid_ref[i] != 1)
    def _():
        y_ref[...] = jnp.zeros_like(y_ref)


def _moe_call(blk_ea, blk_eb, blk_valid, hs, wg, wu, wd):
    nb = blk_ea.shape[0]
    d = wg.shape[1]
    de = wg.shape[2]
    rows = MOE_BLOCK * ROW_TILE
    tok = pl.BlockSpec((rows, LANES), lambda i, ea, eb, va: (i, 0))
    w_a = lambda shape: pl.BlockSpec((1,) + shape, lambda i, ea, eb, va: (ea[i], 0, 0))
    w_b = lambda shape: pl.BlockSpec((1,) + shape, lambda i, ea, eb, va: (eb[i], 0, 0))
    grid_spec = pltpu.PrefetchScalarGridSpec(
        num_scalar_prefetch=3,
        grid=(nb,),
        in_specs=[tok, w_a((d, de)), w_a((d, de)), w_a((de, d)), w_b((d, de)), w_b((d, de)), w_b((de, d))],
        out_specs=tok,
    )
    return pl.pallas_call(
        _moe_kernel,
        grid_spec=grid_spec,
        out_shape=jax.ShapeDtypeStruct((nb * rows, LANES), F32),
        compiler_params=pltpu.CompilerParams(dimension_semantics=("arbitrary",), vmem_limit_bytes=VMEM_LIMIT),
        name="moe",
    )(blk_ea, blk_eb, blk_valid, hs, wg, wu, wd, wg, wu, wd)


def _final_kernel(dest_ref, xmid_ref, mod_ref, fn_ref, y_hbm, o_ref, fbuf, sem):
    n = dest_ref.shape[0]
    d = xmid_ref.shape[1]

    def body(o, carry):
        for k in range(DMA_UNROLL):
            t = o * DMA_UNROLL + k
            src = pl.multiple_of(dest_ref[t] * ROW_TILE, ROW_TILE)
            dst = pl.multiple_of(t * ROW_TILE, ROW_TILE)
            pltpu.make_async_copy(y_hbm.at[pl.ds(src, ROW_TILE)], fbuf.at[pl.ds(dst, ROW_TILE)],
                                  sem).start(priority=k % 2)
        return carry

    lax.fori_loop(0, n // DMA_UNROLL, body, 0)
    pltpu.make_async_copy(y_hbm.at[pl.ds(0, n * ROW_TILE)], fbuf, sem).wait()
    f = jnp.concatenate([fbuf[pl.ds(c, n, stride=ROW_TILE), :] for c in range(d // LANES)], axis=-1)
    x = xmid_ref[...] + mod_ref[0][5:6] * f
    ms = jnp.mean(x * x, axis=-1, keepdims=True)
    o_ref[...] = x * lax.rsqrt(ms + NORM_EPS) * fn_ref[...]


def _final_call(dest, xmid, mod3, fn, y, seq):
    t, d = xmid.shape
    tm = TM_ROWS
    tpb = seq // tm
    return pl.pallas_call(
        _final_kernel,
        grid=(t // tm,),
        in_specs=[
            pl.BlockSpec((tm,), lambda i: (i,), memory_space=pltpu.SMEM),
            pl.BlockSpec((tm, d), lambda i: (i, 0)),
            pl.BlockSpec((1, 6, d), lambda i: (i // tpb, 0, 0)),
            pl.BlockSpec((1, d), lambda i: (0, 0)),
            pl.BlockSpec(memory_space=pl.ANY),
        ],
        out_specs=pl.BlockSpec((tm, d), lambda i: (i, 0)),
        out_shape=jax.ShapeDtypeStruct((t, d), F32),
        scratch_shapes=[pltpu.VMEM((tm * ROW_TILE, LANES), F32), pltpu.SemaphoreType.DMA(())],
        compiler_params=pltpu.CompilerParams(dimension_semantics=("arbitrary",), vmem_limit_bytes=VMEM_LIMIT),
        name="final",
    )(dest, xmid, mod3, fn, y)


def _pair_tables():
    ea, eb = [], []
    for g in range(N_GROUPS):
        for a in range(EXPERTS_PER_GROUP):
            for b in range(a + 1, EXPERTS_PER_GROUP):
                ea.append(g * EXPERTS_PER_GROUP + a)
                eb.append(g * EXPERTS_PER_GROUP + b)
    return np.asarray(ea, np.int32), np.asarray(eb, np.int32)


def _routing_plan(info, counts, t):
    cls = info[0].astype(jnp.int32)
    rank = info[1].astype(jnp.int32)
    cnt = counts[0, :N_CLASSES].astype(jnp.int32)
    padded = (cnt + MOE_BLOCK - 1) // MOE_BLOCK * MOE_BLOCK
    pad_end = jnp.cumsum(padded)
    pad_start = pad_end - padded
    cls_ids = jnp.arange(N_CLASSES, dtype=jnp.int32)
    dest = rank + jnp.sum(jnp.where(cls[:, None] == cls_ids[None, :], pad_start[None, :], 0), axis=1)
    nb = t // MOE_BLOCK + N_CLASSES
    nb_used = pad_end[-1] // MOE_BLOCK
    blk = jnp.arange(nb, dtype=jnp.int32)
    blk_cls = jnp.clip(jnp.searchsorted(pad_end, blk * MOE_BLOCK, side="right"), 0, N_CLASSES - 1)
    valid = blk < nb_used
    last_cls = blk_cls[jnp.maximum(nb_used - 1, 0)]
    blk_cls = jnp.where(valid, blk_cls, last_cls).astype(jnp.int32)
    tab_a, tab_b = _pair_tables()
    blk_ea = jnp.asarray(tab_a)[blk_cls]
    blk_eb = jnp.asarray(tab_b)[blk_cls]
    return dest, blk_ea, blk_eb, valid.astype(jnp.int32), nb


def kernel(x, c, ctx, c_ctx, w_mod, b_mod, norm_mix, w_in, rpb, w_att_out, conv_w, conv_b, conv_ln_g, conv_ln_b,
           w_conv_out, w_o, norm_ffn, w_router_group, b_router_group, w_router_expert, b_router_expert,
           w_exp_gate, w_exp_up, w_exp_down, final_norm):
    b, seq, d = x.shape
    n_ctx = ctx.shape[1]
    t = b * seq
    assert w_mod.shape[0] == 1, "single layer"
    assert seq % TM_ROWS == 0 and seq // GRID_W >= WIN_ROWS and n_ctx % LANES == 0 and (b * n_ctx) % 512 == 0
    assert d == 1024

    mod_rows = -(-(b + 1) // SUBLANES) * SUBLANES
    cc = jnp.zeros((mod_rows, d), F32).at[:b].set(c).at[b].set(c_ctx)
    m_all = _mod_call(cc, w_mod[0], b_mod[0][None, :])
    mod_lat = m_all[:b].reshape(b, 6, d)
    mod_ctx = m_all[b:b + 1].reshape(1, 6, d)

    x2 = x.reshape(t, d)
    g_mix = norm_mix[0][None, :]
    w_in_b = w_in[0].astype(BF16)
    wqv_t = jnp.concatenate([w_in_b[:, Q0:K0], w_in_b[:, V0:GLU0]], axis=1).T
    w_rest = jnp.concatenate([w_in_b[:, K0:V0], w_in_b[:, GLU0:]], axis=1)
    cw = jnp.zeros((32, D_CONV), F32).at[:CONV_WIDTH].set(conv_w[0])
    qt, k, vt, u, ga, gb = _proj_call(x2, mod_lat, g_mix, wqv_t, w_rest, seq)
    yn = _conv_call(u, cw, conv_b[0][None, :], conv_ln_g[0][None, :], conv_ln_b[0][None, :], seq)
    kc, vct = _ctx_kv_call(ctx.reshape(b * n_ctx, d), mod_ctx, g_mix, w_in_b[:, K0:V0], w_in_b[:, V0:GLU0].T)

    att = _attn_call(qt, k, vt, kc, vct, _bias_table(rpb[0]), seq, n_ctx)

    wr = jnp.zeros((d, LANES), F32).at[:, :N_GROUPS].set(w_router_group[0])
    wr = wr.at[:, N_GROUPS:N_GROUPS + N_EXPERTS].set(w_router_expert[0])
    wr_hi = wr.astype(BF16)
    wr = jnp.concatenate([wr_hi, (wr - wr_hi.astype(F32)).astype(BF16)], axis=1)
    br = jnp.zeros((1, LANES), F32).at[0, :N_GROUPS].set(b_router_group[0])
    br = br.at[0, N_GROUPS:N_GROUPS + N_EXPERTS].set(b_router_expert[0])
    x_mid, hp, info, counts = _merge_call(
        att, yn, ga, gb, x2, mod_lat, w_att_out[0].astype(BF16), w_conv_out[0].astype(BF16), w_o[0].astype(BF16), norm_ffn[0][None, :],
        wr, br, seq)

    dest, blk_ea, blk_eb, blk_valid, nb = _routing_plan(info, counts, t)
    hs = _dispatch_call(dest, hp, nb * MOE_BLOCK)
    y = _moe_call(blk_ea, blk_eb, blk_valid, hs,
                  w_exp_gate[0].astype(BF16), w_exp_up[0].astype(BF16), w_exp_down[0].astype(BF16))
    out = _final_call(dest, x_mid, mod_lat, final_norm[None, :], y, seq)
    return out.reshape(b, seq, d)
```

```python
import functools

import numpy as np
import jax
import jax.numpy as jnp
from jax import lax
from jax.experimental import pallas as pl
from jax.experimental.pallas import tpu as pltpu

F32 = jnp.float32
BF16 = jnp.bfloat16
U32 = jnp.uint32

GRID_W = 64
N_HEADS = 8
HEAD_DIM = 64
D_ATT = N_HEADS * HEAD_DIM
WIN_ROWS = 8
WIN_COLS = 16
D_CONV = 512
CONV_WIDTH = 31
N_GROUPS = 4
EXPERTS_PER_GROUP = 8
N_EXPERTS = N_GROUPS * EXPERTS_PER_GROUP
PAIRS_PER_GROUP = EXPERTS_PER_GROUP * (EXPERTS_PER_GROUP - 1) // 2
N_CLASSES = N_GROUPS * PAIRS_PER_GROUP
NORM_EPS = 1e-6
NEG_INF = -1e30

LANES = 128
SUBLANES = 8
ROW_TILE = SUBLANES
HALO = 16
VMEM_LIMIT = 56 * 1024 * 1024

TM_PROJ = 512
TM_MERGE = 512
MERGE_PARTS = 1
TM_ROWS = 1024
MOE_BLOCK = 256
CONV_CHUNK = 64
DMA_UNROLL = 8

HIGHEST = lax.Precision.HIGHEST


def _norm_mod(x, g, shift, scale):
    ms = jnp.mean(x * x, axis=-1, keepdims=True)
    y = x * lax.rsqrt(ms + NORM_EPS) * g
    return y * (1.0 + scale) + shift


def _sigmoid(x):
    return jax.nn.sigmoid(x)


def _mod_kernel(c_ref, w_ref, b_ref, o_ref):
    c = c_ref[...]
    s = c * _sigmoid(c)
    o_ref[...] = jnp.dot(s, w_ref[...], precision=HIGHEST, preferred_element_type=F32) + b_ref[...]


def _mod_call(cc, w_mod, b_mod):
    rows, d = cc.shape
    n = w_mod.shape[1]
    tn = 1024
    return pl.pallas_call(
        _mod_kernel,
        grid=(n // tn,),
        in_specs=[
            pl.BlockSpec((rows, d), lambda j: (0, 0)),
            pl.BlockSpec((d, tn), lambda j: (0, j)),
            pl.BlockSpec((1, tn), lambda j: (0, j)),
        ],
        out_specs=pl.BlockSpec((rows, tn), lambda j: (0, j)),
        out_shape=jax.ShapeDtypeStruct((rows, n), F32),
        compiler_params=pltpu.CompilerParams(dimension_semantics=("arbitrary",), vmem_limit_bytes=VMEM_LIMIT),
        name="mod",
    )(cc, w_mod, b_mod)


Q0, K0, V0, GLU0 = 0, D_ATT, 2 * D_ATT, 3 * D_ATT
GA0 = GLU0 + 2 * D_CONV


NT_DIMS = (((1,), (1,)), ((), ()))


def _store_transposed(res_t, ref, scale=None):
    for j in range(ref.shape[0]):
        blk = res_t[:, j * LANES:(j + 1) * LANES]
        if scale is not None:
            blk = blk * scale
        ref[j] = blk.astype(ref.dtype)


def _conv_kernel(u_ref, up_ref, un_ref, cw_ref, cb_ref, lg_ref, lb_ref, yn_ref, ubuf, shifted,
                 *, tiles_per_seq):
    i = pl.program_id(0)
    tm = u_ref.shape[0]
    first = (i % tiles_per_seq) == 0
    last = (i % tiles_per_seq) == tiles_per_seq - 1
    ubuf[0:HALO, :] = jnp.where(first, 0.0, up_ref[...].astype(F32))
    for r0 in range(0, tm, CONV_CHUNK):
        ubuf[HALO + r0:HALO + r0 + CONV_CHUNK, :] = u_ref[r0:r0 + CONV_CHUNK, :].astype(F32)
    ubuf[HALO + tm:, :] = jnp.where(last, 0.0, un_ref[...].astype(F32))

    span = shifted.shape[1]
    for ph in range(SUBLANES):
        for r0 in range(0, span, CONV_CHUNK):
            n = min(CONV_CHUNK, span - r0)
            shifted[ph, r0:r0 + n, :] = ubuf[ph + r0:ph + r0 + n, :]

    base = HALO - CONV_WIDTH // 2
    inv_c = 1.0 / D_CONV
    for tc in range(tm // CONV_CHUNK):
        accs = []
        for lc in range(D_CONV // LANES):
            ls = slice(lc * LANES, (lc + 1) * LANES)
            acc = jnp.zeros((CONV_CHUNK, LANES), F32)
            for j in range(CONV_WIDTH):
                tiles, ph = divmod(base + j, SUBLANES)
                lo = tc * CONV_CHUNK + tiles * SUBLANES
                acc = acc + shifted[ph, lo:lo + CONV_CHUNK, ls] * cw_ref[j:j + 1, ls]
            accs.append(acc + cb_ref[:, ls])
        mu = sum(jnp.sum(a, axis=-1, keepdims=True) for a in accs) * inv_c
        cen = [a - mu for a in accs]
        var = sum(jnp.sum(c * c, axis=-1, keepdims=True) for c in cen) * inv_c
        rstd = lax.rsqrt(var + NORM_EPS)
        for lc, c in enumerate(cen):
            ls = slice(lc * LANES, (lc + 1) * LANES)
            yn = c * rstd * lg_ref[:, ls] + lb_ref[:, ls]
            yn_ref[tc * CONV_CHUNK:(tc + 1) * CONV_CHUNK, ls] = (yn * _sigmoid(yn)).astype(yn_ref.dtype)


def _conv_call(u, cw, cb, lg, lb, seq):
    t = u.shape[0]
    tm = TM_MERGE
    tpb = seq // tm
    hb = tm // HALO
    n_halo = t // HALO
    row = lambda i: (i, 0)
    const = lambda i: (0, 0)
    return pl.pallas_call(
        functools.partial(_conv_kernel, tiles_per_seq=tpb),
        grid=(t // tm,),
        in_specs=[
            pl.BlockSpec((tm, D_CONV), row),
            pl.BlockSpec((HALO, D_CONV), lambda i: (jnp.maximum(i * hb - 1, 0), 0)),
            pl.BlockSpec((HALO, D_CONV), lambda i: (jnp.minimum((i + 1) * hb, n_halo - 1), 0)),
            pl.BlockSpec(cw.shape, const),
            pl.BlockSpec((1, D_CONV), const),
            pl.BlockSpec((1, D_CONV), const),
            pl.BlockSpec((1, D_CONV), const),
        ],
        out_specs=pl.BlockSpec((tm, D_CONV), row),
        out_shape=jax.ShapeDtypeStruct((t, D_CONV), BF16),
        scratch_shapes=[
            pltpu.VMEM((tm + 2 * HALO, D_CONV), F32),
            pltpu.VMEM((SUBLANES, tm + 2 * HALO - SUBLANES, D_CONV), F32),
        ],
        compiler_params=pltpu.CompilerParams(dimension_semantics=("arbitrary",), vmem_limit_bytes=VMEM_LIMIT),
        name="conv",
    )(u, u, u, cw, cb, lg, lb)


def _proj_kernel(x_ref, mod_ref, g_ref, wqv_ref, w_ref, qt_ref, k_ref, vt_ref, u_ref, ga_ref, gb_ref):
    d = x_ref.shape[1]
    mod = mod_ref[0]
    h = _norm_mod(x_ref[...], g_ref[...], mod[0:1], mod[1:2]).astype(BF16)

    qv_t = lax.dot_general(wqv_ref[...], h, NT_DIMS, preferred_element_type=F32)
    _store_transposed(qv_t[:D_ATT], qt_ref, HEAD_DIM ** -0.5)
    _store_transposed(qv_t[D_ATT:], vt_ref)

    def seg(lo, hi):
        return jnp.dot(h, w_ref[:, lo:hi], preferred_element_type=F32)

    k_ref[...] = seg(0, D_ATT).astype(BF16)
    a = seg(D_ATT, D_ATT + D_CONV)
    g = seg(D_ATT + D_CONV, D_ATT + 2 * D_CONV)
    u_ref[...] = (a * _sigmoid(g)).astype(BF16)
    g0 = D_ATT + 2 * D_CONV
    ga_ref[...] = _sigmoid(seg(g0, g0 + d)).astype(BF16)
    gb_ref[...] = _sigmoid(seg(g0 + d, g0 + 2 * d)).astype(BF16)


def _proj_call(x2, mod3, g, wqv_t, w_rest, seq):
    t, d = x2.shape
    tm = TM_PROJ
    tpb = seq // tm
    nblk = tm // LANES
    outs = [
        jax.ShapeDtypeStruct((t // LANES, D_ATT, LANES), BF16),
        jax.ShapeDtypeStruct((t, D_ATT), BF16),
        jax.ShapeDtypeStruct((t // LANES, D_ATT, LANES), BF16),
        jax.ShapeDtypeStruct((t, D_CONV), BF16),
        jax.ShapeDtypeStruct((t, d), BF16),
        jax.ShapeDtypeStruct((t, d), BF16),
    ]
    row = lambda i: (i, 0)
    blk3 = lambda i: (i, 0, 0)
    return pl.pallas_call(
        _proj_kernel,
        grid=(t // tm,),
        in_specs=[
            pl.BlockSpec((tm, d), row),
            pl.BlockSpec((1, 6, d), lambda i: (i // tpb, 0, 0)),
            pl.BlockSpec((1, d), lambda i: (0, 0)),
            pl.BlockSpec(wqv_t.shape, lambda i: (0, 0)),
            pl.BlockSpec(w_rest.shape, lambda i: (0, 0)),
        ],
        out_specs=[
            pl.BlockSpec((nblk, D_ATT, LANES), blk3),
            pl.BlockSpec((tm, D_ATT), row),
            pl.BlockSpec((nblk, D_ATT, LANES), blk3),
            pl.BlockSpec((tm, D_CONV), row),
            pl.BlockSpec((tm, d), row),
            pl.BlockSpec((tm, d), row),
        ],
        out_shape=outs,
        compiler_params=pltpu.CompilerParams(dimension_semantics=("arbitrary",), vmem_limit_bytes=VMEM_LIMIT),
        name="proj",
    )(x2, mod3, g, wqv_t, w_rest)


def _ctx_kv_kernel(x_ref, mod_ref, g_ref, wk_ref, wvt_ref, k_ref, vt_ref):
    mod = mod_ref[0]
    h = _norm_mod(x_ref[...], g_ref[...], mod[0:1], mod[1:2]).astype(BF16)
    k_ref[...] = jnp.dot(h, wk_ref[...], preferred_element_type=F32).astype(BF16)
    _store_transposed(lax.dot_general(wvt_ref[...], h, NT_DIMS, preferred_element_type=F32), vt_ref)


def _ctx_kv_call(c2, mod3, g, w_k, w_vt):
    t, d = c2.shape
    tm = 512
    nblk = tm // LANES
    row = lambda i: (i, 0)
    return pl.pallas_call(
        _ctx_kv_kernel,
        grid=(t // tm,),
        in_specs=[
            pl.BlockSpec((tm, d), row),
            pl.BlockSpec((1, 6, d), lambda i: (0, 0, 0)),
            pl.BlockSpec((1, d), lambda i: (0, 0)),
            pl.BlockSpec(w_k.shape, lambda i: (0, 0)),
            pl.BlockSpec(w_vt.shape, lambda i: (0, 0)),
        ],
        out_specs=[pl.BlockSpec((tm, D_ATT), row), pl.BlockSpec((nblk, D_ATT, LANES), lambda i: (i, 0, 0))],
        out_shape=[jax.ShapeDtypeStruct((t, D_ATT), BF16), jax.ShapeDtypeStruct((t // LANES, D_ATT, LANES), BF16)],
        compiler_params=pltpu.CompilerParams(dimension_semantics=("arbitrary",), vmem_limit_bytes=VMEM_LIMIT),
        name="ctx_kv",
    )(c2, mod3, g, w_k, w_vt)


HEADS_PER_GROUP = 4
GROUP_W = HEADS_PER_GROUP * HEAD_DIM
N_HEAD_GROUPS = N_HEADS // HEADS_PER_GROUP


def _attn_kernel(qt_ref, k_ref, vt_ref, kc_ref, vct_ref, bias_ref, o_ref, vboth):
    nblk = qt_ref.shape[0]
    rows = 2 * nblk
    n_loc = WIN_ROWS * GRID_W
    n_ctx_blk = vct_ref.shape[0]
    half = GRID_W

    for j in range(nblk):
        vboth[0, j] = vt_ref[j]
    for j in range(nblk - 1):
        vboth[1, j] = jnp.concatenate([vt_ref[j][:, half:], vt_ref[j + 1][:, :half]], axis=1)
    vboth[1, nblk - 1] = jnp.zeros_like(vt_ref[0])

    lane = lax.broadcasted_iota(jnp.int32, (D_ATT, LANES), 1)
    low = lane < half
    rblk = lax.broadcasted_iota(jnp.int32, (GROUP_W, GROUP_W), 0) // HEAD_DIM
    cblk = lax.broadcasted_iota(jnp.int32, (GROUP_W, GROUP_W), 1) // HEAD_DIM
    diag = rblk == cblk
    low64 = lax.broadcasted_iota(jnp.int32, (HEAD_DIM, LANES), 1) < half
    kc = kc_ref[...]

    def one_row(row, tiled, side):
        r_start = jnp.clip(row - WIN_ROWS // 2, 0, rows - WIN_ROWS)
        b_off = pl.multiple_of((WIN_ROWS - 1 - (row - r_start)) * GRID_W, GRID_W)
        kw = k_ref[pl.ds(pl.multiple_of(r_start * GRID_W, GRID_W), n_loc), :]
        vwin = vboth[r_start % 2, pl.ds(r_start // 2, n_loc // LANES)]
        parts = []
        for g in range(N_HEAD_GROUPS):
            fs = slice(g * GROUP_W, (g + 1) * GROUP_W)
            tg = tiled[fs, :]
            w = jnp.where(diag, jnp.concatenate([tg, tg], axis=1), jnp.zeros((), BF16))
            s_loc = jnp.dot(kw[:, fs], w, preferred_element_type=F32) + bias_ref[g, pl.ds(b_off, n_loc), :]
            s_ctx = jnp.dot(kc[:, fs], w, preferred_element_type=F32)
            m = jnp.maximum(jnp.max(s_loc, axis=0, keepdims=True), jnp.max(s_ctx, axis=0, keepdims=True))
            p_loc = jnp.exp(s_loc - m)
            p_ctx = jnp.exp(s_ctx - m)
            l = jnp.sum(p_loc, axis=0, keepdims=True) + jnp.sum(p_ctx, axis=0, keepdims=True)
            p = jnp.concatenate([p_loc.astype(BF16), p_ctx.astype(BF16)], axis=0)
            vt_g = jnp.concatenate([vwin[j][fs, :] for j in range(n_loc // LANES)]
                                   + [vct_ref[j][fs, :] for j in range(n_ctx_blk)], axis=1)
            o_t = jnp.dot(vt_g, p, preferred_element_type=F32) * (1.0 / l)
            for j in range(HEADS_PER_GROUP):
                blk = o_t[j * HEAD_DIM:(j + 1) * HEAD_DIM, (j // 2) * LANES:(j // 2 + 1) * LANES]
                if j % 2 != side:
                    blk = pltpu.roll(blk, half, axis=1)
                parts.append(blk)
        return parts

    def body(i, carry):
        xq = qt_ref[i]
        rolled = jnp.concatenate([xq[:, half:], xq[:, :half]], axis=1)
        parts_a = one_row(2 * i, jnp.where(low, xq, rolled), 0)
        parts_b = one_row(2 * i + 1, jnp.where(low, rolled, xq), 1)
        out = jnp.concatenate([jnp.where(low64, a, b) for a, b in zip(parts_a, parts_b)], axis=0)
        o_ref[i] = out.astype(o_ref.dtype)
        return carry

    lax.fori_loop(0, nblk, body, 0, unroll=2)


def _attn_call(qt, k, vt, kc, vct, bias, seq, n_ctx):
    t = k.shape[0]
    b = t // seq
    nblk = seq // LANES
    lat3 = pl.BlockSpec((nblk, D_ATT, LANES), lambda i: (i, 0, 0))
    return pl.pallas_call(
        _attn_kernel,
        grid=(b,),
        in_specs=[lat3,
                  pl.BlockSpec((seq, D_ATT), lambda i: (i, 0)),
                  lat3,
                  pl.BlockSpec((n_ctx, D_ATT), lambda i: (i, 0)),
                  pl.BlockSpec((n_ctx // LANES, D_ATT, LANES), lambda i: (i, 0, 0)),
                  pl.BlockSpec(bias.shape, lambda i: (0, 0, 0))],
        out_specs=lat3,
        out_shape=jax.ShapeDtypeStruct((t // LANES, D_ATT, LANES), BF16),
        scratch_shapes=[pltpu.VMEM((2, nblk, D_ATT, LANES), BF16)],
        compiler_params=pltpu.CompilerParams(dimension_semantics=("arbitrary",), vmem_limit_bytes=VMEM_LIMIT),
        name="attn",
    )(qt, k, vt, kc, vct, bias)


def _bias_table(rpb):
    cq = np.arange(GRID_W)[:, None]
    ck = np.arange(GRID_W)[None, :]
    c_start = np.clip(cq - WIN_COLS // 2, 0, GRID_W - WIN_COLS)
    col_mask = (ck >= c_start) & (ck < c_start + WIN_COLS)
    dc_idx = np.clip(ck - cq + WIN_COLS - 1, 0, 2 * WIN_COLS - 2)
    tab = rpb[:, :, dc_idx]
    tab = jnp.where(col_mask[None, None], tab, NEG_INF)
    n_dr = tab.shape[1]
    tab = tab.reshape(N_HEAD_GROUPS, HEADS_PER_GROUP, n_dr, GRID_W, GRID_W)
    tab = tab.transpose(0, 2, 4, 1, 3)
    return tab.reshape(N_HEAD_GROUPS, n_dr * GRID_W, GROUP_W).astype(F32)


def _merge_rows(r0, n, att_ref, yn_ref, ga_ref, gb_ref, x_ref, mod,
                wa_ref, wc_ref, wo_ref, nf_ref, wr_ref, br_ref, xmid_ref, hp_ref):
    d = x_ref.shape[1]
    rs = slice(r0, r0 + n)
    y_conv = jnp.dot(yn_ref[rs, :], wc_ref[...], preferred_element_type=F32)

    att = jnp.concatenate([att_ref[j].astype(F32).T for j in range(r0 // LANES, (r0 + n) // LANES)], axis=0)
    y_att = jnp.dot(att.astype(BF16), wa_ref[...], preferred_element_type=F32)
    mix = ga_ref[rs, :].astype(F32) * y_att + gb_ref[rs, :].astype(F32) * y_conv
    y = jnp.dot(mix.astype(BF16), wo_ref[...], preferred_element_type=F32)
    x_mid = x_ref[rs, :] + mod[2:3] * y
    xmid_ref[rs, :] = x_mid
    h2 = _norm_mod(x_mid, nf_ref[...], mod[3:4], mod[4:5])

    h2_hi = h2.astype(BF16)
    h2_lo = (h2 - h2_hi.astype(F32)).astype(BF16)
    t_hi = jnp.dot(h2_hi, wr_ref[...], preferred_element_type=F32)
    t_lo = jnp.dot(h2_lo, wr_ref[:, :LANES], preferred_element_type=F32)
    logits = t_hi[:, :LANES] + t_hi[:, LANES:] + t_lo + br_ref[...]
    lane = lax.broadcasted_iota(jnp.int32, logits.shape, 1).astype(F32)
    big = float(LANES)
    is_g = lane < N_GROUPS
    gl = jnp.where(is_g, logits, -jnp.inf)
    gmax = jnp.max(gl, axis=-1, keepdims=True)
    g_idx = jnp.min(jnp.where(gl == gmax, lane, big), axis=-1, keepdims=True)
    p_group = 1.0 / jnp.sum(jnp.where(is_g, jnp.exp(gl - gmax), 0.0), axis=-1, keepdims=True)
    e_lo = N_GROUPS + EXPERTS_PER_GROUP * g_idx
    in_grp = (lane >= e_lo) & (lane < e_lo + EXPERTS_PER_GROUP)
    el = jnp.where(in_grp, logits, -jnp.inf)
    v1 = jnp.max(el, axis=-1, keepdims=True)
    i1 = jnp.min(jnp.where(el == v1, lane, big), axis=-1, keepdims=True)
    el2 = jnp.where(lane == i1, -jnp.inf, el)
    v2 = jnp.max(el2, axis=-1, keepdims=True)
    i2 = jnp.min(jnp.where(el2 == v2, lane, big), axis=-1, keepdims=True)
    e2 = jnp.exp(v2 - v1)
    gate1 = p_group / (1.0 + e2)
    gate2 = p_group * e2 / (1.0 + e2)
    j1 = i1 - e_lo
    j2 = i2 - e_lo
    ja = jnp.minimum(j1, j2)
    jb = jnp.maximum(j1, j2)
    gate_a = jnp.where(j1 < j2, gate1, gate2)
    gate_b = jnp.where(j1 < j2, gate2, gate1)
    pair = ja * (2 * EXPERTS_PER_GROUP - 1 - ja) * 0.5 + (jb - ja - 1.0)
    cls = g_idx * PAIRS_PER_GROUP + pair

    half = d // 2
    hi = lax.bitcast_convert_type(h2[:, :half].astype(BF16).astype(F32), U32)
    lo = lax.bitcast_convert_type(h2[:, half:].astype(BF16).astype(F32), U32)
    words = hi | (lo >> 16)
    n_words = half // LANES
    base = r0 * ROW_TILE
    for c in range(n_words):
        hp_ref[pl.ds(base + c, n, stride=ROW_TILE), :] = words[:, c * LANES:(c + 1) * LANES]
    gates = jnp.where(lane == 0.0, gate_a, jnp.where(lane == 1.0, gate_b, 0.0))
    hp_ref[pl.ds(base + n_words, n, stride=ROW_TILE), :] = lax.bitcast_convert_type(gates, U32)
    zero = jnp.zeros((n, LANES), U32)
    for c in range(n_words + 1, ROW_TILE):
        hp_ref[pl.ds(base + c, n, stride=ROW_TILE), :] = zero
    return cls


def _merge_kernel(att_ref, yn_ref, ga_ref, gb_ref, x_ref, mod_ref,
                  wa_ref, wc_ref, wo_ref, nf_ref, wr_ref, br_ref, tri_ref,
                  xmid_ref, hp_ref, info_ref, cnt_ref, run_ref):
    i = pl.program_id(0)
    tm = x_ref.shape[0]

    @pl.when(i == 0)
    def _():
        run_ref[...] = jnp.zeros_like(run_ref)

    mod = mod_ref[0]
    n = tm // MERGE_PARTS
    cls = jnp.concatenate(
        [_merge_rows(p * n, n, att_ref, yn_ref, ga_ref, gb_ref, x_ref, mod, wa_ref, wc_ref, wo_ref, nf_ref,
                     wr_ref, br_ref, xmid_ref, hp_ref) for p in range(MERGE_PARTS)], axis=0)

    lane = lax.broadcasted_iota(jnp.int32, (tm, LANES), 1).astype(F32)
    onehot = lane == cls
    oh_f = jnp.where(onehot, 1.0, 0.0)
    before = jnp.dot(tri_ref[...], oh_f.astype(BF16), preferred_element_type=F32)
    run = run_ref[0:1, :]
    rank = jnp.sum(oh_f * (before + run), axis=-1, keepdims=True)
    new_run = run + jnp.sum(oh_f, axis=0, keepdims=True)
    run_ref[...] = jnp.broadcast_to(new_run, run_ref.shape)
    cnt_ref[...] = jnp.broadcast_to(new_run, cnt_ref.shape)

    info = jnp.where(lane == 0.0, cls, jnp.where(lane == 1.0, rank, 0.0))
    info_ref[...] = info.T[0:SUBLANES, :]


def _merge_call(att, yn, ga, gb, x2, mod3, wa, wc, wo, nf, wr, br, seq):
    t, d = x2.shape
    tm = TM_MERGE
    tpb = seq // tm
    tri = jnp.asarray(np.tril(np.ones((tm, tm), np.float32), -1), BF16)
    row = lambda i: (i, 0)
    const = lambda i: (0, 0)
    return pl.pallas_call(
        _merge_kernel,
        grid=(t // tm,),
        in_specs=[
            pl.BlockSpec((tm // LANES, D_ATT, LANES), lambda i: (i, 0, 0)),
            pl.BlockSpec((tm, D_CONV), row),
            pl.BlockSpec((tm, d), row),
            pl.BlockSpec((tm, d), row),
            pl.BlockSpec((tm, d), row),
            pl.BlockSpec((1, 6, d), lambda i: (i // tpb, 0, 0)),
            pl.BlockSpec(wa.shape, const),
            pl.BlockSpec(wc.shape, const),
            pl.BlockSpec(wo.shape, const),
            pl.BlockSpec((1, d), const),
            pl.BlockSpec(wr.shape, const),
            pl.BlockSpec((1, LANES), const),
            pl.BlockSpec((tm, tm), const),
        ],
        out_specs=[
            pl.BlockSpec((tm, d), row),
            pl.BlockSpec((tm * ROW_TILE, LANES), row),
            pl.BlockSpec((SUBLANES, tm), lambda i: (0, i)),
            pl.BlockSpec((SUBLANES, LANES), const),
        ],
        out_shape=[
            jax.ShapeDtypeStruct((t, d), F32),
            jax.ShapeDtypeStruct((t * ROW_TILE, LANES), U32),
            jax.ShapeDtypeStruct((SUBLANES, t), F32),
            jax.ShapeDtypeStruct((SUBLANES, LANES), F32),
        ],
        scratch_shapes=[pltpu.VMEM((SUBLANES, LANES), F32)],
        compiler_params=pltpu.CompilerParams(dimension_semantics=("arbitrary",), vmem_limit_bytes=VMEM_LIMIT),
        name="merge",
    )(att, yn, ga, gb, x2, mod3, wa, wc, wo, nf, wr, br, tri)


def _dispatch_kernel(dest_ref, tail_ref, hp_ref, hs_hbm, zbuf, sem, zsem):
    n = dest_ref.shape[0]
    blk_rows = zbuf.shape[0]

    @pl.when(pl.program_id(0) == 0)
    def _():
        zbuf[...] = jnp.zeros_like(zbuf)

        def tail_copy(c):
            start = pl.multiple_of(tail_ref[c] * ROW_TILE, ROW_TILE)
            return pltpu.make_async_copy(zbuf, hs_hbm.at[pl.ds(start, blk_rows)], zsem)

        def start_one(c, carry):
            @pl.when(tail_ref[c] >= 0)
            def _():
                tail_copy(c).start()
            return carry

        def wait_one(c, carry):
            @pl.when(tail_ref[c] >= 0)
            def _():
                tail_copy(c).wait()
            return carry

        lax.fori_loop(0, N_CLASSES, start_one, 0)
        lax.fori_loop(0, N_CLASSES, wait_one, 0)

        def unused_copy(b):
            start = pl.multiple_of(b * blk_rows, blk_rows)
            return pltpu.make_async_copy(zbuf, hs_hbm.at[pl.ds(start, blk_rows)], zsem)

        def start_unused(b, carry):
            unused_copy(b).start()
            return carry

        def wait_unused(b, carry):
            unused_copy(b).wait()
            return carry

        n_blocks = hs_hbm.shape[0] // blk_rows
        lax.fori_loop(tail_ref[N_CLASSES], n_blocks, start_unused, 0)
        lax.fori_loop(tail_ref[N_CLASSES], n_blocks, wait_unused, 0)

    def body(o, carry):
        for k in range(DMA_UNROLL):
            t = o * DMA_UNROLL + k
            src = pl.multiple_of(t * ROW_TILE, ROW_TILE)
            dst = pl.multiple_of(dest_ref[t] * ROW_TILE, ROW_TILE)
            pltpu.make_async_copy(hp_ref.at[pl.ds(src, ROW_TILE)], hs_hbm.at[pl.ds(dst, ROW_TILE)],
                                  sem).start(priority=k % 2)
        return carry

    lax.fori_loop(0, n // DMA_UNROLL, body, 0)
    pltpu.make_async_copy(hp_ref, hs_hbm.at[pl.ds(0, n * ROW_TILE)], sem).wait()


def _dispatch_call(dest, tail_start, hp, p_rows):
    t = dest.shape[0]
    return pl.pallas_call(
        _dispatch_kernel,
        grid=(t // TM_ROWS,),
        in_specs=[
            pl.BlockSpec((TM_ROWS,), lambda i: (i,), memory_space=pltpu.SMEM),
            pl.BlockSpec(memory_space=pltpu.SMEM),
            pl.BlockSpec((TM_ROWS * ROW_TILE, LANES), lambda i: (i, 0)),
        ],
        out_specs=pl.BlockSpec(memory_space=pl.ANY),
        out_shape=jax.ShapeDtypeStruct((p_rows * ROW_TILE, LANES), U32),
        scratch_shapes=[pltpu.VMEM((MOE_BLOCK * ROW_TILE, LANES), U32),
                        pltpu.SemaphoreType.DMA(()), pltpu.SemaphoreType.DMA(())],
        compiler_params=pltpu.CompilerParams(dimension_semantics=("arbitrary",), has_side_effects=True,
                                             vmem_limit_bytes=VMEM_LIMIT),
        name="dispatch",
    )(dest, tail_start, hp)


def _moe_kernel(ea_ref, eb_ref, valid_ref, hs_ref, wga_ref, wua_ref, wda_ref, wgb_ref, wub_ref, wdb_ref, y_ref):
    i = pl.program_id(0)
    blk = hs_ref.shape[0] // ROW_TILE
    d = wga_ref.shape[1]
    n_words = d // 2 // LANES

    @pl.when(valid_ref[i] == 1)
    def _():
        his, los = [], []
        for c in range(n_words):
            w = hs_ref[pl.ds(c, blk, stride=ROW_TILE), :]
            his.append(lax.bitcast_convert_type(w & jnp.uint32(0xFFFF0000), F32).astype(BF16))
            los.append(lax.bitcast_convert_type(w << 16, F32).astype(BF16))
        x = jnp.concatenate(his + los, axis=-1)
        gates = lax.bitcast_convert_type(hs_ref[pl.ds(n_words, blk, stride=ROW_TILE), :], F32)
        gate_a = gates[:, 0:1]
        gate_b = gates[:, 1:2]

        def mlp(wg, wu, wd):
            g = jnp.dot(x, wg[0], preferred_element_type=F32)
            u = jnp.dot(x, wu[0], preferred_element_type=F32)
            a = (g * _sigmoid(g) * u).astype(BF16)
            return jnp.dot(a, wd[0], preferred_element_type=F32)

        y = gate_a * mlp(wga_ref, wua_ref, wda_ref) + gate_b * mlp(wgb_ref, wub_ref, wdb_ref)
        for c in range(d // LANES):
            y_ref[pl.ds(c, blk, stride=ROW_TILE), :] = y[:, c * LANES:(c + 1) * LANES]

    @pl.when(valid_ref[i] != 1)
    def _():
        y_ref[...] = jnp.zeros_like(y_ref)


def _moe_call(blk_ea, blk_eb, blk_valid, hs, wg, wu, wd):
    nb = blk_ea.shape[0]
    d = wg.shape[1]
    de = wg.shape[2]
    rows = MOE_BLOCK * ROW_TILE
    tok = pl.BlockSpec((rows, LANES), lambda i, ea, eb, va: (i, 0))
    w_a = lambda shape: pl.BlockSpec((1,) + shape, lambda i, ea, eb, va: (ea[i], 0, 0))
    w_b = lambda shape: pl.BlockSpec((1,) + shape, lambda i, ea, eb, va: (eb[i], 0, 0))
    grid_spec = pltpu.PrefetchScalarGridSpec(
        num_scalar_prefetch=3,
        grid=(nb,),
        in_specs=[tok, w_a((d, de)), w_a((d, de)), w_a((de, d)), w_b((d, de)), w_b((d, de)), w_b((de, d))],
        out_specs=tok,
    )
    return pl.pallas_call(
        _moe_kernel,
        grid_spec=grid_spec,
        out_shape=jax.ShapeDtypeStruct((nb * rows, LANES), F32),
        compiler_params=pltpu.CompilerParams(dimension_semantics=("arbitrary",), vmem_limit_bytes=VMEM_LIMIT),
        name="moe",
    )(blk_ea, blk_eb, blk_valid, hs, wg, wu, wd, wg, wu, wd)


def _final_kernel(dest_ref, dest_next_ref, xmid_ref, mod_ref, fn_ref, y_hbm, o_ref, fbuf, sems):
    i = pl.program_id(0)
    n_steps = pl.num_programs(0)
    n = dest_ref.shape[0]
    d = xmid_ref.shape[1]

    def issue(idx_ref, slot):
        def body(o, carry):
            for k in range(DMA_UNROLL):
                t = o * DMA_UNROLL + k
                src = pl.multiple_of(idx_ref[t] * ROW_TILE, ROW_TILE)
                dst = pl.multiple_of(t * ROW_TILE, ROW_TILE)
                pltpu.make_async_copy(y_hbm.at[pl.ds(src, ROW_TILE)], fbuf.at[slot, pl.ds(dst, ROW_TILE)],
                                      sems.at[slot]).start(priority=k % 2)
            return carry

        lax.fori_loop(0, n // DMA_UNROLL, body, 0)

    slot = i % 2

    @pl.when(i == 0)
    def _():
        issue(dest_ref, 0)

    @pl.when(i + 1 < n_steps)
    def _():
        issue(dest_next_ref, 1 - slot)

    pltpu.make_async_copy(y_hbm.at[pl.ds(0, n * ROW_TILE)], fbuf.at[slot], sems.at[slot]).wait()
    f = jnp.concatenate([fbuf[slot, pl.ds(c, n, stride=ROW_TILE), :] for c in range(d // LANES)], axis=-1)
    x = xmid_ref[...] + mod_ref[0][5:6] * f
    ms = jnp.mean(x * x, axis=-1, keepdims=True)
    o_ref[...] = x * lax.rsqrt(ms + NORM_EPS) * fn_ref[...]


def _final_call(dest, xmid, mod3, fn, y, seq):
    t, d = xmid.shape
    tm = TM_ROWS
    tpb = seq // tm
    n_steps = t // tm
    return pl.pallas_call(
        _final_kernel,
        grid=(n_steps,),
        in_specs=[
            pl.BlockSpec((tm,), lambda i: (i,), memory_space=pltpu.SMEM),
            pl.BlockSpec((tm,), lambda i: (jnp.minimum(i + 1, n_steps - 1),), memory_space=pltpu.SMEM),
            pl.BlockSpec((tm, d), lambda i: (i, 0)),
            pl.BlockSpec((1, 6, d), lambda i: (i // tpb, 0, 0)),
            pl.BlockSpec((1, d), lambda i: (0, 0)),
            pl.BlockSpec(memory_space=pl.ANY),
        ],
        out_specs=pl.BlockSpec((tm, d), lambda i: (i, 0)),
        out_shape=jax.ShapeDtypeStruct((t, d), F32),
        scratch_shapes=[pltpu.VMEM((2, tm * ROW_TILE, LANES), F32), pltpu.SemaphoreType.DMA((2,))],
        compiler_params=pltpu.CompilerParams(dimension_semantics=("arbitrary",), vmem_limit_bytes=VMEM_LIMIT),
        name="final",
    )(dest, dest, xmid, mod3, fn, y)


def _pair_tables():
    ea, eb = [], []
    for g in range(N_GROUPS):
        for a in range(EXPERTS_PER_GROUP):
            for b in range(a + 1, EXPERTS_PER_GROUP):
                ea.append(g * EXPERTS_PER_GROUP + a)
                eb.append(g * EXPERTS_PER_GROUP + b)
    return np.asarray(ea, np.int32), np.asarray(eb, np.int32)


def _routing_plan(info, counts, t):
    cls = info[0].astype(jnp.int32)
    rank = info[1].astype(jnp.int32)
    cnt = counts[0, :N_CLASSES].astype(jnp.int32)
    padded = (cnt + MOE_BLOCK - 1) // MOE_BLOCK * MOE_BLOCK
    pad_end = jnp.cumsum(padded)
    pad_start = pad_end - padded
    cls_ids = jnp.arange(N_CLASSES, dtype=jnp.int32)
    dest = rank + jnp.sum(jnp.where(cls[:, None] == cls_ids[None, :], pad_start[None, :], 0), axis=1)
    nb = t // MOE_BLOCK + N_CLASSES
    nb_used = pad_end[-1] // MOE_BLOCK
    blk = jnp.arange(nb, dtype=jnp.int32)
    blk_cls = jnp.clip(jnp.searchsorted(pad_end, blk * MOE_BLOCK, side="right"), 0, N_CLASSES - 1)
    valid = blk < nb_used
    last_cls = blk_cls[jnp.maximum(nb_used - 1, 0)]
    blk_cls = jnp.where(valid, blk_cls, last_cls).astype(jnp.int32)
    tab_a, tab_b = _pair_tables()
    blk_ea = jnp.asarray(tab_a)[blk_cls]
    blk_eb = jnp.asarray(tab_b)[blk_cls]
    tail_start = jnp.where(cnt > 0, pad_end - MOE_BLOCK, -1).astype(jnp.int32)
    tail_start = jnp.concatenate([tail_start, nb_used[None].astype(jnp.int32)])
    return dest, tail_start, blk_ea, blk_eb, valid.astype(jnp.int32), nb


def kernel(x, c, ctx, c_ctx, w_mod, b_mod, norm_mix, w_in, rpb, w_att_out, conv_w, conv_b, conv_ln_g, conv_ln_b,
           w_conv_out, w_o, norm_ffn, w_router_group, b_router_group, w_router_expert, b_router_expert,
           w_exp_gate, w_exp_up, w_exp_down, final_norm):
    b, seq, d = x.shape
    n_ctx = ctx.shape[1]
    t = b * seq
    assert w_mod.shape[0] == 1, "single layer"
    assert seq % TM_ROWS == 0 and seq // GRID_W >= WIN_ROWS and n_ctx % LANES == 0 and (b * n_ctx) % 512 == 0
    assert d == 1024

    mod_rows = -(-(b + 1) // SUBLANES) * SUBLANES
    cc = jnp.zeros((mod_rows, d), F32).at[:b].set(c).at[b].set(c_ctx)
    m_all = _mod_call(cc, w_mod[0], b_mod[0][None, :])
    mod_lat = m_all[:b].reshape(b, 6, d)
    mod_ctx = m_all[b:b + 1].reshape(1, 6, d)

    x2 = x.reshape(t, d)
    g_mix = norm_mix[0][None, :]
    w_in_b = w_in[0].astype(BF16)
    wqv_t = jnp.concatenate([w_in_b[:, Q0:K0], w_in_b[:, V0:GLU0]], axis=1).T
    w_rest = jnp.concatenate([w_in_b[:, K0:V0], w_in_b[:, GLU0:]], axis=1)
    cw = jnp.zeros((32, D_CONV), F32).at[:CONV_WIDTH].set(conv_w[0])
    qt, k, vt, u, ga, gb = _proj_call(x2, mod_lat, g_mix, wqv_t, w_rest, seq)
    yn = _conv_call(u, cw, conv_b[0][None, :], conv_ln_g[0][None, :], conv_ln_b[0][None, :], seq)
    kc, vct = _ctx_kv_call(ctx.reshape(b * n_ctx, d), mod_ctx, g_mix, w_in_b[:, K0:V0], w_in_b[:, V0:GLU0].T)

    att = _attn_call(qt, k, vt, kc, vct, _bias_table(rpb[0]), seq, n_ctx)

    wr = jnp.zeros((d, LANES), F32).at[:, :N_GROUPS].set(w_router_group[0])
    wr = wr.at[:, N_GROUPS:N_GROUPS + N_EXPERTS].set(w_router_expert[0])
    wr_hi = wr.astype(BF16)
    wr = jnp.concatenate([wr_hi, (wr - wr_hi.astype(F32)).astype(BF16)], axis=1)
    br = jnp.zeros((1, LANES), F32).at[0, :N_GROUPS].set(b_router_group[0])
    br = br.at[0, N_GROUPS:N_GROUPS + N_EXPERTS].set(b_router_expert[0])
    x_mid, hp, info, counts = _merge_call(
        att, yn, ga, gb, x2, mod_lat, w_att_out[0].astype(BF16), w_conv_out[0].astype(BF16), w_o[0].astype(BF16), norm_ffn[0][None, :],
        wr, br, seq)

    dest, tail_start, blk_ea, blk_eb, blk_valid, nb = _routing_plan(info, counts, t)
    hs = _dispatch_call(dest, tail_start, hp, nb * MOE_BLOCK)
    y = _moe_call(blk_ea, blk_eb, blk_valid, hs,
                  w_exp_gate[0].astype(BF16), w_exp_up[0].astype(BF16), w_exp_down[0].astype(BF16))
    out = _final_call(dest, x_mid, mod_lat, final_norm[None, :], y, seq)
    return out.reshape(b, seq, d)
```

```python
import functools

import numpy as np
import jax
import jax.numpy as jnp
from jax import lax
from jax.experimental import pallas as pl
from jax.experimental.pallas import tpu as pltpu

F32 = jnp.float32
BF16 = jnp.bfloat16
U32 = jnp.uint32

GRID_W = 64
N_HEADS = 8
HEAD_DIM = 64
D_ATT = N_HEADS * HEAD_DIM
WIN_ROWS = 8
WIN_COLS = 16
D_CONV = 512
CONV_WIDTH = 31
N_GROUPS = 4
EXPERTS_PER_GROUP = 8
N_EXPERTS = N_GROUPS * EXPERTS_PER_GROUP
PAIRS_PER_GROUP = EXPERTS_PER_GROUP * (EXPERTS_PER_GROUP - 1) // 2
N_CLASSES = N_GROUPS * PAIRS_PER_GROUP
ROUTER_E0 = 8
NORM_EPS = 1e-6
NEG_INF = -1e30

LANES = 128
SUBLANES = 8
ROW_TILE = SUBLANES
HALO = 16
VMEM_LIMIT = 56 * 1024 * 1024

TM_PROJ = 512
TM_MERGE = 512
MERGE_PARTS = 1
TM_ROWS = 1024
MOE_BLOCK = 256
CONV_CHUNK = 64
DMA_UNROLL = 8

HIGHEST = lax.Precision.HIGHEST


def _norm_mod(x, g, shift, scale):
    ms = jnp.mean(x * x, axis=-1, keepdims=True)
    y = x * lax.rsqrt(ms + NORM_EPS) * g
    return y * (1.0 + scale) + shift


def _sigmoid(x):
    return jax.nn.sigmoid(x)


def _mod_kernel(c_ref, w_ref, b_ref, o_ref):
    c = c_ref[...]
    s = c * _sigmoid(c)
    o_ref[...] = jnp.dot(s, w_ref[...], precision=HIGHEST, preferred_element_type=F32) + b_ref[...]


def _mod_call(cc, w_mod, b_mod):
    rows, d = cc.shape
    n = w_mod.shape[1]
    tn = 1024
    return pl.pallas_call(
        _mod_kernel,
        grid=(n // tn,),
        in_specs=[
            pl.BlockSpec((rows, d), lambda j: (0, 0)),
            pl.BlockSpec((d, tn), lambda j: (0, j)),
            pl.BlockSpec((1, tn), lambda j: (0, j)),
        ],
        out_specs=pl.BlockSpec((rows, tn), lambda j: (0, j)),
        out_shape=jax.ShapeDtypeStruct((rows, n), F32),
        compiler_params=pltpu.CompilerParams(dimension_semantics=("arbitrary",), vmem_limit_bytes=VMEM_LIMIT),
        name="mod",
    )(cc, w_mod, b_mod)


Q0, K0, V0, GLU0 = 0, D_ATT, 2 * D_ATT, 3 * D_ATT
GA0 = GLU0 + 2 * D_CONV


NT_DIMS = (((1,), (1,)), ((), ()))
LOG2_E = 1.4426950408889634
SCORE_SCALE = HEAD_DIM ** -0.5 * LOG2_E


def _store_transposed(res_t, ref, scale=None):
    for j in range(ref.shape[0]):
        blk = res_t[:, j * LANES:(j + 1) * LANES]
        if scale is not None:
            blk = blk * scale
        ref[j] = blk.astype(ref.dtype)


def _conv_kernel(u_ref, up_ref, un_ref, cw_ref, cb_ref, lg_ref, lb_ref, yn_ref, ubuf, shifted,
                 *, tiles_per_seq):
    i = pl.program_id(0)
    tm = u_ref.shape[0]
    first = (i % tiles_per_seq) == 0
    last = (i % tiles_per_seq) == tiles_per_seq - 1
    ubuf[0:HALO, :] = jnp.where(first, 0.0, up_ref[...].astype(F32))
    for r0 in range(0, tm, CONV_CHUNK):
        ubuf[HALO + r0:HALO + r0 + CONV_CHUNK, :] = u_ref[r0:r0 + CONV_CHUNK, :].astype(F32)
    ubuf[HALO + tm:, :] = jnp.where(last, 0.0, un_ref[...].astype(F32))

    span = shifted.shape[1]
    for ph in range(SUBLANES):
        for r0 in range(0, span, CONV_CHUNK):
            n = min(CONV_CHUNK, span - r0)
            shifted[ph, r0:r0 + n, :] = ubuf[ph + r0:ph + r0 + n, :]

    base = HALO - CONV_WIDTH // 2
    inv_c = 1.0 / D_CONV
    for tc in range(tm // CONV_CHUNK):
        accs = []
        for lc in range(D_CONV // LANES):
            ls = slice(lc * LANES, (lc + 1) * LANES)
            acc = jnp.zeros((CONV_CHUNK, LANES), F32)
            for j in range(CONV_WIDTH):
                tiles, ph = divmod(base + j, SUBLANES)
                lo = tc * CONV_CHUNK + tiles * SUBLANES
                acc = acc + shifted[ph, lo:lo + CONV_CHUNK, ls] * cw_ref[j:j + 1, ls]
            accs.append(acc + cb_ref[:, ls])
        mu = sum(jnp.sum(a, axis=-1, keepdims=True) for a in accs) * inv_c
        cen = [a - mu for a in accs]
        var = sum(jnp.sum(c * c, axis=-1, keepdims=True) for c in cen) * inv_c
        rstd = lax.rsqrt(var + NORM_EPS)
        for lc, c in enumerate(cen):
            ls = slice(lc * LANES, (lc + 1) * LANES)
            yn = c * rstd * lg_ref[:, ls] + lb_ref[:, ls]
            yn_ref[tc * CONV_CHUNK:(tc + 1) * CONV_CHUNK, ls] = (yn * _sigmoid(yn)).astype(yn_ref.dtype)


def _conv_call(u, cw, cb, lg, lb, seq):
    t = u.shape[0]
    tm = TM_MERGE
    tpb = seq // tm
    hb = tm // HALO
    n_halo = t // HALO
    row = lambda i: (i, 0)
    const = lambda i: (0, 0)
    return pl.pallas_call(
        functools.partial(_conv_kernel, tiles_per_seq=tpb),
        grid=(t // tm,),
        in_specs=[
            pl.BlockSpec((tm, D_CONV), row),
            pl.BlockSpec((HALO, D_CONV), lambda i: (jnp.maximum(i * hb - 1, 0), 0)),
            pl.BlockSpec((HALO, D_CONV), lambda i: (jnp.minimum((i + 1) * hb, n_halo - 1), 0)),
            pl.BlockSpec(cw.shape, const),
            pl.BlockSpec((1, D_CONV), const),
            pl.BlockSpec((1, D_CONV), const),
            pl.BlockSpec((1, D_CONV), const),
        ],
        out_specs=pl.BlockSpec((tm, D_CONV), row),
        out_shape=jax.ShapeDtypeStruct((t, D_CONV), BF16),
        scratch_shapes=[
            pltpu.VMEM((tm + 2 * HALO, D_CONV), F32),
            pltpu.VMEM((SUBLANES, tm + 2 * HALO - SUBLANES, D_CONV), F32),
        ],
        compiler_params=pltpu.CompilerParams(dimension_semantics=("arbitrary",), vmem_limit_bytes=VMEM_LIMIT),
        name="conv",
    )(u, u, u, cw, cb, lg, lb)


def _proj_kernel(x_ref, mod_ref, g_ref, wqv_ref, w_ref, qt_ref, k_ref, vt_ref, u_ref, ga_ref, gb_ref):
    d = x_ref.shape[1]
    mod = mod_ref[0]
    h = _norm_mod(x_ref[...], g_ref[...], mod[0:1], mod[1:2]).astype(BF16)

    qv_t = lax.dot_general(wqv_ref[...], h, NT_DIMS, preferred_element_type=F32)
    _store_transposed(qv_t[:D_ATT], qt_ref, SCORE_SCALE)
    _store_transposed(qv_t[D_ATT:], vt_ref)

    def seg(lo, hi):
        return jnp.dot(h, w_ref[:, lo:hi], preferred_element_type=F32)

    k_ref[...] = seg(0, D_ATT).astype(BF16)
    a = seg(D_ATT, D_ATT + D_CONV)
    g = seg(D_ATT + D_CONV, D_ATT + 2 * D_CONV)
    u_ref[...] = (a * _sigmoid(g)).astype(BF16)
    g0 = D_ATT + 2 * D_CONV
    ga_ref[...] = _sigmoid(seg(g0, g0 + d)).astype(BF16)
    gb_ref[...] = _sigmoid(seg(g0 + d, g0 + 2 * d)).astype(BF16)


def _proj_call(x2, mod3, g, wqv_t, w_rest, seq):
    t, d = x2.shape
    tm = TM_PROJ
    tpb = seq // tm
    nblk = tm // LANES
    outs = [
        jax.ShapeDtypeStruct((t // LANES, D_ATT, LANES), BF16),
        jax.ShapeDtypeStruct((t, D_ATT), BF16),
        jax.ShapeDtypeStruct((t // LANES, D_ATT, LANES), BF16),
        jax.ShapeDtypeStruct((t, D_CONV), BF16),
        jax.ShapeDtypeStruct((t, d), BF16),
        jax.ShapeDtypeStruct((t, d), BF16),
    ]
    row = lambda i: (i, 0)
    blk3 = lambda i: (i, 0, 0)
    return pl.pallas_call(
        _proj_kernel,
        grid=(t // tm,),
        in_specs=[
            pl.BlockSpec((tm, d), row),
            pl.BlockSpec((1, 6, d), lambda i: (i // tpb, 0, 0)),
            pl.BlockSpec((1, d), lambda i: (0, 0)),
            pl.BlockSpec(wqv_t.shape, lambda i: (0, 0)),
            pl.BlockSpec(w_rest.shape, lambda i: (0, 0)),
        ],
        out_specs=[
            pl.BlockSpec((nblk, D_ATT, LANES), blk3),
            pl.BlockSpec((tm, D_ATT), row),
            pl.BlockSpec((nblk, D_ATT, LANES), blk3),
            pl.BlockSpec((tm, D_CONV), row),
            pl.BlockSpec((tm, d), row),
            pl.BlockSpec((tm, d), row),
        ],
        out_shape=outs,
        compiler_params=pltpu.CompilerParams(dimension_semantics=("arbitrary",), vmem_limit_bytes=VMEM_LIMIT),
        name="proj",
    )(x2, mod3, g, wqv_t, w_rest)


def _ctx_kv_kernel(x_ref, mod_ref, g_ref, wk_ref, wvt_ref, k_ref, vt_ref):
    mod = mod_ref[0]
    h = _norm_mod(x_ref[...], g_ref[...], mod[0:1], mod[1:2]).astype(BF16)
    k_ref[...] = jnp.dot(h, wk_ref[...], preferred_element_type=F32).astype(BF16)
    _store_transposed(lax.dot_general(wvt_ref[...], h, NT_DIMS, preferred_element_type=F32), vt_ref)


def _ctx_kv_call(c2, mod3, g, w_k, w_vt):
    t, d = c2.shape
    tm = 512
    nblk = tm // LANES
    row = lambda i: (i, 0)
    return pl.pallas_call(
        _ctx_kv_kernel,
        grid=(t // tm,),
        in_specs=[
            pl.BlockSpec((tm, d), row),
            pl.BlockSpec((1, 6, d), lambda i: (0, 0, 0)),
            pl.BlockSpec((1, d), lambda i: (0, 0)),
            pl.BlockSpec(w_k.shape, lambda i: (0, 0)),
            pl.BlockSpec(w_vt.shape, lambda i: (0, 0)),
        ],
        out_specs=[pl.BlockSpec((tm, D_ATT), row), pl.BlockSpec((nblk, D_ATT, LANES), lambda i: (i, 0, 0))],
        out_shape=[jax.ShapeDtypeStruct((t, D_ATT), BF16), jax.ShapeDtypeStruct((t // LANES, D_ATT, LANES), BF16)],
        compiler_params=pltpu.CompilerParams(dimension_semantics=("arbitrary",), vmem_limit_bytes=VMEM_LIMIT),
        name="ctx_kv",
    )(c2, mod3, g, w_k, w_vt)


HEADS_PER_GROUP = 4
GROUP_W = HEADS_PER_GROUP * HEAD_DIM
N_HEAD_GROUPS = N_HEADS // HEADS_PER_GROUP


def _attn_kernel(qt_ref, k_ref, vt_ref, kc_ref, vct_ref, bias_ref, o_ref, vboth):
    nblk = qt_ref.shape[0]
    rows = 2 * nblk
    n_loc = WIN_ROWS * GRID_W
    n_ctx_blk = vct_ref.shape[0]
    half = GRID_W

    for j in range(nblk):
        vboth[0, j] = vt_ref[j]
    for j in range(nblk - 1):
        vboth[1, j] = jnp.concatenate([vt_ref[j][:, half:], vt_ref[j + 1][:, :half]], axis=1)
    vboth[1, nblk - 1] = jnp.zeros_like(vt_ref[0])

    lane = lax.broadcasted_iota(jnp.int32, (D_ATT, LANES), 1)
    low = lane < half
    rblk = lax.broadcasted_iota(jnp.int32, (GROUP_W, GROUP_W), 0) // HEAD_DIM
    cblk = lax.broadcasted_iota(jnp.int32, (GROUP_W, GROUP_W), 1) // HEAD_DIM
    diag = rblk == cblk
    low64 = lax.broadcasted_iota(jnp.int32, (HEAD_DIM, LANES), 1) < half
    kc = kc_ref[...]

    def one_row(row, tiled, side):
        r_start = jnp.clip(row - WIN_ROWS // 2, 0, rows - WIN_ROWS)
        b_off = pl.multiple_of((WIN_ROWS - 1 - (row - r_start)) * GRID_W, GRID_W)
        kw = k_ref[pl.ds(pl.multiple_of(r_start * GRID_W, GRID_W), n_loc), :]
        vwin = vboth[r_start % 2, pl.ds(r_start // 2, n_loc // LANES)]
        parts = []
        for g in range(N_HEAD_GROUPS):
            fs = slice(g * GROUP_W, (g + 1) * GROUP_W)
            tg = tiled[fs, :]
            w = jnp.where(diag, jnp.concatenate([tg, tg], axis=1), jnp.zeros((), BF16))
            s_loc = jnp.dot(kw[:, fs], w, preferred_element_type=F32) + bias_ref[g, pl.ds(b_off, n_loc), :]
            s_ctx = jnp.dot(kc[:, fs], w, preferred_element_type=F32)
            m = jnp.maximum(jnp.max(s_loc, axis=0, keepdims=True), jnp.max(s_ctx, axis=0, keepdims=True))
            p_loc = jnp.exp2(s_loc - m)
            p_ctx = jnp.exp2(s_ctx - m)
            l = jnp.sum(p_loc, axis=0, keepdims=True) + jnp.sum(p_ctx, axis=0, keepdims=True)
            p = jnp.concatenate([p_loc.astype(BF16), p_ctx.astype(BF16)], axis=0)
            vt_g = jnp.concatenate([vwin[j][fs, :] for j in range(n_loc // LANES)]
                                   + [vct_ref[j][fs, :] for j in range(n_ctx_blk)], axis=1)
            o_t = jnp.dot(vt_g, p, preferred_element_type=F32) * (1.0 / l)
            for j in range(HEADS_PER_GROUP):
                blk = o_t[j * HEAD_DIM:(j + 1) * HEAD_DIM, (j // 2) * LANES:(j // 2 + 1) * LANES]
                if j % 2 != side:
                    blk = pltpu.roll(blk, half, axis=1)
                parts.append(blk)
        return parts

    def body(i, carry):
        xq = qt_ref[i]
        rolled = jnp.concatenate([xq[:, half:], xq[:, :half]], axis=1)
        parts_a = one_row(2 * i, jnp.where(low, xq, rolled), 0)
        parts_b = one_row(2 * i + 1, jnp.where(low, rolled, xq), 1)
        out = jnp.concatenate([jnp.where(low64, a, b) for a, b in zip(parts_a, parts_b)], axis=0)
        o_ref[i] = out.astype(o_ref.dtype)
        return carry

    lax.fori_loop(0, nblk, body, 0, unroll=4)


def _attn_call(qt, k, vt, kc, vct, bias, seq, n_ctx):
    t = k.shape[0]
    b = t // seq
    nblk = seq // LANES
    lat3 = pl.BlockSpec((nblk, D_ATT, LANES), lambda i: (i, 0, 0))
    return pl.pallas_call(
        _attn_kernel,
        grid=(b,),
        in_specs=[lat3,
                  pl.BlockSpec((seq, D_ATT), lambda i: (i, 0)),
                  lat3,
                  pl.BlockSpec((n_ctx, D_ATT), lambda i: (i, 0)),
                  pl.BlockSpec((n_ctx // LANES, D_ATT, LANES), lambda i: (i, 0, 0)),
                  pl.BlockSpec(bias.shape, lambda i: (0, 0, 0))],
        out_specs=lat3,
        out_shape=jax.ShapeDtypeStruct((t // LANES, D_ATT, LANES), BF16),
        scratch_shapes=[pltpu.VMEM((2, nblk, D_ATT, LANES), BF16)],
        compiler_params=pltpu.CompilerParams(dimension_semantics=("arbitrary",), vmem_limit_bytes=VMEM_LIMIT),
        name="attn",
    )(qt, k, vt, kc, vct, bias)


def _bias_table(rpb):
    cq = np.arange(GRID_W)[:, None]
    ck = np.arange(GRID_W)[None, :]
    c_start = np.clip(cq - WIN_COLS // 2, 0, GRID_W - WIN_COLS)
    col_mask = (ck >= c_start) & (ck < c_start + WIN_COLS)
    dc_idx = np.clip(ck - cq + WIN_COLS - 1, 0, 2 * WIN_COLS - 2)
    tab = rpb[:, :, dc_idx]
    tab = jnp.where(col_mask[None, None], tab, NEG_INF)
    n_dr = tab.shape[1]
    tab = tab.reshape(N_HEAD_GROUPS, HEADS_PER_GROUP, n_dr, GRID_W, GRID_W)
    tab = tab.transpose(0, 2, 4, 1, 3)
    return (tab.reshape(N_HEAD_GROUPS, n_dr * GRID_W, GROUP_W) * LOG2_E).astype(F32)


def _merge_rows(r0, n, att_ref, yn_ref, ga_ref, gb_ref, x_ref, mod,
                wa_ref, wc_ref, wo_ref, nf_ref, wr_ref, br_ref, xmid_ref, hp_ref):
    d = x_ref.shape[1]
    rs = slice(r0, r0 + n)
    y_conv = jnp.dot(yn_ref[rs, :], wc_ref[...], preferred_element_type=F32)

    att = jnp.concatenate([att_ref[j].astype(F32).T for j in range(r0 // LANES, (r0 + n) // LANES)], axis=0)
    y_att = jnp.dot(att.astype(BF16), wa_ref[...], preferred_element_type=F32)
    mix = ga_ref[rs, :].astype(F32) * y_att + gb_ref[rs, :].astype(F32) * y_conv
    y = jnp.dot(mix.astype(BF16), wo_ref[...], preferred_element_type=F32)
    x_mid = x_ref[rs, :] + mod[2:3] * y
    xmid_ref[rs, :] = x_mid
    h2 = _norm_mod(x_mid, nf_ref[...], mod[3:4], mod[4:5])

    h2_hi = h2.astype(BF16)
    h2_lo = (h2 - h2_hi.astype(F32)).astype(BF16)
    t_hi = jnp.dot(h2_hi, wr_ref[...], preferred_element_type=F32)
    t_lo = jnp.dot(h2_lo, wr_ref[:, :LANES], preferred_element_type=F32)
    logits = t_hi[:, :LANES] + t_hi[:, LANES:] + t_lo + br_ref[...]
    lt = logits.T
    sub = lax.broadcasted_iota(jnp.int32, (SUBLANES, n), 0).astype(F32)
    big = float(LANES)
    is_g = sub < N_GROUPS
    gl = jnp.where(is_g, lt[0:SUBLANES], -jnp.inf)
    gmax = jnp.max(gl, axis=0, keepdims=True)
    g_idx = jnp.min(jnp.where(gl == gmax, sub, big), axis=0, keepdims=True)
    p_group = 1.0 / jnp.sum(jnp.where(is_g, jnp.exp(gl - gmax), 0.0), axis=0, keepdims=True)
    el = lt[ROUTER_E0:ROUTER_E0 + EXPERTS_PER_GROUP]
    for g in range(1, N_GROUPS):
        lo_g = ROUTER_E0 + g * EXPERTS_PER_GROUP
        el = jnp.where(g_idx == float(g), lt[lo_g:lo_g + EXPERTS_PER_GROUP], el)
    v1 = jnp.max(el, axis=0, keepdims=True)
    j1 = jnp.min(jnp.where(el == v1, sub, big), axis=0, keepdims=True)
    el2 = jnp.where(sub == j1, -jnp.inf, el)
    v2 = jnp.max(el2, axis=0, keepdims=True)
    j2 = jnp.min(jnp.where(el2 == v2, sub, big), axis=0, keepdims=True)
    e2 = jnp.exp(v2 - v1)
    gate1 = p_group / (1.0 + e2)
    gate2 = p_group * e2 / (1.0 + e2)
    ja = jnp.minimum(j1, j2)
    jb = jnp.maximum(j1, j2)
    gate_a = jnp.where(j1 < j2, gate1, gate2)
    gate_b = jnp.where(j1 < j2, gate2, gate1)
    pair = ja * (2 * EXPERTS_PER_GROUP - 1 - ja) * 0.5 + (jb - ja - 1.0)
    cls = g_idx * PAIRS_PER_GROUP + pair
    gate_rows = jnp.where(sub == 0.0, gate_a, jnp.where(sub == 1.0, gate_b, 0.0))
    gates = jnp.concatenate([gate_rows, jnp.zeros((LANES - SUBLANES, n), F32)], axis=0).T

    half = d // 2
    hi = lax.bitcast_convert_type(h2[:, :half].astype(BF16).astype(F32), U32)
    lo = lax.bitcast_convert_type(h2[:, half:].astype(BF16).astype(F32), U32)
    words = hi | (lo >> 16)
    n_words = half // LANES
    base = r0 * ROW_TILE
    for c in range(n_words):
        hp_ref[pl.ds(base + c, n, stride=ROW_TILE), :] = words[:, c * LANES:(c + 1) * LANES]
    hp_ref[pl.ds(base + n_words, n, stride=ROW_TILE), :] = lax.bitcast_convert_type(gates, U32)
    zero = jnp.zeros((n, LANES), U32)
    for c in range(n_words + 1, ROW_TILE):
        hp_ref[pl.ds(base + c, n, stride=ROW_TILE), :] = zero
    return cls


def _merge_kernel(att_ref, yn_ref, ga_ref, gb_ref, x_ref, mod_ref,
                  wa_ref, wc_ref, wo_ref, nf_ref, wr_ref, br_ref, tri_ref,
                  xmid_ref, hp_ref, info_ref, cnt_ref, run_ref):
    i = pl.program_id(0)
    tm = x_ref.shape[0]

    @pl.when(i == 0)
    def _():
        run_ref[...] = jnp.zeros_like(run_ref)

    mod = mod_ref[0]
    n = tm // MERGE_PARTS
    cls = jnp.concatenate(
        [_merge_rows(p * n, n, att_ref, yn_ref, ga_ref, gb_ref, x_ref, mod, wa_ref, wc_ref, wo_ref, nf_ref,
                     wr_ref, br_ref, xmid_ref, hp_ref) for p in range(MERGE_PARTS)], axis=1)

    crow = lax.broadcasted_iota(jnp.int32, (LANES, tm), 0).astype(F32)
    oh_f = jnp.where(crow == cls, 1.0, 0.0)
    before = jnp.dot(oh_f.astype(BF16), tri_ref[...], preferred_element_type=F32)
    run = run_ref[:, 0:1]
    rank = jnp.sum(oh_f * (before + run), axis=0, keepdims=True)
    new_run = run + jnp.sum(oh_f, axis=1, keepdims=True)
    run_ref[...] = jnp.broadcast_to(new_run, run_ref.shape)
    cnt_ref[...] = jnp.broadcast_to(new_run, cnt_ref.shape)

    sub = lax.broadcasted_iota(jnp.int32, (SUBLANES, tm), 0)
    info_ref[...] = jnp.where(sub == 0, cls, jnp.where(sub == 1, rank, 0.0))


def _merge_call(att, yn, ga, gb, x2, mod3, wa, wc, wo, nf, wr, br, seq):
    t, d = x2.shape
    tm = TM_MERGE
    tpb = seq // tm
    tri = jnp.asarray(np.triu(np.ones((tm, tm), np.float32), 1), BF16)
    row = lambda i: (i, 0)
    const = lambda i: (0, 0)
    return pl.pallas_call(
        _merge_kernel,
        grid=(t // tm,),
        in_specs=[
            pl.BlockSpec((tm // LANES, D_ATT, LANES), lambda i: (i, 0, 0)),
            pl.BlockSpec((tm, D_CONV), row),
            pl.BlockSpec((tm, d), row),
            pl.BlockSpec((tm, d), row),
            pl.BlockSpec((tm, d), row),
            pl.BlockSpec((1, 6, d), lambda i: (i // tpb, 0, 0)),
            pl.BlockSpec(wa.shape, const),
            pl.BlockSpec(wc.shape, const),
            pl.BlockSpec(wo.shape, const),
            pl.BlockSpec((1, d), const),
            pl.BlockSpec(wr.shape, const),
            pl.BlockSpec((1, LANES), const),
            pl.BlockSpec((tm, tm), const),
        ],
        out_specs=[
            pl.BlockSpec((tm, d), row),
            pl.BlockSpec((tm * ROW_TILE, LANES), row),
            pl.BlockSpec((SUBLANES, tm), lambda i: (0, i)),
            pl.BlockSpec((LANES, LANES), const),
        ],
        out_shape=[
            jax.ShapeDtypeStruct((t, d), F32),
            jax.ShapeDtypeStruct((t * ROW_TILE, LANES), U32),
            jax.ShapeDtypeStruct((SUBLANES, t), F32),
            jax.ShapeDtypeStruct((LANES, LANES), F32),
        ],
        scratch_shapes=[pltpu.VMEM((LANES, LANES), F32)],
        compiler_params=pltpu.CompilerParams(dimension_semantics=("arbitrary",), vmem_limit_bytes=VMEM_LIMIT),
        name="merge",
    )(att, yn, ga, gb, x2, mod3, wa, wc, wo, nf, wr, br, tri)


def _dispatch_kernel(dest_ref, tail_ref, hp_ref, hs_hbm, zbuf, sem, zsem):
    n = dest_ref.shape[0]
    blk_rows = zbuf.shape[0]

    @pl.when(pl.program_id(0) == 0)
    def _():
        zbuf[...] = jnp.zeros_like(zbuf)

        def tail_copy(c):
            start = pl.multiple_of(tail_ref[c] * ROW_TILE, ROW_TILE)
            return pltpu.make_async_copy(zbuf, hs_hbm.at[pl.ds(start, blk_rows)], zsem)

        def start_one(c, carry):
            @pl.when(tail_ref[c] >= 0)
            def _():
                tail_copy(c).start()
            return carry

        def wait_one(c, carry):
            @pl.when(tail_ref[c] >= 0)
            def _():
                tail_copy(c).wait()
            return carry

        lax.fori_loop(0, N_CLASSES, start_one, 0)
        lax.fori_loop(0, N_CLASSES, wait_one, 0)

        def unused_copy(b):
            start = pl.multiple_of(b * blk_rows, blk_rows)
            return pltpu.make_async_copy(zbuf, hs_hbm.at[pl.ds(start, blk_rows)], zsem)

        def start_unused(b, carry):
            unused_copy(b).start()
            return carry

        def wait_unused(b, carry):
            unused_copy(b).wait()
            return carry

        n_blocks = hs_hbm.shape[0] // blk_rows
        lax.fori_loop(tail_ref[N_CLASSES], n_blocks, start_unused, 0)
        lax.fori_loop(tail_ref[N_CLASSES], n_blocks, wait_unused, 0)

    def body(o, carry):
        for k in range(DMA_UNROLL):
            t = o * DMA_UNROLL + k
            src = pl.multiple_of(t * ROW_TILE, ROW_TILE)
            dst = pl.multiple_of(dest_ref[t] * ROW_TILE, ROW_TILE)
            pltpu.make_async_copy(hp_ref.at[pl.ds(src, ROW_TILE)], hs_hbm.at[pl.ds(dst, ROW_TILE)],
                                  sem).start(priority=k % 2)
        return carry

    lax.fori_loop(0, n // DMA_UNROLL, body, 0)
    pltpu.make_async_copy(hp_ref, hs_hbm.at[pl.ds(0, n * ROW_TILE)], sem).wait()


def _dispatch_call(dest, tail_start, hp, p_rows):
    t = dest.shape[0]
    return pl.pallas_call(
        _dispatch_kernel,
        grid=(t // TM_ROWS,),
        in_specs=[
            pl.BlockSpec((TM_ROWS,), lambda i: (i,), memory_space=pltpu.SMEM),
            pl.BlockSpec(memory_space=pltpu.SMEM),
            pl.BlockSpec((TM_ROWS * ROW_TILE, LANES), lambda i: (i, 0)),
        ],
        out_specs=pl.BlockSpec(memory_space=pl.ANY),
        out_shape=jax.ShapeDtypeStruct((p_rows * ROW_TILE, LANES), U32),
        scratch_shapes=[pltpu.VMEM((MOE_BLOCK * ROW_TILE, LANES), U32),
                        pltpu.SemaphoreType.DMA(()), pltpu.SemaphoreType.DMA(())],
        compiler_params=pltpu.CompilerParams(dimension_semantics=("arbitrary",), has_side_effects=True,
                                             vmem_limit_bytes=VMEM_LIMIT),
        name="dispatch",
    )(dest, tail_start, hp)


def _moe_kernel(ea_ref, eb_ref, rows_ref, hs_ref, wga_ref, wua_ref, wda_ref, wgb_ref, wub_ref, wdb_ref, y_ref):
    i = pl.program_id(0)
    blk = hs_ref.shape[0] // ROW_TILE
    d = wga_ref.shape[1]
    n_words = d // 2 // LANES
    n_real = rows_ref[i]

    def run(n):
        his, los = [], []
        for c in range(n_words):
            w = hs_ref[pl.ds(c, n, stride=ROW_TILE), :]
            his.append(lax.bitcast_convert_type(w & jnp.uint32(0xFFFF0000), F32).astype(BF16))
            los.append(lax.bitcast_convert_type(w << 16, F32).astype(BF16))
        x = jnp.concatenate(his + los, axis=-1)
        gates = lax.bitcast_convert_type(hs_ref[pl.ds(n_words, n, stride=ROW_TILE), :], F32)
        gate_a = gates[:, 0:1]
        gate_b = gates[:, 1:2]

        def mlp(wg, wu, wd):
            g = jnp.dot(x, wg[0], preferred_element_type=F32)
            u = jnp.dot(x, wu[0], preferred_element_type=F32)
            a = (g * _sigmoid(g) * u).astype(BF16)
            return jnp.dot(a, wd[0], preferred_element_type=F32)

        y = gate_a * mlp(wga_ref, wua_ref, wda_ref) + gate_b * mlp(wgb_ref, wub_ref, wdb_ref)
        for c in range(d // LANES):
            y_ref[pl.ds(c, n, stride=ROW_TILE), :] = y[:, c * LANES:(c + 1) * LANES]

    @pl.when(n_real > blk // 2)
    def _():
        run(blk)

    @pl.when((n_real > 0) & (n_real <= blk // 2))
    def _():
        run(blk // 2)
        y_ref[blk // 2 * ROW_TILE:, :] = jnp.zeros((blk // 2 * ROW_TILE, LANES), y_ref.dtype)

    @pl.when(n_real == 0)
    def _():
        y_ref[...] = jnp.zeros_like(y_ref)


def _moe_call(blk_ea, blk_eb, blk_rows, hs, wg, wu, wd):
    nb = blk_ea.shape[0]
    d = wg.shape[1]
    de = wg.shape[2]
    rows = MOE_BLOCK * ROW_TILE
    tok = pl.BlockSpec((rows, LANES), lambda i, ea, eb, va: (i, 0))
    w_a = lambda shape: pl.BlockSpec((1,) + shape, lambda i, ea, eb, va: (ea[i], 0, 0))
    w_b = lambda shape: pl.BlockSpec((1,) + shape, lambda i, ea, eb, va: (eb[i], 0, 0))
    grid_spec = pltpu.PrefetchScalarGridSpec(
        num_scalar_prefetch=3,
        grid=(nb,),
        in_specs=[tok, w_a((d, de)), w_a((d, de)), w_a((de, d)), w_b((d, de)), w_b((d, de)), w_b((de, d))],
        out_specs=tok,
    )
    return pl.pallas_call(
        _moe_kernel,
        grid_spec=grid_spec,
        out_shape=jax.ShapeDtypeStruct((nb * rows, LANES), F32),
        compiler_params=pltpu.CompilerParams(dimension_semantics=("arbitrary",), vmem_limit_bytes=VMEM_LIMIT),
        name="moe",
    )(blk_ea, blk_eb, blk_rows, hs, wg, wu, wd, wg, wu, wd)


def _final_kernel(dest_ref, dest_next_ref, xmid_ref, mod_ref, fn_ref, y_hbm, o_ref, fbuf, sems):
    i = pl.program_id(0)
    n_steps = pl.num_programs(0)
    n = dest_ref.shape[0]
    d = xmid_ref.shape[1]

    def issue(idx_ref, slot):
        def body(o, carry):
            for k in range(DMA_UNROLL):
                t = o * DMA_UNROLL + k
                src = pl.multiple_of(idx_ref[t] * ROW_TILE, ROW_TILE)
                dst = pl.multiple_of(t * ROW_TILE, ROW_TILE)
                pltpu.make_async_copy(y_hbm.at[pl.ds(src, ROW_TILE)], fbuf.at[slot, pl.ds(dst, ROW_TILE)],
                                      sems.at[slot]).start(priority=k % 2)
            return carry

        lax.fori_loop(0, n // DMA_UNROLL, body, 0)

    slot = i % 2

    @pl.when(i == 0)
    def _():
        issue(dest_ref, 0)

    @pl.when(i + 1 < n_steps)
    def _():
        issue(dest_next_ref, 1 - slot)

    pltpu.make_async_copy(y_hbm.at[pl.ds(0, n * ROW_TILE)], fbuf.at[slot], sems.at[slot]).wait()
    f = jnp.concatenate([fbuf[slot, pl.ds(c, n, stride=ROW_TILE), :] for c in range(d // LANES)], axis=-1)
    x = xmid_ref[...] + mod_ref[0][5:6] * f
    ms = jnp.mean(x * x, axis=-1, keepdims=True)
    o_ref[...] = x * lax.rsqrt(ms + NORM_EPS) * fn_ref[...]


def _final_call(dest, xmid, mod3, fn, y, seq):
    t, d = xmid.shape
    tm = TM_ROWS
    tpb = seq // tm
    n_steps = t // tm
    return pl.pallas_call(
        _final_kernel,
        grid=(n_steps,),
        in_specs=[
            pl.BlockSpec((tm,), lambda i: (i,), memory_space=pltpu.SMEM),
            pl.BlockSpec((tm,), lambda i: (jnp.minimum(i + 1, n_steps - 1),), memory_space=pltpu.SMEM),
            pl.BlockSpec((tm, d), lambda i: (i, 0)),
            pl.BlockSpec((1, 6, d), lambda i: (i // tpb, 0, 0)),
            pl.BlockSpec((1, d), lambda i: (0, 0)),
            pl.BlockSpec(memory_space=pl.ANY),
        ],
        out_specs=pl.BlockSpec((tm, d), lambda i: (i, 0)),
        out_shape=jax.ShapeDtypeStruct((t, d), F32),
        scratch_shapes=[pltpu.VMEM((2, tm * ROW_TILE, LANES), F32), pltpu.SemaphoreType.DMA((2,))],
        compiler_params=pltpu.CompilerParams(dimension_semantics=("arbitrary",), vmem_limit_bytes=VMEM_LIMIT),
        name="final",
    )(dest, dest, xmid, mod3, fn, y)


def _pair_tables():
    ea, eb = [], []
    for g in range(N_GROUPS):
        for a in range(EXPERTS_PER_GROUP):
            for b in range(a + 1, EXPERTS_PER_GROUP):
                ea.append(g * EXPERTS_PER_GROUP + a)
                eb.append(g * EXPERTS_PER_GROUP + b)
    return np.asarray(ea, np.int32), np.asarray(eb, np.int32)


def _routing_plan(info, counts, t):
    cls = info[0].astype(jnp.int32)
    rank = info[1].astype(jnp.int32)
    cnt = counts[:N_CLASSES, 0].astype(jnp.int32)
    padded = (cnt + MOE_BLOCK - 1) // MOE_BLOCK * MOE_BLOCK
    pad_end = jnp.cumsum(padded)
    pad_start = pad_end - padded
    cls_ids = jnp.arange(N_CLASSES, dtype=jnp.int32)
    dest = rank + jnp.sum(jnp.where(cls[:, None] == cls_ids[None, :], pad_start[None, :], 0), axis=1)
    nb = t // MOE_BLOCK + N_CLASSES
    nb_used = pad_end[-1] // MOE_BLOCK
    blk = jnp.arange(nb, dtype=jnp.int32)
    blk_cls = jnp.clip(jnp.searchsorted(pad_end, blk * MOE_BLOCK, side="right"), 0, N_CLASSES - 1)
    valid = blk < nb_used
    last_cls = blk_cls[jnp.maximum(nb_used - 1, 0)]
    blk_cls = jnp.where(valid, blk_cls, last_cls).astype(jnp.int32)
    tab_a, tab_b = _pair_tables()
    blk_ea = jnp.asarray(tab_a)[blk_cls]
    blk_eb = jnp.asarray(tab_b)[blk_cls]
    tail_start = jnp.where(cnt > 0, pad_end - MOE_BLOCK, -1).astype(jnp.int32)
    tail_start = jnp.concatenate([tail_start, nb_used[None].astype(jnp.int32)])
    blk_rows = jnp.clip(cnt[blk_cls] - (blk * MOE_BLOCK - pad_start[blk_cls]), 0, MOE_BLOCK)
    blk_rows = jnp.where(valid, blk_rows, 0).astype(jnp.int32)
    return dest, tail_start, blk_ea, blk_eb, blk_rows, nb


def kernel(x, c, ctx, c_ctx, w_mod, b_mod, norm_mix, w_in, rpb, w_att_out, conv_w, conv_b, conv_ln_g, conv_ln_b,
           w_conv_out, w_o, norm_ffn, w_router_group, b_router_group, w_router_expert, b_router_expert,
           w_exp_gate, w_exp_up, w_exp_down, final_norm):
    b, seq, d = x.shape
    n_ctx = ctx.shape[1]
    t = b * seq
    assert w_mod.shape[0] == 1, "single layer"
    assert seq % TM_ROWS == 0 and seq // GRID_W >= WIN_ROWS and n_ctx % LANES == 0 and (b * n_ctx) % 512 == 0
    assert d == 1024

    mod_rows = -(-(b + 1) // SUBLANES) * SUBLANES
    cc = jnp.zeros((mod_rows, d), F32).at[:b].set(c).at[b].set(c_ctx)
    m_all = _mod_call(cc, w_mod[0], b_mod[0][None, :])
    mod_lat = m_all[:b].reshape(b, 6, d)
    mod_ctx = m_all[b:b + 1].reshape(1, 6, d)

    x2 = x.reshape(t, d)
    g_mix = norm_mix[0][None, :]
    w_in_b = w_in[0].astype(BF16)
    wqv_t = jnp.concatenate([w_in_b[:, Q0:K0], w_in_b[:, V0:GLU0]], axis=1).T
    w_rest = jnp.concatenate([w_in_b[:, K0:V0], w_in_b[:, GLU0:]], axis=1)
    cw = jnp.zeros((32, D_CONV), F32).at[:CONV_WIDTH].set(conv_w[0])
    qt, k, vt, u, ga, gb = _proj_call(x2, mod_lat, g_mix, wqv_t, w_rest, seq)
    yn = _conv_call(u, cw, conv_b[0][None, :], conv_ln_g[0][None, :], conv_ln_b[0][None, :], seq)
    kc, vct = _ctx_kv_call(ctx.reshape(b * n_ctx, d), mod_ctx, g_mix, w_in_b[:, K0:V0], w_in_b[:, V0:GLU0].T)

    att = _attn_call(qt, k, vt, kc, vct, _bias_table(rpb[0]), seq, n_ctx)

    wr = jnp.zeros((d, LANES), F32).at[:, :N_GROUPS].set(w_router_group[0])
    wr = wr.at[:, ROUTER_E0:ROUTER_E0 + N_EXPERTS].set(w_router_expert[0])
    wr_hi = wr.astype(BF16)
    wr = jnp.concatenate([wr_hi, (wr - wr_hi.astype(F32)).astype(BF16)], axis=1)
    br = jnp.zeros((1, LANES), F32).at[0, :N_GROUPS].set(b_router_group[0])
    br = br.at[0, ROUTER_E0:ROUTER_E0 + N_EXPERTS].set(b_router_expert[0])
    x_mid, hp, info, counts = _merge_call(
        att, yn, ga, gb, x2, mod_lat, w_att_out[0].astype(BF16), w_conv_out[0].astype(BF16), w_o[0].astype(BF16), norm_ffn[0][None, :],
        wr, br, seq)

    dest, tail_start, blk_ea, blk_eb, blk_rows, nb = _routing_plan(info, counts, t)
    hs = _dispatch_call(dest, tail_start, hp, nb * MOE_BLOCK)
    y = _moe_call(blk_ea, blk_eb, blk_rows, hs,
                  w_exp_gate[0].astype(BF16), w_exp_up[0].astype(BF16), w_exp_down[0].astype(BF16))
    out = _final_call(dest, x_mid, mod_lat, final_norm[None, :], y, seq)
    return out.reshape(b, seq, d)
```

```python
import functools

import numpy as np
import jax
import jax.numpy as jnp
from jax import lax
from jax.experimental import pallas as pl
from jax.experimental.pallas import tpu as pltpu

F32 = jnp.float32
BF16 = jnp.bfloat16
U32 = jnp.uint32

GRID_W = 64
N_HEADS = 8
HEAD_DIM = 64
D_ATT = N_HEADS * HEAD_DIM
WIN_ROWS = 8
WIN_COLS = 16
D_CONV = 512
CONV_WIDTH = 31
N_GROUPS = 4
EXPERTS_PER_GROUP = 8
N_EXPERTS = N_GROUPS * EXPERTS_PER_GROUP
PAIRS_PER_GROUP = EXPERTS_PER_GROUP * (EXPERTS_PER_GROUP - 1) // 2
N_CLASSES = N_GROUPS * PAIRS_PER_GROUP
ROUTER_E0 = 8
NORM_EPS = 1e-6
NEG_INF = -1e30

LANES = 128
SUBLANES = 8
ROW_TILE = SUBLANES
HALO = 16
VMEM_LIMIT = 56 * 1024 * 1024

TM_PROJ = 1024
TM_MERGE = 512
MERGE_PARTS = 1
TM_ROWS = 1024
MOE_BLOCK = 256
CONV_CHUNK = 64
DMA_UNROLL = 8

HIGHEST = lax.Precision.HIGHEST


def _norm_mod(x, g, shift, scale):
    ms = jnp.mean(x * x, axis=-1, keepdims=True)
    y = x * lax.rsqrt(ms + NORM_EPS) * g
    return y * (1.0 + scale) + shift


def _sigmoid(x):
    return jax.nn.sigmoid(x)


def _mod_kernel(c_ref, w_ref, b_ref, o_ref):
    c = c_ref[...]
    s = c * _sigmoid(c)
    o_ref[...] = jnp.dot(s, w_ref[...], precision=HIGHEST, preferred_element_type=F32) + b_ref[...]


def _mod_call(cc, w_mod, b_mod):
    rows, d = cc.shape
    n = w_mod.shape[1]
    tn = 1024
    return pl.pallas_call(
        _mod_kernel,
        grid=(n // tn,),
        in_specs=[
            pl.BlockSpec((rows, d), lambda j: (0, 0)),
            pl.BlockSpec((d, tn), lambda j: (0, j)),
            pl.BlockSpec((1, tn), lambda j: (0, j)),
        ],
        out_specs=pl.BlockSpec((rows, tn), lambda j: (0, j)),
        out_shape=jax.ShapeDtypeStruct((rows, n), F32),
        compiler_params=pltpu.CompilerParams(dimension_semantics=("arbitrary",), vmem_limit_bytes=VMEM_LIMIT),
        name="mod",
    )(cc, w_mod, b_mod)


Q0, K0, V0, GLU0 = 0, D_ATT, 2 * D_ATT, 3 * D_ATT
GA0 = GLU0 + 2 * D_CONV


NT_DIMS = (((1,), (1,)), ((), ()))
LOG2_E = 1.4426950408889634
SCORE_SCALE = HEAD_DIM ** -0.5 * LOG2_E


def _store_transposed(res_t, ref, scale=None):
    for j in range(ref.shape[0]):
        blk = res_t[:, j * LANES:(j + 1) * LANES]
        if scale is not None:
            blk = blk * scale
        ref[j] = blk.astype(ref.dtype)


def _conv_kernel(u_ref, up_ref, un_ref, cw_ref, cb_ref, lg_ref, lb_ref, yn_ref, ubuf, shifted,
                 *, tiles_per_seq):
    i = pl.program_id(0)
    tm = u_ref.shape[0]
    first = (i % tiles_per_seq) == 0
    last = (i % tiles_per_seq) == tiles_per_seq - 1
    ubuf[0:HALO, :] = jnp.where(first, 0.0, up_ref[...].astype(F32))
    for r0 in range(0, tm, CONV_CHUNK):
        ubuf[HALO + r0:HALO + r0 + CONV_CHUNK, :] = u_ref[r0:r0 + CONV_CHUNK, :].astype(F32)
    ubuf[HALO + tm:, :] = jnp.where(last, 0.0, un_ref[...].astype(F32))

    span = shifted.shape[1]
    for ph in range(SUBLANES):
        for r0 in range(0, span, CONV_CHUNK):
            n = min(CONV_CHUNK, span - r0)
            shifted[ph, r0:r0 + n, :] = ubuf[ph + r0:ph + r0 + n, :]

    base = HALO - CONV_WIDTH // 2
    inv_c = 1.0 / D_CONV
    for tc in range(tm // CONV_CHUNK):
        accs = []
        for lc in range(D_CONV // LANES):
            ls = slice(lc * LANES, (lc + 1) * LANES)
            acc = jnp.zeros((CONV_CHUNK, LANES), F32)
            for j in range(CONV_WIDTH):
                tiles, ph = divmod(base + j, SUBLANES)
                lo = tc * CONV_CHUNK + tiles * SUBLANES
                acc = acc + shifted[ph, lo:lo + CONV_CHUNK, ls] * cw_ref[j:j + 1, ls]
            accs.append(acc + cb_ref[:, ls])
        mu = sum(jnp.sum(a, axis=-1, keepdims=True) for a in accs) * inv_c
        cen = [a - mu for a in accs]
        var = sum(jnp.sum(c * c, axis=-1, keepdims=True) for c in cen) * inv_c
        rstd = lax.rsqrt(var + NORM_EPS)
        for lc, c in enumerate(cen):
            ls = slice(lc * LANES, (lc + 1) * LANES)
            yn = c * rstd * lg_ref[:, ls] + lb_ref[:, ls]
            yn_ref[tc * CONV_CHUNK:(tc + 1) * CONV_CHUNK, ls] = (yn * _sigmoid(yn)).astype(yn_ref.dtype)


def _conv_call(u, cw, cb, lg, lb, seq):
    t = u.shape[0]
    tm = TM_MERGE
    tpb = seq // tm
    hb = tm // HALO
    n_halo = t // HALO
    row = lambda i: (i, 0)
    const = lambda i: (0, 0)
    return pl.pallas_call(
        functools.partial(_conv_kernel, tiles_per_seq=tpb),
        grid=(t // tm,),
        in_specs=[
            pl.BlockSpec((tm, D_CONV), row),
            pl.BlockSpec((HALO, D_CONV), lambda i: (jnp.maximum(i * hb - 1, 0), 0)),
            pl.BlockSpec((HALO, D_CONV), lambda i: (jnp.minimum((i + 1) * hb, n_halo - 1), 0)),
            pl.BlockSpec(cw.shape, const),
            pl.BlockSpec((1, D_CONV), const),
            pl.BlockSpec((1, D_CONV), const),
            pl.BlockSpec((1, D_CONV), const),
        ],
        out_specs=pl.BlockSpec((tm, D_CONV), row),
        out_shape=jax.ShapeDtypeStruct((t, D_CONV), BF16),
        scratch_shapes=[
            pltpu.VMEM((tm + 2 * HALO, D_CONV), F32),
            pltpu.VMEM((SUBLANES, tm + 2 * HALO - SUBLANES, D_CONV), F32),
        ],
        compiler_params=pltpu.CompilerParams(dimension_semantics=("arbitrary",), vmem_limit_bytes=VMEM_LIMIT),
        name="conv",
    )(u, u, u, cw, cb, lg, lb)


def _proj_kernel(x_ref, mod_ref, g_ref, wqv_ref, w_ref, qt_ref, k_ref, vt_ref, u_ref, ga_ref, gb_ref):
    d = x_ref.shape[1]
    mod = mod_ref[0]
    h = _norm_mod(x_ref[...], g_ref[...], mod[0:1], mod[1:2]).astype(BF16)

    qv_t = lax.dot_general(wqv_ref[...], h, NT_DIMS, preferred_element_type=F32)
    _store_transposed(qv_t[:D_ATT], qt_ref, SCORE_SCALE)
    _store_transposed(qv_t[D_ATT:], vt_ref)

    def seg(lo, hi):
        return jnp.dot(h, w_ref[:, lo:hi], preferred_element_type=F32)

    k_ref[...] = seg(0, D_ATT).astype(BF16)
    a = seg(D_ATT, D_ATT + D_CONV)
    g = seg(D_ATT + D_CONV, D_ATT + 2 * D_CONV)
    u_ref[...] = (a * _sigmoid(g)).astype(BF16)
    g0 = D_ATT + 2 * D_CONV
    ga_ref[...] = _sigmoid(seg(g0, g0 + d)).astype(BF16)
    gb_ref[...] = _sigmoid(seg(g0 + d, g0 + 2 * d)).astype(BF16)


def _proj_call(x2, mod3, g, wqv_t, w_rest, seq):
    t, d = x2.shape
    tm = TM_PROJ
    tpb = seq // tm
    nblk = tm // LANES
    outs = [
        jax.ShapeDtypeStruct((t // LANES, D_ATT, LANES), BF16),
        jax.ShapeDtypeStruct((t, D_ATT), BF16),
        jax.ShapeDtypeStruct((t // LANES, D_ATT, LANES), BF16),
        jax.ShapeDtypeStruct((t, D_CONV), BF16),
        jax.ShapeDtypeStruct((t, d), BF16),
        jax.ShapeDtypeStruct((t, d), BF16),
    ]
    row = lambda i: (i, 0)
    blk3 = lambda i: (i, 0, 0)
    return pl.pallas_call(
        _proj_kernel,
        grid=(t // tm,),
        in_specs=[
            pl.BlockSpec((tm, d), row),
            pl.BlockSpec((1, 6, d), lambda i: (i // tpb, 0, 0)),
            pl.BlockSpec((1, d), lambda i: (0, 0)),
            pl.BlockSpec(wqv_t.shape, lambda i: (0, 0), pipeline_mode=pl.Buffered(1)),
            pl.BlockSpec(w_rest.shape, lambda i: (0, 0), pipeline_mode=pl.Buffered(1)),
        ],
        out_specs=[
            pl.BlockSpec((nblk, D_ATT, LANES), blk3),
            pl.BlockSpec((tm, D_ATT), row),
            pl.BlockSpec((nblk, D_ATT, LANES), blk3),
            pl.BlockSpec((tm, D_CONV), row),
            pl.BlockSpec((tm, d), row),
            pl.BlockSpec((tm, d), row),
        ],
        out_shape=outs,
        compiler_params=pltpu.CompilerParams(dimension_semantics=("arbitrary",), vmem_limit_bytes=VMEM_LIMIT),
        name="proj",
    )(x2, mod3, g, wqv_t, w_rest)


def _ctx_kv_kernel(x_ref, mod_ref, g_ref, wk_ref, wvt_ref, k_ref, vt_ref):
    mod = mod_ref[0]
    h = _norm_mod(x_ref[...], g_ref[...], mod[0:1], mod[1:2]).astype(BF16)
    k_ref[...] = jnp.dot(h, wk_ref[...], preferred_element_type=F32).astype(BF16)
    _store_transposed(lax.dot_general(wvt_ref[...], h, NT_DIMS, preferred_element_type=F32), vt_ref)


def _ctx_kv_call(c2, mod3, g, w_k, w_vt):
    t, d = c2.shape
    tm = 512
    nblk = tm // LANES
    row = lambda i: (i, 0)
    return pl.pallas_call(
        _ctx_kv_kernel,
        grid=(t // tm,),
        in_specs=[
            pl.BlockSpec((tm, d), row),
            pl.BlockSpec((1, 6, d), lambda i: (0, 0, 0)),
            pl.BlockSpec((1, d), lambda i: (0, 0)),
            pl.BlockSpec(w_k.shape, lambda i: (0, 0)),
            pl.BlockSpec(w_vt.shape, lambda i: (0, 0)),
        ],
        out_specs=[pl.BlockSpec((tm, D_ATT), row), pl.BlockSpec((nblk, D_ATT, LANES), lambda i: (i, 0, 0))],
        out_shape=[jax.ShapeDtypeStruct((t, D_ATT), BF16), jax.ShapeDtypeStruct((t // LANES, D_ATT, LANES), BF16)],
        compiler_params=pltpu.CompilerParams(dimension_semantics=("arbitrary",), vmem_limit_bytes=VMEM_LIMIT),
        name="ctx_kv",
    )(c2, mod3, g, w_k, w_vt)


HEADS_PER_GROUP = 4
GROUP_W = HEADS_PER_GROUP * HEAD_DIM
N_HEAD_GROUPS = N_HEADS // HEADS_PER_GROUP


def _attn_kernel(qt_ref, k_ref, vt_ref, kc_ref, vct_ref, bias_ref, o_ref, vboth):
    nblk = qt_ref.shape[0]
    rows = 2 * nblk
    n_loc = WIN_ROWS * GRID_W
    n_ctx_blk = vct_ref.shape[0]
    half = GRID_W

    for j in range(nblk):
        vboth[0, j] = vt_ref[j]
    for j in range(nblk - 1):
        vboth[1, j] = jnp.concatenate([vt_ref[j][:, half:], vt_ref[j + 1][:, :half]], axis=1)
    vboth[1, nblk - 1] = jnp.zeros_like(vt_ref[0])

    lane = lax.broadcasted_iota(jnp.int32, (D_ATT, LANES), 1)
    low = lane < half
    rblk = lax.broadcasted_iota(jnp.int32, (GROUP_W, GROUP_W), 0) // HEAD_DIM
    cblk = lax.broadcasted_iota(jnp.int32, (GROUP_W, GROUP_W), 1) // HEAD_DIM
    diag = rblk == cblk
    low64 = lax.broadcasted_iota(jnp.int32, (HEAD_DIM, LANES), 1) < half
    kc = kc_ref[...]

    def one_row(row, tiled, side):
        r_start = jnp.clip(row - WIN_ROWS // 2, 0, rows - WIN_ROWS)
        b_off = pl.multiple_of((WIN_ROWS - 1 - (row - r_start)) * GRID_W, GRID_W)
        kw = k_ref[pl.ds(pl.multiple_of(r_start * GRID_W, GRID_W), n_loc), :]
        vwin = vboth[r_start % 2, pl.ds(r_start // 2, n_loc // LANES)]
        parts = []
        for g in range(N_HEAD_GROUPS):
            fs = slice(g * GROUP_W, (g + 1) * GROUP_W)
            tg = tiled[fs, :]
            w = jnp.where(diag, jnp.concatenate([tg, tg], axis=1), jnp.zeros((), BF16))
            s_loc = jnp.dot(kw[:, fs], w, preferred_element_type=F32) + bias_ref[g, pl.ds(b_off, n_loc), :]
            s_ctx = jnp.dot(kc[:, fs], w, preferred_element_type=F32)
            m = jnp.maximum(jnp.max(s_loc, axis=0, keepdims=True), jnp.max(s_ctx, axis=0, keepdims=True))
            p_loc = jnp.exp2(s_loc - m)
            p_ctx = jnp.exp2(s_ctx - m)
            l = jnp.sum(p_loc, axis=0, keepdims=True) + jnp.sum(p_ctx, axis=0, keepdims=True)
            p = jnp.concatenate([p_loc.astype(BF16), p_ctx.astype(BF16)], axis=0)
            vt_g = jnp.concatenate([vwin[j][fs, :] for j in range(n_loc // LANES)]
                                   + [vct_ref[j][fs, :] for j in range(n_ctx_blk)], axis=1)
            o_t = jnp.dot(vt_g, p, preferred_element_type=F32) * (1.0 / l)
            for j in range(HEADS_PER_GROUP):
                blk = o_t[j * HEAD_DIM:(j + 1) * HEAD_DIM, (j // 2) * LANES:(j // 2 + 1) * LANES]
                if j % 2 != side:
                    blk = pltpu.roll(blk, half, axis=1)
                parts.append(blk)
        return parts

    def body(i, carry):
        xq = qt_ref[i]
        rolled = jnp.concatenate([xq[:, half:], xq[:, :half]], axis=1)
        parts_a = one_row(2 * i, jnp.where(low, xq, rolled), 0)
        parts_b = one_row(2 * i + 1, jnp.where(low, rolled, xq), 1)
        out = jnp.concatenate([jnp.where(low64, a, b) for a, b in zip(parts_a, parts_b)], axis=0)
        o_ref[i] = out.astype(o_ref.dtype)
        return carry

    lax.fori_loop(0, nblk, body, 0, unroll=8)


def _attn_call(qt, k, vt, kc, vct, bias, seq, n_ctx):
    t = k.shape[0]
    b = t // seq
    nblk = seq // LANES
    lat3 = pl.BlockSpec((nblk, D_ATT, LANES), lambda i: (i, 0, 0))
    return pl.pallas_call(
        _attn_kernel,
        grid=(b,),
        in_specs=[lat3,
                  pl.BlockSpec((seq, D_ATT), lambda i: (i, 0)),
                  lat3,
                  pl.BlockSpec((n_ctx, D_ATT), lambda i: (i, 0)),
                  pl.BlockSpec((n_ctx // LANES, D_ATT, LANES), lambda i: (i, 0, 0)),
                  pl.BlockSpec(bias.shape, lambda i: (0, 0, 0))],
        out_specs=lat3,
        out_shape=jax.ShapeDtypeStruct((t // LANES, D_ATT, LANES), BF16),
        scratch_shapes=[pltpu.VMEM((2, nblk, D_ATT, LANES), BF16)],
        compiler_params=pltpu.CompilerParams(dimension_semantics=("arbitrary",), vmem_limit_bytes=VMEM_LIMIT),
        name="attn",
    )(qt, k, vt, kc, vct, bias)


def _bias_table(rpb):
    cq = np.arange(GRID_W)[:, None]
    ck = np.arange(GRID_W)[None, :]
    c_start = np.clip(cq - WIN_COLS // 2, 0, GRID_W - WIN_COLS)
    col_mask = (ck >= c_start) & (ck < c_start + WIN_COLS)
    n_dc = rpb.shape[2]
    lead = GRID_W - WIN_COLS
    w = jnp.pad(rpb, ((0, 0), (0, 0), (lead, 2 * GRID_W - lead - n_dc)))
    skew = jnp.tile(w, (1, 1, GRID_W + 1))[:, :, :GRID_W * (2 * GRID_W + 1)]
    tab = skew.reshape(rpb.shape[0], rpb.shape[1], GRID_W, 2 * GRID_W + 1)[:, :, ::-1, :GRID_W]
    tab = jnp.where(col_mask[None, None], tab, NEG_INF)
    n_dr = tab.shape[1]
    tab = tab.reshape(N_HEAD_GROUPS, HEADS_PER_GROUP, n_dr, GRID_W, GRID_W)
    tab = tab.transpose(0, 2, 4, 1, 3)
    return (tab.reshape(N_HEAD_GROUPS, n_dr * GRID_W, GROUP_W) * LOG2_E).astype(F32)


def _merge_rows(r0, n, att_ref, yn_ref, ga_ref, gb_ref, x_ref, mod,
                wa_ref, wc_ref, wo_ref, nf_ref, wr_ref, br_ref, xmid_ref, hp_ref):
    d = x_ref.shape[1]
    rs = slice(r0, r0 + n)
    y_conv = jnp.dot(yn_ref[rs, :], wc_ref[...], preferred_element_type=F32)

    att = jnp.concatenate([att_ref[j].astype(F32).T for j in range(r0 // LANES, (r0 + n) // LANES)], axis=0)
    y_att = jnp.dot(att.astype(BF16), wa_ref[...], preferred_element_type=F32)
    mix = ga_ref[rs, :].astype(F32) * y_att + gb_ref[rs, :].astype(F32) * y_conv
    y = jnp.dot(mix.astype(BF16), wo_ref[...], preferred_element_type=F32)
    x_mid = x_ref[rs, :] + mod[2:3] * y
    xmid_ref[rs, :] = x_mid
    h2 = _norm_mod(x_mid, nf_ref[...], mod[3:4], mod[4:5])

    h2_hi = h2.astype(BF16)
    h2_lo = (h2 - h2_hi.astype(F32)).astype(BF16)
    t_hi = jnp.dot(h2_hi, wr_ref[...], preferred_element_type=F32)
    t_lo = jnp.dot(h2_lo, wr_ref[:, :LANES], preferred_element_type=F32)
    logits = t_hi[:, :LANES] + t_hi[:, LANES:] + t_lo + br_ref[...]
    lt = logits.T
    sub = lax.broadcasted_iota(jnp.int32, (SUBLANES, n), 0).astype(F32)
    big = float(LANES)
    is_g = sub < N_GROUPS
    gl = jnp.where(is_g, lt[0:SUBLANES], -jnp.inf)
    gmax = jnp.max(gl, axis=0, keepdims=True)
    g_idx = jnp.min(jnp.where(gl == gmax, sub, big), axis=0, keepdims=True)
    p_group = 1.0 / jnp.sum(jnp.where(is_g, jnp.exp(gl - gmax), 0.0), axis=0, keepdims=True)
    el = lt[ROUTER_E0:ROUTER_E0 + EXPERTS_PER_GROUP]
    for g in range(1, N_GROUPS):
        lo_g = ROUTER_E0 + g * EXPERTS_PER_GROUP
        el = jnp.where(g_idx == float(g), lt[lo_g:lo_g + EXPERTS_PER_GROUP], el)
    v1 = jnp.max(el, axis=0, keepdims=True)
    j1 = jnp.min(jnp.where(el == v1, sub, big), axis=0, keepdims=True)
    el2 = jnp.where(sub == j1, -jnp.inf, el)
    v2 = jnp.max(el2, axis=0, keepdims=True)
    j2 = jnp.min(jnp.where(el2 == v2, sub, big), axis=0, keepdims=True)
    e2 = jnp.exp(v2 - v1)
    gate1 = p_group / (1.0 + e2)
    gate2 = p_group * e2 / (1.0 + e2)
    ja = jnp.minimum(j1, j2)
    jb = jnp.maximum(j1, j2)
    gate_a = jnp.where(j1 < j2, gate1, gate2)
    gate_b = jnp.where(j1 < j2, gate2, gate1)
    pair = ja * (2 * EXPERTS_PER_GROUP - 1 - ja) * 0.5 + (jb - ja - 1.0)
    cls = g_idx * PAIRS_PER_GROUP + pair
    gate_rows = jnp.where(sub == 0.0, gate_a, jnp.where(sub == 1.0, gate_b, 0.0))
    gates = jnp.concatenate([gate_rows, jnp.zeros((LANES - SUBLANES, n), F32)], axis=0).T

    half = d // 2
    hi = lax.bitcast_convert_type(h2[:, :half].astype(BF16).astype(F32), U32)
    lo = lax.bitcast_convert_type(h2[:, half:].astype(BF16).astype(F32), U32)
    words = hi | (lo >> 16)
    n_words = half // LANES
    base = r0 * ROW_TILE
    for c in range(n_words):
        hp_ref[pl.ds(base + c, n, stride=ROW_TILE), :] = words[:, c * LANES:(c + 1) * LANES]
    hp_ref[pl.ds(base + n_words, n, stride=ROW_TILE), :] = lax.bitcast_convert_type(gates, U32)
    zero = jnp.zeros((n, LANES), U32)
    for c in range(n_words + 1, ROW_TILE):
        hp_ref[pl.ds(base + c, n, stride=ROW_TILE), :] = zero
    return cls


def _merge_kernel(att_ref, yn_ref, ga_ref, gb_ref, x_ref, mod_ref,
                  wa_ref, wc_ref, wo_ref, nf_ref, wr_ref, br_ref, tri_ref,
                  xmid_ref, hp_ref, info_ref, cnt_ref, run_ref):
    i = pl.program_id(0)
    tm = x_ref.shape[0]

    @pl.when(i == 0)
    def _():
        run_ref[...] = jnp.zeros_like(run_ref)

    mod = mod_ref[0]
    n = tm // MERGE_PARTS
    cls = jnp.concatenate(
        [_merge_rows(p * n, n, att_ref, yn_ref, ga_ref, gb_ref, x_ref, mod, wa_ref, wc_ref, wo_ref, nf_ref,
                     wr_ref, br_ref, xmid_ref, hp_ref) for p in range(MERGE_PARTS)], axis=1)

    crow = lax.broadcasted_iota(jnp.int32, (LANES, tm), 0).astype(F32)
    oh_f = jnp.where(crow == cls, 1.0, 0.0)
    before = jnp.dot(oh_f.astype(BF16), tri_ref[...], preferred_element_type=F32)
    run = run_ref[:, 0:1]
    rank = jnp.sum(oh_f * (before + run), axis=0, keepdims=True)
    new_run = run + jnp.sum(oh_f, axis=1, keepdims=True)
    run_ref[...] = jnp.broadcast_to(new_run, run_ref.shape)
    cnt_ref[...] = jnp.broadcast_to(new_run, cnt_ref.shape)

    sub = lax.broadcasted_iota(jnp.int32, (SUBLANES, tm), 0)
    info_ref[...] = jnp.where(sub == 0, cls, jnp.where(sub == 1, rank, 0.0))


def _merge_call(att, yn, ga, gb, x2, mod3, wa, wc, wo, nf, wr, br, seq):
    t, d = x2.shape
    tm = TM_MERGE
    tpb = seq // tm
    tri = jnp.asarray(np.triu(np.ones((tm, tm), np.float32), 1), BF16)
    row = lambda i: (i, 0)
    const = lambda i: (0, 0)
    return pl.pallas_call(
        _merge_kernel,
        grid=(t // tm,),
        in_specs=[
            pl.BlockSpec((tm // LANES, D_ATT, LANES), lambda i: (i, 0, 0)),
            pl.BlockSpec((tm, D_CONV), row),
            pl.BlockSpec((tm, d), row),
            pl.BlockSpec((tm, d), row),
            pl.BlockSpec((tm, d), row),
            pl.BlockSpec((1, 6, d), lambda i: (i // tpb, 0, 0)),
            pl.BlockSpec(wa.shape, const),
            pl.BlockSpec(wc.shape, const),
            pl.BlockSpec(wo.shape, const),
            pl.BlockSpec((1, d), const),
            pl.BlockSpec(wr.shape, const),
            pl.BlockSpec((1, LANES), const),
            pl.BlockSpec((tm, tm), const),
        ],
        out_specs=[
            pl.BlockSpec((tm, d), row),
            pl.BlockSpec((tm * ROW_TILE, LANES), row),
            pl.BlockSpec((SUBLANES, tm), lambda i: (0, i)),
            pl.BlockSpec((LANES, LANES), const),
        ],
        out_shape=[
            jax.ShapeDtypeStruct((t, d), F32),
            jax.ShapeDtypeStruct((t * ROW_TILE, LANES), U32),
            jax.ShapeDtypeStruct((SUBLANES, t), F32),
            jax.ShapeDtypeStruct((LANES, LANES), F32),
        ],
        scratch_shapes=[pltpu.VMEM((LANES, LANES), F32)],
        compiler_params=pltpu.CompilerParams(dimension_semantics=("arbitrary",), vmem_limit_bytes=VMEM_LIMIT),
        name="merge",
    )(att, yn, ga, gb, x2, mod3, wa, wc, wo, nf, wr, br, tri)


def _dispatch_kernel(dest_ref, tail_ref, hp_ref, hs_hbm, zbuf, sem, zsem):
    n = dest_ref.shape[0]
    blk_rows = zbuf.shape[0]

    @pl.when(pl.program_id(0) == 0)
    def _():
        zbuf[...] = jnp.zeros_like(zbuf)

        def tail_copy(c):
            start = pl.multiple_of(tail_ref[c] * ROW_TILE, ROW_TILE)
            return pltpu.make_async_copy(zbuf, hs_hbm.at[pl.ds(start, blk_rows)], zsem)

        def start_one(c, carry):
            @pl.when(tail_ref[c] >= 0)
            def _():
                tail_copy(c).start()
            return carry

        def wait_one(c, carry):
            @pl.when(tail_ref[c] >= 0)
            def _():
                tail_copy(c).wait()
            return carry

        lax.fori_loop(0, N_CLASSES, start_one, 0)
        lax.fori_loop(0, N_CLASSES, wait_one, 0)

        def unused_copy(b):
            start = pl.multiple_of(b * blk_rows, blk_rows)
            return pltpu.make_async_copy(zbuf, hs_hbm.at[pl.ds(start, blk_rows)], zsem)

        def start_unused(b, carry):
            unused_copy(b).start()
            return carry

        def wait_unused(b, carry):
            unused_copy(b).wait()
            return carry

        n_blocks = hs_hbm.shape[0] // blk_rows
        lax.fori_loop(tail_ref[N_CLASSES], n_blocks, start_unused, 0)
        lax.fori_loop(tail_ref[N_CLASSES], n_blocks, wait_unused, 0)

    def body(o, carry):
        for k in range(DMA_UNROLL):
            t = o * DMA_UNROLL + k
            src = pl.multiple_of(t * ROW_TILE, ROW_TILE)
            dst = pl.multiple_of(dest_ref[t] * ROW_TILE, ROW_TILE)
            pltpu.make_async_copy(hp_ref.at[pl.ds(src, ROW_TILE)], hs_hbm.at[pl.ds(dst, ROW_TILE)],
                                  sem).start(priority=k % 2)
        return carry

    lax.fori_loop(0, n // DMA_UNROLL, body, 0)
    pltpu.make_async_copy(hp_ref, hs_hbm.at[pl.ds(0, n * ROW_TILE)], sem).wait()


def _dispatch_call(dest, tail_start, hp, p_rows):
    t = dest.shape[0]
    return pl.pallas_call(
        _dispatch_kernel,
        grid=(t // TM_ROWS,),
        in_specs=[
            pl.BlockSpec((TM_ROWS,), lambda i: (i,), memory_space=pltpu.SMEM),
            pl.BlockSpec(memory_space=pltpu.SMEM),
            pl.BlockSpec((TM_ROWS * ROW_TILE, LANES), lambda i: (i, 0)),
        ],
        out_specs=pl.BlockSpec(memory_space=pl.ANY),
        out_shape=jax.ShapeDtypeStruct((p_rows * ROW_TILE, LANES), U32),
        scratch_shapes=[pltpu.VMEM((MOE_BLOCK * ROW_TILE, LANES), U32),
                        pltpu.SemaphoreType.DMA(()), pltpu.SemaphoreType.DMA(())],
        compiler_params=pltpu.CompilerParams(dimension_semantics=("arbitrary",), has_side_effects=True,
                                             vmem_limit_bytes=VMEM_LIMIT),
        name="dispatch",
    )(dest, tail_start, hp)


def _moe_kernel(ea_ref, eb_ref, rows_ref, hs_ref, wga_ref, wua_ref, wda_ref, wgb_ref, wub_ref, wdb_ref, y_ref):
    i = pl.program_id(0)
    blk = hs_ref.shape[0] // ROW_TILE
    d = wga_ref.shape[1]
    n_words = d // 2 // LANES
    n_real = rows_ref[i]

    def run(n):
        his, los = [], []
        for c in range(n_words):
            w = hs_ref[pl.ds(c, n, stride=ROW_TILE), :]
            his.append(lax.bitcast_convert_type(w & jnp.uint32(0xFFFF0000), F32).astype(BF16))
            los.append(lax.bitcast_convert_type(w << 16, F32).astype(BF16))
        x = jnp.concatenate(his + los, axis=-1)
        gates = lax.bitcast_convert_type(hs_ref[pl.ds(n_words, n, stride=ROW_TILE), :], F32)
        gate_a = gates[:, 0:1]
        gate_b = gates[:, 1:2]

        def mlp(wg, wu, wd):
            g = jnp.dot(x, wg[0], preferred_element_type=F32)
            u = jnp.dot(x, wu[0], preferred_element_type=F32)
            a = (g * _sigmoid(g) * u).astype(BF16)
            return jnp.dot(a, wd[0], preferred_element_type=F32)

        y = gate_a * mlp(wga_ref, wua_ref, wda_ref) + gate_b * mlp(wgb_ref, wub_ref, wdb_ref)
        for c in range(d // LANES):
            y_ref[pl.ds(c, n, stride=ROW_TILE), :] = y[:, c * LANES:(c + 1) * LANES]

    @pl.when(n_real > blk // 2)
    def _():
        run(blk)

    @pl.when((n_real > 0) & (n_real <= blk // 2))
    def _():
        run(blk // 2)
        y_ref[blk // 2 * ROW_TILE:, :] = jnp.zeros((blk // 2 * ROW_TILE, LANES), y_ref.dtype)

    @pl.when(n_real == 0)
    def _():
        y_ref[...] = jnp.zeros_like(y_ref)


def _moe_call(blk_ea, blk_eb, blk_rows, hs, wg, wu, wd):
    nb = blk_ea.shape[0]
    d = wg.shape[1]
    de = wg.shape[2]
    rows = MOE_BLOCK * ROW_TILE
    tok = pl.BlockSpec((rows, LANES), lambda i, ea, eb, va: (i, 0))
    w_a = lambda shape: pl.BlockSpec((1,) + shape, lambda i, ea, eb, va: (ea[i], 0, 0))
    w_b = lambda shape: pl.BlockSpec((1,) + shape, lambda i, ea, eb, va: (eb[i], 0, 0))
    grid_spec = pltpu.PrefetchScalarGridSpec(
        num_scalar_prefetch=3,
        grid=(nb,),
        in_specs=[tok, w_a((d, de)), w_a((d, de)), w_a((de, d)), w_b((d, de)), w_b((d, de)), w_b((de, d))],
        out_specs=tok,
    )
    return pl.pallas_call(
        _moe_kernel,
        grid_spec=grid_spec,
        out_shape=jax.ShapeDtypeStruct((nb * rows, LANES), F32),
        compiler_params=pltpu.CompilerParams(dimension_semantics=("arbitrary",), vmem_limit_bytes=VMEM_LIMIT),
        name="moe",
    )(blk_ea, blk_eb, blk_rows, hs, wg, wu, wd, wg, wu, wd)


def _final_kernel(dest_ref, dest_next_ref, xmid_ref, mod_ref, fn_ref, y_hbm, o_ref, fbuf, sems):
    i = pl.program_id(0)
    n_steps = pl.num_programs(0)
    n = dest_ref.shape[0]
    d = xmid_ref.shape[1]

    def issue(idx_ref, slot):
        def body(o, carry):
            for k in range(DMA_UNROLL):
                t = o * DMA_UNROLL + k
                src = pl.multiple_of(idx_ref[t] * ROW_TILE, ROW_TILE)
                dst = pl.multiple_of(t * ROW_TILE, ROW_TILE)
                pltpu.make_async_copy(y_hbm.at[pl.ds(src, ROW_TILE)], fbuf.at[slot, pl.ds(dst, ROW_TILE)],
                                      sems.at[slot]).start(priority=k % 2)
            return carry

        lax.fori_loop(0, n // DMA_UNROLL, body, 0)

    slot = i % 2

    @pl.when(i == 0)
    def _():
        issue(dest_ref, 0)

    @pl.when(i + 1 < n_steps)
    def _():
        issue(dest_next_ref, 1 - slot)

    pltpu.make_async_copy(y_hbm.at[pl.ds(0, n * ROW_TILE)], fbuf.at[slot], sems.at[slot]).wait()
    f = jnp.concatenate([fbuf[slot, pl.ds(c, n, stride=ROW_TILE), :] for c in range(d // LANES)], axis=-1)
    x = xmid_ref[...] + mod_ref[0][5:6] * f
    ms = jnp.mean(x * x, axis=-1, keepdims=True)
    o_ref[...] = x * lax.rsqrt(ms + NORM_EPS) * fn_ref[...]


def _final_call(dest, xmid, mod3, fn, y, seq):
    t, d = xmid.shape
    tm = TM_ROWS
    tpb = seq // tm
    n_steps = t // tm
    return pl.pallas_call(
        _final_kernel,
        grid=(n_steps,),
        in_specs=[
            pl.BlockSpec((tm,), lambda i: (i,), memory_space=pltpu.SMEM),
            pl.BlockSpec((tm,), lambda i: (jnp.minimum(i + 1, n_steps - 1),), memory_space=pltpu.SMEM),
            pl.BlockSpec((tm, d), lambda i: (i, 0)),
            pl.BlockSpec((1, 6, d), lambda i: (i // tpb, 0, 0)),
            pl.BlockSpec((1, d), lambda i: (0, 0)),
            pl.BlockSpec(memory_space=pl.ANY),
        ],
        out_specs=pl.BlockSpec((tm, d), lambda i: (i, 0)),
        out_shape=jax.ShapeDtypeStruct((t, d), F32),
        scratch_shapes=[pltpu.VMEM((2, tm * ROW_TILE, LANES), F32), pltpu.SemaphoreType.DMA((2,))],
        compiler_params=pltpu.CompilerParams(dimension_semantics=("arbitrary",), vmem_limit_bytes=VMEM_LIMIT),
        name="final",
    )(dest, dest, xmid, mod3, fn, y)


def _pair_tables():
    ea, eb = [], []
    for g in range(N_GROUPS):
        for a in range(EXPERTS_PER_GROUP):
            for b in range(a + 1, EXPERTS_PER_GROUP):
                ea.append(g * EXPERTS_PER_GROUP + a)
                eb.append(g * EXPERTS_PER_GROUP + b)
    return np.asarray(ea, np.int32), np.asarray(eb, np.int32)


def _routing_plan(info, counts, t):
    cls = info[0].astype(jnp.int32)
    rank = info[1].astype(jnp.int32)
    cnt = counts[:N_CLASSES, 0].astype(jnp.int32)
    padded = (cnt + MOE_BLOCK - 1) // MOE_BLOCK * MOE_BLOCK
    pad_end = jnp.cumsum(padded)
    pad_start = pad_end - padded
    cls_ids = jnp.arange(N_CLASSES, dtype=jnp.int32)
    dest = rank + jnp.sum(jnp.where(cls[:, None] == cls_ids[None, :], pad_start[None, :], 0), axis=1)
    nb = t // MOE_BLOCK + N_CLASSES
    nb_used = pad_end[-1] // MOE_BLOCK
    blk = jnp.arange(nb, dtype=jnp.int32)
    blk_cls = jnp.sum(pad_end[None, :] <= (blk * MOE_BLOCK)[:, None], axis=1)
    blk_cls = jnp.clip(blk_cls, 0, N_CLASSES - 1)
    valid = blk < nb_used
    last_cls = blk_cls[jnp.maximum(nb_used - 1, 0)]
    blk_cls = jnp.where(valid, blk_cls, last_cls).astype(jnp.int32)
    tab_a, tab_b = _pair_tables()
    blk_ea = jnp.asarray(tab_a)[blk_cls]
    blk_eb = jnp.asarray(tab_b)[blk_cls]
    tail_start = jnp.where(cnt > 0, pad_end - MOE_BLOCK, -1).astype(jnp.int32)
    tail_start = jnp.concatenate([tail_start, nb_used[None].astype(jnp.int32)])
    blk_rows = jnp.clip(cnt[blk_cls] - (blk * MOE_BLOCK - pad_start[blk_cls]), 0, MOE_BLOCK)
    blk_rows = jnp.where(valid, blk_rows, 0).astype(jnp.int32)
    return dest, tail_start, blk_ea, blk_eb, blk_rows, nb


def kernel(x, c, ctx, c_ctx, w_mod, b_mod, norm_mix, w_in, rpb, w_att_out, conv_w, conv_b, conv_ln_g, conv_ln_b,
           w_conv_out, w_o, norm_ffn, w_router_group, b_router_group, w_router_expert, b_router_expert,
           w_exp_gate, w_exp_up, w_exp_down, final_norm):
    b, seq, d = x.shape
    n_ctx = ctx.shape[1]
    t = b * seq
    assert w_mod.shape[0] == 1, "single layer"
    assert seq % TM_ROWS == 0 and seq // GRID_W >= WIN_ROWS and n_ctx % LANES == 0 and (b * n_ctx) % 512 == 0
    assert d == 1024

    mod_rows = -(-(b + 1) // SUBLANES) * SUBLANES
    cc = jnp.zeros((mod_rows, d), F32).at[:b].set(c).at[b].set(c_ctx)
    m_all = _mod_call(cc, w_mod[0], b_mod[0][None, :])
    mod_lat = m_all[:b].reshape(b, 6, d)
    mod_ctx = m_all[b:b + 1].reshape(1, 6, d)

    x2 = x.reshape(t, d)
    g_mix = norm_mix[0][None, :]
    w_in_b = w_in[0].astype(BF16)
    wqv_t = jnp.concatenate([w_in_b[:, Q0:K0], w_in_b[:, V0:GLU0]], axis=1).T
    w_rest = jnp.concatenate([w_in_b[:, K0:V0], w_in_b[:, GLU0:]], axis=1)
    cw = jnp.zeros((32, D_CONV), F32).at[:CONV_WIDTH].set(conv_w[0])
    qt, k, vt, u, ga, gb = _proj_call(x2, mod_lat, g_mix, wqv_t, w_rest, seq)
    yn = _conv_call(u, cw, conv_b[0][None, :], conv_ln_g[0][None, :], conv_ln_b[0][None, :], seq)
    kc, vct = _ctx_kv_call(ctx.reshape(b * n_ctx, d), mod_ctx, g_mix, w_in_b[:, K0:V0], w_in_b[:, V0:GLU0].T)

    att = _attn_call(qt, k, vt, kc, vct, _bias_table(rpb[0]), seq, n_ctx)

    wr = jnp.zeros((d, LANES), F32).at[:, :N_GROUPS].set(w_router_group[0])
    wr = wr.at[:, ROUTER_E0:ROUTER_E0 + N_EXPERTS].set(w_router_expert[0])
    wr_hi = wr.astype(BF16)
    wr = jnp.concatenate([wr_hi, (wr - wr_hi.astype(F32)).astype(BF16)], axis=1)
    br = jnp.zeros((1, LANES), F32).at[0, :N_GROUPS].set(b_router_group[0])
    br = br.at[0, ROUTER_E0:ROUTER_E0 + N_EXPERTS].set(b_router_expert[0])
    x_mid, hp, info, counts = _merge_call(
        att, yn, ga, gb, x2, mod_lat, w_att_out[0].astype(BF16), w_conv_out[0].astype(BF16), w_o[0].astype(BF16), norm_ffn[0][None, :],
        wr, br, seq)

    dest, tail_start, blk_ea, blk_eb, blk_rows, nb = _routing_plan(info, counts, t)
    hs = _dispatch_call(dest, tail_start, hp, nb * MOE_BLOCK)
    y = _moe_call(blk_ea, blk_eb, blk_rows, hs,
                  w_exp_gate[0].astype(BF16), w_exp_up[0].astype(BF16), w_exp_down[0].astype(BF16))
    out = _final_call(dest, x_mid, mod_lat, final_norm[None, :], y, seq)
    return out.reshape(b, seq, d)
```

```python
import functools

import numpy as np
import jax
import jax.numpy as jnp
from jax import lax
from jax.experimental import pallas as pl
from jax.experimental.pallas import tpu as pltpu

F32 = jnp.float32
BF16 = jnp.bfloat16
U32 = jnp.uint32

GRID_W = 64
N_HEADS = 8
HEAD_DIM = 64
D_ATT = N_HEADS * HEAD_DIM
WIN_ROWS = 8
WIN_COLS = 16
D_CONV = 512
CONV_WIDTH = 31
N_GROUPS = 4
EXPERTS_PER_GROUP = 8
N_EXPERTS = N_GROUPS * EXPERTS_PER_GROUP
PAIRS_PER_GROUP = EXPERTS_PER_GROUP * (EXPERTS_PER_GROUP - 1) // 2
N_CLASSES = N_GROUPS * PAIRS_PER_GROUP
ROUTER_E0 = 8
NORM_EPS = 1e-6
NEG_INF = -1e30

LANES = 128
SUBLANES = 8
ROW_TILE = SUBLANES
HALO = 16
VMEM_LIMIT = 56 * 1024 * 1024

TM_PROJ = 1024
TM_MERGE = 512
MERGE_PARTS = 1
TM_ROWS = 1024
MOE_BLOCK = 256
WEIGHT_SLOTS = 3
CONV_CHUNK = 64
DMA_UNROLL = 8

HIGHEST = lax.Precision.HIGHEST


def _norm_mod(x, g, shift, scale):
    ms = jnp.mean(x * x, axis=-1, keepdims=True)
    y = x * lax.rsqrt(ms + NORM_EPS) * g
    return y * (1.0 + scale) + shift


def _sigmoid(x):
    return jax.nn.sigmoid(x)


def _mod_kernel(c_ref, w_ref, b_ref, o_ref):
    c = c_ref[...]
    s = c * _sigmoid(c)
    o_ref[...] = jnp.dot(s, w_ref[...], precision=HIGHEST, preferred_element_type=F32) + b_ref[...]


def _mod_call(cc, w_mod, b_mod):
    rows, d = cc.shape
    n = w_mod.shape[1]
    tn = 1024
    return pl.pallas_call(
        _mod_kernel,
        grid=(n // tn,),
        in_specs=[
            pl.BlockSpec((rows, d), lambda j: (0, 0)),
            pl.BlockSpec((d, tn), lambda j: (0, j)),
            pl.BlockSpec((1, tn), lambda j: (0, j)),
        ],
        out_specs=pl.BlockSpec((rows, tn), lambda j: (0, j)),
        out_shape=jax.ShapeDtypeStruct((rows, n), F32),
        compiler_params=pltpu.CompilerParams(dimension_semantics=("arbitrary",), vmem_limit_bytes=VMEM_LIMIT),
        name="mod",
    )(cc, w_mod, b_mod)


Q0, K0, V0, GLU0 = 0, D_ATT, 2 * D_ATT, 3 * D_ATT
GA0 = GLU0 + 2 * D_CONV


NT_DIMS = (((1,), (1,)), ((), ()))
LOG2_E = 1.4426950408889634
SCORE_SCALE = HEAD_DIM ** -0.5 * LOG2_E


def _store_transposed(res_t, ref, scale=None):
    for j in range(ref.shape[0]):
        blk = res_t[:, j * LANES:(j + 1) * LANES]
        if scale is not None:
            blk = blk * scale
        ref[j] = blk.astype(ref.dtype)


def _conv_kernel(u_ref, up_ref, un_ref, cw_ref, cb_ref, lg_ref, lb_ref, yn_ref, ubuf, shifted,
                 *, tiles_per_seq):
    i = pl.program_id(0)
    tm = u_ref.shape[0]
    first = (i % tiles_per_seq) == 0
    last = (i % tiles_per_seq) == tiles_per_seq - 1
    ubuf[0:HALO, :] = jnp.where(first, 0.0, up_ref[...].astype(F32))
    for r0 in range(0, tm, CONV_CHUNK):
        ubuf[HALO + r0:HALO + r0 + CONV_CHUNK, :] = u_ref[r0:r0 + CONV_CHUNK, :].astype(F32)
    ubuf[HALO + tm:, :] = jnp.where(last, 0.0, un_ref[...].astype(F32))

    span = shifted.shape[1]
    for ph in range(SUBLANES):
        for r0 in range(0, span, CONV_CHUNK):
            n = min(CONV_CHUNK, span - r0)
            shifted[ph, r0:r0 + n, :] = ubuf[ph + r0:ph + r0 + n, :]

    base = HALO - CONV_WIDTH // 2
    inv_c = 1.0 / D_CONV
    for tc in range(tm // CONV_CHUNK):
        accs = []
        for lc in range(D_CONV // LANES):
            ls = slice(lc * LANES, (lc + 1) * LANES)
            acc = jnp.zeros((CONV_CHUNK, LANES), F32)
            for j in range(CONV_WIDTH):
                tiles, ph = divmod(base + j, SUBLANES)
                lo = tc * CONV_CHUNK + tiles * SUBLANES
                acc = acc + shifted[ph, lo:lo + CONV_CHUNK, ls] * cw_ref[j:j + 1, ls]
            accs.append(acc + cb_ref[:, ls])
        mu = sum(jnp.sum(a, axis=-1, keepdims=True) for a in accs) * inv_c
        cen = [a - mu for a in accs]
        var = sum(jnp.sum(c * c, axis=-1, keepdims=True) for c in cen) * inv_c
        rstd = lax.rsqrt(var + NORM_EPS)
        for lc, c in enumerate(cen):
            ls = slice(lc * LANES, (lc + 1) * LANES)
            yn = c * rstd * lg_ref[:, ls] + lb_ref[:, ls]
            yn_ref[tc * CONV_CHUNK:(tc + 1) * CONV_CHUNK, ls] = (yn * _sigmoid(yn)).astype(yn_ref.dtype)


def _conv_call(u, cw, cb, lg, lb, seq):
    t = u.shape[0]
    tm = TM_MERGE
    tpb = seq // tm
    hb = tm // HALO
    n_halo = t // HALO
    row = lambda i: (i, 0)
    const = lambda i: (0, 0)
    return pl.pallas_call(
        functools.partial(_conv_kernel, tiles_per_seq=tpb),
        grid=(t // tm,),
        in_specs=[
            pl.BlockSpec((tm, D_CONV), row),
            pl.BlockSpec((HALO, D_CONV), lambda i: (jnp.maximum(i * hb - 1, 0), 0)),
            pl.BlockSpec((HALO, D_CONV), lambda i: (jnp.minimum((i + 1) * hb, n_halo - 1), 0)),
            pl.BlockSpec(cw.shape, const),
            pl.BlockSpec((1, D_CONV), const),
            pl.BlockSpec((1, D_CONV), const),
            pl.BlockSpec((1, D_CONV), const),
        ],
        out_specs=pl.BlockSpec((tm, D_CONV), row),
        out_shape=jax.ShapeDtypeStruct((t, D_CONV), BF16),
        scratch_shapes=[
            pltpu.VMEM((tm + 2 * HALO, D_CONV), F32),
            pltpu.VMEM((SUBLANES, tm + 2 * HALO - SUBLANES, D_CONV), F32),
        ],
        compiler_params=pltpu.CompilerParams(dimension_semantics=("arbitrary",), vmem_limit_bytes=VMEM_LIMIT),
        name="conv",
    )(u, u, u, cw, cb, lg, lb)


def _proj_kernel(x_ref, mod_ref, g_ref, wqv_ref, w_ref, qt_ref, k_ref, vt_ref, u_ref, ga_ref, gb_ref):
    d = x_ref.shape[1]
    mod = mod_ref[0]
    h = _norm_mod(x_ref[...], g_ref[...], mod[0:1], mod[1:2]).astype(BF16)

    qv_t = lax.dot_general(wqv_ref[...], h, NT_DIMS, preferred_element_type=F32)
    _store_transposed(qv_t[:D_ATT], qt_ref, SCORE_SCALE)
    _store_transposed(qv_t[D_ATT:], vt_ref)

    def seg(lo, hi):
        return jnp.dot(h, w_ref[:, lo:hi], preferred_element_type=F32)

    k_ref[...] = seg(0, D_ATT).astype(BF16)
    a = seg(D_ATT, D_ATT + D_CONV)
    g = seg(D_ATT + D_CONV, D_ATT + 2 * D_CONV)
    u_ref[...] = (a * _sigmoid(g)).astype(BF16)
    g0 = D_ATT + 2 * D_CONV
    ga_ref[...] = _sigmoid(seg(g0, g0 + d)).astype(BF16)
    gb_ref[...] = _sigmoid(seg(g0 + d, g0 + 2 * d)).astype(BF16)


def _proj_call(x2, mod3, g, wqv_t, w_rest, seq):
    t, d = x2.shape
    tm = TM_PROJ
    tpb = seq // tm
    nblk = tm // LANES
    outs = [
        jax.ShapeDtypeStruct((t // LANES, D_ATT, LANES), BF16),
        jax.ShapeDtypeStruct((t, D_ATT), BF16),
        jax.ShapeDtypeStruct((t // LANES, D_ATT, LANES), BF16),
        jax.ShapeDtypeStruct((t, D_CONV), BF16),
        jax.ShapeDtypeStruct((t, d), BF16),
        jax.ShapeDtypeStruct((t, d), BF16),
    ]
    row = lambda i: (i, 0)
    blk3 = lambda i: (i, 0, 0)
    return pl.pallas_call(
        _proj_kernel,
        grid=(t // tm,),
        in_specs=[
            pl.BlockSpec((tm, d), row),
            pl.BlockSpec((1, 6, d), lambda i: (i // tpb, 0, 0)),
            pl.BlockSpec((1, d), lambda i: (0, 0)),
            pl.BlockSpec(wqv_t.shape, lambda i: (0, 0), pipeline_mode=pl.Buffered(1)),
            pl.BlockSpec(w_rest.shape, lambda i: (0, 0), pipeline_mode=pl.Buffered(1)),
        ],
        out_specs=[
            pl.BlockSpec((nblk, D_ATT, LANES), blk3),
            pl.BlockSpec((tm, D_ATT), row),
            pl.BlockSpec((nblk, D_ATT, LANES), blk3),
            pl.BlockSpec((tm, D_CONV), row),
            pl.BlockSpec((tm, d), row),
            pl.BlockSpec((tm, d), row),
        ],
        out_shape=outs,
        compiler_params=pltpu.CompilerParams(dimension_semantics=("arbitrary",), vmem_limit_bytes=VMEM_LIMIT),
        name="proj",
    )(x2, mod3, g, wqv_t, w_rest)


def _ctx_kv_kernel(x_ref, mod_ref, g_ref, wk_ref, wvt_ref, k_ref, vt_ref):
    mod = mod_ref[0]
    h = _norm_mod(x_ref[...], g_ref[...], mod[0:1], mod[1:2]).astype(BF16)
    k_ref[...] = jnp.dot(h, wk_ref[...], preferred_element_type=F32).astype(BF16)
    _store_transposed(lax.dot_general(wvt_ref[...], h, NT_DIMS, preferred_element_type=F32), vt_ref)


def _ctx_kv_call(c2, mod3, g, w_k, w_vt):
    t, d = c2.shape
    tm = 512
    nblk = tm // LANES
    row = lambda i: (i, 0)
    return pl.pallas_call(
        _ctx_kv_kernel,
        grid=(t // tm,),
        in_specs=[
            pl.BlockSpec((tm, d), row),
            pl.BlockSpec((1, 6, d), lambda i: (0, 0, 0)),
            pl.BlockSpec((1, d), lambda i: (0, 0)),
            pl.BlockSpec(w_k.shape, lambda i: (0, 0)),
            pl.BlockSpec(w_vt.shape, lambda i: (0, 0)),
        ],
        out_specs=[pl.BlockSpec((tm, D_ATT), row), pl.BlockSpec((nblk, D_ATT, LANES), lambda i: (i, 0, 0))],
        out_shape=[jax.ShapeDtypeStruct((t, D_ATT), BF16), jax.ShapeDtypeStruct((t // LANES, D_ATT, LANES), BF16)],
        compiler_params=pltpu.CompilerParams(dimension_semantics=("arbitrary",), vmem_limit_bytes=VMEM_LIMIT),
        name="ctx_kv",
    )(c2, mod3, g, w_k, w_vt)


HEADS_PER_GROUP = 4
GROUP_W = HEADS_PER_GROUP * HEAD_DIM
N_HEAD_GROUPS = N_HEADS // HEADS_PER_GROUP


def _attn_kernel(qt_ref, k_ref, vt_ref, kc_ref, vct_ref, bias_ref, o_ref, vboth):
    nblk = qt_ref.shape[0]
    rows = 2 * nblk
    n_loc = WIN_ROWS * GRID_W
    n_ctx_blk = vct_ref.shape[0]
    half = GRID_W

    for j in range(nblk):
        vboth[0, j] = vt_ref[j]
    for j in range(nblk - 1):
        vboth[1, j] = jnp.concatenate([vt_ref[j][:, half:], vt_ref[j + 1][:, :half]], axis=1)
    vboth[1, nblk - 1] = jnp.zeros_like(vt_ref[0])

    lane = lax.broadcasted_iota(jnp.int32, (D_ATT, LANES), 1)
    low = lane < half
    rblk = lax.broadcasted_iota(jnp.int32, (GROUP_W, GROUP_W), 0) // HEAD_DIM
    cblk = lax.broadcasted_iota(jnp.int32, (GROUP_W, GROUP_W), 1) // HEAD_DIM
    diag = rblk == cblk
    low64 = lax.broadcasted_iota(jnp.int32, (HEAD_DIM, LANES), 1) < half
    kc = kc_ref[...]

    def one_row(row, tiled, side):
        r_start = jnp.clip(row - WIN_ROWS // 2, 0, rows - WIN_ROWS)
        b_off = pl.multiple_of((WIN_ROWS - 1 - (row - r_start)) * GRID_W, GRID_W)
        kw = k_ref[pl.ds(pl.multiple_of(r_start * GRID_W, GRID_W), n_loc), :]
        vwin = vboth[r_start % 2, pl.ds(r_start // 2, n_loc // LANES)]
        parts = []
        for g in range(N_HEAD_GROUPS):
            fs = slice(g * GROUP_W, (g + 1) * GROUP_W)
            tg = tiled[fs, :]
            w = jnp.where(diag, jnp.concatenate([tg, tg], axis=1), jnp.zeros((), BF16))
            s_loc = jnp.dot(kw[:, fs], w, preferred_element_type=F32) + bias_ref[g, pl.ds(b_off, n_loc), :]
            s_ctx = jnp.dot(kc[:, fs], w, preferred_element_type=F32)
            m = jnp.maximum(jnp.max(s_loc, axis=0, keepdims=True), jnp.max(s_ctx, axis=0, keepdims=True))
            p_loc = jnp.exp2(s_loc - m)
            p_ctx = jnp.exp2(s_ctx - m)
            l = jnp.sum(p_loc, axis=0, keepdims=True) + jnp.sum(p_ctx, axis=0, keepdims=True)
            p = jnp.concatenate([p_loc.astype(BF16), p_ctx.astype(BF16)], axis=0)
            vt_g = jnp.concatenate([vwin[j][fs, :] for j in range(n_loc // LANES)]
                                   + [vct_ref[j][fs, :] for j in range(n_ctx_blk)], axis=1)
            o_t = jnp.dot(vt_g, p, preferred_element_type=F32) * (1.0 / l)
            for j in range(HEADS_PER_GROUP):
                blk = o_t[j * HEAD_DIM:(j + 1) * HEAD_DIM, (j // 2) * LANES:(j // 2 + 1) * LANES]
                if j % 2 != side:
                    blk = pltpu.roll(blk, half, axis=1)
                parts.append(blk)
        return parts

    def body(i, carry):
        xq = qt_ref[i]
        rolled = jnp.concatenate([xq[:, half:], xq[:, :half]], axis=1)
        parts_a = one_row(2 * i, jnp.where(low, xq, rolled), 0)
        parts_b = one_row(2 * i + 1, jnp.where(low, rolled, xq), 1)
        out = jnp.concatenate([jnp.where(low64, a, b) for a, b in zip(parts_a, parts_b)], axis=0)
        o_ref[i] = out.astype(o_ref.dtype)
        return carry

    lax.fori_loop(0, nblk, body, 0, unroll=8)


def _attn_call(qt, k, vt, kc, vct, bias, seq, n_ctx):
    t = k.shape[0]
    b = t // seq
    nblk = seq // LANES
    lat3 = pl.BlockSpec((nblk, D_ATT, LANES), lambda i: (i, 0, 0))
    return pl.pallas_call(
        _attn_kernel,
        grid=(b,),
        in_specs=[lat3,
                  pl.BlockSpec((seq, D_ATT), lambda i: (i, 0)),
                  lat3,
                  pl.BlockSpec((n_ctx, D_ATT), lambda i: (i, 0)),
                  pl.BlockSpec((n_ctx // LANES, D_ATT, LANES), lambda i: (i, 0, 0)),
                  pl.BlockSpec(bias.shape, lambda i: (0, 0, 0))],
        out_specs=lat3,
        out_shape=jax.ShapeDtypeStruct((t // LANES, D_ATT, LANES), BF16),
        scratch_shapes=[pltpu.VMEM((2, nblk, D_ATT, LANES), BF16)],
        compiler_params=pltpu.CompilerParams(dimension_semantics=("arbitrary",), vmem_limit_bytes=VMEM_LIMIT),
        name="attn",
    )(qt, k, vt, kc, vct, bias)


def _bias_table(rpb):
    cq = np.arange(GRID_W)[:, None]
    ck = np.arange(GRID_W)[None, :]
    c_start = np.clip(cq - WIN_COLS // 2, 0, GRID_W - WIN_COLS)
    col_mask = (ck >= c_start) & (ck < c_start + WIN_COLS)
    n_dc = rpb.shape[2]
    lead = GRID_W - WIN_COLS
    w = jnp.pad(rpb, ((0, 0), (0, 0), (lead, 2 * GRID_W - lead - n_dc)))
    skew = jnp.tile(w, (1, 1, GRID_W + 1))[:, :, :GRID_W * (2 * GRID_W + 1)]
    tab = skew.reshape(rpb.shape[0], rpb.shape[1], GRID_W, 2 * GRID_W + 1)[:, :, ::-1, :GRID_W]
    tab = jnp.where(col_mask[None, None], tab, NEG_INF)
    n_dr = tab.shape[1]
    tab = tab.reshape(N_HEAD_GROUPS, HEADS_PER_GROUP, n_dr, GRID_W, GRID_W)
    tab = tab.transpose(0, 2, 4, 1, 3)
    return (tab.reshape(N_HEAD_GROUPS, n_dr * GRID_W, GROUP_W) * LOG2_E).astype(F32)


def _merge_rows(r0, n, att_ref, yn_ref, ga_ref, gb_ref, x_ref, mod,
                wa_ref, wc_ref, wo_ref, nf_ref, wr_ref, br_ref, xmid_ref, hp_ref):
    d = x_ref.shape[1]
    rs = slice(r0, r0 + n)
    y_conv = jnp.dot(yn_ref[rs, :], wc_ref[...], preferred_element_type=F32)

    att = jnp.concatenate([att_ref[j].astype(F32).T for j in range(r0 // LANES, (r0 + n) // LANES)], axis=0)
    y_att = jnp.dot(att.astype(BF16), wa_ref[...], preferred_element_type=F32)
    mix = ga_ref[rs, :].astype(F32) * y_att + gb_ref[rs, :].astype(F32) * y_conv
    y = jnp.dot(mix.astype(BF16), wo_ref[...], preferred_element_type=F32)
    x_mid = x_ref[rs, :] + mod[2:3] * y
    xmid_ref[rs, :] = x_mid
    h2 = _norm_mod(x_mid, nf_ref[...], mod[3:4], mod[4:5])

    h2_hi = h2.astype(BF16)
    h2_lo = (h2 - h2_hi.astype(F32)).astype(BF16)
    t_hi = jnp.dot(h2_hi, wr_ref[...], preferred_element_type=F32)
    t_lo = jnp.dot(h2_lo, wr_ref[:, :LANES], preferred_element_type=F32)
    logits = t_hi[:, :LANES] + t_hi[:, LANES:] + t_lo + br_ref[...]
    lt = logits.T
    sub = lax.broadcasted_iota(jnp.int32, (SUBLANES, n), 0).astype(F32)
    big = float(LANES)
    is_g = sub < N_GROUPS
    gl = jnp.where(is_g, lt[0:SUBLANES], -jnp.inf)
    gmax = jnp.max(gl, axis=0, keepdims=True)
    g_idx = jnp.min(jnp.where(gl == gmax, sub, big), axis=0, keepdims=True)
    p_group = 1.0 / jnp.sum(jnp.where(is_g, jnp.exp(gl - gmax), 0.0), axis=0, keepdims=True)
    el = lt[ROUTER_E0:ROUTER_E0 + EXPERTS_PER_GROUP]
    for g in range(1, N_GROUPS):
        lo_g = ROUTER_E0 + g * EXPERTS_PER_GROUP
        el = jnp.where(g_idx == float(g), lt[lo_g:lo_g + EXPERTS_PER_GROUP], el)
    v1 = jnp.max(el, axis=0, keepdims=True)
    j1 = jnp.min(jnp.where(el == v1, sub, big), axis=0, keepdims=True)
    el2 = jnp.where(sub == j1, -jnp.inf, el)
    v2 = jnp.max(el2, axis=0, keepdims=True)
    j2 = jnp.min(jnp.where(el2 == v2, sub, big), axis=0, keepdims=True)
    e2 = jnp.exp(v2 - v1)
    gate1 = p_group / (1.0 + e2)
    gate2 = p_group * e2 / (1.0 + e2)
    ja = jnp.minimum(j1, j2)
    jb = jnp.maximum(j1, j2)
    gate_a = jnp.where(j1 < j2, gate1, gate2)
    gate_b = jnp.where(j1 < j2, gate2, gate1)
    pair = ja * (2 * EXPERTS_PER_GROUP - 1 - ja) * 0.5 + (jb - ja - 1.0)
    cls = g_idx * PAIRS_PER_GROUP + pair
    gate_rows = jnp.where(sub == 0.0, gate_a, jnp.where(sub == 1.0, gate_b, 0.0))
    gates = jnp.concatenate([gate_rows, jnp.zeros((LANES - SUBLANES, n), F32)], axis=0).T

    half = d // 2
    hi = lax.bitcast_convert_type(h2[:, :half].astype(BF16).astype(F32), U32)
    lo = lax.bitcast_convert_type(h2[:, half:].astype(BF16).astype(F32), U32)
    words = hi | (lo >> 16)
    n_words = half // LANES
    base = r0 * ROW_TILE
    for c in range(n_words):
        hp_ref[pl.ds(base + c, n, stride=ROW_TILE), :] = words[:, c * LANES:(c + 1) * LANES]
    hp_ref[pl.ds(base + n_words, n, stride=ROW_TILE), :] = lax.bitcast_convert_type(gates, U32)
    zero = jnp.zeros((n, LANES), U32)
    for c in range(n_words + 1, ROW_TILE):
        hp_ref[pl.ds(base + c, n, stride=ROW_TILE), :] = zero
    return cls


def _merge_kernel(att_ref, yn_ref, ga_ref, gb_ref, x_ref, mod_ref,
                  wa_ref, wc_ref, wo_ref, nf_ref, wr_ref, br_ref, tri_ref,
                  xmid_ref, hp_ref, info_ref, cnt_ref, run_ref):
    i = pl.program_id(0)
    tm = x_ref.shape[0]

    @pl.when(i == 0)
    def _():
        run_ref[...] = jnp.zeros_like(run_ref)

    mod = mod_ref[0]
    n = tm // MERGE_PARTS
    cls = jnp.concatenate(
        [_merge_rows(p * n, n, att_ref, yn_ref, ga_ref, gb_ref, x_ref, mod, wa_ref, wc_ref, wo_ref, nf_ref,
                     wr_ref, br_ref, xmid_ref, hp_ref) for p in range(MERGE_PARTS)], axis=1)

    crow = lax.broadcasted_iota(jnp.int32, (LANES, tm), 0).astype(F32)
    oh_f = jnp.where(crow == cls, 1.0, 0.0)
    before = jnp.dot(oh_f.astype(BF16), tri_ref[...], preferred_element_type=F32)
    run = run_ref[:, 0:1]
    rank = jnp.sum(oh_f * (before + run), axis=0, keepdims=True)
    new_run = run + jnp.sum(oh_f, axis=1, keepdims=True)
    run_ref[...] = jnp.broadcast_to(new_run, run_ref.shape)
    cnt_ref[...] = jnp.broadcast_to(new_run, cnt_ref.shape)

    sub = lax.broadcasted_iota(jnp.int32, (SUBLANES, tm), 0)
    info_ref[...] = jnp.where(sub == 0, cls, jnp.where(sub == 1, rank, 0.0))


def _merge_call(att, yn, ga, gb, x2, mod3, wa, wc, wo, nf, wr, br, seq):
    t, d = x2.shape
    tm = TM_MERGE
    tpb = seq // tm
    tri = jnp.asarray(np.triu(np.ones((tm, tm), np.float32), 1), BF16)
    row = lambda i: (i, 0)
    const = lambda i: (0, 0)
    return pl.pallas_call(
        _merge_kernel,
        grid=(t // tm,),
        in_specs=[
            pl.BlockSpec((tm // LANES, D_ATT, LANES), lambda i: (i, 0, 0)),
            pl.BlockSpec((tm, D_CONV), row),
            pl.BlockSpec((tm, d), row),
            pl.BlockSpec((tm, d), row),
            pl.BlockSpec((tm, d), row),
            pl.BlockSpec((1, 6, d), lambda i: (i // tpb, 0, 0)),
            pl.BlockSpec(wa.shape, const),
            pl.BlockSpec(wc.shape, const),
            pl.BlockSpec(wo.shape, const),
            pl.BlockSpec((1, d), const),
            pl.BlockSpec(wr.shape, const),
            pl.BlockSpec((1, LANES), const),
            pl.BlockSpec((tm, tm), const),
        ],
        out_specs=[
            pl.BlockSpec((tm, d), row),
            pl.BlockSpec((tm * ROW_TILE, LANES), row),
            pl.BlockSpec((SUBLANES, tm), lambda i: (0, i)),
            pl.BlockSpec((LANES, LANES), const),
        ],
        out_shape=[
            jax.ShapeDtypeStruct((t, d), F32),
            jax.ShapeDtypeStruct((t * ROW_TILE, LANES), U32),
            jax.ShapeDtypeStruct((SUBLANES, t), F32),
            jax.ShapeDtypeStruct((LANES, LANES), F32),
        ],
        scratch_shapes=[pltpu.VMEM((LANES, LANES), F32)],
        compiler_params=pltpu.CompilerParams(dimension_semantics=("arbitrary",), vmem_limit_bytes=VMEM_LIMIT),
        name="merge",
    )(att, yn, ga, gb, x2, mod3, wa, wc, wo, nf, wr, br, tri)


def _dispatch_kernel(dest_ref, tail_ref, hp_ref, hs_hbm, zbuf, sem, zsem):
    n = dest_ref.shape[0]
    blk_rows = zbuf.shape[0]

    @pl.when(pl.program_id(0) == 0)
    def _():
        zbuf[...] = jnp.zeros_like(zbuf)

        def tail_copy(c):
            start = pl.multiple_of(tail_ref[c] * ROW_TILE, ROW_TILE)
            return pltpu.make_async_copy(zbuf, hs_hbm.at[pl.ds(start, blk_rows)], zsem)

        def start_one(c, carry):
            @pl.when(tail_ref[c] >= 0)
            def _():
                tail_copy(c).start()
            return carry

        def wait_one(c, carry):
            @pl.when(tail_ref[c] >= 0)
            def _():
                tail_copy(c).wait()
            return carry

        lax.fori_loop(0, N_CLASSES, start_one, 0)
        lax.fori_loop(0, N_CLASSES, wait_one, 0)

        def unused_copy(b):
            start = pl.multiple_of(b * blk_rows, blk_rows)
            return pltpu.make_async_copy(zbuf, hs_hbm.at[pl.ds(start, blk_rows)], zsem)

        def start_unused(b, carry):
            unused_copy(b).start()
            return carry

        def wait_unused(b, carry):
            unused_copy(b).wait()
            return carry

        n_blocks = hs_hbm.shape[0] // blk_rows
        lax.fori_loop(tail_ref[N_CLASSES], n_blocks, start_unused, 0)
        lax.fori_loop(tail_ref[N_CLASSES], n_blocks, wait_unused, 0)

    def body(o, carry):
        for k in range(DMA_UNROLL):
            t = o * DMA_UNROLL + k
            src = pl.multiple_of(t * ROW_TILE, ROW_TILE)
            dst = pl.multiple_of(dest_ref[t] * ROW_TILE, ROW_TILE)
            pltpu.make_async_copy(hp_ref.at[pl.ds(src, ROW_TILE)], hs_hbm.at[pl.ds(dst, ROW_TILE)],
                                  sem).start(priority=k % 2)
        return carry

    lax.fori_loop(0, n // DMA_UNROLL, body, 0)
    pltpu.make_async_copy(hp_ref, hs_hbm.at[pl.ds(0, n * ROW_TILE)], sem).wait()


def _dispatch_call(dest, tail_start, hp, p_rows):
    t = dest.shape[0]
    return pl.pallas_call(
        _dispatch_kernel,
        grid=(t // TM_ROWS,),
        in_specs=[
            pl.BlockSpec((TM_ROWS,), lambda i: (i,), memory_space=pltpu.SMEM),
            pl.BlockSpec(memory_space=pltpu.SMEM),
            pl.BlockSpec((TM_ROWS * ROW_TILE, LANES), lambda i: (i, 0)),
        ],
        out_specs=pl.BlockSpec(memory_space=pl.ANY),
        out_shape=jax.ShapeDtypeStruct((p_rows * ROW_TILE, LANES), U32),
        scratch_shapes=[pltpu.VMEM((MOE_BLOCK * ROW_TILE, LANES), U32),
                        pltpu.SemaphoreType.DMA(()), pltpu.SemaphoreType.DMA(())],
        compiler_params=pltpu.CompilerParams(dimension_semantics=("arbitrary",), has_side_effects=True,
                                             vmem_limit_bytes=VMEM_LIMIT),
        name="dispatch",
    )(dest, tail_start, hp)


def _moe_kernel(rows_ref, seg_ref, first_ref, fetch_ref, pre_ref, hs_ref, wg_hbm, wu_hbm, wd_hbm, y_ref,
                wg_buf, wu_buf, wd_buf, sems):
    i = pl.program_id(0)
    blk = hs_ref.shape[0] // ROW_TILE
    d = wg_buf.shape[2]
    n_words = d // 2 // LANES
    n_real = rows_ref[i]

    def copies(side, expert, slot):
        return [pltpu.make_async_copy(src.at[expert], buf.at[side, slot], sems.at[side, slot])
                for src, buf in ((wg_hbm, wg_buf), (wu_hbm, wu_buf), (wd_hbm, wd_buf))]

    for side in range(2):
        @pl.when(i == 0)
        def _():
            for s in range(2):
                @pl.when(pre_ref[side, s] >= 0)
                def _():
                    for cp in copies(side, pre_ref[side, s], s):
                        cp.start()

        @pl.when(fetch_ref[side, i] >= 0)
        def _():
            for cp in copies(side, fetch_ref[side, i], (seg_ref[side, i] + 2) % WEIGHT_SLOTS):
                cp.start()

    for side in range(2):
        @pl.when(first_ref[side, i] == 1)
        def _():
            for cp in copies(side, 0, seg_ref[side, i] % WEIGHT_SLOTS):
                cp.wait()

    slot_a = seg_ref[0, i] % WEIGHT_SLOTS
    slot_b = seg_ref[1, i] % WEIGHT_SLOTS
    wga_ref, wua_ref, wda_ref = wg_buf.at[0, slot_a], wu_buf.at[0, slot_a], wd_buf.at[0, slot_a]
    wgb_ref, wub_ref, wdb_ref = wg_buf.at[1, slot_b], wu_buf.at[1, slot_b], wd_buf.at[1, slot_b]

    def run(n):
        his, los = [], []
        for c in range(n_words):
            w = hs_ref[pl.ds(c, n, stride=ROW_TILE), :]
            his.append(lax.bitcast_convert_type(w & jnp.uint32(0xFFFF0000), F32).astype(BF16))
            los.append(lax.bitcast_convert_type(w << 16, F32).astype(BF16))
        x = jnp.concatenate(his + los, axis=-1)
        gates = lax.bitcast_convert_type(hs_ref[pl.ds(n_words, n, stride=ROW_TILE), :], F32)
        gate_a = gates[:, 0:1]
        gate_b = gates[:, 1:2]

        def mlp(wg, wu, wd):
            g = jnp.dot(x, wg[...], preferred_element_type=F32)
            u = jnp.dot(x, wu[...], preferred_element_type=F32)
            a = (g * _sigmoid(g) * u).astype(BF16)
            return jnp.dot(a, wd[...], preferred_element_type=F32)

        y = gate_a * mlp(wga_ref, wua_ref, wda_ref) + gate_b * mlp(wgb_ref, wub_ref, wdb_ref)
        for c in range(d // LANES):
            y_ref[pl.ds(c, n, stride=ROW_TILE), :] = y[:, c * LANES:(c + 1) * LANES]

    @pl.when(n_real > blk // 2)
    def _():
        run(blk)

    @pl.when((n_real > 0) & (n_real <= blk // 2))
    def _():
        run(blk // 2)
        y_ref[blk // 2 * ROW_TILE:, :] = jnp.zeros((blk // 2 * ROW_TILE, LANES), y_ref.dtype)

    @pl.when(n_real == 0)
    def _():
        y_ref[...] = jnp.zeros_like(y_ref)


def _weight_schedule(blk_e, blk_rows):
    nb = blk_e.shape[0]
    valid = blk_rows > 0
    prev = jnp.concatenate([jnp.full((1,), -1, jnp.int32), blk_e[:-1]])
    first = valid & (blk_e != prev)
    seg = jnp.cumsum(first.astype(jnp.int32)) - 1
    n_seg = jnp.sum(first.astype(jnp.int32))
    seg_e = jnp.full((nb + 3,), -1, jnp.int32).at[jnp.where(first, seg, nb + 2)].set(blk_e)
    seg_e = seg_e.at[nb + 2].set(-1)
    fetch = jnp.where(first & (seg + 2 < n_seg), seg_e[jnp.clip(seg + 2, 0, nb + 1)], -1)
    return jnp.maximum(seg, 0), first.astype(jnp.int32), fetch.astype(jnp.int32), seg_e[:2]


def _moe_call(blk_ea, blk_eb, blk_rows, hs, wg, wu, wd):
    nb = blk_ea.shape[0]
    d = wg.shape[1]
    de = wg.shape[2]
    rows = MOE_BLOCK * ROW_TILE
    sched = [_weight_schedule(e, blk_rows) for e in (blk_ea, blk_eb)]
    seg, first, fetch, pre = (jnp.stack([s[k] for s in sched]) for k in range(4))
    tok = pl.BlockSpec((rows, LANES), lambda i, *_: (i, 0))
    hbm = pl.BlockSpec(memory_space=pl.ANY)
    grid_spec = pltpu.PrefetchScalarGridSpec(
        num_scalar_prefetch=5,
        grid=(nb,),
        in_specs=[tok, hbm, hbm, hbm],
        out_specs=tok,
        scratch_shapes=[
            pltpu.VMEM((2, WEIGHT_SLOTS, d, de), BF16),
            pltpu.VMEM((2, WEIGHT_SLOTS, d, de), BF16),
            pltpu.VMEM((2, WEIGHT_SLOTS, de, d), BF16),
            pltpu.SemaphoreType.DMA((2, WEIGHT_SLOTS)),
        ],
    )
    return pl.pallas_call(
        _moe_kernel,
        grid_spec=grid_spec,
        out_shape=jax.ShapeDtypeStruct((nb * rows, LANES), F32),
        compiler_params=pltpu.CompilerParams(dimension_semantics=("arbitrary",), vmem_limit_bytes=VMEM_LIMIT),
        name="moe",
    )(blk_rows, seg, first, fetch, pre, hs, wg, wu, wd)


def _final_kernel(dest_ref, dest_next_ref, xmid_ref, mod_ref, fn_ref, y_hbm, o_ref, fbuf, sems):
    i = pl.program_id(0)
    n_steps = pl.num_programs(0)
    n = dest_ref.shape[0]
    d = xmid_ref.shape[1]

    def issue(idx_ref, slot):
        def body(o, carry):
            for k in range(DMA_UNROLL):
                t = o * DMA_UNROLL + k
                src = pl.multiple_of(idx_ref[t] * ROW_TILE, ROW_TILE)
                dst = pl.multiple_of(t * ROW_TILE, ROW_TILE)
                pltpu.make_async_copy(y_hbm.at[pl.ds(src, ROW_TILE)], fbuf.at[slot, pl.ds(dst, ROW_TILE)],
                                      sems.at[slot]).start(priority=k % 2)
            return carry

        lax.fori_loop(0, n // DMA_UNROLL, body, 0)

    slot = i % 2

    @pl.when(i == 0)
    def _():
        issue(dest_ref, 0)

    @pl.when(i + 1 < n_steps)
    def _():
        issue(dest_next_ref, 1 - slot)

    pltpu.make_async_copy(y_hbm.at[pl.ds(0, n * ROW_TILE)], fbuf.at[slot], sems.at[slot]).wait()
    f = jnp.concatenate([fbuf[slot, pl.ds(c, n, stride=ROW_TILE), :] for c in range(d // LANES)], axis=-1)
    x = xmid_ref[...] + mod_ref[0][5:6] * f
    ms = jnp.mean(x * x, axis=-1, keepdims=True)
    o_ref[...] = x * lax.rsqrt(ms + NORM_EPS) * fn_ref[...]


def _final_call(dest, xmid, mod3, fn, y, seq):
    t, d = xmid.shape
    tm = TM_ROWS
    tpb = seq // tm
    n_steps = t // tm
    return pl.pallas_call(
        _final_kernel,
        grid=(n_steps,),
        in_specs=[
            pl.BlockSpec((tm,), lambda i: (i,), memory_space=pltpu.SMEM),
            pl.BlockSpec((tm,), lambda i: (jnp.minimum(i + 1, n_steps - 1),), memory_space=pltpu.SMEM),
            pl.BlockSpec((tm, d), lambda i: (i, 0)),
            pl.BlockSpec((1, 6, d), lambda i: (i // tpb, 0, 0)),
            pl.BlockSpec((1, d), lambda i: (0, 0)),
            pl.BlockSpec(memory_space=pl.ANY),
        ],
        out_specs=pl.BlockSpec((tm, d), lambda i: (i, 0)),
        out_shape=jax.ShapeDtypeStruct((t, d), F32),
        scratch_shapes=[pltpu.VMEM((2, tm * ROW_TILE, LANES), F32), pltpu.SemaphoreType.DMA((2,))],
        compiler_params=pltpu.CompilerParams(dimension_semantics=("arbitrary",), vmem_limit_bytes=VMEM_LIMIT),
        name="final",
    )(dest, dest, xmid, mod3, fn, y)


def _pair_tables():
    ea, eb = [], []
    for g in range(N_GROUPS):
        for a in range(EXPERTS_PER_GROUP):
            for b in range(a + 1, EXPERTS_PER_GROUP):
                ea.append(g * EXPERTS_PER_GROUP + a)
                eb.append(g * EXPERTS_PER_GROUP + b)
    return np.asarray(ea, np.int32), np.asarray(eb, np.int32)


def _routing_plan(info, counts, t):
    cls = info[0].astype(jnp.int32)
    rank = info[1].astype(jnp.int32)
    cnt = counts[:N_CLASSES, 0].astype(jnp.int32)
    padded = (cnt + MOE_BLOCK - 1) // MOE_BLOCK * MOE_BLOCK
    pad_end = jnp.cumsum(padded)
    pad_start = pad_end - padded
    cls_ids = jnp.arange(N_CLASSES, dtype=jnp.int32)
    dest = rank + jnp.sum(jnp.where(cls[:, None] == cls_ids[None, :], pad_start[None, :], 0), axis=1)
    nb = t // MOE_BLOCK + N_CLASSES
    nb_used = pad_end[-1] // MOE_BLOCK
    blk = jnp.arange(nb, dtype=jnp.int32)
    blk_cls = jnp.sum(pad_end[None, :] <= (blk * MOE_BLOCK)[:, None], axis=1)
    blk_cls = jnp.clip(blk_cls, 0, N_CLASSES - 1)
    valid = blk < nb_used
    last_cls = blk_cls[jnp.maximum(nb_used - 1, 0)]
    blk_cls = jnp.where(valid, blk_cls, last_cls).astype(jnp.int32)
    tab_a, tab_b = _pair_tables()
    blk_ea = jnp.asarray(tab_a)[blk_cls]
    blk_eb = jnp.asarray(tab_b)[blk_cls]
    tail_start = jnp.where(cnt > 0, pad_end - MOE_BLOCK, -1).astype(jnp.int32)
    tail_start = jnp.concatenate([tail_start, nb_used[None].astype(jnp.int32)])
    blk_rows = jnp.clip(cnt[blk_cls] - (blk * MOE_BLOCK - pad_start[blk_cls]), 0, MOE_BLOCK)
    blk_rows = jnp.where(valid, blk_rows, 0).astype(jnp.int32)
    return dest, tail_start, blk_ea, blk_eb, blk_rows, nb


def kernel(x, c, ctx, c_ctx, w_mod, b_mod, norm_mix, w_in, rpb, w_att_out, conv_w, conv_b, conv_ln_g, conv_ln_b,
           w_conv_out, w_o, norm_ffn, w_router_group, b_router_group, w_router_expert, b_router_expert,
           w_exp_gate, w_exp_up, w_exp_down, final_norm):
    b, seq, d = x.shape
    n_ctx = ctx.shape[1]
    t = b * seq
    assert w_mod.shape[0] == 1, "single layer"
    assert seq % TM_ROWS == 0 and seq // GRID_W >= WIN_ROWS and n_ctx % LANES == 0 and (b * n_ctx) % 512 == 0
    assert d == 1024

    mod_rows = -(-(b + 1) // SUBLANES) * SUBLANES
    cc = jnp.zeros((mod_rows, d), F32).at[:b].set(c).at[b].set(c_ctx)
    m_all = _mod_call(cc, w_mod[0], b_mod[0][None, :])
    mod_lat = m_all[:b].reshape(b, 6, d)
    mod_ctx = m_all[b:b + 1].reshape(1, 6, d)

    x2 = x.reshape(t, d)
    g_mix = norm_mix[0][None, :]
    w_in_b = w_in[0].astype(BF16)
    wqv_t = jnp.concatenate([w_in_b[:, Q0:K0], w_in_b[:, V0:GLU0]], axis=1).T
    w_rest = jnp.concatenate([w_in_b[:, K0:V0], w_in_b[:, GLU0:]], axis=1)
    cw = jnp.zeros((32, D_CONV), F32).at[:CONV_WIDTH].set(conv_w[0])
    qt, k, vt, u, ga, gb = _proj_call(x2, mod_lat, g_mix, wqv_t, w_rest, seq)
    yn = _conv_call(u, cw, conv_b[0][None, :], conv_ln_g[0][None, :], conv_ln_b[0][None, :], seq)
    kc, vct = _ctx_kv_call(ctx.reshape(b * n_ctx, d), mod_ctx, g_mix, w_in_b[:, K0:V0], w_in_b[:, V0:GLU0].T)

    att = _attn_call(qt, k, vt, kc, vct, _bias_table(rpb[0]), seq, n_ctx)

    wr = jnp.zeros((d, LANES), F32).at[:, :N_GROUPS].set(w_router_group[0])
    wr = wr.at[:, ROUTER_E0:ROUTER_E0 + N_EXPERTS].set(w_router_expert[0])
    wr_hi = wr.astype(BF16)
    wr = jnp.concatenate([wr_hi, (wr - wr_hi.astype(F32)).astype(BF16)], axis=1)
    br = jnp.zeros((1, LANES), F32).at[0, :N_GROUPS].set(b_router_group[0])
    br = br.at[0, ROUTER_E0:ROUTER_E0 + N_EXPERTS].set(b_router_expert[0])
    x_mid, hp, info, counts = _merge_call(
        att, yn, ga, gb, x2, mod_lat, w_att_out[0].astype(BF16), w_conv_out[0].astype(BF16), w_o[0].astype(BF16), norm_ffn[0][None, :],
        wr, br, seq)

    dest, tail_start, blk_ea, blk_eb, blk_rows, nb = _routing_plan(info, counts, t)
    hs = _dispatch_call(dest, tail_start, hp, nb * MOE_BLOCK)
    y = _moe_call(blk_ea, blk_eb, blk_rows, hs,
                  w_exp_gate[0].astype(BF16), w_exp_up[0].astype(BF16), w_exp_down[0].astype(BF16))
    out = _final_call(dest, x_mid, mod_lat, final_norm[None, :], y, seq)
    return out.reshape(b, seq, d)
```

```python
import functools

import numpy as np
import jax
import jax.numpy as jnp
from jax import lax
from jax.experimental import pallas as pl
from jax.experimental.pallas import tpu as pltpu

F32 = jnp.float32
BF16 = jnp.bfloat16
U32 = jnp.uint32

GRID_W = 64
N_HEADS = 8
HEAD_DIM = 64
D_ATT = N_HEADS * HEAD_DIM
WIN_ROWS = 8
WIN_COLS = 16
D_CONV = 512
CONV_WIDTH = 31
N_GROUPS = 4
EXPERTS_PER_GROUP = 8
N_EXPERTS = N_GROUPS * EXPERTS_PER_GROUP
PAIRS_PER_GROUP = EXPERTS_PER_GROUP * (EXPERTS_PER_GROUP - 1) // 2
N_CLASSES = N_GROUPS * PAIRS_PER_GROUP
ROUTER_E0 = 8
NORM_EPS = 1e-6
NEG_INF = -1e30

LANES = 128
SUBLANES = 8
ROW_TILE = SUBLANES
HALO = 16
VMEM_LIMIT = 56 * 1024 * 1024

TM_PROJ = 1024
TM_MERGE = 512
MERGE_PARTS = 1
TM_ROWS = 1024
TM_DISPATCH = 2048
TM_CONV = 1024
MOE_BLOCK = 256
MOE_BLOCKS_PER_STEP = 2
WEIGHT_SLOTS = 3
CONV_CHUNK = 64
DMA_UNROLL = 8

HIGHEST = lax.Precision.HIGHEST


def _norm_mod(x, g, shift, scale):
    ms = jnp.mean(x * x, axis=-1, keepdims=True)
    y = x * lax.rsqrt(ms + NORM_EPS) * g
    return y * (1.0 + scale) + shift


def _sigmoid(x):
    return jax.nn.sigmoid(x)


def _mod_kernel(c_ref, w_ref, b_ref, o_ref):
    c = c_ref[...]
    s = c * _sigmoid(c)
    o_ref[...] = jnp.dot(s, w_ref[...], precision=HIGHEST, preferred_element_type=F32) + b_ref[...]


def _mod_call(cc, w_mod, b_mod):
    rows, d = cc.shape
    n = w_mod.shape[1]
    tn = 1024
    return pl.pallas_call(
        _mod_kernel,
        grid=(n // tn,),
        in_specs=[
            pl.BlockSpec((rows, d), lambda j: (0, 0)),
            pl.BlockSpec((d, tn), lambda j: (0, j)),
            pl.BlockSpec((1, tn), lambda j: (0, j)),
        ],
        out_specs=pl.BlockSpec((rows, tn), lambda j: (0, j)),
        out_shape=jax.ShapeDtypeStruct((rows, n), F32),
        compiler_params=pltpu.CompilerParams(dimension_semantics=("arbitrary",), vmem_limit_bytes=VMEM_LIMIT),
        name="mod",
    )(cc, w_mod, b_mod)


Q0, K0, V0, GLU0 = 0, D_ATT, 2 * D_ATT, 3 * D_ATT
GA0 = GLU0 + 2 * D_CONV


NT_DIMS = (((1,), (1,)), ((), ()))
LOG2_E = 1.4426950408889634
SCORE_SCALE = HEAD_DIM ** -0.5 * LOG2_E


def _store_transposed(res_t, ref, scale=None):
    for j in range(ref.shape[0]):
        blk = res_t[:, j * LANES:(j + 1) * LANES]
        if scale is not None:
            blk = blk * scale
        ref[j] = blk.astype(ref.dtype)


def _conv_kernel(u_ref, up_ref, un_ref, cw_ref, cb_ref, lg_ref, lb_ref, yn_ref, ubuf, shifted,
                 *, tiles_per_seq):
    i = pl.program_id(0)
    tm = u_ref.shape[0]
    first = (i % tiles_per_seq) == 0
    last = (i % tiles_per_seq) == tiles_per_seq - 1
    ubuf[0:HALO, :] = jnp.where(first, 0.0, up_ref[...].astype(F32))
    for r0 in range(0, tm, CONV_CHUNK):
        ubuf[HALO + r0:HALO + r0 + CONV_CHUNK, :] = u_ref[r0:r0 + CONV_CHUNK, :].astype(F32)
    ubuf[HALO + tm:, :] = jnp.where(last, 0.0, un_ref[...].astype(F32))

    span = shifted.shape[1]
    for ph in range(SUBLANES):
        for r0 in range(0, span, CONV_CHUNK):
            n = min(CONV_CHUNK, span - r0)
            shifted[ph, r0:r0 + n, :] = ubuf[ph + r0:ph + r0 + n, :]

    base = HALO - CONV_WIDTH // 2
    inv_c = 1.0 / D_CONV
    for tc in range(tm // CONV_CHUNK):
        accs = []
        for lc in range(D_CONV // LANES):
            ls = slice(lc * LANES, (lc + 1) * LANES)
            acc = jnp.zeros((CONV_CHUNK, LANES), F32)
            for j in range(CONV_WIDTH):
                tiles, ph = divmod(base + j, SUBLANES)
                lo = tc * CONV_CHUNK + tiles * SUBLANES
                acc = acc + shifted[ph, lo:lo + CONV_CHUNK, ls] * cw_ref[j:j + 1, ls]
            accs.append(acc + cb_ref[:, ls])
        mu = sum(jnp.sum(a, axis=-1, keepdims=True) for a in accs) * inv_c
        cen = [a - mu for a in accs]
        var = sum(jnp.sum(c * c, axis=-1, keepdims=True) for c in cen) * inv_c
        rstd = lax.rsqrt(var + NORM_EPS)
        for lc, c in enumerate(cen):
            ls = slice(lc * LANES, (lc + 1) * LANES)
            yn = c * rstd * lg_ref[:, ls] + lb_ref[:, ls]
            yn_ref[tc * CONV_CHUNK:(tc + 1) * CONV_CHUNK, ls] = (yn * _sigmoid(yn)).astype(yn_ref.dtype)


def _conv_call(u, cw, cb, lg, lb, seq):
    t = u.shape[0]
    tm = TM_CONV
    tpb = seq // tm
    hb = tm // HALO
    n_halo = t // HALO
    row = lambda i: (i, 0)
    const = lambda i: (0, 0)
    return pl.pallas_call(
        functools.partial(_conv_kernel, tiles_per_seq=tpb),
        grid=(t // tm,),
        in_specs=[
            pl.BlockSpec((tm, D_CONV), row),
            pl.BlockSpec((HALO, D_CONV), lambda i: (jnp.maximum(i * hb - 1, 0), 0)),
            pl.BlockSpec((HALO, D_CONV), lambda i: (jnp.minimum((i + 1) * hb, n_halo - 1), 0)),
            pl.BlockSpec(cw.shape, const),
            pl.BlockSpec((1, D_CONV), const),
            pl.BlockSpec((1, D_CONV), const),
            pl.BlockSpec((1, D_CONV), const),
        ],
        out_specs=pl.BlockSpec((tm, D_CONV), row),
        out_shape=jax.ShapeDtypeStruct((t, D_CONV), BF16),
        scratch_shapes=[
            pltpu.VMEM((tm + 2 * HALO, D_CONV), F32),
            pltpu.VMEM((SUBLANES, tm + 2 * HALO - SUBLANES, D_CONV), F32),
        ],
        compiler_params=pltpu.CompilerParams(dimension_semantics=("arbitrary",), vmem_limit_bytes=VMEM_LIMIT),
        name="conv",
    )(u, u, u, cw, cb, lg, lb)


def _proj_kernel(x_ref, mod_ref, g_ref, wqv_ref, w_ref, qt_ref, k_ref, vt_ref, u_ref, ga_ref, gb_ref):
    d = x_ref.shape[1]
    mod = mod_ref[0]
    h = _norm_mod(x_ref[...], g_ref[...], mod[0:1], mod[1:2]).astype(BF16)

    qv_t = lax.dot_general(wqv_ref[...], h, NT_DIMS, preferred_element_type=F32)
    _store_transposed(qv_t[:D_ATT], qt_ref, SCORE_SCALE)
    _store_transposed(qv_t[D_ATT:], vt_ref)

    def seg(lo, hi):
        return jnp.dot(h, w_ref[:, lo:hi], preferred_element_type=F32)

    k_ref[...] = seg(0, D_ATT).astype(BF16)
    a = seg(D_ATT, D_ATT + D_CONV)
    g = seg(D_ATT + D_CONV, D_ATT + 2 * D_CONV)
    u_ref[...] = (a * _sigmoid(g)).astype(BF16)
    g0 = D_ATT + 2 * D_CONV
    ga_ref[...] = _sigmoid(seg(g0, g0 + d)).astype(BF16)
    gb_ref[...] = _sigmoid(seg(g0 + d, g0 + 2 * d)).astype(BF16)


def _proj_call(x2, mod3, g, wqv_t, w_rest, seq):
    t, d = x2.shape
    tm = TM_PROJ
    tpb = seq // tm
    nblk = tm // LANES
    outs = [
        jax.ShapeDtypeStruct((t // LANES, D_ATT, LANES), BF16),
        jax.ShapeDtypeStruct((t, D_ATT), BF16),
        jax.ShapeDtypeStruct((t // LANES, D_ATT, LANES), BF16),
        jax.ShapeDtypeStruct((t, D_CONV), BF16),
        jax.ShapeDtypeStruct((t, d), BF16),
        jax.ShapeDtypeStruct((t, d), BF16),
    ]
    row = lambda i: (i, 0)
    blk3 = lambda i: (i, 0, 0)
    return pl.pallas_call(
        _proj_kernel,
        grid=(t // tm,),
        in_specs=[
            pl.BlockSpec((tm, d), row),
            pl.BlockSpec((1, 6, d), lambda i: (i // tpb, 0, 0)),
            pl.BlockSpec((1, d), lambda i: (0, 0)),
            pl.BlockSpec(wqv_t.shape, lambda i: (0, 0), pipeline_mode=pl.Buffered(1)),
            pl.BlockSpec(w_rest.shape, lambda i: (0, 0), pipeline_mode=pl.Buffered(1)),
        ],
        out_specs=[
            pl.BlockSpec((nblk, D_ATT, LANES), blk3),
            pl.BlockSpec((tm, D_ATT), row),
            pl.BlockSpec((nblk, D_ATT, LANES), blk3),
            pl.BlockSpec((tm, D_CONV), row),
            pl.BlockSpec((tm, d), row),
            pl.BlockSpec((tm, d), row),
        ],
        out_shape=outs,
        compiler_params=pltpu.CompilerParams(dimension_semantics=("arbitrary",), vmem_limit_bytes=VMEM_LIMIT),
        name="proj",
    )(x2, mod3, g, wqv_t, w_rest)


def _ctx_kv_kernel(x_ref, mod_ref, g_ref, wk_ref, wvt_ref, k_ref, vt_ref):
    mod = mod_ref[0]
    h = _norm_mod(x_ref[...], g_ref[...], mod[0:1], mod[1:2]).astype(BF16)
    k_ref[...] = jnp.dot(h, wk_ref[...], preferred_element_type=F32).astype(BF16)
    _store_transposed(lax.dot_general(wvt_ref[...], h, NT_DIMS, preferred_element_type=F32), vt_ref)


def _ctx_kv_call(c2, mod3, g, w_k, w_vt):
    t, d = c2.shape
    tm = 512
    nblk = tm // LANES
    row = lambda i: (i, 0)
    return pl.pallas_call(
        _ctx_kv_kernel,
        grid=(t // tm,),
        in_specs=[
            pl.BlockSpec((tm, d), row),
            pl.BlockSpec((1, 6, d), lambda i: (0, 0, 0)),
            pl.BlockSpec((1, d), lambda i: (0, 0)),
            pl.BlockSpec(w_k.shape, lambda i: (0, 0)),
            pl.BlockSpec(w_vt.shape, lambda i: (0, 0)),
        ],
        out_specs=[pl.BlockSpec((tm, D_ATT), row), pl.BlockSpec((nblk, D_ATT, LANES), lambda i: (i, 0, 0))],
        out_shape=[jax.ShapeDtypeStruct((t, D_ATT), BF16), jax.ShapeDtypeStruct((t // LANES, D_ATT, LANES), BF16)],
        compiler_params=pltpu.CompilerParams(dimension_semantics=("arbitrary",), vmem_limit_bytes=VMEM_LIMIT),
        name="ctx_kv",
    )(c2, mod3, g, w_k, w_vt)


HEADS_PER_GROUP = 4
GROUP_W = HEADS_PER_GROUP * HEAD_DIM
N_HEAD_GROUPS = N_HEADS // HEADS_PER_GROUP


def _attn_kernel(qt_ref, k_ref, vt_ref, kc_ref, vct_ref, bias_ref, o_ref, vboth):
    nblk = qt_ref.shape[0]
    rows = 2 * nblk
    n_loc = WIN_ROWS * GRID_W
    n_ctx_blk = vct_ref.shape[0]
    half = GRID_W

    for j in range(nblk):
        vboth[0, j] = vt_ref[j]
    for j in range(nblk - 1):
        vboth[1, j] = jnp.concatenate([vt_ref[j][:, half:], vt_ref[j + 1][:, :half]], axis=1)
    vboth[1, nblk - 1] = jnp.zeros_like(vt_ref[0])

    lane = lax.broadcasted_iota(jnp.int32, (D_ATT, LANES), 1)
    low = lane < half
    rblk = lax.broadcasted_iota(jnp.int32, (GROUP_W, GROUP_W), 0) // HEAD_DIM
    cblk = lax.broadcasted_iota(jnp.int32, (GROUP_W, GROUP_W), 1) // HEAD_DIM
    diag = rblk == cblk
    low64 = lax.broadcasted_iota(jnp.int32, (HEAD_DIM, LANES), 1) < half
    kc = kc_ref[...]

    def one_row(row, tiled, side):
        r_start = jnp.clip(row - WIN_ROWS // 2, 0, rows - WIN_ROWS)
        b_off = pl.multiple_of((WIN_ROWS - 1 - (row - r_start)) * GRID_W, GRID_W)
        kw = k_ref[pl.ds(pl.multiple_of(r_start * GRID_W, GRID_W), n_loc), :]
        vwin = vboth[r_start % 2, pl.ds(r_start // 2, n_loc // LANES)]
        parts = []
        for g in range(N_HEAD_GROUPS):
            fs = slice(g * GROUP_W, (g + 1) * GROUP_W)
            tg = tiled[fs, :]
            w = jnp.where(diag, jnp.concatenate([tg, tg], axis=1), jnp.zeros((), BF16))
            s_loc = jnp.dot(kw[:, fs], w, preferred_element_type=F32) + bias_ref[g, pl.ds(b_off, n_loc), :]
            s_ctx = jnp.dot(kc[:, fs], w, preferred_element_type=F32)
            m = jnp.maximum(jnp.max(s_loc, axis=0, keepdims=True), jnp.max(s_ctx, axis=0, keepdims=True))
            p_loc = jnp.exp2(s_loc - m)
            p_ctx = jnp.exp2(s_ctx - m)
            l = jnp.sum(p_loc, axis=0, keepdims=True) + jnp.sum(p_ctx, axis=0, keepdims=True)
            p = jnp.concatenate([p_loc.astype(BF16), p_ctx.astype(BF16)], axis=0)
            vt_g = jnp.concatenate([vwin[j][fs, :] for j in range(n_loc // LANES)]
                                   + [vct_ref[j][fs, :] for j in range(n_ctx_blk)], axis=1)
            o_t = jnp.dot(vt_g, p, preferred_element_type=F32) * (1.0 / l)
            for j in range(HEADS_PER_GROUP):
                blk = o_t[j * HEAD_DIM:(j + 1) * HEAD_DIM, (j // 2) * LANES:(j // 2 + 1) * LANES]
                if j % 2 != side:
                    blk = pltpu.roll(blk, half, axis=1)
                parts.append(blk)
        return parts

    def body(i, carry):
        xq = qt_ref[i]
        rolled = jnp.concatenate([xq[:, half:], xq[:, :half]], axis=1)
        parts_a = one_row(2 * i, jnp.where(low, xq, rolled), 0)
        parts_b = one_row(2 * i + 1, jnp.where(low, rolled, xq), 1)
        out = jnp.concatenate([jnp.where(low64, a, b) for a, b in zip(parts_a, parts_b)], axis=0)
        o_ref[i] = out.astype(o_ref.dtype)
        return carry

    lax.fori_loop(0, nblk, body, 0, unroll=8)


def _attn_call(qt, k, vt, kc, vct, bias, seq, n_ctx):
    t = k.shape[0]
    b = t // seq
    nblk = seq // LANES
    lat3 = pl.BlockSpec((nblk, D_ATT, LANES), lambda i: (i, 0, 0))
    return pl.pallas_call(
        _attn_kernel,
        grid=(b,),
        in_specs=[lat3,
                  pl.BlockSpec((seq, D_ATT), lambda i: (i, 0)),
                  lat3,
                  pl.BlockSpec((n_ctx, D_ATT), lambda i: (i, 0)),
                  pl.BlockSpec((n_ctx // LANES, D_ATT, LANES), lambda i: (i, 0, 0)),
                  pl.BlockSpec(bias.shape, lambda i: (0, 0, 0))],
        out_specs=lat3,
        out_shape=jax.ShapeDtypeStruct((t // LANES, D_ATT, LANES), BF16),
        scratch_shapes=[pltpu.VMEM((2, nblk, D_ATT, LANES), BF16)],
        compiler_params=pltpu.CompilerParams(dimension_semantics=("arbitrary",), vmem_limit_bytes=VMEM_LIMIT),
        name="attn",
    )(qt, k, vt, kc, vct, bias)


def _bias_table(rpb):
    cq = np.arange(GRID_W)[:, None]
    ck = np.arange(GRID_W)[None, :]
    c_start = np.clip(cq - WIN_COLS // 2, 0, GRID_W - WIN_COLS)
    col_mask = (ck >= c_start) & (ck < c_start + WIN_COLS)
    n_dc = rpb.shape[2]
    lead = GRID_W - WIN_COLS
    w = jnp.pad(rpb, ((0, 0), (0, 0), (lead, 2 * GRID_W - lead - n_dc)))
    skew = jnp.tile(w, (1, 1, GRID_W + 1))[:, :, :GRID_W * (2 * GRID_W + 1)]
    tab = skew.reshape(rpb.shape[0], rpb.shape[1], GRID_W, 2 * GRID_W + 1)[:, :, ::-1, :GRID_W]
    tab = jnp.where(col_mask[None, None], tab, NEG_INF)
    n_dr = tab.shape[1]
    tab = tab.reshape(N_HEAD_GROUPS, HEADS_PER_GROUP, n_dr, GRID_W, GRID_W)
    tab = tab.transpose(0, 2, 4, 1, 3)
    return (tab.reshape(N_HEAD_GROUPS, n_dr * GRID_W, GROUP_W) * LOG2_E).astype(F32)


def _merge_rows(r0, n, att_ref, yn_ref, ga_ref, gb_ref, x_ref, mod,
                wa_ref, wc_ref, wo_ref, nf_ref, wr_ref, br_ref, xmid_ref, hp_ref):
    d = x_ref.shape[1]
    rs = slice(r0, r0 + n)
    y_conv = jnp.dot(yn_ref[rs, :], wc_ref[...], preferred_element_type=F32)

    att = jnp.concatenate([att_ref[j].astype(F32).T for j in range(r0 // LANES, (r0 + n) // LANES)], axis=0)
    y_att = jnp.dot(att.astype(BF16), wa_ref[...], preferred_element_type=F32)
    mix = ga_ref[rs, :].astype(F32) * y_att + gb_ref[rs, :].astype(F32) * y_conv
    y = jnp.dot(mix.astype(BF16), wo_ref[...], preferred_element_type=F32)
    x_mid = x_ref[rs, :] + mod[2:3] * y
    xmid_ref[rs, :] = x_mid
    h2 = _norm_mod(x_mid, nf_ref[...], mod[3:4], mod[4:5])

    h2_hi = h2.astype(BF16)
    h2_lo = (h2 - h2_hi.astype(F32)).astype(BF16)
    t_hi = jnp.dot(h2_hi, wr_ref[...], preferred_element_type=F32)
    t_lo = jnp.dot(h2_lo, wr_ref[:, :LANES], preferred_element_type=F32)
    logits = t_hi[:, :LANES] + t_hi[:, LANES:] + t_lo + br_ref[...]
    lt = logits.T
    sub = lax.broadcasted_iota(jnp.int32, (SUBLANES, n), 0).astype(F32)
    big = float(LANES)
    is_g = sub < N_GROUPS
    gl = jnp.where(is_g, lt[0:SUBLANES], -jnp.inf)
    gmax = jnp.max(gl, axis=0, keepdims=True)
    g_idx = jnp.min(jnp.where(gl == gmax, sub, big), axis=0, keepdims=True)
    p_group = 1.0 / jnp.sum(jnp.where(is_g, jnp.exp(gl - gmax), 0.0), axis=0, keepdims=True)
    el = lt[ROUTER_E0:ROUTER_E0 + EXPERTS_PER_GROUP]
    for g in range(1, N_GROUPS):
        lo_g = ROUTER_E0 + g * EXPERTS_PER_GROUP
        el = jnp.where(g_idx == float(g), lt[lo_g:lo_g + EXPERTS_PER_GROUP], el)
    v1 = jnp.max(el, axis=0, keepdims=True)
    j1 = jnp.min(jnp.where(el == v1, sub, big), axis=0, keepdims=True)
    el2 = jnp.where(sub == j1, -jnp.inf, el)
    v2 = jnp.max(el2, axis=0, keepdims=True)
    j2 = jnp.min(jnp.where(el2 == v2, sub, big), axis=0, keepdims=True)
    e2 = jnp.exp(v2 - v1)
    gate1 = p_group / (1.0 + e2)
    gate2 = p_group * e2 / (1.0 + e2)
    ja = jnp.minimum(j1, j2)
    jb = jnp.maximum(j1, j2)
    gate_a = jnp.where(j1 < j2, gate1, gate2)
    gate_b = jnp.where(j1 < j2, gate2, gate1)
    pair = ja * (2 * EXPERTS_PER_GROUP - 1 - ja) * 0.5 + (jb - ja - 1.0)
    cls = g_idx * PAIRS_PER_GROUP + pair
    gate_rows = jnp.where(sub == 0.0, gate_a, jnp.where(sub == 1.0, gate_b, 0.0))
    gates = jnp.concatenate([gate_rows, jnp.zeros((LANES - SUBLANES, n), F32)], axis=0).T

    half = d // 2
    hi = lax.bitcast_convert_type(h2[:, :half].astype(BF16).astype(F32), U32)
    lo = lax.bitcast_convert_type(h2[:, half:].astype(BF16).astype(F32), U32)
    words = hi | (lo >> 16)
    n_words = half // LANES
    base = r0 * ROW_TILE
    for c in range(n_words):
        hp_ref[pl.ds(base + c, n, stride=ROW_TILE), :] = words[:, c * LANES:(c + 1) * LANES]
    hp_ref[pl.ds(base + n_words, n, stride=ROW_TILE), :] = lax.bitcast_convert_type(gates, U32)
    zero = jnp.zeros((n, LANES), U32)
    for c in range(n_words + 1, ROW_TILE):
        hp_ref[pl.ds(base + c, n, stride=ROW_TILE), :] = zero
    return cls


def _merge_kernel(att_ref, yn_ref, ga_ref, gb_ref, x_ref, mod_ref,
                  wa_ref, wc_ref, wo_ref, nf_ref, wr_ref, br_ref, tri_ref,
                  xmid_ref, hp_ref, info_ref, cnt_ref, run_ref):
    i = pl.program_id(0)
    tm = x_ref.shape[0]

    @pl.when(i == 0)
    def _():
        run_ref[...] = jnp.zeros_like(run_ref)

    mod = mod_ref[0]
    n = tm // MERGE_PARTS
    cls = jnp.concatenate(
        [_merge_rows(p * n, n, att_ref, yn_ref, ga_ref, gb_ref, x_ref, mod, wa_ref, wc_ref, wo_ref, nf_ref,
                     wr_ref, br_ref, xmid_ref, hp_ref) for p in range(MERGE_PARTS)], axis=1)

    crow = lax.broadcasted_iota(jnp.int32, (LANES, tm), 0).astype(F32)
    oh_f = jnp.where(crow == cls, 1.0, 0.0)
    before = jnp.dot(oh_f.astype(BF16), tri_ref[...], preferred_element_type=F32)
    run = run_ref[:, 0:1]
    rank = jnp.sum(oh_f * (before + run), axis=0, keepdims=True)
    new_run = run + jnp.sum(oh_f, axis=1, keepdims=True)
    run_ref[...] = jnp.broadcast_to(new_run, run_ref.shape)
    cnt_ref[...] = jnp.broadcast_to(new_run, cnt_ref.shape)

    sub = lax.broadcasted_iota(jnp.int32, (SUBLANES, tm), 0)
    info_ref[...] = jnp.where(sub == 0, cls, jnp.where(sub == 1, rank, 0.0))


def _merge_call(att, yn, ga, gb, x2, mod3, wa, wc, wo, nf, wr, br, seq):
    t, d = x2.shape
    tm = TM_MERGE
    tpb = seq // tm
    tri = jnp.asarray(np.triu(np.ones((tm, tm), np.float32), 1), BF16)
    row = lambda i: (i, 0)
    const = lambda i: (0, 0)
    return pl.pallas_call(
        _merge_kernel,
        grid=(t // tm,),
        in_specs=[
            pl.BlockSpec((tm // LANES, D_ATT, LANES), lambda i: (i, 0, 0)),
            pl.BlockSpec((tm, D_CONV), row),
            pl.BlockSpec((tm, d), row),
            pl.BlockSpec((tm, d), row),
            pl.BlockSpec((tm, d), row),
            pl.BlockSpec((1, 6, d), lambda i: (i // tpb, 0, 0)),
            pl.BlockSpec(wa.shape, const),
            pl.BlockSpec(wc.shape, const),
            pl.BlockSpec(wo.shape, const),
            pl.BlockSpec((1, d), const),
            pl.BlockSpec(wr.shape, const),
            pl.BlockSpec((1, LANES), const),
            pl.BlockSpec((tm, tm), const),
        ],
        out_specs=[
            pl.BlockSpec((tm, d), row),
            pl.BlockSpec((tm * ROW_TILE, LANES), row),
            pl.BlockSpec((SUBLANES, tm), lambda i: (0, i)),
            pl.BlockSpec((LANES, LANES), const),
        ],
        out_shape=[
            jax.ShapeDtypeStruct((t, d), F32),
            jax.ShapeDtypeStruct((t * ROW_TILE, LANES), U32),
            jax.ShapeDtypeStruct((SUBLANES, t), F32),
            jax.ShapeDtypeStruct((LANES, LANES), F32),
        ],
        scratch_shapes=[pltpu.VMEM((LANES, LANES), F32)],
        compiler_params=pltpu.CompilerParams(dimension_semantics=("arbitrary",), vmem_limit_bytes=VMEM_LIMIT),
        name="merge",
    )(att, yn, ga, gb, x2, mod3, wa, wc, wo, nf, wr, br, tri)


def _dispatch_kernel(dest_ref, tail_ref, hp_ref, hs_hbm, zbuf, sem, zsem):
    n = dest_ref.shape[0]
    blk_rows = zbuf.shape[0]

    @pl.when(pl.program_id(0) == 0)
    def _():
        zbuf[...] = jnp.zeros_like(zbuf)

        def tail_copy(c):
            start = pl.multiple_of(tail_ref[c] * ROW_TILE, ROW_TILE)
            return pltpu.make_async_copy(zbuf, hs_hbm.at[pl.ds(start, blk_rows)], zsem)

        def start_one(c, carry):
            @pl.when(tail_ref[c] >= 0)
            def _():
                tail_copy(c).start()
            return carry

        def wait_one(c, carry):
            @pl.when(tail_ref[c] >= 0)
            def _():
                tail_copy(c).wait()
            return carry

        lax.fori_loop(0, N_CLASSES, start_one, 0)
        lax.fori_loop(0, N_CLASSES, wait_one, 0)

        def unused_copy(b):
            start = pl.multiple_of(b * blk_rows, blk_rows)
            return pltpu.make_async_copy(zbuf, hs_hbm.at[pl.ds(start, blk_rows)], zsem)

        def start_unused(b, carry):
            unused_copy(b).start()
            return carry

        def wait_unused(b, carry):
            unused_copy(b).wait()
            return carry

        n_blocks = hs_hbm.shape[0] // blk_rows
        lax.fori_loop(tail_ref[N_CLASSES], n_blocks, start_unused, 0)
        lax.fori_loop(tail_ref[N_CLASSES], n_blocks, wait_unused, 0)

    def body(o, carry):
        for k in range(DMA_UNROLL):
            t = o * DMA_UNROLL + k
            src = pl.multiple_of(t * ROW_TILE, ROW_TILE)
            dst = pl.multiple_of(dest_ref[t] * ROW_TILE, ROW_TILE)
            pltpu.make_async_copy(hp_ref.at[pl.ds(src, ROW_TILE)], hs_hbm.at[pl.ds(dst, ROW_TILE)],
                                  sem).start(priority=k % 2)
        return carry

    lax.fori_loop(0, n // DMA_UNROLL, body, 0)
    pltpu.make_async_copy(hp_ref, hs_hbm.at[pl.ds(0, n * ROW_TILE)], sem).wait()


def _dispatch_call(dest, tail_start, hp, p_rows):
    t = dest.shape[0]
    return pl.pallas_call(
        _dispatch_kernel,
        grid=(t // TM_DISPATCH,),
        in_specs=[
            pl.BlockSpec((TM_DISPATCH,), lambda i: (i,), memory_space=pltpu.SMEM),
            pl.BlockSpec(memory_space=pltpu.SMEM),
            pl.BlockSpec((TM_DISPATCH * ROW_TILE, LANES), lambda i: (i, 0)),
        ],
        out_specs=pl.BlockSpec(memory_space=pl.ANY),
        out_shape=jax.ShapeDtypeStruct((p_rows * ROW_TILE, LANES), U32),
        scratch_shapes=[pltpu.VMEM((MOE_BLOCK * ROW_TILE, LANES), U32),
                        pltpu.SemaphoreType.DMA(()), pltpu.SemaphoreType.DMA(())],
        compiler_params=pltpu.CompilerParams(dimension_semantics=("arbitrary",), has_side_effects=True,
                                             vmem_limit_bytes=VMEM_LIMIT),
        name="dispatch",
    )(dest, tail_start, hp)


def _moe_kernel(rows_ref, seg_ref, first_ref, fetch_ref, pre_ref, hs_ref, wg_hbm, wu_hbm, wd_hbm, y_ref,
                wg_buf, wu_buf, wd_buf, sems):
    rows = hs_ref.shape[0] // MOE_BLOCKS_PER_STEP
    for sub in range(MOE_BLOCKS_PER_STEP):
        _moe_block(pl.program_id(0) * MOE_BLOCKS_PER_STEP + sub, rows_ref, seg_ref, first_ref, fetch_ref, pre_ref,
                   hs_ref.at[pl.ds(sub * rows, rows)], wg_hbm, wu_hbm, wd_hbm, y_ref.at[pl.ds(sub * rows, rows)],
                   wg_buf, wu_buf, wd_buf, sems)


def _moe_block(i, rows_ref, seg_ref, first_ref, fetch_ref, pre_ref, hs_ref, wg_hbm, wu_hbm, wd_hbm, y_ref,
               wg_buf, wu_buf, wd_buf, sems):
    blk = hs_ref.shape[0] // ROW_TILE
    d = wg_buf.shape[2]
    n_words = d // 2 // LANES
    n_real = rows_ref[i]

    def copies(side, expert, slot):
        return [pltpu.make_async_copy(src.at[expert], buf.at[side, slot], sems.at[side, slot])
                for src, buf in ((wg_hbm, wg_buf), (wu_hbm, wu_buf), (wd_hbm, wd_buf))]

    for side in range(2):
        @pl.when(i == 0)
        def _():
            for s in range(2):
                @pl.when(pre_ref[side, s] >= 0)
                def _():
                    for cp in copies(side, pre_ref[side, s], s):
                        cp.start()

        @pl.when(fetch_ref[side, i] >= 0)
        def _():
            for cp in copies(side, fetch_ref[side, i], (seg_ref[side, i] + 2) % WEIGHT_SLOTS):
                cp.start()

    for side in range(2):
        @pl.when(first_ref[side, i] == 1)
        def _():
            for cp in copies(side, 0, seg_ref[side, i] % WEIGHT_SLOTS):
                cp.wait()

    slot_a = seg_ref[0, i] % WEIGHT_SLOTS
    slot_b = seg_ref[1, i] % WEIGHT_SLOTS
    wga_ref, wua_ref, wda_ref = wg_buf.at[0, slot_a], wu_buf.at[0, slot_a], wd_buf.at[0, slot_a]
    wgb_ref, wub_ref, wdb_ref = wg_buf.at[1, slot_b], wu_buf.at[1, slot_b], wd_buf.at[1, slot_b]

    def run(n):
        his, los = [], []
        for c in range(n_words):
            w = hs_ref[pl.ds(c, n, stride=ROW_TILE), :]
            his.append(lax.bitcast_convert_type(w & jnp.uint32(0xFFFF0000), F32).astype(BF16))
            los.append(lax.bitcast_convert_type(w << 16, F32).astype(BF16))
        x = jnp.concatenate(his + los, axis=-1)
        gates = lax.bitcast_convert_type(hs_ref[pl.ds(n_words, n, stride=ROW_TILE), :], F32)
        gate_a = gates[:, 0:1]
        gate_b = gates[:, 1:2]

        def mlp(wg, wu, wd):
            g = jnp.dot(x, wg[...], preferred_element_type=F32)
            u = jnp.dot(x, wu[...], preferred_element_type=F32)
            a = (g * _sigmoid(g) * u).astype(BF16)
            return jnp.dot(a, wd[...], preferred_element_type=F32)

        y = gate_a * mlp(wga_ref, wua_ref, wda_ref) + gate_b * mlp(wgb_ref, wub_ref, wdb_ref)
        for c in range(d // LANES):
            y_ref[pl.ds(c, n, stride=ROW_TILE), :] = y[:, c * LANES:(c + 1) * LANES]

    @pl.when(n_real > blk // 2)
    def _():
        run(blk)

    @pl.when((n_real > 0) & (n_real <= blk // 2))
    def _():
        run(blk // 2)
        y_ref[blk // 2 * ROW_TILE:, :] = jnp.zeros((blk // 2 * ROW_TILE, LANES), y_ref.dtype)

    @pl.when(n_real == 0)
    def _():
        y_ref[...] = jnp.zeros_like(y_ref)


def _weight_schedule(blk_e, blk_rows):
    nb = blk_e.shape[0]
    valid = blk_rows > 0
    prev = jnp.concatenate([jnp.full((1,), -1, jnp.int32), blk_e[:-1]])
    first = valid & (blk_e != prev)
    seg = jnp.cumsum(first.astype(jnp.int32)) - 1
    n_seg = jnp.sum(first.astype(jnp.int32))
    seg_e = jnp.full((nb + 3,), -1, jnp.int32).at[jnp.where(first, seg, nb + 2)].set(blk_e)
    seg_e = seg_e.at[nb + 2].set(-1)
    fetch = jnp.where(first & (seg + 2 < n_seg), seg_e[jnp.clip(seg + 2, 0, nb + 1)], -1)
    return jnp.maximum(seg, 0), first.astype(jnp.int32), fetch.astype(jnp.int32), seg_e[:2]


def _moe_call(blk_ea, blk_eb, blk_rows, hs, wg, wu, wd):
    nb = blk_ea.shape[0]
    d = wg.shape[1]
    de = wg.shape[2]
    rows = MOE_BLOCKS_PER_STEP * MOE_BLOCK * ROW_TILE
    sched = [_weight_schedule(e, blk_rows) for e in (blk_ea, blk_eb)]
    seg, first, fetch, pre = (jnp.stack([s[k] for s in sched]) for k in range(4))
    tok = pl.BlockSpec((rows, LANES), lambda i, *_: (i, 0))
    hbm = pl.BlockSpec(memory_space=pl.ANY)
    grid_spec = pltpu.PrefetchScalarGridSpec(
        num_scalar_prefetch=5,
        grid=(nb // MOE_BLOCKS_PER_STEP,),
        in_specs=[tok, hbm, hbm, hbm],
        out_specs=tok,
        scratch_shapes=[
            pltpu.VMEM((2, WEIGHT_SLOTS, d, de), BF16),
            pltpu.VMEM((2, WEIGHT_SLOTS, d, de), BF16),
            pltpu.VMEM((2, WEIGHT_SLOTS, de, d), BF16),
            pltpu.SemaphoreType.DMA((2, WEIGHT_SLOTS)),
        ],
    )
    return pl.pallas_call(
        _moe_kernel,
        grid_spec=grid_spec,
        out_shape=jax.ShapeDtypeStruct((nb * MOE_BLOCK * ROW_TILE, LANES), F32),
        compiler_params=pltpu.CompilerParams(dimension_semantics=("arbitrary",), vmem_limit_bytes=VMEM_LIMIT),
        name="moe",
    )(blk_rows, seg, first, fetch, pre, hs, wg, wu, wd)


def _final_kernel(dest_ref, dest_next_ref, xmid_ref, mod_ref, fn_ref, y_hbm, o_ref, fbuf, sems):
    i = pl.program_id(0)
    n_steps = pl.num_programs(0)
    n = dest_ref.shape[0]
    d = xmid_ref.shape[1]

    def issue(idx_ref, slot):
        def body(o, carry):
            for k in range(DMA_UNROLL):
                t = o * DMA_UNROLL + k
                src = pl.multiple_of(idx_ref[t] * ROW_TILE, ROW_TILE)
                dst = pl.multiple_of(t * ROW_TILE, ROW_TILE)
                pltpu.make_async_copy(y_hbm.at[pl.ds(src, ROW_TILE)], fbuf.at[slot, pl.ds(dst, ROW_TILE)],
                                      sems.at[slot]).start(priority=k % 2)
            return carry

        lax.fori_loop(0, n // DMA_UNROLL, body, 0)

    slot = i % 2

    @pl.when(i == 0)
    def _():
        issue(dest_ref, 0)

    @pl.when(i + 1 < n_steps)
    def _():
        issue(dest_next_ref, 1 - slot)

    pltpu.make_async_copy(y_hbm.at[pl.ds(0, n * ROW_TILE)], fbuf.at[slot], sems.at[slot]).wait()
    f = jnp.concatenate([fbuf[slot, pl.ds(c, n, stride=ROW_TILE), :] for c in range(d // LANES)], axis=-1)
    x = xmid_ref[...] + mod_ref[0][5:6] * f
    ms = jnp.mean(x * x, axis=-1, keepdims=True)
    o_ref[...] = x * lax.rsqrt(ms + NORM_EPS) * fn_ref[...]


def _final_call(dest, xmid, mod3, fn, y, seq):
    t, d = xmid.shape
    tm = TM_ROWS
    tpb = seq // tm
    n_steps = t // tm
    return pl.pallas_call(
        _final_kernel,
        grid=(n_steps,),
        in_specs=[
            pl.BlockSpec((tm,), lambda i: (i,), memory_space=pltpu.SMEM),
            pl.BlockSpec((tm,), lambda i: (jnp.minimum(i + 1, n_steps - 1),), memory_space=pltpu.SMEM),
            pl.BlockSpec((tm, d), lambda i: (i, 0)),
            pl.BlockSpec((1, 6, d), lambda i: (i // tpb, 0, 0)),
            pl.BlockSpec((1, d), lambda i: (0, 0)),
            pl.BlockSpec(memory_space=pl.ANY),
        ],
        out_specs=pl.BlockSpec((tm, d), lambda i: (i, 0)),
        out_shape=jax.ShapeDtypeStruct((t, d), F32),
        scratch_shapes=[pltpu.VMEM((2, tm * ROW_TILE, LANES), F32), pltpu.SemaphoreType.DMA((2,))],
        compiler_params=pltpu.CompilerParams(dimension_semantics=("arbitrary",), vmem_limit_bytes=VMEM_LIMIT),
        name="final",
    )(dest, dest, xmid, mod3, fn, y)


def _pair_tables():
    ea, eb = [], []
    for g in range(N_GROUPS):
        for a in range(EXPERTS_PER_GROUP):
            for b in range(a + 1, EXPERTS_PER_GROUP):
                ea.append(g * EXPERTS_PER_GROUP + a)
                eb.append(g * EXPERTS_PER_GROUP + b)
    return np.asarray(ea, np.int32), np.asarray(eb, np.int32)


def _routing_plan(info, counts, t):
    cls = info[0].astype(jnp.int32)
    rank = info[1].astype(jnp.int32)
    cnt = counts[:N_CLASSES, 0].astype(jnp.int32)
    padded = (cnt + MOE_BLOCK - 1) // MOE_BLOCK * MOE_BLOCK
    pad_end = jnp.cumsum(padded)
    pad_start = pad_end - padded
    cls_ids = jnp.arange(N_CLASSES, dtype=jnp.int32)
    dest = rank + jnp.sum(jnp.where(cls[:, None] == cls_ids[None, :], pad_start[None, :], 0), axis=1)
    nb = -(-(t // MOE_BLOCK + N_CLASSES) // MOE_BLOCKS_PER_STEP) * MOE_BLOCKS_PER_STEP
    nb_used = pad_end[-1] // MOE_BLOCK
    blk = jnp.arange(nb, dtype=jnp.int32)
    blk_cls = jnp.sum(pad_end[None, :] <= (blk * MOE_BLOCK)[:, None], axis=1)
    blk_cls = jnp.clip(blk_cls, 0, N_CLASSES - 1)
    valid = blk < nb_used
    last_cls = blk_cls[jnp.maximum(nb_used - 1, 0)]
    blk_cls = jnp.where(valid, blk_cls, last_cls).astype(jnp.int32)
    tab_a, tab_b = _pair_tables()
    blk_ea = jnp.asarray(tab_a)[blk_cls]
    blk_eb = jnp.asarray(tab_b)[blk_cls]
    tail_start = jnp.where(cnt > 0, pad_end - MOE_BLOCK, -1).astype(jnp.int32)
    tail_start = jnp.concatenate([tail_start, nb_used[None].astype(jnp.int32)])
    blk_rows = jnp.clip(cnt[blk_cls] - (blk * MOE_BLOCK - pad_start[blk_cls]), 0, MOE_BLOCK)
    blk_rows = jnp.where(valid, blk_rows, 0).astype(jnp.int32)
    return dest, tail_start, blk_ea, blk_eb, blk_rows, nb


def kernel(x, c, ctx, c_ctx, w_mod, b_mod, norm_mix, w_in, rpb, w_att_out, conv_w, conv_b, conv_ln_g, conv_ln_b,
           w_conv_out, w_o, norm_ffn, w_router_group, b_router_group, w_router_expert, b_router_expert,
           w_exp_gate, w_exp_up, w_exp_down, final_norm):
    b, seq, d = x.shape
    n_ctx = ctx.shape[1]
    t = b * seq
    assert w_mod.shape[0] == 1, "single layer"
    assert seq % TM_ROWS == 0 and t % TM_DISPATCH == 0 and seq % TM_CONV == 0 and seq // GRID_W >= WIN_ROWS and n_ctx % LANES == 0 and (b * n_ctx) % 512 == 0
    assert d == 1024

    mod_rows = -(-(b + 1) // SUBLANES) * SUBLANES
    cc = jnp.zeros((mod_rows, d), F32).at[:b].set(c).at[b].set(c_ctx)
    m_all = _mod_call(cc, w_mod[0], b_mod[0][None, :])
    mod_lat = m_all[:b].reshape(b, 6, d)
    mod_ctx = m_all[b:b + 1].reshape(1, 6, d)

    x2 = x.reshape(t, d)
    g_mix = norm_mix[0][None, :]
    w_in_b = w_in[0].astype(BF16)
    wqv_t = jnp.concatenate([w_in_b[:, Q0:K0], w_in_b[:, V0:GLU0]], axis=1).T
    w_rest = jnp.concatenate([w_in_b[:, K0:V0], w_in_b[:, GLU0:]], axis=1)
    cw = jnp.zeros((32, D_CONV), F32).at[:CONV_WIDTH].set(conv_w[0])
    qt, k, vt, u, ga, gb = _proj_call(x2, mod_lat, g_mix, wqv_t, w_rest, seq)
    yn = _conv_call(u, cw, conv_b[0][None, :], conv_ln_g[0][None, :], conv_ln_b[0][None, :], seq)
    kc, vct = _ctx_kv_call(ctx.reshape(b * n_ctx, d), mod_ctx, g_mix, w_in_b[:, K0:V0], w_in_b[:, V0:GLU0].T)

    att = _attn_call(qt, k, vt, kc, vct, _bias_table(rpb[0]), seq, n_ctx)

    wr = jnp.zeros((d, LANES), F32).at[:, :N_GROUPS].set(w_router_group[0])
    wr = wr.at[:, ROUTER_E0:ROUTER_E0 + N_EXPERTS].set(w_router_expert[0])
    wr_hi = wr.astype(BF16)
    wr = jnp.concatenate([wr_hi, (wr - wr_hi.astype(F32)).astype(BF16)], axis=1)
    br = jnp.zeros((1, LANES), F32).at[0, :N_GROUPS].set(b_router_group[0])
    br = br.at[0, ROUTER_E0:ROUTER_E0 + N_EXPERTS].set(b_router_expert[0])
    x_mid, hp, info, counts = _merge_call(
        att, yn, ga, gb, x2, mod_lat, w_att_out[0].astype(BF16), w_conv_out[0].astype(BF16), w_o[0].astype(BF16), norm_ffn[0][None, :],
        wr, br, seq)

    dest, tail_start, blk_ea, blk_eb, blk_rows, nb = _routing_plan(info, counts, t)
    hs = _dispatch_call(dest, tail_start, hp, nb * MOE_BLOCK)
    y = _moe_call(blk_ea, blk_eb, blk_rows, hs,
                  w_exp_gate[0].astype(BF16), w_exp_up[0].astype(BF16), w_exp_down[0].astype(BF16))
    out = _final_call(dest, x_mid, mod_lat, final_norm[None, :], y, seq)
    return out.reshape(b, seq, d)
```

```python
import functools

import numpy as np
import jax
import jax.numpy as jnp
from jax import lax
from jax.experimental import pallas as pl
from jax.experimental.pallas import tpu as pltpu

F32 = jnp.float32
BF16 = jnp.bfloat16
U32 = jnp.uint32

GRID_W = 64
N_HEADS = 8
HEAD_DIM = 64
D_ATT = N_HEADS * HEAD_DIM
WIN_ROWS = 8
WIN_COLS = 16
D_CONV = 512
CONV_WIDTH = 31
N_GROUPS = 4
EXPERTS_PER_GROUP = 8
N_EXPERTS = N_GROUPS * EXPERTS_PER_GROUP
PAIRS_PER_GROUP = EXPERTS_PER_GROUP * (EXPERTS_PER_GROUP - 1) // 2
N_CLASSES = N_GROUPS * PAIRS_PER_GROUP
ROUTER_E0 = 8
NORM_EPS = 1e-6
NEG_INF = -1e30

LANES = 128
SUBLANES = 8
ROW_TILE = SUBLANES
HALO = 16
VMEM_LIMIT = 56 * 1024 * 1024

TM_PROJ = 1024
TM_MERGE = 512
MERGE_PARTS = 1
TM_ROWS = 1024
TM_DISPATCH = 2048
TM_CONV = 1024
MOE_BLOCK = 256
MOE_BLOCKS_PER_STEP = 2
WEIGHT_SLOTS = 4
CONV_CHUNK = 64
DMA_UNROLL = 8

HIGHEST = lax.Precision.HIGHEST


def _norm_mod(x, g, shift, scale):
    ms = jnp.mean(x * x, axis=-1, keepdims=True)
    y = x * lax.rsqrt(ms + NORM_EPS) * g
    return y * (1.0 + scale) + shift


def _sigmoid(x):
    return jax.nn.sigmoid(x)


def _mod_kernel(c_ref, w_ref, b_ref, o_ref):
    c = c_ref[...]
    s = c * _sigmoid(c)
    o_ref[...] = jnp.dot(s, w_ref[...], precision=HIGHEST, preferred_element_type=F32) + b_ref[...]


def _mod_call(cc, w_mod, b_mod):
    rows, d = cc.shape
    n = w_mod.shape[1]
    tn = 1024
    return pl.pallas_call(
        _mod_kernel,
        grid=(n // tn,),
        in_specs=[
            pl.BlockSpec((rows, d), lambda j: (0, 0)),
            pl.BlockSpec((d, tn), lambda j: (0, j)),
            pl.BlockSpec((1, tn), lambda j: (0, j)),
        ],
        out_specs=pl.BlockSpec((rows, tn), lambda j: (0, j)),
        out_shape=jax.ShapeDtypeStruct((rows, n), F32),
        compiler_params=pltpu.CompilerParams(dimension_semantics=("arbitrary",), vmem_limit_bytes=VMEM_LIMIT),
        name="mod",
    )(cc, w_mod, b_mod)


Q0, K0, V0, GLU0 = 0, D_ATT, 2 * D_ATT, 3 * D_ATT
GA0 = GLU0 + 2 * D_CONV


NT_DIMS = (((1,), (1,)), ((), ()))
LOG2_E = 1.4426950408889634
SCORE_SCALE = HEAD_DIM ** -0.5 * LOG2_E


def _store_transposed(res_t, ref, scale=None):
    for j in range(ref.shape[0]):
        blk = res_t[:, j * LANES:(j + 1) * LANES]
        if scale is not None:
            blk = blk * scale
        ref[j] = blk.astype(ref.dtype)


def _conv_kernel(u_ref, up_ref, un_ref, cw_ref, cb_ref, lg_ref, lb_ref, yn_ref, ubuf, shifted,
                 *, tiles_per_seq):
    i = pl.program_id(0)
    tm = u_ref.shape[0]
    first = (i % tiles_per_seq) == 0
    last = (i % tiles_per_seq) == tiles_per_seq - 1
    ubuf[0:HALO, :] = jnp.where(first, 0.0, up_ref[...].astype(F32))
    for r0 in range(0, tm, CONV_CHUNK):
        ubuf[HALO + r0:HALO + r0 + CONV_CHUNK, :] = u_ref[r0:r0 + CONV_CHUNK, :].astype(F32)
    ubuf[HALO + tm:, :] = jnp.where(last, 0.0, un_ref[...].astype(F32))

    span = shifted.shape[1]
    for ph in range(SUBLANES):
        for r0 in range(0, span, CONV_CHUNK):
            n = min(CONV_CHUNK, span - r0)
            shifted[ph, r0:r0 + n, :] = ubuf[ph + r0:ph + r0 + n, :]

    base = HALO - CONV_WIDTH // 2
    inv_c = 1.0 / D_CONV
    for tc in range(tm // CONV_CHUNK):
        accs = []
        for lc in range(D_CONV // LANES):
            ls = slice(lc * LANES, (lc + 1) * LANES)
            acc = jnp.zeros((CONV_CHUNK, LANES), F32)
            for j in range(CONV_WIDTH):
                tiles, ph = divmod(base + j, SUBLANES)
                lo = tc * CONV_CHUNK + tiles * SUBLANES
                acc = acc + shifted[ph, lo:lo + CONV_CHUNK, ls] * cw_ref[j:j + 1, ls]
            accs.append(acc + cb_ref[:, ls])
        mu = sum(jnp.sum(a, axis=-1, keepdims=True) for a in accs) * inv_c
        cen = [a - mu for a in accs]
        var = sum(jnp.sum(c * c, axis=-1, keepdims=True) for c in cen) * inv_c
        rstd = lax.rsqrt(var + NORM_EPS)
        for lc, c in enumerate(cen):
            ls = slice(lc * LANES, (lc + 1) * LANES)
            yn = c * rstd * lg_ref[:, ls] + lb_ref[:, ls]
            yn_ref[tc * CONV_CHUNK:(tc + 1) * CONV_CHUNK, ls] = (yn * _sigmoid(yn)).astype(yn_ref.dtype)


def _conv_call(u, cw, cb, lg, lb, seq):
    t = u.shape[0]
    tm = TM_CONV
    tpb = seq // tm
    hb = tm // HALO
    n_halo = t // HALO
    row = lambda i: (i, 0)
    const = lambda i: (0, 0)
    return pl.pallas_call(
        functools.partial(_conv_kernel, tiles_per_seq=tpb),
        grid=(t // tm,),
        in_specs=[
            pl.BlockSpec((tm, D_CONV), row),
            pl.BlockSpec((HALO, D_CONV), lambda i: (jnp.maximum(i * hb - 1, 0), 0)),
            pl.BlockSpec((HALO, D_CONV), lambda i: (jnp.minimum((i + 1) * hb, n_halo - 1), 0)),
            pl.BlockSpec(cw.shape, const),
            pl.BlockSpec((1, D_CONV), const),
            pl.BlockSpec((1, D_CONV), const),
            pl.BlockSpec((1, D_CONV), const),
        ],
        out_specs=pl.BlockSpec((tm, D_CONV), row),
        out_shape=jax.ShapeDtypeStruct((t, D_CONV), BF16),
        scratch_shapes=[
            pltpu.VMEM((tm + 2 * HALO, D_CONV), F32),
            pltpu.VMEM((SUBLANES, tm + 2 * HALO - SUBLANES, D_CONV), F32),
        ],
        compiler_params=pltpu.CompilerParams(dimension_semantics=("arbitrary",), vmem_limit_bytes=VMEM_LIMIT),
        name="conv",
    )(u, u, u, cw, cb, lg, lb)


def _proj_kernel(x_ref, mod_ref, g_ref, wqv_ref, w_ref, qt_ref, k_ref, vt_ref, u_ref, ga_ref, gb_ref):
    d = x_ref.shape[1]
    mod = mod_ref[0]
    h = _norm_mod(x_ref[...], g_ref[...], mod[0:1], mod[1:2]).astype(BF16)

    qv_t = lax.dot_general(wqv_ref[...], h, NT_DIMS, preferred_element_type=F32)
    _store_transposed(qv_t[:D_ATT], qt_ref, SCORE_SCALE)
    _store_transposed(qv_t[D_ATT:], vt_ref)

    def seg(lo, hi):
        return jnp.dot(h, w_ref[:, lo:hi], preferred_element_type=F32)

    k_ref[...] = seg(0, D_ATT).astype(BF16)
    a = seg(D_ATT, D_ATT + D_CONV)
    g = seg(D_ATT + D_CONV, D_ATT + 2 * D_CONV)
    u_ref[...] = (a * _sigmoid(g)).astype(BF16)
    g0 = D_ATT + 2 * D_CONV
    ga_ref[...] = _sigmoid(seg(g0, g0 + d)).astype(BF16)
    gb_ref[...] = _sigmoid(seg(g0 + d, g0 + 2 * d)).astype(BF16)


def _proj_call(x2, mod3, g, wqv_t, w_rest, seq):
    t, d = x2.shape
    tm = TM_PROJ
    tpb = seq // tm
    nblk = tm // LANES
    outs = [
        jax.ShapeDtypeStruct((t // LANES, D_ATT, LANES), BF16),
        jax.ShapeDtypeStruct((t, D_ATT), BF16),
        jax.ShapeDtypeStruct((t // LANES, D_ATT, LANES), BF16),
        jax.ShapeDtypeStruct((t, D_CONV), BF16),
        jax.ShapeDtypeStruct((t, d), BF16),
        jax.ShapeDtypeStruct((t, d), BF16),
    ]
    row = lambda i: (i, 0)
    blk3 = lambda i: (i, 0, 0)
    return pl.pallas_call(
        _proj_kernel,
        grid=(t // tm,),
        in_specs=[
            pl.BlockSpec((tm, d), row),
            pl.BlockSpec((1, 6, d), lambda i: (i // tpb, 0, 0)),
            pl.BlockSpec((1, d), lambda i: (0, 0)),
            pl.BlockSpec(wqv_t.shape, lambda i: (0, 0), pipeline_mode=pl.Buffered(1)),
            pl.BlockSpec(w_rest.shape, lambda i: (0, 0), pipeline_mode=pl.Buffered(1)),
        ],
        out_specs=[
            pl.BlockSpec((nblk, D_ATT, LANES), blk3),
            pl.BlockSpec((tm, D_ATT), row),
            pl.BlockSpec((nblk, D_ATT, LANES), blk3),
            pl.BlockSpec((tm, D_CONV), row),
            pl.BlockSpec((tm, d), row),
            pl.BlockSpec((tm, d), row),
        ],
        out_shape=outs,
        compiler_params=pltpu.CompilerParams(dimension_semantics=("arbitrary",), vmem_limit_bytes=VMEM_LIMIT),
        name="proj",
    )(x2, mod3, g, wqv_t, w_rest)


def _ctx_kv_kernel(x_ref, mod_ref, g_ref, wk_ref, wvt_ref, k_ref, vt_ref):
    mod = mod_ref[0]
    h = _norm_mod(x_ref[...], g_ref[...], mod[0:1], mod[1:2]).astype(BF16)
    k_ref[...] = jnp.dot(h, wk_ref[...], preferred_element_type=F32).astype(BF16)
    _store_transposed(lax.dot_general(wvt_ref[...], h, NT_DIMS, preferred_element_type=F32), vt_ref)


def _ctx_kv_call(c2, mod3, g, w_k, w_vt):
    t, d = c2.shape
    tm = 512
    nblk = tm // LANES
    row = lambda i: (i, 0)
    return pl.pallas_call(
        _ctx_kv_kernel,
        grid=(t // tm,),
        in_specs=[
            pl.BlockSpec((tm, d), row),
            pl.BlockSpec((1, 6, d), lambda i: (0, 0, 0)),
            pl.BlockSpec((1, d), lambda i: (0, 0)),
            pl.BlockSpec(w_k.shape, lambda i: (0, 0)),
            pl.BlockSpec(w_vt.shape, lambda i: (0, 0)),
        ],
        out_specs=[pl.BlockSpec((tm, D_ATT), row), pl.BlockSpec((nblk, D_ATT, LANES), lambda i: (i, 0, 0))],
        out_shape=[jax.ShapeDtypeStruct((t, D_ATT), BF16), jax.ShapeDtypeStruct((t // LANES, D_ATT, LANES), BF16)],
        compiler_params=pltpu.CompilerParams(dimension_semantics=("arbitrary",), vmem_limit_bytes=VMEM_LIMIT),
        name="ctx_kv",
    )(c2, mod3, g, w_k, w_vt)


HEADS_PER_GROUP = 4
GROUP_W = HEADS_PER_GROUP * HEAD_DIM
N_HEAD_GROUPS = N_HEADS // HEADS_PER_GROUP


def _attn_kernel(qt_ref, k_ref, vt_ref, kc_ref, vct_ref, bias_ref, o_ref, vboth):
    nblk = qt_ref.shape[0]
    rows = 2 * nblk
    n_loc = WIN_ROWS * GRID_W
    n_ctx_blk = vct_ref.shape[0]
    half = GRID_W

    for j in range(nblk):
        vboth[0, j] = vt_ref[j]
    for j in range(nblk - 1):
        vboth[1, j] = jnp.concatenate([vt_ref[j][:, half:], vt_ref[j + 1][:, :half]], axis=1)
    vboth[1, nblk - 1] = jnp.zeros_like(vt_ref[0])

    lane = lax.broadcasted_iota(jnp.int32, (D_ATT, LANES), 1)
    low = lane < half
    rblk = lax.broadcasted_iota(jnp.int32, (GROUP_W, GROUP_W), 0) // HEAD_DIM
    cblk = lax.broadcasted_iota(jnp.int32, (GROUP_W, GROUP_W), 1) // HEAD_DIM
    diag = rblk == cblk
    low64 = lax.broadcasted_iota(jnp.int32, (HEAD_DIM, LANES), 1) < half
    kc = kc_ref[...]

    def one_row(row, tiled, side):
        r_start = jnp.clip(row - WIN_ROWS // 2, 0, rows - WIN_ROWS)
        b_off = pl.multiple_of((WIN_ROWS - 1 - (row - r_start)) * GRID_W, GRID_W)
        kw = k_ref[pl.ds(pl.multiple_of(r_start * GRID_W, GRID_W), n_loc), :]
        vwin = vboth[r_start % 2, pl.ds(r_start // 2, n_loc // LANES)]
        parts = []
        for g in range(N_HEAD_GROUPS):
            fs = slice(g * GROUP_W, (g + 1) * GROUP_W)
            tg = tiled[fs, :]
            w = jnp.where(diag, jnp.concatenate([tg, tg], axis=1), jnp.zeros((), BF16))
            s_loc = jnp.dot(kw[:, fs], w, preferred_element_type=F32) + bias_ref[g, pl.ds(b_off, n_loc), :]
            s_ctx = jnp.dot(kc[:, fs], w, preferred_element_type=F32)
            m = jnp.maximum(jnp.max(s_loc, axis=0, keepdims=True), jnp.max(s_ctx, axis=0, keepdims=True))
            p_loc = jnp.exp2(s_loc - m)
            p_ctx = jnp.exp2(s_ctx - m)
            l = jnp.sum(p_loc, axis=0, keepdims=True) + jnp.sum(p_ctx, axis=0, keepdims=True)
            p = jnp.concatenate([p_loc.astype(BF16), p_ctx.astype(BF16)], axis=0)
            vt_g = jnp.concatenate([vwin[j][fs, :] for j in range(n_loc // LANES)]
                                   + [vct_ref[j][fs, :] for j in range(n_ctx_blk)], axis=1)
            o_t = jnp.dot(vt_g, p, preferred_element_type=F32) * (1.0 / l)
            for j in range(HEADS_PER_GROUP):
                blk = o_t[j * HEAD_DIM:(j + 1) * HEAD_DIM, (j // 2) * LANES:(j // 2 + 1) * LANES]
                if j % 2 != side:
                    blk = pltpu.roll(blk, half, axis=1)
                parts.append(blk)
        return parts

    def body(i, carry):
        xq = qt_ref[i]
        rolled = jnp.concatenate([xq[:, half:], xq[:, :half]], axis=1)
        parts_a = one_row(2 * i, jnp.where(low, xq, rolled), 0)
        parts_b = one_row(2 * i + 1, jnp.where(low, rolled, xq), 1)
        out = jnp.concatenate([jnp.where(low64, a, b) for a, b in zip(parts_a, parts_b)], axis=0)
        o_ref[i] = out.astype(o_ref.dtype)
        return carry

    lax.fori_loop(0, nblk, body, 0, unroll=8)


def _attn_call(qt, k, vt, kc, vct, bias, seq, n_ctx):
    t = k.shape[0]
    b = t // seq
    nblk = seq // LANES
    lat3 = pl.BlockSpec((nblk, D_ATT, LANES), lambda i: (i, 0, 0))
    return pl.pallas_call(
        _attn_kernel,
        grid=(b,),
        in_specs=[lat3,
                  pl.BlockSpec((seq, D_ATT), lambda i: (i, 0)),
                  lat3,
                  pl.BlockSpec((n_ctx, D_ATT), lambda i: (i, 0)),
                  pl.BlockSpec((n_ctx // LANES, D_ATT, LANES), lambda i: (i, 0, 0)),
                  pl.BlockSpec(bias.shape, lambda i: (0, 0, 0))],
        out_specs=lat3,
        out_shape=jax.ShapeDtypeStruct((t // LANES, D_ATT, LANES), BF16),
        scratch_shapes=[pltpu.VMEM((2, nblk, D_ATT, LANES), BF16)],
        compiler_params=pltpu.CompilerParams(dimension_semantics=("arbitrary",), vmem_limit_bytes=VMEM_LIMIT),
        name="attn",
    )(qt, k, vt, kc, vct, bias)


def _bias_table(rpb):
    cq = np.arange(GRID_W)[:, None]
    ck = np.arange(GRID_W)[None, :]
    c_start = np.clip(cq - WIN_COLS // 2, 0, GRID_W - WIN_COLS)
    col_mask = (ck >= c_start) & (ck < c_start + WIN_COLS)
    n_dc = rpb.shape[2]
    lead = GRID_W - WIN_COLS
    w = jnp.pad(rpb, ((0, 0), (0, 0), (lead, 2 * GRID_W - lead - n_dc)))
    skew = jnp.tile(w, (1, 1, GRID_W + 1))[:, :, :GRID_W * (2 * GRID_W + 1)]
    tab = skew.reshape(rpb.shape[0], rpb.shape[1], GRID_W, 2 * GRID_W + 1)[:, :, ::-1, :GRID_W]
    tab = jnp.where(col_mask[None, None], tab, NEG_INF)
    n_dr = tab.shape[1]
    tab = tab.reshape(N_HEAD_GROUPS, HEADS_PER_GROUP, n_dr, GRID_W, GRID_W)
    tab = tab.transpose(0, 2, 4, 1, 3)
    return (tab.reshape(N_HEAD_GROUPS, n_dr * GRID_W, GROUP_W) * LOG2_E).astype(F32)


def _merge_rows(r0, n, att_ref, yn_ref, ga_ref, gb_ref, x_ref, mod,
                wa_ref, wc_ref, wo_ref, nf_ref, wr_ref, br_ref, xmid_ref, hp_ref):
    d = x_ref.shape[1]
    rs = slice(r0, r0 + n)
    y_conv = jnp.dot(yn_ref[rs, :], wc_ref[...], preferred_element_type=F32)

    att = jnp.concatenate([att_ref[j].astype(F32).T for j in range(r0 // LANES, (r0 + n) // LANES)], axis=0)
    y_att = jnp.dot(att.astype(BF16), wa_ref[...], preferred_element_type=F32)
    mix = ga_ref[rs, :].astype(F32) * y_att + gb_ref[rs, :].astype(F32) * y_conv
    y = jnp.dot(mix.astype(BF16), wo_ref[...], preferred_element_type=F32)
    x_mid = x_ref[rs, :] + mod[2:3] * y
    xmid_ref[rs, :] = x_mid
    h2 = _norm_mod(x_mid, nf_ref[...], mod[3:4], mod[4:5])

    h2_hi = h2.astype(BF16)
    h2_lo = (h2 - h2_hi.astype(F32)).astype(BF16)
    t_hi = jnp.dot(h2_hi, wr_ref[...], preferred_element_type=F32)
    t_lo = jnp.dot(h2_lo, wr_ref[:, :LANES], preferred_element_type=F32)
    logits = t_hi[:, :LANES] + t_hi[:, LANES:] + t_lo + br_ref[...]
    lt = logits.T
    sub = lax.broadcasted_iota(jnp.int32, (SUBLANES, n), 0).astype(F32)
    big = float(LANES)
    is_g = sub < N_GROUPS
    gl = jnp.where(is_g, lt[0:SUBLANES], -jnp.inf)
    gmax = jnp.max(gl, axis=0, keepdims=True)
    g_idx = jnp.min(jnp.where(gl == gmax, sub, big), axis=0, keepdims=True)
    p_group = 1.0 / jnp.sum(jnp.where(is_g, jnp.exp(gl - gmax), 0.0), axis=0, keepdims=True)
    el = lt[ROUTER_E0:ROUTER_E0 + EXPERTS_PER_GROUP]
    for g in range(1, N_GROUPS):
        lo_g = ROUTER_E0 + g * EXPERTS_PER_GROUP
        el = jnp.where(g_idx == float(g), lt[lo_g:lo_g + EXPERTS_PER_GROUP], el)
    v1 = jnp.max(el, axis=0, keepdims=True)
    j1 = jnp.min(jnp.where(el == v1, sub, big), axis=0, keepdims=True)
    el2 = jnp.where(sub == j1, -jnp.inf, el)
    v2 = jnp.max(el2, axis=0, keepdims=True)
    j2 = jnp.min(jnp.where(el2 == v2, sub, big), axis=0, keepdims=True)
    e2 = jnp.exp(v2 - v1)
    gate1 = p_group / (1.0 + e2)
    gate2 = p_group * e2 / (1.0 + e2)
    ja = jnp.minimum(j1, j2)
    jb = jnp.maximum(j1, j2)
    gate_a = jnp.where(j1 < j2, gate1, gate2)
    gate_b = jnp.where(j1 < j2, gate2, gate1)
    pair = ja * (2 * EXPERTS_PER_GROUP - 1 - ja) * 0.5 + (jb - ja - 1.0)
    cls = g_idx * PAIRS_PER_GROUP + pair
    gate_rows = jnp.where(sub == 0.0, gate_a, jnp.where(sub == 1.0, gate_b, 0.0))
    gates = jnp.concatenate([gate_rows, jnp.zeros((LANES - SUBLANES, n), F32)], axis=0).T

    half = d // 2
    hi = lax.bitcast_convert_type(h2[:, :half].astype(BF16).astype(F32), U32)
    lo = lax.bitcast_convert_type(h2[:, half:].astype(BF16).astype(F32), U32)
    words = hi | (lo >> 16)
    n_words = half // LANES
    base = r0 * ROW_TILE
    for c in range(n_words):
        hp_ref[pl.ds(base + c, n, stride=ROW_TILE), :] = words[:, c * LANES:(c + 1) * LANES]
    hp_ref[pl.ds(base + n_words, n, stride=ROW_TILE), :] = lax.bitcast_convert_type(gates, U32)
    zero = jnp.zeros((n, LANES), U32)
    for c in range(n_words + 1, ROW_TILE):
        hp_ref[pl.ds(base + c, n, stride=ROW_TILE), :] = zero
    return cls


def _merge_kernel(att_ref, yn_ref, ga_ref, gb_ref, x_ref, mod_ref,
                  wa_ref, wc_ref, wo_ref, nf_ref, wr_ref, br_ref, tri_ref,
                  xmid_ref, hp_ref, info_ref, cnt_ref, run_ref):
    i = pl.program_id(0)
    tm = x_ref.shape[0]

    @pl.when(i == 0)
    def _():
        run_ref[...] = jnp.zeros_like(run_ref)

    mod = mod_ref[0]
    n = tm // MERGE_PARTS
    cls = jnp.concatenate(
        [_merge_rows(p * n, n, att_ref, yn_ref, ga_ref, gb_ref, x_ref, mod, wa_ref, wc_ref, wo_ref, nf_ref,
                     wr_ref, br_ref, xmid_ref, hp_ref) for p in range(MERGE_PARTS)], axis=1)

    crow = lax.broadcasted_iota(jnp.int32, (LANES, tm), 0).astype(F32)
    oh_f = jnp.where(crow == cls, 1.0, 0.0)
    before = jnp.dot(oh_f.astype(BF16), tri_ref[...], preferred_element_type=F32)
    run = run_ref[:, 0:1]
    rank = jnp.sum(oh_f * (before + run), axis=0, keepdims=True)
    new_run = run + jnp.sum(oh_f, axis=1, keepdims=True)
    run_ref[...] = jnp.broadcast_to(new_run, run_ref.shape)
    cnt_ref[...] = jnp.broadcast_to(new_run, cnt_ref.shape)

    sub = lax.broadcasted_iota(jnp.int32, (SUBLANES, tm), 0)
    info_ref[...] = jnp.where(sub == 0, cls, jnp.where(sub == 1, rank, 0.0))


def _merge_call(att, yn, ga, gb, x2, mod3, wa, wc, wo, nf, wr, br, seq):
    t, d = x2.shape
    tm = TM_MERGE
    tpb = seq // tm
    tri = jnp.asarray(np.triu(np.ones((tm, tm), np.float32), 1), BF16)
    row = lambda i: (i, 0)
    const = lambda i: (0, 0)
    return pl.pallas_call(
        _merge_kernel,
        grid=(t // tm,),
        in_specs=[
            pl.BlockSpec((tm // LANES, D_ATT, LANES), lambda i: (i, 0, 0)),
            pl.BlockSpec((tm, D_CONV), row),
            pl.BlockSpec((tm, d), row),
            pl.BlockSpec((tm, d), row),
            pl.BlockSpec((tm, d), row),
            pl.BlockSpec((1, 6, d), lambda i: (i // tpb, 0, 0)),
            pl.BlockSpec(wa.shape, const),
            pl.BlockSpec(wc.shape, const),
            pl.BlockSpec(wo.shape, const),
            pl.BlockSpec((1, d), const),
            pl.BlockSpec(wr.shape, const),
            pl.BlockSpec((1, LANES), const),
            pl.BlockSpec((tm, tm), const),
        ],
        out_specs=[
            pl.BlockSpec((tm, d), row),
            pl.BlockSpec((tm * ROW_TILE, LANES), row),
            pl.BlockSpec((SUBLANES, tm), lambda i: (0, i)),
            pl.BlockSpec((LANES, LANES), const),
        ],
        out_shape=[
            jax.ShapeDtypeStruct((t, d), F32),
            jax.ShapeDtypeStruct((t * ROW_TILE, LANES), U32),
            jax.ShapeDtypeStruct((SUBLANES, t), F32),
            jax.ShapeDtypeStruct((LANES, LANES), F32),
        ],
        scratch_shapes=[pltpu.VMEM((LANES, LANES), F32)],
        compiler_params=pltpu.CompilerParams(dimension_semantics=("arbitrary",), vmem_limit_bytes=VMEM_LIMIT),
        name="merge",
    )(att, yn, ga, gb, x2, mod3, wa, wc, wo, nf, wr, br, tri)


def _dispatch_kernel(dest_ref, tail_ref, hp_ref, hs_hbm, zbuf, sem, zsem):
    n = dest_ref.shape[0]
    blk_rows = zbuf.shape[0]

    @pl.when(pl.program_id(0) == 0)
    def _():
        zbuf[...] = jnp.zeros_like(zbuf)

        def tail_copy(c):
            start = pl.multiple_of(tail_ref[c] * ROW_TILE, ROW_TILE)
            return pltpu.make_async_copy(zbuf, hs_hbm.at[pl.ds(start, blk_rows)], zsem)

        def start_one(c, carry):
            @pl.when(tail_ref[c] >= 0)
            def _():
                tail_copy(c).start()
            return carry

        def wait_one(c, carry):
            @pl.when(tail_ref[c] >= 0)
            def _():
                tail_copy(c).wait()
            return carry

        lax.fori_loop(0, N_CLASSES, start_one, 0)
        lax.fori_loop(0, N_CLASSES, wait_one, 0)

        def unused_copy(b):
            start = pl.multiple_of(b * blk_rows, blk_rows)
            return pltpu.make_async_copy(zbuf, hs_hbm.at[pl.ds(start, blk_rows)], zsem)

        def start_unused(b, carry):
            unused_copy(b).start()
            return carry

        def wait_unused(b, carry):
            unused_copy(b).wait()
            return carry

        n_blocks = hs_hbm.shape[0] // blk_rows
        lax.fori_loop(tail_ref[N_CLASSES], n_blocks, start_unused, 0)
        lax.fori_loop(tail_ref[N_CLASSES], n_blocks, wait_unused, 0)

    def body(o, carry):
        for k in range(DMA_UNROLL):
            t = o * DMA_UNROLL + k
            src = pl.multiple_of(t * ROW_TILE, ROW_TILE)
            dst = pl.multiple_of(dest_ref[t] * ROW_TILE, ROW_TILE)
            pltpu.make_async_copy(hp_ref.at[pl.ds(src, ROW_TILE)], hs_hbm.at[pl.ds(dst, ROW_TILE)],
                                  sem).start(priority=k % 2)
        return carry

    lax.fori_loop(0, n // DMA_UNROLL, body, 0)
    pltpu.make_async_copy(hp_ref, hs_hbm.at[pl.ds(0, n * ROW_TILE)], sem).wait()


def _dispatch_call(dest, tail_start, hp, p_rows):
    t = dest.shape[0]
    return pl.pallas_call(
        _dispatch_kernel,
        grid=(t // TM_DISPATCH,),
        in_specs=[
            pl.BlockSpec((TM_DISPATCH,), lambda i: (i,), memory_space=pltpu.SMEM),
            pl.BlockSpec(memory_space=pltpu.SMEM),
            pl.BlockSpec((TM_DISPATCH * ROW_TILE, LANES), lambda i: (i, 0)),
        ],
        out_specs=pl.BlockSpec(memory_space=pl.ANY),
        out_shape=jax.ShapeDtypeStruct((p_rows * ROW_TILE, LANES), U32),
        scratch_shapes=[pltpu.VMEM((MOE_BLOCK * ROW_TILE, LANES), U32),
                        pltpu.SemaphoreType.DMA(()), pltpu.SemaphoreType.DMA(())],
        compiler_params=pltpu.CompilerParams(dimension_semantics=("arbitrary",), has_side_effects=True,
                                             vmem_limit_bytes=VMEM_LIMIT),
        name="dispatch",
    )(dest, tail_start, hp)


def _moe_kernel(rows_ref, seg_ref, first_ref, fetch_ref, pre_ref, hs_ref, wg_hbm, wu_hbm, wd_hbm, y_ref,
                wg_buf, wu_buf, wd_buf, sems):
    base = pl.program_id(0) * MOE_BLOCKS_PER_STEP
    rows = hs_ref.shape[0] // MOE_BLOCKS_PER_STEP
    blk = rows // ROW_TILE
    d = wg_buf.shape[2]
    n_words = d // 2 // LANES

    def copies(side, expert, slot):
        return [pltpu.make_async_copy(src.at[expert], buf.at[side, slot], sems.at[side, slot])
                for src, buf in ((wg_hbm, wg_buf), (wu_hbm, wu_buf), (wd_hbm, wd_buf))]

    for sub in range(MOE_BLOCKS_PER_STEP):
        b = base + sub
        for side in range(2):
            @pl.when(b == 0)
            def _():
                for s in range(2):
                    @pl.when(pre_ref[side, s] >= 0)
                    def _():
                        for cp in copies(side, pre_ref[side, s], s):
                            cp.start()

            @pl.when(fetch_ref[side, b] >= 0)
            def _():
                for cp in copies(side, fetch_ref[side, b], (seg_ref[side, b] + 2) % WEIGHT_SLOTS):
                    cp.start()

        for side in range(2):
            @pl.when(first_ref[side, b] == 1)
            def _():
                for cp in copies(side, 0, seg_ref[side, b] % WEIGHT_SLOTS):
                    cp.wait()

    def run(sub):
        b = base + sub
        off = sub * rows
        slot_a = seg_ref[0, b] % WEIGHT_SLOTS
        slot_b = seg_ref[1, b] % WEIGHT_SLOTS
        his, los = [], []
        for c in range(n_words):
            w = hs_ref[pl.ds(off + c, blk, stride=ROW_TILE), :]
            his.append(lax.bitcast_convert_type(w & jnp.uint32(0xFFFF0000), F32).astype(BF16))
            los.append(lax.bitcast_convert_type(w << 16, F32).astype(BF16))
        x = jnp.concatenate(his + los, axis=-1)
        gates = lax.bitcast_convert_type(hs_ref[pl.ds(off + n_words, blk, stride=ROW_TILE), :], F32)

        def mlp(side, slot):
            g = jnp.dot(x, wg_buf[side, slot], preferred_element_type=F32)
            u = jnp.dot(x, wu_buf[side, slot], preferred_element_type=F32)
            a = (g * _sigmoid(g) * u).astype(BF16)
            return jnp.dot(a, wd_buf[side, slot], preferred_element_type=F32)

        y = gates[:, 0:1] * mlp(0, slot_a) + gates[:, 1:2] * mlp(1, slot_b)
        for c in range(d // LANES):
            y_ref[pl.ds(off + c, blk, stride=ROW_TILE), :] = y[:, c * LANES:(c + 1) * LANES]

    used = [rows_ref[base + sub] > 0 for sub in range(MOE_BLOCKS_PER_STEP)]
    all_used = functools.reduce(jnp.logical_and, used)

    @pl.when(all_used)
    def _():
        for sub in range(MOE_BLOCKS_PER_STEP):
            run(sub)

    for sub in range(MOE_BLOCKS_PER_STEP):
        @pl.when(jnp.logical_not(all_used) & used[sub])
        def _():
            run(sub)

        @pl.when(jnp.logical_not(used[sub]))
        def _():
            y_ref[sub * rows:(sub + 1) * rows, :] = jnp.zeros((rows, LANES), y_ref.dtype)


def _weight_schedule(blk_e, blk_rows):
    nb = blk_e.shape[0]
    valid = blk_rows > 0
    prev = jnp.concatenate([jnp.full((1,), -1, jnp.int32), blk_e[:-1]])
    first = valid & (blk_e != prev)
    seg = jnp.cumsum(first.astype(jnp.int32)) - 1
    n_seg = jnp.sum(first.astype(jnp.int32))
    seg_e = jnp.full((nb + 3,), -1, jnp.int32).at[jnp.where(first, seg, nb + 2)].set(blk_e)
    seg_e = seg_e.at[nb + 2].set(-1)
    fetch = jnp.where(first & (seg + 2 < n_seg), seg_e[jnp.clip(seg + 2, 0, nb + 1)], -1)
    return jnp.maximum(seg, 0), first.astype(jnp.int32), fetch.astype(jnp.int32), seg_e[:2]


def _moe_call(blk_ea, blk_eb, blk_rows, hs, wg, wu, wd):
    nb = blk_ea.shape[0]
    d = wg.shape[1]
    de = wg.shape[2]
    rows = MOE_BLOCKS_PER_STEP * MOE_BLOCK * ROW_TILE
    sched = [_weight_schedule(e, blk_rows) for e in (blk_ea, blk_eb)]
    seg, first, fetch, pre = (jnp.stack([s[k] for s in sched]) for k in range(4))
    tok = pl.BlockSpec((rows, LANES), lambda i, *_: (i, 0))
    hbm = pl.BlockSpec(memory_space=pl.ANY)
    grid_spec = pltpu.PrefetchScalarGridSpec(
        num_scalar_prefetch=5,
        grid=(nb // MOE_BLOCKS_PER_STEP,),
        in_specs=[tok, hbm, hbm, hbm],
        out_specs=tok,
        scratch_shapes=[
            pltpu.VMEM((2, WEIGHT_SLOTS, d, de), BF16),
            pltpu.VMEM((2, WEIGHT_SLOTS, d, de), BF16),
            pltpu.VMEM((2, WEIGHT_SLOTS, de, d), BF16),
            pltpu.SemaphoreType.DMA((2, WEIGHT_SLOTS)),
        ],
    )
    return pl.pallas_call(
        _moe_kernel,
        grid_spec=grid_spec,
        out_shape=jax.ShapeDtypeStruct((nb * MOE_BLOCK * ROW_TILE, LANES), F32),
        compiler_params=pltpu.CompilerParams(dimension_semantics=("arbitrary",), vmem_limit_bytes=VMEM_LIMIT),
        name="moe",
    )(blk_rows, seg, first, fetch, pre, hs, wg, wu, wd)


def _final_kernel(dest_ref, dest_next_ref, xmid_ref, mod_ref, fn_ref, y_hbm, o_ref, fbuf, sems):
    i = pl.program_id(0)
    n_steps = pl.num_programs(0)
    n = dest_ref.shape[0]
    d = xmid_ref.shape[1]

    def issue(idx_ref, slot):
        def body(o, carry):
            for k in range(DMA_UNROLL):
                t = o * DMA_UNROLL + k
                src = pl.multiple_of(idx_ref[t] * ROW_TILE, ROW_TILE)
                dst = pl.multiple_of(t * ROW_TILE, ROW_TILE)
                pltpu.make_async_copy(y_hbm.at[pl.ds(src, ROW_TILE)], fbuf.at[slot, pl.ds(dst, ROW_TILE)],
                                      sems.at[slot]).start(priority=k % 2)
            return carry

        lax.fori_loop(0, n // DMA_UNROLL, body, 0)

    slot = i % 2

    @pl.when(i == 0)
    def _():
        issue(dest_ref, 0)

    @pl.when(i + 1 < n_steps)
    def _():
        issue(dest_next_ref, 1 - slot)

    pltpu.make_async_copy(y_hbm.at[pl.ds(0, n * ROW_TILE)], fbuf.at[slot], sems.at[slot]).wait()
    f = jnp.concatenate([fbuf[slot, pl.ds(c, n, stride=ROW_TILE), :] for c in range(d // LANES)], axis=-1)
    x = xmid_ref[...] + mod_ref[0][5:6] * f
    ms = jnp.mean(x * x, axis=-1, keepdims=True)
    o_ref[...] = x * lax.rsqrt(ms + NORM_EPS) * fn_ref[...]


def _final_call(dest, xmid, mod3, fn, y, seq):
    t, d = xmid.shape
    tm = TM_ROWS
    tpb = seq // tm
    n_steps = t // tm
    return pl.pallas_call(
        _final_kernel,
        grid=(n_steps,),
        in_specs=[
            pl.BlockSpec((tm,), lambda i: (i,), memory_space=pltpu.SMEM),
            pl.BlockSpec((tm,), lambda i: (jnp.minimum(i + 1, n_steps - 1),), memory_space=pltpu.SMEM),
            pl.BlockSpec((tm, d), lambda i: (i, 0)),
            pl.BlockSpec((1, 6, d), lambda i: (i // tpb, 0, 0)),
            pl.BlockSpec((1, d), lambda i: (0, 0)),
            pl.BlockSpec(memory_space=pl.ANY),
        ],
        out_specs=pl.BlockSpec((tm, d), lambda i: (i, 0)),
        out_shape=jax.ShapeDtypeStruct((t, d), F32),
        scratch_shapes=[pltpu.VMEM((2, tm * ROW_TILE, LANES), F32), pltpu.SemaphoreType.DMA((2,))],
        compiler_params=pltpu.CompilerParams(dimension_semantics=("arbitrary",), vmem_limit_bytes=VMEM_LIMIT),
        name="final",
    )(dest, dest, xmid, mod3, fn, y)


def _pair_tables():
    ea, eb = [], []
    for g in range(N_GROUPS):
        for a in range(EXPERTS_PER_GROUP):
            for b in range(a + 1, EXPERTS_PER_GROUP):
                ea.append(g * EXPERTS_PER_GROUP + a)
                eb.append(g * EXPERTS_PER_GROUP + b)
    return np.asarray(ea, np.int32), np.asarray(eb, np.int32)


def _routing_plan(info, counts, t):
    cls = info[0].astype(jnp.int32)
    rank = info[1].astype(jnp.int32)
    cnt = counts[:N_CLASSES, 0].astype(jnp.int32)
    padded = (cnt + MOE_BLOCK - 1) // MOE_BLOCK * MOE_BLOCK
    pad_end = jnp.cumsum(padded)
    pad_start = pad_end - padded
    cls_ids = jnp.arange(N_CLASSES, dtype=jnp.int32)
    dest = rank + jnp.sum(jnp.where(cls[:, None] == cls_ids[None, :], pad_start[None, :], 0), axis=1)
    nb = -(-(t // MOE_BLOCK + N_CLASSES) // MOE_BLOCKS_PER_STEP) * MOE_BLOCKS_PER_STEP
    nb_used = pad_end[-1] // MOE_BLOCK
    blk = jnp.arange(nb, dtype=jnp.int32)
    blk_cls = jnp.sum(pad_end[None, :] <= (blk * MOE_BLOCK)[:, None], axis=1)
    blk_cls = jnp.clip(blk_cls, 0, N_CLASSES - 1)
    valid = blk < nb_used
    last_cls = blk_cls[jnp.maximum(nb_used - 1, 0)]
    blk_cls = jnp.where(valid, blk_cls, last_cls).astype(jnp.int32)
    tab_a, tab_b = _pair_tables()
    blk_ea = jnp.asarray(tab_a)[blk_cls]
    blk_eb = jnp.asarray(tab_b)[blk_cls]
    tail_start = jnp.where(cnt > 0, pad_end - MOE_BLOCK, -1).astype(jnp.int32)
    tail_start = jnp.concatenate([tail_start, nb_used[None].astype(jnp.int32)])
    blk_rows = jnp.clip(cnt[blk_cls] - (blk * MOE_BLOCK - pad_start[blk_cls]), 0, MOE_BLOCK)
    blk_rows = jnp.where(valid, blk_rows, 0).astype(jnp.int32)
    return dest, tail_start, blk_ea, blk_eb, blk_rows, nb


def kernel(x, c, ctx, c_ctx, w_mod, b_mod, norm_mix, w_in, rpb, w_att_out, conv_w, conv_b, conv_ln_g, conv_ln_b,
           w_conv_out, w_o, norm_ffn, w_router_group, b_router_group, w_router_expert, b_router_expert,
           w_exp_gate, w_exp_up, w_exp_down, final_norm):
    b, seq, d = x.shape
    n_ctx = ctx.shape[1]
    t = b * seq
    assert w_mod.shape[0] == 1, "single layer"
    assert seq % TM_ROWS == 0 and t % TM_DISPATCH == 0 and seq % TM_CONV == 0 and seq // GRID_W >= WIN_ROWS and n_ctx % LANES == 0 and (b * n_ctx) % 512 == 0
    assert d == 1024

    mod_rows = -(-(b + 1) // SUBLANES) * SUBLANES
    cc = jnp.zeros((mod_rows, d), F32).at[:b].set(c).at[b].set(c_ctx)
    m_all = _mod_call(cc, w_mod[0], b_mod[0][None, :])
    mod_lat = m_all[:b].reshape(b, 6, d)
    mod_ctx = m_all[b:b + 1].reshape(1, 6, d)

    x2 = x.reshape(t, d)
    g_mix = norm_mix[0][None, :]
    w_in_b = w_in[0].astype(BF16)
    wqv_t = jnp.concatenate([w_in_b[:, Q0:K0], w_in_b[:, V0:GLU0]], axis=1).T
    w_rest = jnp.concatenate([w_in_b[:, K0:V0], w_in_b[:, GLU0:]], axis=1)
    cw = jnp.zeros((32, D_CONV), F32).at[:CONV_WIDTH].set(conv_w[0])
    qt, k, vt, u, ga, gb = _proj_call(x2, mod_lat, g_mix, wqv_t, w_rest, seq)
    yn = _conv_call(u, cw, conv_b[0][None, :], conv_ln_g[0][None, :], conv_ln_b[0][None, :], seq)
    kc, vct = _ctx_kv_call(ctx.reshape(b * n_ctx, d), mod_ctx, g_mix, w_in_b[:, K0:V0], w_in_b[:, V0:GLU0].T)

    att = _attn_call(qt, k, vt, kc, vct, _bias_table(rpb[0]), seq, n_ctx)

    wr = jnp.zeros((d, LANES), F32).at[:, :N_GROUPS].set(w_router_group[0])
    wr = wr.at[:, ROUTER_E0:ROUTER_E0 + N_EXPERTS].set(w_router_expert[0])
    wr_hi = wr.astype(BF16)
    wr = jnp.concatenate([wr_hi, (wr - wr_hi.astype(F32)).astype(BF16)], axis=1)
    br = jnp.zeros((1, LANES), F32).at[0, :N_GROUPS].set(b_router_group[0])
    br = br.at[0, ROUTER_E0:ROUTER_E0 + N_EXPERTS].set(b_router_expert[0])
    x_mid, hp, info, counts = _merge_call(
        att, yn, ga, gb, x2, mod_lat, w_att_out[0].astype(BF16), w_conv_out[0].astype(BF16), w_o[0].astype(BF16), norm_ffn[0][None, :],
        wr, br, seq)

    dest, tail_start, blk_ea, blk_eb, blk_rows, nb = _routing_plan(info, counts, t)
    hs = _dispatch_call(dest, tail_start, hp, nb * MOE_BLOCK)
    y = _moe_call(blk_ea, blk_eb, blk_rows, hs,
                  w_exp_gate[0].astype(BF16), w_exp_up[0].astype(BF16), w_exp_down[0].astype(BF16))
    out = _final_call(dest, x_mid, mod_lat, final_norm[None, :], y, seq)
    return out.reshape(b, seq, d)
```

```python
import functools

import numpy as np
import jax
import jax.numpy as jnp
from jax import lax
from jax.experimental import pallas as pl
from jax.experimental.pallas import tpu as pltpu

F32 = jnp.float32
BF16 = jnp.bfloat16
U32 = jnp.uint32

GRID_W = 64
N_HEADS = 8
HEAD_DIM = 64
D_ATT = N_HEADS * HEAD_DIM
WIN_ROWS = 8
WIN_COLS = 16
D_CONV = 512
CONV_WIDTH = 31
N_GROUPS = 4
EXPERTS_PER_GROUP = 8
N_EXPERTS = N_GROUPS * EXPERTS_PER_GROUP
PAIRS_PER_GROUP = EXPERTS_PER_GROUP * (EXPERTS_PER_GROUP - 1) // 2
N_CLASSES = N_GROUPS * PAIRS_PER_GROUP
ROUTER_E0 = 8
NORM_EPS = 1e-6
NEG_INF = -1e30

LANES = 128
SUBLANES = 8
ROW_TILE = SUBLANES
HALO = 16
VMEM_LIMIT = 56 * 1024 * 1024

TM_PROJ = 1024
TM_MERGE = 512
MERGE_PARTS = 1
TM_ROWS = 1024
TM_DISPATCH = 2048
TM_CONV = 1024
MOE_BLOCK = 256
MOE_BLOCKS_PER_STEP = 2
WEIGHT_SLOTS = MOE_BLOCKS_PER_STEP + 2
CONV_CHUNK = 64
DMA_UNROLL = 8

HIGHEST = lax.Precision.HIGHEST


def _norm_mod(x, g, shift, scale):
    ms = jnp.mean(x * x, axis=-1, keepdims=True)
    y = x * lax.rsqrt(ms + NORM_EPS) * g
    return y * (1.0 + scale) + shift


def _sigmoid(x):
    return jax.nn.sigmoid(x)


def _mod_kernel(c_ref, w_ref, b_ref, o_ref):
    c = c_ref[...]
    s = c * _sigmoid(c)
    o_ref[...] = jnp.dot(s, w_ref[...], precision=HIGHEST, preferred_element_type=F32) + b_ref[...]


def _mod_call(cc, w_mod, b_mod):
    rows, d = cc.shape
    n = w_mod.shape[1]
    tn = 1024
    return pl.pallas_call(
        _mod_kernel,
        grid=(n // tn,),
        in_specs=[
            pl.BlockSpec((rows, d), lambda j: (0, 0)),
            pl.BlockSpec((d, tn), lambda j: (0, j)),
            pl.BlockSpec((1, tn), lambda j: (0, j)),
        ],
        out_specs=pl.BlockSpec((rows, tn), lambda j: (0, j)),
        out_shape=jax.ShapeDtypeStruct((rows, n), F32),
        compiler_params=pltpu.CompilerParams(dimension_semantics=("arbitrary",), vmem_limit_bytes=VMEM_LIMIT),
        name="mod",
    )(cc, w_mod, b_mod)


Q0, K0, V0, GLU0 = 0, D_ATT, 2 * D_ATT, 3 * D_ATT
GA0 = GLU0 + 2 * D_CONV


NT_DIMS = (((1,), (1,)), ((), ()))
LOG2_E = 1.4426950408889634
SCORE_SCALE = HEAD_DIM ** -0.5 * LOG2_E


def _store_transposed(res_t, ref, scale=None):
    for j in range(ref.shape[0]):
        blk = res_t[:, j * LANES:(j + 1) * LANES]
        if scale is not None:
            blk = blk * scale
        ref[j] = blk.astype(ref.dtype)


def _conv_kernel(u_ref, up_ref, un_ref, cw_ref, cb_ref, lg_ref, lb_ref, yn_ref, ubuf, shifted,
                 *, tiles_per_seq):
    i = pl.program_id(0)
    tm = u_ref.shape[0]
    first = (i % tiles_per_seq) == 0
    last = (i % tiles_per_seq) == tiles_per_seq - 1
    ubuf[0:HALO, :] = jnp.where(first, 0.0, up_ref[...].astype(F32))
    for r0 in range(0, tm, CONV_CHUNK):
        ubuf[HALO + r0:HALO + r0 + CONV_CHUNK, :] = u_ref[r0:r0 + CONV_CHUNK, :].astype(F32)
    ubuf[HALO + tm:, :] = jnp.where(last, 0.0, un_ref[...].astype(F32))

    span = shifted.shape[1]
    for ph in range(SUBLANES):
        for r0 in range(0, span, CONV_CHUNK):
            n = min(CONV_CHUNK, span - r0)
            shifted[ph, r0:r0 + n, :] = ubuf[ph + r0:ph + r0 + n, :]

    base = HALO - CONV_WIDTH // 2
    inv_c = 1.0 / D_CONV
    for tc in range(tm // CONV_CHUNK):
        accs = []
        for lc in range(D_CONV // LANES):
            ls = slice(lc * LANES, (lc + 1) * LANES)
            acc = jnp.zeros((CONV_CHUNK, LANES), F32)
            for j in range(CONV_WIDTH):
                tiles, ph = divmod(base + j, SUBLANES)
                lo = tc * CONV_CHUNK + tiles * SUBLANES
                acc = acc + shifted[ph, lo:lo + CONV_CHUNK, ls] * cw_ref[j:j + 1, ls]
            accs.append(acc + cb_ref[:, ls])
        mu = sum(jnp.sum(a, axis=-1, keepdims=True) for a in accs) * inv_c
        cen = [a - mu for a in accs]
        var = sum(jnp.sum(c * c, axis=-1, keepdims=True) for c in cen) * inv_c
        rstd = lax.rsqrt(var + NORM_EPS)
        for lc, c in enumerate(cen):
            ls = slice(lc * LANES, (lc + 1) * LANES)
            yn = c * rstd * lg_ref[:, ls] + lb_ref[:, ls]
            yn_ref[tc * CONV_CHUNK:(tc + 1) * CONV_CHUNK, ls] = (yn * _sigmoid(yn)).astype(yn_ref.dtype)


def _conv_call(u, cw, cb, lg, lb, seq):
    t = u.shape[0]
    tm = TM_CONV
    tpb = seq // tm
    hb = tm // HALO
    n_halo = t // HALO
    row = lambda i: (i, 0)
    const = lambda i: (0, 0)
    return pl.pallas_call(
        functools.partial(_conv_kernel, tiles_per_seq=tpb),
        grid=(t // tm,),
        in_specs=[
            pl.BlockSpec((tm, D_CONV), row),
            pl.BlockSpec((HALO, D_CONV), lambda i: (jnp.maximum(i * hb - 1, 0), 0)),
            pl.BlockSpec((HALO, D_CONV), lambda i: (jnp.minimum((i + 1) * hb, n_halo - 1), 0)),
            pl.BlockSpec(cw.shape, const),
            pl.BlockSpec((1, D_CONV), const),
            pl.BlockSpec((1, D_CONV), const),
            pl.BlockSpec((1, D_CONV), const),
        ],
        out_specs=pl.BlockSpec((tm, D_CONV), row),
        out_shape=jax.ShapeDtypeStruct((t, D_CONV), BF16),
        scratch_shapes=[
            pltpu.VMEM((tm + 2 * HALO, D_CONV), F32),
            pltpu.VMEM((SUBLANES, tm + 2 * HALO - SUBLANES, D_CONV), F32),
        ],
        compiler_params=pltpu.CompilerParams(dimension_semantics=("arbitrary",), vmem_limit_bytes=VMEM_LIMIT),
        name="conv",
    )(u, u, u, cw, cb, lg, lb)


def _proj_kernel(x_ref, mod_ref, g_ref, wqv_ref, w_ref, qt_ref, k_ref, vt_ref, u_ref, ga_ref, gb_ref):
    d = x_ref.shape[1]
    mod = mod_ref[0]
    h = _norm_mod(x_ref[...], g_ref[...], mod[0:1], mod[1:2]).astype(BF16)

    qv_t = lax.dot_general(wqv_ref[...], h, NT_DIMS, preferred_element_type=F32)
    _store_transposed(qv_t[:D_ATT], qt_ref, SCORE_SCALE)
    _store_transposed(qv_t[D_ATT:], vt_ref)

    def seg(lo, hi):
        return jnp.dot(h, w_ref[:, lo:hi], preferred_element_type=F32)

    k_ref[...] = seg(0, D_ATT).astype(BF16)
    a = seg(D_ATT, D_ATT + D_CONV)
    g = seg(D_ATT + D_CONV, D_ATT + 2 * D_CONV)
    u_ref[...] = (a * _sigmoid(g)).astype(BF16)
    g0 = D_ATT + 2 * D_CONV
    ga_ref[...] = _sigmoid(seg(g0, g0 + d)).astype(BF16)
    gb_ref[...] = _sigmoid(seg(g0 + d, g0 + 2 * d)).astype(BF16)


def _proj_call(x2, mod3, g, wqv_t, w_rest, seq):
    t, d = x2.shape
    tm = TM_PROJ
    tpb = seq // tm
    nblk = tm // LANES
    outs = [
        jax.ShapeDtypeStruct((t // LANES, D_ATT, LANES), BF16),
        jax.ShapeDtypeStruct((t, D_ATT), BF16),
        jax.ShapeDtypeStruct((t // LANES, D_ATT, LANES), BF16),
        jax.ShapeDtypeStruct((t, D_CONV), BF16),
        jax.ShapeDtypeStruct((t, d), BF16),
        jax.ShapeDtypeStruct((t, d), BF16),
    ]
    row = lambda i: (i, 0)
    blk3 = lambda i: (i, 0, 0)
    return pl.pallas_call(
        _proj_kernel,
        grid=(t // tm,),
        in_specs=[
            pl.BlockSpec((tm, d), row),
            pl.BlockSpec((1, 6, d), lambda i: (i // tpb, 0, 0)),
            pl.BlockSpec((1, d), lambda i: (0, 0)),
            pl.BlockSpec(wqv_t.shape, lambda i: (0, 0), pipeline_mode=pl.Buffered(1)),
            pl.BlockSpec(w_rest.shape, lambda i: (0, 0), pipeline_mode=pl.Buffered(1)),
        ],
        out_specs=[
            pl.BlockSpec((nblk, D_ATT, LANES), blk3),
            pl.BlockSpec((tm, D_ATT), row),
            pl.BlockSpec((nblk, D_ATT, LANES), blk3),
            pl.BlockSpec((tm, D_CONV), row),
            pl.BlockSpec((tm, d), row),
            pl.BlockSpec((tm, d), row),
        ],
        out_shape=outs,
        compiler_params=pltpu.CompilerParams(dimension_semantics=("arbitrary",), vmem_limit_bytes=VMEM_LIMIT),
        name="proj",
    )(x2, mod3, g, wqv_t, w_rest)


def _ctx_kv_kernel(x_ref, mod_ref, g_ref, wk_ref, wvt_ref, k_ref, vt_ref):
    mod = mod_ref[0]
    h = _norm_mod(x_ref[...], g_ref[...], mod[0:1], mod[1:2]).astype(BF16)
    k_ref[...] = jnp.dot(h, wk_ref[...], preferred_element_type=F32).astype(BF16)
    _store_transposed(lax.dot_general(wvt_ref[...], h, NT_DIMS, preferred_element_type=F32), vt_ref)


def _ctx_kv_call(c2, mod3, g, w_k, w_vt):
    t, d = c2.shape
    tm = 512
    nblk = tm // LANES
    row = lambda i: (i, 0)
    return pl.pallas_call(
        _ctx_kv_kernel,
        grid=(t // tm,),
        in_specs=[
            pl.BlockSpec((tm, d), row),
            pl.BlockSpec((1, 6, d), lambda i: (0, 0, 0)),
            pl.BlockSpec((1, d), lambda i: (0, 0)),
            pl.BlockSpec(w_k.shape, lambda i: (0, 0)),
            pl.BlockSpec(w_vt.shape, lambda i: (0, 0)),
        ],
        out_specs=[pl.BlockSpec((tm, D_ATT), row), pl.BlockSpec((nblk, D_ATT, LANES), lambda i: (i, 0, 0))],
        out_shape=[jax.ShapeDtypeStruct((t, D_ATT), BF16), jax.ShapeDtypeStruct((t // LANES, D_ATT, LANES), BF16)],
        compiler_params=pltpu.CompilerParams(dimension_semantics=("arbitrary",), vmem_limit_bytes=VMEM_LIMIT),
        name="ctx_kv",
    )(c2, mod3, g, w_k, w_vt)


HEADS_PER_GROUP = 4
GROUP_W = HEADS_PER_GROUP * HEAD_DIM
N_HEAD_GROUPS = N_HEADS // HEADS_PER_GROUP


def _attn_kernel(qt_ref, k_ref, vt_ref, kc_ref, vct_ref, bias_ref, o_ref, vboth):
    nblk = qt_ref.shape[0]
    rows = 2 * nblk
    n_loc = WIN_ROWS * GRID_W
    n_ctx_blk = vct_ref.shape[0]
    half = GRID_W

    for j in range(nblk):
        vboth[0, j] = vt_ref[j]
    for j in range(nblk - 1):
        vboth[1, j] = jnp.concatenate([vt_ref[j][:, half:], vt_ref[j + 1][:, :half]], axis=1)
    vboth[1, nblk - 1] = jnp.zeros_like(vt_ref[0])

    lane = lax.broadcasted_iota(jnp.int32, (D_ATT, LANES), 1)
    low = lane < half
    rblk = lax.broadcasted_iota(jnp.int32, (GROUP_W, GROUP_W), 0) // HEAD_DIM
    cblk = lax.broadcasted_iota(jnp.int32, (GROUP_W, GROUP_W), 1) // HEAD_DIM
    diag = rblk == cblk
    low64 = lax.broadcasted_iota(jnp.int32, (HEAD_DIM, LANES), 1) < half
    kc = kc_ref[...]

    def one_row(row, tiled, side):
        r_start = jnp.clip(row - WIN_ROWS // 2, 0, rows - WIN_ROWS)
        b_off = pl.multiple_of((WIN_ROWS - 1 - (row - r_start)) * GRID_W, GRID_W)
        kw = k_ref[pl.ds(pl.multiple_of(r_start * GRID_W, GRID_W), n_loc), :]
        vwin = vboth[r_start % 2, pl.ds(r_start // 2, n_loc // LANES)]
        parts = []
        for g in range(N_HEAD_GROUPS):
            fs = slice(g * GROUP_W, (g + 1) * GROUP_W)
            tg = tiled[fs, :]
            w = jnp.where(diag, jnp.concatenate([tg, tg], axis=1), jnp.zeros((), BF16))
            s = jnp.dot(jnp.concatenate([kw[:, fs], kc[:, fs]], axis=0), w, preferred_element_type=F32)
            s_loc = s[:n_loc] + bias_ref[g, pl.ds(b_off, n_loc), :]
            s_ctx = s[n_loc:]
            m = jnp.maximum(jnp.max(s_loc, axis=0, keepdims=True), jnp.max(s_ctx, axis=0, keepdims=True))
            p_loc = jnp.exp2(s_loc - m)
            p_ctx = jnp.exp2(s_ctx - m)
            l = jnp.sum(p_loc, axis=0, keepdims=True) + jnp.sum(p_ctx, axis=0, keepdims=True)
            p = jnp.concatenate([p_loc.astype(BF16), p_ctx.astype(BF16)], axis=0)
            vt_g = jnp.concatenate([vwin[j][fs, :] for j in range(n_loc // LANES)]
                                   + [vct_ref[j][fs, :] for j in range(n_ctx_blk)], axis=1)
            o_t = jnp.dot(vt_g, p, preferred_element_type=F32) * (1.0 / l)
            for j in range(HEADS_PER_GROUP):
                blk = o_t[j * HEAD_DIM:(j + 1) * HEAD_DIM, (j // 2) * LANES:(j // 2 + 1) * LANES]
                if j % 2 != side:
                    blk = pltpu.roll(blk, half, axis=1)
                parts.append(blk)
        return parts

    def body(i, carry):
        xq = qt_ref[i]
        rolled = jnp.concatenate([xq[:, half:], xq[:, :half]], axis=1)
        parts_a = one_row(2 * i, jnp.where(low, xq, rolled), 0)
        parts_b = one_row(2 * i + 1, jnp.where(low, rolled, xq), 1)
        out = jnp.concatenate([jnp.where(low64, a, b) for a, b in zip(parts_a, parts_b)], axis=0)
        o_ref[i] = out.astype(o_ref.dtype)
        return carry

    lax.fori_loop(0, nblk, body, 0, unroll=8)


def _attn_call(qt, k, vt, kc, vct, bias, seq, n_ctx):
    t = k.shape[0]
    b = t // seq
    nblk = seq // LANES
    lat3 = pl.BlockSpec((nblk, D_ATT, LANES), lambda i: (i, 0, 0))
    return pl.pallas_call(
        _attn_kernel,
        grid=(b,),
        in_specs=[lat3,
                  pl.BlockSpec((seq, D_ATT), lambda i: (i, 0)),
                  lat3,
                  pl.BlockSpec((n_ctx, D_ATT), lambda i: (i, 0)),
                  pl.BlockSpec((n_ctx // LANES, D_ATT, LANES), lambda i: (i, 0, 0)),
                  pl.BlockSpec(bias.shape, lambda i: (0, 0, 0))],
        out_specs=lat3,
        out_shape=jax.ShapeDtypeStruct((t // LANES, D_ATT, LANES), BF16),
        scratch_shapes=[pltpu.VMEM((2, nblk, D_ATT, LANES), BF16)],
        compiler_params=pltpu.CompilerParams(dimension_semantics=("arbitrary",), vmem_limit_bytes=VMEM_LIMIT),
        name="attn",
    )(qt, k, vt, kc, vct, bias)


def _bias_table(rpb):
    cq = np.arange(GRID_W)[:, None]
    ck = np.arange(GRID_W)[None, :]
    c_start = np.clip(cq - WIN_COLS // 2, 0, GRID_W - WIN_COLS)
    col_mask = (ck >= c_start) & (ck < c_start + WIN_COLS)
    n_dc = rpb.shape[2]
    lead = GRID_W - WIN_COLS
    w = jnp.pad(rpb, ((0, 0), (0, 0), (lead, 2 * GRID_W - lead - n_dc)))
    skew = jnp.tile(w, (1, 1, GRID_W + 1))[:, :, :GRID_W * (2 * GRID_W + 1)]
    tab = skew.reshape(rpb.shape[0], rpb.shape[1], GRID_W, 2 * GRID_W + 1)[:, :, ::-1, :GRID_W]
    tab = jnp.where(col_mask[None, None], tab, NEG_INF)
    n_dr = tab.shape[1]
    tab = tab.reshape(N_HEAD_GROUPS, HEADS_PER_GROUP, n_dr, GRID_W, GRID_W)
    tab = tab.transpose(0, 2, 4, 1, 3)
    return (tab.reshape(N_HEAD_GROUPS, n_dr * GRID_W, GROUP_W) * LOG2_E).astype(F32)


def _merge_rows(r0, n, att_ref, yn_ref, ga_ref, gb_ref, x_ref, mod,
                wa_ref, wc_ref, wo_ref, nf_ref, wr_ref, br_ref, xmid_ref, hp_ref):
    d = x_ref.shape[1]
    rs = slice(r0, r0 + n)
    y_conv = jnp.dot(yn_ref[rs, :], wc_ref[...], preferred_element_type=F32)

    att = jnp.concatenate([att_ref[j].astype(F32).T for j in range(r0 // LANES, (r0 + n) // LANES)], axis=0)
    y_att = jnp.dot(att.astype(BF16), wa_ref[...], preferred_element_type=F32)
    mix = ga_ref[rs, :].astype(F32) * y_att + gb_ref[rs, :].astype(F32) * y_conv
    y = jnp.dot(mix.astype(BF16), wo_ref[...], preferred_element_type=F32)
    x_mid = x_ref[rs, :] + mod[2:3] * y
    xmid_ref[rs, :] = x_mid
    h2 = _norm_mod(x_mid, nf_ref[...], mod[3:4], mod[4:5])

    h2_hi = h2.astype(BF16)
    h2_lo = (h2 - h2_hi.astype(F32)).astype(BF16)
    t_hi = jnp.dot(h2_hi, wr_ref[...], preferred_element_type=F32)
    t_lo = jnp.dot(h2_lo, wr_ref[:, :LANES], preferred_element_type=F32)
    logits = t_hi[:, :LANES] + t_hi[:, LANES:] + t_lo + br_ref[...]
    lt = logits.T
    sub = lax.broadcasted_iota(jnp.int32, (SUBLANES, n), 0).astype(F32)
    big = float(LANES)
    is_g = sub < N_GROUPS
    gl = jnp.where(is_g, lt[0:SUBLANES], -jnp.inf)
    gmax = jnp.max(gl, axis=0, keepdims=True)
    g_idx = jnp.min(jnp.where(gl == gmax, sub, big), axis=0, keepdims=True)
    p_group = 1.0 / jnp.sum(jnp.where(is_g, jnp.exp(gl - gmax), 0.0), axis=0, keepdims=True)
    el = lt[ROUTER_E0:ROUTER_E0 + EXPERTS_PER_GROUP]
    for g in range(1, N_GROUPS):
        lo_g = ROUTER_E0 + g * EXPERTS_PER_GROUP
        el = jnp.where(g_idx == float(g), lt[lo_g:lo_g + EXPERTS_PER_GROUP], el)
    v1 = jnp.max(el, axis=0, keepdims=True)
    j1 = jnp.min(jnp.where(el == v1, sub, big), axis=0, keepdims=True)
    el2 = jnp.where(sub == j1, -jnp.inf, el)
    v2 = jnp.max(el2, axis=0, keepdims=True)
    j2 = jnp.min(jnp.where(el2 == v2, sub, big), axis=0, keepdims=True)
    e2 = jnp.exp(v2 - v1)
    gate1 = p_group / (1.0 + e2)
    gate2 = p_group * e2 / (1.0 + e2)
    ja = jnp.minimum(j1, j2)
    jb = jnp.maximum(j1, j2)
    gate_a = jnp.where(j1 < j2, gate1, gate2)
    gate_b = jnp.where(j1 < j2, gate2, gate1)
    pair = ja * (2 * EXPERTS_PER_GROUP - 1 - ja) * 0.5 + (jb - ja - 1.0)
    cls = g_idx * PAIRS_PER_GROUP + pair
    gate_rows = jnp.where(sub == 0.0, gate_a, jnp.where(sub == 1.0, gate_b, 0.0))
    gates = jnp.concatenate([gate_rows, jnp.zeros((LANES - SUBLANES, n), F32)], axis=0).T

    half = d // 2
    hi = lax.bitcast_convert_type(h2[:, :half].astype(BF16).astype(F32), U32)
    lo = lax.bitcast_convert_type(h2[:, half:].astype(BF16).astype(F32), U32)
    words = hi | (lo >> 16)
    n_words = half // LANES
    base = r0 * ROW_TILE
    for c in range(n_words):
        hp_ref[pl.ds(base + c, n, stride=ROW_TILE), :] = words[:, c * LANES:(c + 1) * LANES]
    hp_ref[pl.ds(base + n_words, n, stride=ROW_TILE), :] = lax.bitcast_convert_type(gates, U32)
    zero = jnp.zeros((n, LANES), U32)
    for c in range(n_words + 1, ROW_TILE):
        hp_ref[pl.ds(base + c, n, stride=ROW_TILE), :] = zero
    return cls


def _merge_kernel(att_ref, yn_ref, ga_ref, gb_ref, x_ref, mod_ref,
                  wa_ref, wc_ref, wo_ref, nf_ref, wr_ref, br_ref, tri_ref,
                  xmid_ref, hp_ref, info_ref, cnt_ref, run_ref):
    i = pl.program_id(0)
    tm = x_ref.shape[0]

    @pl.when(i == 0)
    def _():
        run_ref[...] = jnp.zeros_like(run_ref)

    mod = mod_ref[0]
    n = tm // MERGE_PARTS
    cls = jnp.concatenate(
        [_merge_rows(p * n, n, att_ref, yn_ref, ga_ref, gb_ref, x_ref, mod, wa_ref, wc_ref, wo_ref, nf_ref,
                     wr_ref, br_ref, xmid_ref, hp_ref) for p in range(MERGE_PARTS)], axis=1)

    crow = lax.broadcasted_iota(jnp.int32, (LANES, tm), 0).astype(F32)
    oh_f = jnp.where(crow == cls, 1.0, 0.0)
    before = jnp.dot(oh_f.astype(BF16), tri_ref[...], preferred_element_type=F32)
    run = run_ref[:, 0:1]
    rank = jnp.sum(oh_f * (before + run), axis=0, keepdims=True)
    new_run = run + jnp.sum(oh_f, axis=1, keepdims=True)
    run_ref[...] = jnp.broadcast_to(new_run, run_ref.shape)
    cnt_ref[...] = jnp.broadcast_to(new_run, cnt_ref.shape)

    sub = lax.broadcasted_iota(jnp.int32, (SUBLANES, tm), 0)
    info_ref[...] = jnp.where(sub == 0, cls, jnp.where(sub == 1, rank, 0.0))


def _merge_call(att, yn, ga, gb, x2, mod3, wa, wc, wo, nf, wr, br, seq):
    t, d = x2.shape
    tm = TM_MERGE
    tpb = seq // tm
    tri = jnp.asarray(np.triu(np.ones((tm, tm), np.float32), 1), BF16)
    row = lambda i: (i, 0)
    const = lambda i: (0, 0)
    return pl.pallas_call(
        _merge_kernel,
        grid=(t // tm,),
        in_specs=[
            pl.BlockSpec((tm // LANES, D_ATT, LANES), lambda i: (i, 0, 0)),
            pl.BlockSpec((tm, D_CONV), row),
            pl.BlockSpec((tm, d), row),
            pl.BlockSpec((tm, d), row),
            pl.BlockSpec((tm, d), row),
            pl.BlockSpec((1, 6, d), lambda i: (i // tpb, 0, 0)),
            pl.BlockSpec(wa.shape, const),
            pl.BlockSpec(wc.shape, const),
            pl.BlockSpec(wo.shape, const),
            pl.BlockSpec((1, d), const),
            pl.BlockSpec(wr.shape, const),
            pl.BlockSpec((1, LANES), const),
            pl.BlockSpec((tm, tm), const),
        ],
        out_specs=[
            pl.BlockSpec((tm, d), row),
            pl.BlockSpec((tm * ROW_TILE, LANES), row),
            pl.BlockSpec((SUBLANES, tm), lambda i: (0, i)),
            pl.BlockSpec((LANES, LANES), const),
        ],
        out_shape=[
            jax.ShapeDtypeStruct((t, d), F32),
            jax.ShapeDtypeStruct((t * ROW_TILE, LANES), U32),
            jax.ShapeDtypeStruct((SUBLANES, t), F32),
            jax.ShapeDtypeStruct((LANES, LANES), F32),
        ],
        scratch_shapes=[pltpu.VMEM((LANES, LANES), F32)],
        compiler_params=pltpu.CompilerParams(dimension_semantics=("arbitrary",), vmem_limit_bytes=VMEM_LIMIT),
        name="merge",
    )(att, yn, ga, gb, x2, mod3, wa, wc, wo, nf, wr, br, tri)


def _dispatch_kernel(dest_ref, tail_ref, hp_ref, hs_hbm, zbuf, sem, zsem):
    n = dest_ref.shape[0]
    blk_rows = zbuf.shape[0]

    @pl.when(pl.program_id(0) == 0)
    def _():
        zbuf[...] = jnp.zeros_like(zbuf)

        def tail_copy(c):
            start = pl.multiple_of(tail_ref[c] * ROW_TILE, ROW_TILE)
            return pltpu.make_async_copy(zbuf, hs_hbm.at[pl.ds(start, blk_rows)], zsem)

        def start_one(c, carry):
            @pl.when(tail_ref[c] >= 0)
            def _():
                tail_copy(c).start()
            return carry

        def wait_one(c, carry):
            @pl.when(tail_ref[c] >= 0)
            def _():
                tail_copy(c).wait()
            return carry

        lax.fori_loop(0, N_CLASSES, start_one, 0)
        lax.fori_loop(0, N_CLASSES, wait_one, 0)

        def unused_copy(b):
            start = pl.multiple_of(b * blk_rows, blk_rows)
            return pltpu.make_async_copy(zbuf, hs_hbm.at[pl.ds(start, blk_rows)], zsem)

        def start_unused(b, carry):
            unused_copy(b).start()
            return carry

        def wait_unused(b, carry):
            unused_copy(b).wait()
            return carry

        n_blocks = hs_hbm.shape[0] // blk_rows
        lax.fori_loop(tail_ref[N_CLASSES], n_blocks, start_unused, 0)
        lax.fori_loop(tail_ref[N_CLASSES], n_blocks, wait_unused, 0)

    def body(o, carry):
        for k in range(DMA_UNROLL):
            t = o * DMA_UNROLL + k
            src = pl.multiple_of(t * ROW_TILE, ROW_TILE)
            dst = pl.multiple_of(dest_ref[t], ROW_TILE)
            pltpu.make_async_copy(hp_ref.at[pl.ds(src, ROW_TILE)], hs_hbm.at[pl.ds(dst, ROW_TILE)],
                                  sem).start(priority=k % 2)
        return carry

    lax.fori_loop(0, n // DMA_UNROLL, body, 0)
    pltpu.make_async_copy(hp_ref, hs_hbm.at[pl.ds(0, n * ROW_TILE)], sem).wait()


def _dispatch_call(dest, tail_start, hp, p_rows):
    t = dest.shape[0]
    return pl.pallas_call(
        _dispatch_kernel,
        grid=(t // TM_DISPATCH,),
        in_specs=[
            pl.BlockSpec((TM_DISPATCH,), lambda i: (i,), memory_space=pltpu.SMEM),
            pl.BlockSpec(memory_space=pltpu.SMEM),
            pl.BlockSpec((TM_DISPATCH * ROW_TILE, LANES), lambda i: (i, 0)),
        ],
        out_specs=pl.BlockSpec(memory_space=pl.ANY),
        out_shape=jax.ShapeDtypeStruct((p_rows * ROW_TILE, LANES), U32),
        scratch_shapes=[pltpu.VMEM((MOE_BLOCK * ROW_TILE, LANES), U32),
                        pltpu.SemaphoreType.DMA(()), pltpu.SemaphoreType.DMA(())],
        compiler_params=pltpu.CompilerParams(dimension_semantics=("arbitrary",), has_side_effects=True,
                                             vmem_limit_bytes=VMEM_LIMIT),
        name="dispatch",
    )(dest, tail_start, hp)


def _moe_kernel(rows_ref, seg_ref, first_ref, fetch_ref, pre_ref, hs_ref, wg_hbm, wu_hbm, wd_hbm, y_ref,
                wg_buf, wu_buf, wd_buf, sems):
    base = pl.program_id(0) * MOE_BLOCKS_PER_STEP
    rows = hs_ref.shape[0] // MOE_BLOCKS_PER_STEP
    blk = rows // ROW_TILE
    d = wg_buf.shape[2]
    n_words = d // 2 // LANES

    def copies(side, expert, slot):
        return [pltpu.make_async_copy(src.at[expert], buf.at[side, slot], sems.at[side, slot])
                for src, buf in ((wg_hbm, wg_buf), (wu_hbm, wu_buf), (wd_hbm, wd_buf))]

    for sub in range(MOE_BLOCKS_PER_STEP):
        b = base + sub
        for side in range(2):
            @pl.when(b == 0)
            def _():
                for s in range(2):
                    @pl.when(pre_ref[side, s] >= 0)
                    def _():
                        for cp in copies(side, pre_ref[side, s], s):
                            cp.start()

            @pl.when(fetch_ref[side, b] >= 0)
            def _():
                for cp in copies(side, fetch_ref[side, b], (seg_ref[side, b] + 2) % WEIGHT_SLOTS):
                    cp.start()

        for side in range(2):
            @pl.when(first_ref[side, b] == 1)
            def _():
                for cp in copies(side, 0, seg_ref[side, b] % WEIGHT_SLOTS):
                    cp.wait()

    def run(sub):
        b = base + sub
        off = sub * rows
        slot_a = seg_ref[0, b] % WEIGHT_SLOTS
        slot_b = seg_ref[1, b] % WEIGHT_SLOTS
        his, los = [], []
        for c in range(n_words):
            w = hs_ref[pl.ds(off + c, blk, stride=ROW_TILE), :]
            his.append(lax.bitcast_convert_type(w & jnp.uint32(0xFFFF0000), F32).astype(BF16))
            los.append(lax.bitcast_convert_type(w << 16, F32).astype(BF16))
        x = jnp.concatenate(his + los, axis=-1)
        gates = lax.bitcast_convert_type(hs_ref[pl.ds(off + n_words, blk, stride=ROW_TILE), :], F32)

        def mlp(side, slot):
            g = jnp.dot(x, wg_buf[side, slot], preferred_element_type=F32)
            u = jnp.dot(x, wu_buf[side, slot], preferred_element_type=F32)
            a = (g * _sigmoid(g) * u).astype(BF16)
            return jnp.dot(a, wd_buf[side, slot], preferred_element_type=F32)

        y = gates[:, 0:1] * mlp(0, slot_a) + gates[:, 1:2] * mlp(1, slot_b)
        for c in range(d // LANES):
            y_ref[pl.ds(off + c, blk, stride=ROW_TILE), :] = y[:, c * LANES:(c + 1) * LANES]

    used = [rows_ref[base + sub] > 0 for sub in range(MOE_BLOCKS_PER_STEP)]
    all_used = functools.reduce(jnp.logical_and, used)

    @pl.when(all_used)
    def _():
        for sub in range(MOE_BLOCKS_PER_STEP):
            run(sub)

    for sub in range(MOE_BLOCKS_PER_STEP):
        @pl.when(jnp.logical_not(all_used) & used[sub])
        def _():
            run(sub)

        @pl.when(jnp.logical_not(used[sub]))
        def _():
            y_ref[sub * rows:(sub + 1) * rows, :] = jnp.zeros((rows, LANES), y_ref.dtype)


def _weight_schedule(blk_e, blk_rows):
    nb = blk_e.shape[0]
    valid = blk_rows > 0
    prev = jnp.concatenate([jnp.full((1,), -1, jnp.int32), blk_e[:-1]])
    first = valid & (blk_e != prev)
    seg = jnp.cumsum(first.astype(jnp.int32)) - 1
    n_seg = jnp.sum(first.astype(jnp.int32))
    seg_e = jnp.full((nb + 3,), -1, jnp.int32).at[jnp.where(first, seg, nb + 2)].set(blk_e)
    seg_e = seg_e.at[nb + 2].set(-1)
    fetch = jnp.where(first & (seg + 2 < n_seg), seg_e[jnp.clip(seg + 2, 0, nb + 1)], -1)
    return jnp.maximum(seg, 0), first.astype(jnp.int32), fetch.astype(jnp.int32), seg_e[:2]


def _moe_call(blk_ea, blk_eb, blk_rows, hs, wg, wu, wd):
    nb = blk_ea.shape[0]
    d = wg.shape[1]
    de = wg.shape[2]
    rows = MOE_BLOCKS_PER_STEP * MOE_BLOCK * ROW_TILE
    sched = [_weight_schedule(e, blk_rows) for e in (blk_ea, blk_eb)]
    seg, first, fetch, pre = (jnp.stack([s[k] for s in sched]) for k in range(4))
    tok = pl.BlockSpec((rows, LANES), lambda i, *_: (i, 0))
    hbm = pl.BlockSpec(memory_space=pl.ANY)
    grid_spec = pltpu.PrefetchScalarGridSpec(
        num_scalar_prefetch=5,
        grid=(nb // MOE_BLOCKS_PER_STEP,),
        in_specs=[tok, hbm, hbm, hbm],
        out_specs=tok,
        scratch_shapes=[
            pltpu.VMEM((2, WEIGHT_SLOTS, d, de), BF16),
            pltpu.VMEM((2, WEIGHT_SLOTS, d, de), BF16),
            pltpu.VMEM((2, WEIGHT_SLOTS, de, d), BF16),
            pltpu.SemaphoreType.DMA((2, WEIGHT_SLOTS)),
        ],
    )
    return pl.pallas_call(
        _moe_kernel,
        grid_spec=grid_spec,
        out_shape=jax.ShapeDtypeStruct((nb * MOE_BLOCK * ROW_TILE, LANES), F32),
        compiler_params=pltpu.CompilerParams(dimension_semantics=("arbitrary",), vmem_limit_bytes=VMEM_LIMIT),
        name="moe",
    )(blk_rows, seg, first, fetch, pre, hs, wg, wu, wd)


def _final_kernel(dest_ref, dest_next_ref, xmid_ref, mod_ref, fn_ref, y_hbm, o_ref, fbuf, sems):
    i = pl.program_id(0)
    n_steps = pl.num_programs(0)
    n = dest_ref.shape[0]
    d = xmid_ref.shape[1]

    def issue(idx_ref, slot):
        def body(o, carry):
            for k in range(DMA_UNROLL):
                t = o * DMA_UNROLL + k
                src = pl.multiple_of(idx_ref[t], ROW_TILE)
                dst = pl.multiple_of(t * ROW_TILE, ROW_TILE)
                pltpu.make_async_copy(y_hbm.at[pl.ds(src, ROW_TILE)], fbuf.at[slot, pl.ds(dst, ROW_TILE)],
                                      sems.at[slot]).start(priority=k % 2)
            return carry

        lax.fori_loop(0, n // DMA_UNROLL, body, 0)

    slot = i % 2

    @pl.when(i == 0)
    def _():
        issue(dest_ref, 0)

    for s in range(2):
        @pl.when((i + 1 < n_steps) & (slot == s))
        def _():
            issue(dest_next_ref, 1 - s)

    pltpu.make_async_copy(y_hbm.at[pl.ds(0, n * ROW_TILE)], fbuf.at[slot], sems.at[slot]).wait()
    f = jnp.concatenate([fbuf[slot, pl.ds(c, n, stride=ROW_TILE), :] for c in range(d // LANES)], axis=-1)
    x = xmid_ref[...] + mod_ref[0][5:6] * f
    ms = jnp.mean(x * x, axis=-1, keepdims=True)
    o_ref[...] = x * lax.rsqrt(ms + NORM_EPS) * fn_ref[...]


def _final_call(dest, xmid, mod3, fn, y, seq):
    t, d = xmid.shape
    tm = TM_ROWS
    tpb = seq // tm
    n_steps = t // tm
    return pl.pallas_call(
        _final_kernel,
        grid=(n_steps,),
        in_specs=[
            pl.BlockSpec((tm,), lambda i: (i,), memory_space=pltpu.SMEM),
            pl.BlockSpec((tm,), lambda i: (jnp.minimum(i + 1, n_steps - 1),), memory_space=pltpu.SMEM),
            pl.BlockSpec((tm, d), lambda i: (i, 0)),
            pl.BlockSpec((1, 6, d), lambda i: (i // tpb, 0, 0)),
            pl.BlockSpec((1, d), lambda i: (0, 0)),
            pl.BlockSpec(memory_space=pl.ANY),
        ],
        out_specs=pl.BlockSpec((tm, d), lambda i: (i, 0)),
        out_shape=jax.ShapeDtypeStruct((t, d), F32),
        scratch_shapes=[pltpu.VMEM((2, tm * ROW_TILE, LANES), F32), pltpu.SemaphoreType.DMA((2,))],
        compiler_params=pltpu.CompilerParams(dimension_semantics=("arbitrary",), vmem_limit_bytes=VMEM_LIMIT),
        name="final",
    )(dest, dest, xmid, mod3, fn, y)


def _pair_tables():
    ea, eb = [], []
    for g in range(N_GROUPS):
        for a in range(EXPERTS_PER_GROUP):
            for b in range(a + 1, EXPERTS_PER_GROUP):
                ea.append(g * EXPERTS_PER_GROUP + a)
                eb.append(g * EXPERTS_PER_GROUP + b)
    return np.asarray(ea, np.int32), np.asarray(eb, np.int32)


def _routing_plan(info, counts, t):
    cls = info[0].astype(jnp.int32)
    rank = info[1].astype(jnp.int32)
    cnt = counts[:N_CLASSES, 0].astype(jnp.int32)
    padded = (cnt + MOE_BLOCK - 1) // MOE_BLOCK * MOE_BLOCK
    pad_end = jnp.cumsum(padded)
    pad_start = pad_end - padded
    cls_ids = jnp.arange(N_CLASSES, dtype=jnp.int32)
    dest = rank + jnp.sum(jnp.where(cls[:, None] == cls_ids[None, :], pad_start[None, :], 0), axis=1)
    nb = -(-(t // MOE_BLOCK + N_CLASSES) // MOE_BLOCKS_PER_STEP) * MOE_BLOCKS_PER_STEP
    nb_used = pad_end[-1] // MOE_BLOCK
    blk = jnp.arange(nb, dtype=jnp.int32)
    blk_cls = jnp.sum(pad_end[None, :] <= (blk * MOE_BLOCK)[:, None], axis=1)
    blk_cls = jnp.clip(blk_cls, 0, N_CLASSES - 1)
    valid = blk < nb_used
    last_cls = blk_cls[jnp.maximum(nb_used - 1, 0)]
    blk_cls = jnp.where(valid, blk_cls, last_cls).astype(jnp.int32)
    tab_a, tab_b = _pair_tables()
    blk_ea = jnp.asarray(tab_a)[blk_cls]
    blk_eb = jnp.asarray(tab_b)[blk_cls]
    tail_start = jnp.where(cnt > 0, pad_end - MOE_BLOCK, -1).astype(jnp.int32)
    tail_start = jnp.concatenate([tail_start, nb_used[None].astype(jnp.int32)])
    blk_rows = jnp.clip(cnt[blk_cls] - (blk * MOE_BLOCK - pad_start[blk_cls]), 0, MOE_BLOCK)
    blk_rows = jnp.where(valid, blk_rows, 0).astype(jnp.int32)
    return dest, tail_start, blk_ea, blk_eb, blk_rows, nb


def kernel(x, c, ctx, c_ctx, w_mod, b_mod, norm_mix, w_in, rpb, w_att_out, conv_w, conv_b, conv_ln_g, conv_ln_b,
           w_conv_out, w_o, norm_ffn, w_router_group, b_router_group, w_router_expert, b_router_expert,
           w_exp_gate, w_exp_up, w_exp_down, final_norm):
    b, seq, d = x.shape
    n_ctx = ctx.shape[1]
    t = b * seq
    assert w_mod.shape[0] == 1, "single layer"
    assert seq % TM_ROWS == 0 and t % TM_DISPATCH == 0 and seq % TM_CONV == 0 and seq // GRID_W >= WIN_ROWS and n_ctx % LANES == 0 and (b * n_ctx) % 512 == 0
    assert d == 1024

    mod_rows = -(-(b + 1) // SUBLANES) * SUBLANES
    cc = jnp.zeros((mod_rows, d), F32).at[:b].set(c).at[b].set(c_ctx)
    m_all = _mod_call(cc, w_mod[0], b_mod[0][None, :])
    mod_lat = m_all[:b].reshape(b, 6, d)
    mod_ctx = m_all[b:b + 1].reshape(1, 6, d)

    x2 = x.reshape(t, d)
    g_mix = norm_mix[0][None, :]
    w_in_b = w_in[0].astype(BF16)
    wqv_t = jnp.concatenate([w_in_b[:, Q0:K0], w_in_b[:, V0:GLU0]], axis=1).T
    w_rest = jnp.concatenate([w_in_b[:, K0:V0], w_in_b[:, GLU0:]], axis=1)
    cw = jnp.zeros((32, D_CONV), F32).at[:CONV_WIDTH].set(conv_w[0])
    qt, k, vt, u, ga, gb = _proj_call(x2, mod_lat, g_mix, wqv_t, w_rest, seq)
    yn = _conv_call(u, cw, conv_b[0][None, :], conv_ln_g[0][None, :], conv_ln_b[0][None, :], seq)
    kc, vct = _ctx_kv_call(ctx.reshape(b * n_ctx, d), mod_ctx, g_mix, w_in_b[:, K0:V0], w_in_b[:, V0:GLU0].T)

    att = _attn_call(qt, k, vt, kc, vct, _bias_table(rpb[0]), seq, n_ctx)

    wr = jnp.zeros((d, LANES), F32).at[:, :N_GROUPS].set(w_router_group[0])
    wr = wr.at[:, ROUTER_E0:ROUTER_E0 + N_EXPERTS].set(w_router_expert[0])
    wr_hi = wr.astype(BF16)
    wr = jnp.concatenate([wr_hi, (wr - wr_hi.astype(F32)).astype(BF16)], axis=1)
    br = jnp.zeros((1, LANES), F32).at[0, :N_GROUPS].set(b_router_group[0])
    br = br.at[0, ROUTER_E0:ROUTER_E0 + N_EXPERTS].set(b_router_expert[0])
    x_mid, hp, info, counts = _merge_call(
        att, yn, ga, gb, x2, mod_lat, w_att_out[0].astype(BF16), w_conv_out[0].astype(BF16), w_o[0].astype(BF16), norm_ffn[0][None, :],
        wr, br, seq)

    dest, tail_start, blk_ea, blk_eb, blk_rows, nb = _routing_plan(info, counts, t)
    dest_row = dest * ROW_TILE
    hs = _dispatch_call(dest_row, tail_start, hp, nb * MOE_BLOCK)
    y = _moe_call(blk_ea, blk_eb, blk_rows, hs,
                  w_exp_gate[0].astype(BF16), w_exp_up[0].astype(BF16), w_exp_down[0].astype(BF16))
    out = _final_call(dest_row, x_mid, mod_lat, final_norm[None, :], y, seq)
    return out.reshape(b, seq, d)
```

```python
import functools

import numpy as np
import jax
import jax.numpy as jnp
from jax import lax
from jax.experimental import pallas as pl
from jax.experimental.pallas import tpu as pltpu

F32 = jnp.float32
BF16 = jnp.bfloat16
U32 = jnp.uint32

GRID_W = 64
N_HEADS = 8
HEAD_DIM = 64
D_ATT = N_HEADS * HEAD_DIM
WIN_ROWS = 8
WIN_COLS = 16
D_CONV = 512
CONV_WIDTH = 31
N_GROUPS = 4
EXPERTS_PER_GROUP = 8
N_EXPERTS = N_GROUPS * EXPERTS_PER_GROUP
PAIRS_PER_GROUP = EXPERTS_PER_GROUP * (EXPERTS_PER_GROUP - 1) // 2
N_CLASSES = N_GROUPS * PAIRS_PER_GROUP
ROUTER_E0 = 8
NORM_EPS = 1e-6
NEG_INF = -1e30

LANES = 128
SUBLANES = 8
ROW_TILE = SUBLANES
HALO = 16
VMEM_LIMIT = 56 * 1024 * 1024

TM_PROJ = 1024
TM_MERGE = 512
MERGE_PARTS = 1
TM_ROWS = 1024
TM_DISPATCH = 2048
TM_CONV = 1024
MOE_BLOCK = 256
MOE_BLOCKS_PER_STEP = 2
WEIGHT_SLOTS = MOE_BLOCKS_PER_STEP + 2
CONV_CHUNK = 64
DMA_UNROLL = 8

HIGHEST = lax.Precision.HIGHEST


def _norm_mod(x, g, shift, scale):
    ms = jnp.mean(x * x, axis=-1, keepdims=True)
    y = x * lax.rsqrt(ms + NORM_EPS) * g
    return y * (1.0 + scale) + shift


def _sigmoid(x):
    return jax.nn.sigmoid(x)


def _mod_kernel(c_ref, w_ref, b_ref, o_ref):
    c = c_ref[...]
    s = c * _sigmoid(c)
    o_ref[...] = jnp.dot(s, w_ref[...], precision=HIGHEST, preferred_element_type=F32) + b_ref[...]


def _mod_call(cc, w_mod, b_mod):
    rows, d = cc.shape
    n = w_mod.shape[1]
    tn = 1024
    return pl.pallas_call(
        _mod_kernel,
        grid=(n // tn,),
        in_specs=[
            pl.BlockSpec((rows, d), lambda j: (0, 0)),
            pl.BlockSpec((d, tn), lambda j: (0, j)),
            pl.BlockSpec((1, tn), lambda j: (0, j)),
        ],
        out_specs=pl.BlockSpec((rows, tn), lambda j: (0, j)),
        out_shape=jax.ShapeDtypeStruct((rows, n), F32),
        compiler_params=pltpu.CompilerParams(dimension_semantics=("arbitrary",), vmem_limit_bytes=VMEM_LIMIT),
        name="mod",
    )(cc, w_mod, b_mod)


Q0, K0, V0, GLU0 = 0, D_ATT, 2 * D_ATT, 3 * D_ATT
GA0 = GLU0 + 2 * D_CONV


NT_DIMS = (((1,), (1,)), ((), ()))
LOG2_E = 1.4426950408889634
SCORE_SCALE = HEAD_DIM ** -0.5 * LOG2_E


def _store_transposed(res_t, ref, scale=None):
    for j in range(ref.shape[0]):
        blk = res_t[:, j * LANES:(j + 1) * LANES]
        if scale is not None:
            blk = blk * scale
        ref[j] = blk.astype(ref.dtype)


def _conv_kernel(u_ref, up_ref, un_ref, cw_ref, cb_ref, lg_ref, lb_ref, yn_ref, ubuf, shifted,
                 *, tiles_per_seq):
    i = pl.program_id(0)
    tm = u_ref.shape[0]
    first = (i % tiles_per_seq) == 0
    last = (i % tiles_per_seq) == tiles_per_seq - 1
    ubuf[0:HALO, :] = jnp.where(first, 0.0, up_ref[...].astype(F32))
    for r0 in range(0, tm, CONV_CHUNK):
        ubuf[HALO + r0:HALO + r0 + CONV_CHUNK, :] = u_ref[r0:r0 + CONV_CHUNK, :].astype(F32)
    ubuf[HALO + tm:, :] = jnp.where(last, 0.0, un_ref[...].astype(F32))

    span = shifted.shape[1]
    for ph in range(SUBLANES):
        for r0 in range(0, span, CONV_CHUNK):
            n = min(CONV_CHUNK, span - r0)
            shifted[ph, r0:r0 + n, :] = ubuf[ph + r0:ph + r0 + n, :]

    base = HALO - CONV_WIDTH // 2
    inv_c = 1.0 / D_CONV
    for tc in range(tm // CONV_CHUNK):
        accs = []
        for lc in range(D_CONV // LANES):
            ls = slice(lc * LANES, (lc + 1) * LANES)
            acc = jnp.zeros((CONV_CHUNK, LANES), F32)
            for j in range(CONV_WIDTH):
                tiles, ph = divmod(base + j, SUBLANES)
                lo = tc * CONV_CHUNK + tiles * SUBLANES
                acc = acc + shifted[ph, lo:lo + CONV_CHUNK, ls] * cw_ref[j:j + 1, ls]
            accs.append(acc + cb_ref[:, ls])
        mu = sum(jnp.sum(a, axis=-1, keepdims=True) for a in accs) * inv_c
        cen = [a - mu for a in accs]
        var = sum(jnp.sum(c * c, axis=-1, keepdims=True) for c in cen) * inv_c
        rstd = lax.rsqrt(var + NORM_EPS)
        for lc, c in enumerate(cen):
            ls = slice(lc * LANES, (lc + 1) * LANES)
            yn = c * rstd * lg_ref[:, ls] + lb_ref[:, ls]
            yn_ref[tc * CONV_CHUNK:(tc + 1) * CONV_CHUNK, ls] = (yn * _sigmoid(yn)).astype(yn_ref.dtype)


def _conv_call(u, cw, cb, lg, lb, seq):
    t = u.shape[0]
    tm = TM_CONV
    tpb = seq // tm
    hb = tm // HALO
    n_halo = t // HALO
    row = lambda i: (i, 0)
    const = lambda i: (0, 0)
    return pl.pallas_call(
        functools.partial(_conv_kernel, tiles_per_seq=tpb),
        grid=(t // tm,),
        in_specs=[
            pl.BlockSpec((tm, D_CONV), row),
            pl.BlockSpec((HALO, D_CONV), lambda i: (jnp.maximum(i * hb - 1, 0), 0)),
            pl.BlockSpec((HALO, D_CONV), lambda i: (jnp.minimum((i + 1) * hb, n_halo - 1), 0)),
            pl.BlockSpec(cw.shape, const),
            pl.BlockSpec((1, D_CONV), const),
            pl.BlockSpec((1, D_CONV), const),
            pl.BlockSpec((1, D_CONV), const),
        ],
        out_specs=pl.BlockSpec((tm, D_CONV), row),
        out_shape=jax.ShapeDtypeStruct((t, D_CONV), BF16),
        scratch_shapes=[
            pltpu.VMEM((tm + 2 * HALO, D_CONV), F32),
            pltpu.VMEM((SUBLANES, tm + 2 * HALO - SUBLANES, D_CONV), F32),
        ],
        compiler_params=pltpu.CompilerParams(dimension_semantics=("arbitrary",), vmem_limit_bytes=VMEM_LIMIT),
        name="conv",
    )(u, u, u, cw, cb, lg, lb)


def _proj_kernel(x_ref, mod_ref, g_ref, wqv_ref, w_ref, qt_ref, k_ref, vt_ref, u_ref, ga_ref, gb_ref):
    d = x_ref.shape[1]
    mod = mod_ref[0]
    h = _norm_mod(x_ref[...], g_ref[...], mod[0:1], mod[1:2]).astype(BF16)

    qv_t = lax.dot_general(wqv_ref[...], h, NT_DIMS, preferred_element_type=F32)
    _store_transposed(qv_t[:D_ATT], qt_ref, SCORE_SCALE)
    _store_transposed(qv_t[D_ATT:], vt_ref)

    def seg(lo, hi):
        return jnp.dot(h, w_ref[:, lo:hi], preferred_element_type=F32)

    k_ref[...] = seg(0, D_ATT).astype(BF16)
    a = seg(D_ATT, D_ATT + D_CONV)
    g = seg(D_ATT + D_CONV, D_ATT + 2 * D_CONV)
    u_ref[...] = (a * _sigmoid(g)).astype(BF16)
    g0 = D_ATT + 2 * D_CONV
    ga_ref[...] = _sigmoid(seg(g0, g0 + d)).astype(BF16)
    gb_ref[...] = _sigmoid(seg(g0 + d, g0 + 2 * d)).astype(BF16)


def _proj_call(x2, mod3, g, wqv_t, w_rest, seq):
    t, d = x2.shape
    tm = TM_PROJ
    tpb = seq // tm
    nblk = tm // LANES
    outs = [
        jax.ShapeDtypeStruct((t // LANES, D_ATT, LANES), BF16),
        jax.ShapeDtypeStruct((t, D_ATT), BF16),
        jax.ShapeDtypeStruct((t // LANES, D_ATT, LANES), BF16),
        jax.ShapeDtypeStruct((t, D_CONV), BF16),
        jax.ShapeDtypeStruct((t, d), BF16),
        jax.ShapeDtypeStruct((t, d), BF16),
    ]
    row = lambda i: (i, 0)
    blk3 = lambda i: (i, 0, 0)
    return pl.pallas_call(
        _proj_kernel,
        grid=(t // tm,),
        in_specs=[
            pl.BlockSpec((tm, d), row),
            pl.BlockSpec((1, 6, d), lambda i: (i // tpb, 0, 0)),
            pl.BlockSpec((1, d), lambda i: (0, 0)),
            pl.BlockSpec(wqv_t.shape, lambda i: (0, 0), pipeline_mode=pl.Buffered(1)),
            pl.BlockSpec(w_rest.shape, lambda i: (0, 0), pipeline_mode=pl.Buffered(1)),
        ],
        out_specs=[
            pl.BlockSpec((nblk, D_ATT, LANES), blk3),
            pl.BlockSpec((tm, D_ATT), row),
            pl.BlockSpec((nblk, D_ATT, LANES), blk3),
            pl.BlockSpec((tm, D_CONV), row),
            pl.BlockSpec((tm, d), row),
            pl.BlockSpec((tm, d), row),
        ],
        out_shape=outs,
        compiler_params=pltpu.CompilerParams(dimension_semantics=("arbitrary",), vmem_limit_bytes=VMEM_LIMIT),
        name="proj",
    )(x2, mod3, g, wqv_t, w_rest)


def _ctx_kv_kernel(x_ref, mod_ref, g_ref, wk_ref, wvt_ref, k_ref, vt_ref):
    mod = mod_ref[0]
    h = _norm_mod(x_ref[...], g_ref[...], mod[0:1], mod[1:2]).astype(BF16)
    k_ref[...] = jnp.dot(h, wk_ref[...], preferred_element_type=F32).astype(BF16)
    _store_transposed(lax.dot_general(wvt_ref[...], h, NT_DIMS, preferred_element_type=F32), vt_ref)


def _ctx_kv_call(c2, mod3, g, w_k, w_vt):
    t, d = c2.shape
    tm = 512
    nblk = tm // LANES
    row = lambda i: (i, 0)
    return pl.pallas_call(
        _ctx_kv_kernel,
        grid=(t // tm,),
        in_specs=[
            pl.BlockSpec((tm, d), row),
            pl.BlockSpec((1, 6, d), lambda i: (0, 0, 0)),
            pl.BlockSpec((1, d), lambda i: (0, 0)),
            pl.BlockSpec(w_k.shape, lambda i: (0, 0)),
            pl.BlockSpec(w_vt.shape, lambda i: (0, 0)),
        ],
        out_specs=[pl.BlockSpec((tm, D_ATT), row), pl.BlockSpec((nblk, D_ATT, LANES), lambda i: (i, 0, 0))],
        out_shape=[jax.ShapeDtypeStruct((t, D_ATT), BF16), jax.ShapeDtypeStruct((t // LANES, D_ATT, LANES), BF16)],
        compiler_params=pltpu.CompilerParams(dimension_semantics=("arbitrary",), vmem_limit_bytes=VMEM_LIMIT),
        name="ctx_kv",
    )(c2, mod3, g, w_k, w_vt)


HEADS_PER_GROUP = 4
GROUP_W = HEADS_PER_GROUP * HEAD_DIM
N_HEAD_GROUPS = N_HEADS // HEADS_PER_GROUP


def _attn_kernel(qt_ref, k_ref, vt_ref, kc_ref, vct_ref, bias_ref, o_ref, vboth):
    nblk = qt_ref.shape[0]
    rows = 2 * nblk
    n_loc = WIN_ROWS * GRID_W
    n_ctx_blk = vct_ref.shape[0]
    half = GRID_W

    for j in range(nblk):
        vboth[0, j] = vt_ref[j]
    for j in range(nblk - 1):
        vboth[1, j] = jnp.concatenate([vt_ref[j][:, half:], vt_ref[j + 1][:, :half]], axis=1)
    vboth[1, nblk - 1] = jnp.zeros_like(vt_ref[0])

    lane = lax.broadcasted_iota(jnp.int32, (D_ATT, LANES), 1)
    low = lane < half
    rblk = lax.broadcasted_iota(jnp.int32, (GROUP_W, GROUP_W), 0) // HEAD_DIM
    cblk = lax.broadcasted_iota(jnp.int32, (GROUP_W, GROUP_W), 1) // HEAD_DIM
    diag = rblk == cblk
    low64 = lax.broadcasted_iota(jnp.int32, (HEAD_DIM, LANES), 1) < half
    kc = kc_ref[...]

    def one_row(row, tiled, side):
        r_start = jnp.clip(row - WIN_ROWS // 2, 0, rows - WIN_ROWS)
        b_off = pl.multiple_of((WIN_ROWS - 1 - (row - r_start)) * GRID_W, GRID_W)
        kw = k_ref[pl.ds(pl.multiple_of(r_start * GRID_W, GRID_W), n_loc), :]
        vwin = vboth[r_start % 2, pl.ds(r_start // 2, n_loc // LANES)]
        parts = []
        for g in range(N_HEAD_GROUPS):
            fs = slice(g * GROUP_W, (g + 1) * GROUP_W)
            tg = tiled[fs, :]
            w = jnp.where(diag, jnp.concatenate([tg, tg], axis=1), jnp.zeros((), BF16))
            s = jnp.dot(jnp.concatenate([kw[:, fs], kc[:, fs]], axis=0), w, preferred_element_type=F32)
            s_loc = s[:n_loc] + bias_ref[g, pl.ds(b_off, n_loc), :]
            s_ctx = s[n_loc:]
            m = jnp.maximum(jnp.max(s_loc, axis=0, keepdims=True), jnp.max(s_ctx, axis=0, keepdims=True))
            p_loc = jnp.exp2(s_loc - m)
            p_ctx = jnp.exp2(s_ctx - m)
            l = jnp.sum(p_loc, axis=0, keepdims=True) + jnp.sum(p_ctx, axis=0, keepdims=True)
            p = jnp.concatenate([p_loc.astype(BF16), p_ctx.astype(BF16)], axis=0)
            vt_g = jnp.concatenate([vwin[j][fs, :] for j in range(n_loc // LANES)]
                                   + [vct_ref[j][fs, :] for j in range(n_ctx_blk)], axis=1)
            o_t = jnp.dot(vt_g, p, preferred_element_type=F32) * (1.0 / l)
            for j in range(HEADS_PER_GROUP):
                blk = o_t[j * HEAD_DIM:(j + 1) * HEAD_DIM, (j // 2) * LANES:(j // 2 + 1) * LANES]
                if j % 2 != side:
                    blk = pltpu.roll(blk, half, axis=1)
                parts.append(blk)
        return parts

    def body(i, carry):
        xq = qt_ref[i]
        rolled = jnp.concatenate([xq[:, half:], xq[:, :half]], axis=1)
        parts_a = one_row(2 * i, jnp.where(low, xq, rolled), 0)
        parts_b = one_row(2 * i + 1, jnp.where(low, rolled, xq), 1)
        out = jnp.concatenate([jnp.where(low64, a, b) for a, b in zip(parts_a, parts_b)], axis=0)
        o_ref[i] = out.astype(o_ref.dtype)
        return carry

    lax.fori_loop(0, nblk, body, 0, unroll=8)


def _attn_call(qt, k, vt, kc, vct, bias, seq, n_ctx):
    t = k.shape[0]
    b = t // seq
    nblk = seq // LANES
    lat3 = pl.BlockSpec((nblk, D_ATT, LANES), lambda i: (i, 0, 0))
    return pl.pallas_call(
        _attn_kernel,
        grid=(b,),
        in_specs=[lat3,
                  pl.BlockSpec((seq, D_ATT), lambda i: (i, 0)),
                  lat3,
                  pl.BlockSpec((n_ctx, D_ATT), lambda i: (i, 0)),
                  pl.BlockSpec((n_ctx // LANES, D_ATT, LANES), lambda i: (i, 0, 0)),
                  pl.BlockSpec(bias.shape, lambda i: (0, 0, 0))],
        out_specs=lat3,
        out_shape=jax.ShapeDtypeStruct((t // LANES, D_ATT, LANES), BF16),
        scratch_shapes=[pltpu.VMEM((2, nblk, D_ATT, LANES), BF16)],
        compiler_params=pltpu.CompilerParams(dimension_semantics=("arbitrary",), vmem_limit_bytes=VMEM_LIMIT),
        name="attn",
    )(qt, k, vt, kc, vct, bias)


def _bias_table(rpb):
    cq = np.arange(GRID_W)[:, None]
    ck = np.arange(GRID_W)[None, :]
    c_start = np.clip(cq - WIN_COLS // 2, 0, GRID_W - WIN_COLS)
    col_mask = (ck >= c_start) & (ck < c_start + WIN_COLS)
    n_dc = rpb.shape[2]
    lead = GRID_W - WIN_COLS
    w = jnp.pad(rpb, ((0, 0), (0, 0), (lead, 2 * GRID_W - lead - n_dc)))
    skew = jnp.tile(w, (1, 1, GRID_W + 1))[:, :, :GRID_W * (2 * GRID_W + 1)]
    tab = skew.reshape(rpb.shape[0], rpb.shape[1], GRID_W, 2 * GRID_W + 1)[:, :, ::-1, :GRID_W]
    tab = jnp.where(col_mask[None, None], tab, NEG_INF)
    n_dr = tab.shape[1]
    tab = tab.reshape(N_HEAD_GROUPS, HEADS_PER_GROUP, n_dr, GRID_W, GRID_W)
    tab = tab.transpose(0, 2, 4, 1, 3)
    return (tab.reshape(N_HEAD_GROUPS, n_dr * GRID_W, GROUP_W) * LOG2_E).astype(F32)


def _merge_rows(r0, n, att_ref, yn_ref, ga_ref, gb_ref, x_ref, mod,
                wa_ref, wc_ref, wo_ref, nf_ref, wr_ref, br_ref, xmid_ref, hp_ref):
    d = x_ref.shape[1]
    rs = slice(r0, r0 + n)
    y_conv = jnp.dot(yn_ref[rs, :], wc_ref[...], preferred_element_type=F32)

    att = jnp.concatenate([att_ref[j].astype(F32).T for j in range(r0 // LANES, (r0 + n) // LANES)], axis=0)
    y_att = jnp.dot(att.astype(BF16), wa_ref[...], preferred_element_type=F32)
    mix = ga_ref[rs, :].astype(F32) * y_att + gb_ref[rs, :].astype(F32) * y_conv
    y = jnp.dot(mix.astype(BF16), wo_ref[...], preferred_element_type=F32)
    x_mid = x_ref[rs, :] + mod[2:3] * y
    xmid_ref[rs, :] = x_mid
    h2 = _norm_mod(x_mid, nf_ref[...], mod[3:4], mod[4:5])

    h2_hi = h2.astype(BF16)
    h2_lo = (h2 - h2_hi.astype(F32)).astype(BF16)
    t_hi = jnp.dot(h2_hi, wr_ref[...], preferred_element_type=F32)
    t_lo = jnp.dot(h2_lo, wr_ref[:, :LANES], preferred_element_type=F32)
    logits = t_hi[:, :LANES] + t_hi[:, LANES:] + t_lo + br_ref[...]
    lt = logits.T
    sub = lax.broadcasted_iota(jnp.int32, (SUBLANES, n), 0).astype(F32)
    big = float(LANES)
    is_g = sub < N_GROUPS
    gl = jnp.where(is_g, lt[0:SUBLANES], -jnp.inf)
    gmax = jnp.max(gl, axis=0, keepdims=True)
    g_idx = jnp.min(jnp.where(gl == gmax, sub, big), axis=0, keepdims=True)
    p_group = 1.0 / jnp.sum(jnp.where(is_g, jnp.exp(gl - gmax), 0.0), axis=0, keepdims=True)
    el = lt[ROUTER_E0:ROUTER_E0 + EXPERTS_PER_GROUP]
    for g in range(1, N_GROUPS):
        lo_g = ROUTER_E0 + g * EXPERTS_PER_GROUP
        el = jnp.where(g_idx == float(g), lt[lo_g:lo_g + EXPERTS_PER_GROUP], el)
    v1 = jnp.max(el, axis=0, keepdims=True)
    j1 = jnp.min(jnp.where(el == v1, sub, big), axis=0, keepdims=True)
    el2 = jnp.where(sub == j1, -jnp.inf, el)
    v2 = jnp.max(el2, axis=0, keepdims=True)
    j2 = jnp.min(jnp.where(el2 == v2, sub, big), axis=0, keepdims=True)
    e2 = jnp.exp(v2 - v1)
    gate1 = p_group / (1.0 + e2)
    gate2 = p_group * e2 / (1.0 + e2)
    ja = jnp.minimum(j1, j2)
    jb = jnp.maximum(j1, j2)
    gate_a = jnp.where(j1 < j2, gate1, gate2)
    gate_b = jnp.where(j1 < j2, gate2, gate1)
    pair = ja * (2 * EXPERTS_PER_GROUP - 1 - ja) * 0.5 + (jb - ja - 1.0)
    cls = g_idx * PAIRS_PER_GROUP + pair
    gate_rows = jnp.where(sub == 0.0, gate_a, jnp.where(sub == 1.0, gate_b, 0.0))
    gates = jnp.concatenate([gate_rows, jnp.zeros((LANES - SUBLANES, n), F32)], axis=0).T

    half = d // 2
    hi = lax.bitcast_convert_type(h2[:, :half].astype(BF16).astype(F32), U32)
    lo = lax.bitcast_convert_type(h2[:, half:].astype(BF16).astype(F32), U32)
    words = hi | (lo >> 16)
    n_words = half // LANES
    base = r0 * ROW_TILE
    for c in range(n_words):
        hp_ref[pl.ds(base + c, n, stride=ROW_TILE), :] = words[:, c * LANES:(c + 1) * LANES]
    hp_ref[pl.ds(base + n_words, n, stride=ROW_TILE), :] = lax.bitcast_convert_type(gates, U32)
    zero = jnp.zeros((n, LANES), U32)
    for c in range(n_words + 1, ROW_TILE):
        hp_ref[pl.ds(base + c, n, stride=ROW_TILE), :] = zero
    return cls


def _merge_kernel(att_ref, yn_ref, ga_ref, gb_ref, x_ref, mod_ref,
                  wa_ref, wc_ref, wo_ref, nf_ref, wr_ref, br_ref, tri_ref,
                  xmid_ref, hp_ref, info_ref, cnt_ref, hs_hbm, run_ref, zbuf, zsem, *, n_steps):
    i = pl.program_id(0)
    tm = x_ref.shape[0]

    @pl.when(i == 0)
    def _():
        run_ref[...] = jnp.zeros_like(run_ref)
        zbuf[...] = jnp.zeros_like(zbuf)

    blk_rows = zbuf.shape[0]
    n_blocks = hs_hbm.shape[0] // blk_rows
    per_step = -(-n_blocks // n_steps)

    def clear_copies(fn):
        for k in range(per_step):
            blk = i * per_step + k

            @pl.when(blk < n_blocks)
            def _():
                start = pl.multiple_of(blk * blk_rows, blk_rows)
                fn(pltpu.make_async_copy(zbuf, hs_hbm.at[pl.ds(start, blk_rows)], zsem))

    clear_copies(lambda cp: cp.start())

    mod = mod_ref[0]
    n = tm // MERGE_PARTS
    cls = jnp.concatenate(
        [_merge_rows(p * n, n, att_ref, yn_ref, ga_ref, gb_ref, x_ref, mod, wa_ref, wc_ref, wo_ref, nf_ref,
                     wr_ref, br_ref, xmid_ref, hp_ref) for p in range(MERGE_PARTS)], axis=1)

    crow = lax.broadcasted_iota(jnp.int32, (LANES, tm), 0).astype(F32)
    oh_f = jnp.where(crow == cls, 1.0, 0.0)
    before = jnp.dot(oh_f.astype(BF16), tri_ref[...], preferred_element_type=F32)
    run = run_ref[:, 0:1]
    rank = jnp.sum(oh_f * (before + run), axis=0, keepdims=True)
    new_run = run + jnp.sum(oh_f, axis=1, keepdims=True)
    run_ref[...] = jnp.broadcast_to(new_run, run_ref.shape)
    cnt_ref[...] = jnp.broadcast_to(new_run, cnt_ref.shape)

    sub = lax.broadcasted_iota(jnp.int32, (SUBLANES, tm), 0)
    info_ref[...] = jnp.where(sub == 0, cls, jnp.where(sub == 1, rank, 0.0))

    clear_copies(lambda cp: cp.wait())


def _merge_call(att, yn, ga, gb, x2, mod3, wa, wc, wo, nf, wr, br, seq):
    t, d = x2.shape
    tm = TM_MERGE
    tpb = seq // tm
    tri = jnp.asarray(np.triu(np.ones((tm, tm), np.float32), 1), BF16)
    row = lambda i: (i, 0)
    const = lambda i: (0, 0)
    n_steps = t // tm
    block_rows = MOE_BLOCK * ROW_TILE
    return pl.pallas_call(
        functools.partial(_merge_kernel, n_steps=n_steps),
        grid=(n_steps,),
        in_specs=[
            pl.BlockSpec((tm // LANES, D_ATT, LANES), lambda i: (i, 0, 0)),
            pl.BlockSpec((tm, D_CONV), row),
            pl.BlockSpec((tm, d), row),
            pl.BlockSpec((tm, d), row),
            pl.BlockSpec((tm, d), row),
            pl.BlockSpec((1, 6, d), lambda i: (i // tpb, 0, 0)),
            pl.BlockSpec(wa.shape, const),
            pl.BlockSpec(wc.shape, const),
            pl.BlockSpec(wo.shape, const),
            pl.BlockSpec((1, d), const),
            pl.BlockSpec(wr.shape, const),
            pl.BlockSpec((1, LANES), const),
            pl.BlockSpec((tm, tm), const),
        ],
        out_specs=[
            pl.BlockSpec((tm, d), row),
            pl.BlockSpec((tm * ROW_TILE, LANES), row),
            pl.BlockSpec((SUBLANES, tm), lambda i: (0, i)),
            pl.BlockSpec((LANES, LANES), const),
            pl.BlockSpec(memory_space=pl.ANY),
        ],
        out_shape=[
            jax.ShapeDtypeStruct((t, d), F32),
            jax.ShapeDtypeStruct((t * ROW_TILE, LANES), U32),
            jax.ShapeDtypeStruct((SUBLANES, t), F32),
            jax.ShapeDtypeStruct((LANES, LANES), F32),
            jax.ShapeDtypeStruct((_num_moe_blocks(t) * block_rows, LANES), U32),
        ],
        scratch_shapes=[pltpu.VMEM((LANES, LANES), F32), pltpu.VMEM((block_rows, LANES), U32),
                        pltpu.SemaphoreType.DMA(())],
        compiler_params=pltpu.CompilerParams(dimension_semantics=("arbitrary",), vmem_limit_bytes=VMEM_LIMIT),
        name="merge",
    )(att, yn, ga, gb, x2, mod3, wa, wc, wo, nf, wr, br, tri)


def _dispatch_kernel(dest_ref, hp_ref, cleared_hbm, hs_hbm, sem):
    del cleared_hbm
    n = dest_ref.shape[0]

    def body(o, carry):
        for k in range(DMA_UNROLL):
            t = o * DMA_UNROLL + k
            src = pl.multiple_of(t * ROW_TILE, ROW_TILE)
            dst = pl.multiple_of(dest_ref[t], ROW_TILE)
            pltpu.make_async_copy(hp_ref.at[pl.ds(src, ROW_TILE)], hs_hbm.at[pl.ds(dst, ROW_TILE)],
                                  sem).start(priority=k % 2)
        return carry

    lax.fori_loop(0, n // DMA_UNROLL, body, 0)
    pltpu.make_async_copy(hp_ref, hs_hbm.at[pl.ds(0, n * ROW_TILE)], sem).wait()


def _dispatch_call(dest, hp, cleared):
    t = dest.shape[0]
    return pl.pallas_call(
        _dispatch_kernel,
        grid=(t // TM_DISPATCH,),
        in_specs=[
            pl.BlockSpec((TM_DISPATCH,), lambda i: (i,), memory_space=pltpu.SMEM),
            pl.BlockSpec((TM_DISPATCH * ROW_TILE, LANES), lambda i: (i, 0)),
            pl.BlockSpec(memory_space=pl.ANY),
        ],
        out_specs=pl.BlockSpec(memory_space=pl.ANY),
        out_shape=jax.ShapeDtypeStruct(cleared.shape, cleared.dtype),
        scratch_shapes=[pltpu.SemaphoreType.DMA(())],
        input_output_aliases={2: 0},
        compiler_params=pltpu.CompilerParams(dimension_semantics=("arbitrary",), has_side_effects=True,
                                             vmem_limit_bytes=VMEM_LIMIT),
        name="dispatch",
    )(dest, hp, cleared)


def _moe_kernel(rows_ref, seg_ref, first_ref, fetch_ref, pre_ref, hs_ref, wg_hbm, wu_hbm, wd_hbm, y_ref,
                wg_buf, wu_buf, wd_buf, sems):
    base = pl.program_id(0) * MOE_BLOCKS_PER_STEP
    rows = hs_ref.shape[0] // MOE_BLOCKS_PER_STEP
    blk = rows // ROW_TILE
    d = wg_buf.shape[2]
    n_words = d // 2 // LANES

    def copies(side, expert, slot):
        return [pltpu.make_async_copy(src.at[expert], buf.at[side, slot], sems.at[side, slot])
                for src, buf in ((wg_hbm, wg_buf), (wu_hbm, wu_buf), (wd_hbm, wd_buf))]

    for sub in range(MOE_BLOCKS_PER_STEP):
        b = base + sub
        for side in range(2):
            @pl.when(b == 0)
            def _():
                for s in range(2):
                    @pl.when(pre_ref[side, s] >= 0)
                    def _():
                        for cp in copies(side, pre_ref[side, s], s):
                            cp.start()

            @pl.when(fetch_ref[side, b] >= 0)
            def _():
                for cp in copies(side, fetch_ref[side, b], (seg_ref[side, b] + 2) % WEIGHT_SLOTS):
                    cp.start()

        for side in range(2):
            @pl.when(first_ref[side, b] == 1)
            def _():
                for cp in copies(side, 0, seg_ref[side, b] % WEIGHT_SLOTS):
                    cp.wait()

    def run(sub):
        b = base + sub
        off = sub * rows
        slot_a = seg_ref[0, b] % WEIGHT_SLOTS
        slot_b = seg_ref[1, b] % WEIGHT_SLOTS
        his, los = [], []
        for c in range(n_words):
            w = hs_ref[pl.ds(off + c, blk, stride=ROW_TILE), :]
            his.append(lax.bitcast_convert_type(w & jnp.uint32(0xFFFF0000), F32).astype(BF16))
            los.append(lax.bitcast_convert_type(w << 16, F32).astype(BF16))
        x = jnp.concatenate(his + los, axis=-1)
        gates = lax.bitcast_convert_type(hs_ref[pl.ds(off + n_words, blk, stride=ROW_TILE), :], F32)

        def mlp(side, slot):
            g = jnp.dot(x, wg_buf[side, slot], preferred_element_type=F32)
            u = jnp.dot(x, wu_buf[side, slot], preferred_element_type=F32)
            a = (g * _sigmoid(g) * u).astype(BF16)
            return jnp.dot(a, wd_buf[side, slot], preferred_element_type=F32)

        y = gates[:, 0:1] * mlp(0, slot_a) + gates[:, 1:2] * mlp(1, slot_b)
        for c in range(d // LANES):
            y_ref[pl.ds(off + c, blk, stride=ROW_TILE), :] = y[:, c * LANES:(c + 1) * LANES]

    used = [rows_ref[base + sub] > 0 for sub in range(MOE_BLOCKS_PER_STEP)]
    all_used = functools.reduce(jnp.logical_and, used)

    @pl.when(all_used)
    def _():
        for sub in range(MOE_BLOCKS_PER_STEP):
            run(sub)

    for sub in range(MOE_BLOCKS_PER_STEP):
        @pl.when(jnp.logical_not(all_used) & used[sub])
        def _():
            run(sub)

        @pl.when(jnp.logical_not(used[sub]))
        def _():
            y_ref[sub * rows:(sub + 1) * rows, :] = jnp.zeros((rows, LANES), y_ref.dtype)


def _weight_schedule(blk_e, blk_rows):
    nb = blk_e.shape[0]
    valid = blk_rows > 0
    prev = jnp.concatenate([jnp.full((1,), -1, jnp.int32), blk_e[:-1]])
    first = valid & (blk_e != prev)
    seg = jnp.cumsum(first.astype(jnp.int32)) - 1
    n_seg = jnp.sum(first.astype(jnp.int32))
    seg_e = jnp.full((nb + 3,), -1, jnp.int32).at[jnp.where(first, seg, nb + 2)].set(blk_e)
    seg_e = seg_e.at[nb + 2].set(-1)
    fetch = jnp.where(first & (seg + 2 < n_seg), seg_e[jnp.clip(seg + 2, 0, nb + 1)], -1)
    return jnp.maximum(seg, 0), first.astype(jnp.int32), fetch.astype(jnp.int32), seg_e[:2]


def _moe_call(blk_ea, blk_eb, blk_rows, hs, wg, wu, wd):
    nb = blk_ea.shape[0]
    d = wg.shape[1]
    de = wg.shape[2]
    rows = MOE_BLOCKS_PER_STEP * MOE_BLOCK * ROW_TILE
    sched = [_weight_schedule(e, blk_rows) for e in (blk_ea, blk_eb)]
    seg, first, fetch, pre = (jnp.stack([s[k] for s in sched]) for k in range(4))
    tok = pl.BlockSpec((rows, LANES), lambda i, *_: (i, 0))
    hbm = pl.BlockSpec(memory_space=pl.ANY)
    grid_spec = pltpu.PrefetchScalarGridSpec(
        num_scalar_prefetch=5,
        grid=(nb // MOE_BLOCKS_PER_STEP,),
        in_specs=[tok, hbm, hbm, hbm],
        out_specs=tok,
        scratch_shapes=[
            pltpu.VMEM((2, WEIGHT_SLOTS, d, de), BF16),
            pltpu.VMEM((2, WEIGHT_SLOTS, d, de), BF16),
            pltpu.VMEM((2, WEIGHT_SLOTS, de, d), BF16),
            pltpu.SemaphoreType.DMA((2, WEIGHT_SLOTS)),
        ],
    )
    return pl.pallas_call(
        _moe_kernel,
        grid_spec=grid_spec,
        out_shape=jax.ShapeDtypeStruct((nb * MOE_BLOCK * ROW_TILE, LANES), F32),
        compiler_params=pltpu.CompilerParams(dimension_semantics=("arbitrary",), vmem_limit_bytes=VMEM_LIMIT),
        name="moe",
    )(blk_rows, seg, first, fetch, pre, hs, wg, wu, wd)


def _final_kernel(dest_ref, dest_next_ref, xmid_ref, mod_ref, fn_ref, y_hbm, o_ref, fbuf, sems):
    i = pl.program_id(0)
    n_steps = pl.num_programs(0)
    n = dest_ref.shape[0]
    d = xmid_ref.shape[1]

    def issue(idx_ref, slot):
        def body(o, carry):
            for k in range(DMA_UNROLL):
                t = o * DMA_UNROLL + k
                src = pl.multiple_of(idx_ref[t], ROW_TILE)
                dst = pl.multiple_of(t * ROW_TILE, ROW_TILE)
                pltpu.make_async_copy(y_hbm.at[pl.ds(src, ROW_TILE)], fbuf.at[slot, pl.ds(dst, ROW_TILE)],
                                      sems.at[slot]).start(priority=k % 2)
            return carry

        lax.fori_loop(0, n // DMA_UNROLL, body, 0)

    slot = i % 2

    @pl.when(i == 0)
    def _():
        issue(dest_ref, 0)

    for s in range(2):
        @pl.when((i + 1 < n_steps) & (slot == s))
        def _():
            issue(dest_next_ref, 1 - s)

    pltpu.make_async_copy(y_hbm.at[pl.ds(0, n * ROW_TILE)], fbuf.at[slot], sems.at[slot]).wait()
    f = jnp.concatenate([fbuf[slot, pl.ds(c, n, stride=ROW_TILE), :] for c in range(d // LANES)], axis=-1)
    x = xmid_ref[...] + mod_ref[0][5:6] * f
    ms = jnp.mean(x * x, axis=-1, keepdims=True)
    o_ref[...] = x * lax.rsqrt(ms + NORM_EPS) * fn_ref[...]


def _final_call(dest, xmid, mod3, fn, y, seq):
    t, d = xmid.shape
    tm = TM_ROWS
    tpb = seq // tm
    n_steps = t // tm
    return pl.pallas_call(
        _final_kernel,
        grid=(n_steps,),
        in_specs=[
            pl.BlockSpec((tm,), lambda i: (i,), memory_space=pltpu.SMEM),
            pl.BlockSpec((tm,), lambda i: (jnp.minimum(i + 1, n_steps - 1),), memory_space=pltpu.SMEM),
            pl.BlockSpec((tm, d), lambda i: (i, 0)),
            pl.BlockSpec((1, 6, d), lambda i: (i // tpb, 0, 0)),
            pl.BlockSpec((1, d), lambda i: (0, 0)),
            pl.BlockSpec(memory_space=pl.ANY),
        ],
        out_specs=pl.BlockSpec((tm, d), lambda i: (i, 0)),
        out_shape=jax.ShapeDtypeStruct((t, d), F32),
        scratch_shapes=[pltpu.VMEM((2, tm * ROW_TILE, LANES), F32), pltpu.SemaphoreType.DMA((2,))],
        compiler_params=pltpu.CompilerParams(dimension_semantics=("arbitrary",), vmem_limit_bytes=VMEM_LIMIT),
        name="final",
    )(dest, dest, xmid, mod3, fn, y)


def _pair_tables():
    ea, eb = [], []
    for g in range(N_GROUPS):
        for a in range(EXPERTS_PER_GROUP):
            for b in range(a + 1, EXPERTS_PER_GROUP):
                ea.append(g * EXPERTS_PER_GROUP + a)
                eb.append(g * EXPERTS_PER_GROUP + b)
    return np.asarray(ea, np.int32), np.asarray(eb, np.int32)


def _num_moe_blocks(t):
    return -(-(t // MOE_BLOCK + N_CLASSES) // MOE_BLOCKS_PER_STEP) * MOE_BLOCKS_PER_STEP


def _routing_plan(info, counts, t):
    cls = info[0].astype(jnp.int32)
    rank = info[1].astype(jnp.int32)
    cnt = counts[:N_CLASSES, 0].astype(jnp.int32)
    padded = (cnt + MOE_BLOCK - 1) // MOE_BLOCK * MOE_BLOCK
    pad_end = jnp.cumsum(padded)
    pad_start = pad_end - padded
    cls_ids = jnp.arange(N_CLASSES, dtype=jnp.int32)
    dest = rank + jnp.sum(jnp.where(cls[:, None] == cls_ids[None, :], pad_start[None, :], 0), axis=1)
    nb = _num_moe_blocks(t)
    nb_used = pad_end[-1] // MOE_BLOCK
    blk = jnp.arange(nb, dtype=jnp.int32)
    blk_cls = jnp.sum(pad_end[None, :] <= (blk * MOE_BLOCK)[:, None], axis=1)
    blk_cls = jnp.clip(blk_cls, 0, N_CLASSES - 1)
    valid = blk < nb_used
    last_cls = blk_cls[jnp.maximum(nb_used - 1, 0)]
    blk_cls = jnp.where(valid, blk_cls, last_cls).astype(jnp.int32)
    tab_a, tab_b = _pair_tables()
    blk_ea = jnp.asarray(tab_a)[blk_cls]
    blk_eb = jnp.asarray(tab_b)[blk_cls]
    blk_rows = jnp.clip(cnt[blk_cls] - (blk * MOE_BLOCK - pad_start[blk_cls]), 0, MOE_BLOCK)
    blk_rows = jnp.where(valid, blk_rows, 0).astype(jnp.int32)
    return dest, blk_ea, blk_eb, blk_rows


def kernel(x, c, ctx, c_ctx, w_mod, b_mod, norm_mix, w_in, rpb, w_att_out, conv_w, conv_b, conv_ln_g, conv_ln_b,
           w_conv_out, w_o, norm_ffn, w_router_group, b_router_group, w_router_expert, b_router_expert,
           w_exp_gate, w_exp_up, w_exp_down, final_norm):
    b, seq, d = x.shape
    n_ctx = ctx.shape[1]
    t = b * seq
    assert w_mod.shape[0] == 1, "single layer"
    assert seq % TM_ROWS == 0 and t % TM_DISPATCH == 0 and seq % TM_CONV == 0 and seq // GRID_W >= WIN_ROWS and n_ctx % LANES == 0 and (b * n_ctx) % 512 == 0
    assert d == 1024

    mod_rows = -(-(b + 1) // SUBLANES) * SUBLANES
    cc = jnp.zeros((mod_rows, d), F32).at[:b].set(c).at[b].set(c_ctx)
    m_all = _mod_call(cc, w_mod[0], b_mod[0][None, :])
    mod_lat = m_all[:b].reshape(b, 6, d)
    mod_ctx = m_all[b:b + 1].reshape(1, 6, d)

    x2 = x.reshape(t, d)
    g_mix = norm_mix[0][None, :]
    w_in_b = w_in[0].astype(BF16)
    wqv_t = jnp.concatenate([w_in_b[:, Q0:K0], w_in_b[:, V0:GLU0]], axis=1).T
    w_rest = jnp.concatenate([w_in_b[:, K0:V0], w_in_b[:, GLU0:]], axis=1)
    cw = jnp.zeros((32, D_CONV), F32).at[:CONV_WIDTH].set(conv_w[0])
    qt, k, vt, u, ga, gb = _proj_call(x2, mod_lat, g_mix, wqv_t, w_rest, seq)
    yn = _conv_call(u, cw, conv_b[0][None, :], conv_ln_g[0][None, :], conv_ln_b[0][None, :], seq)
    kc, vct = _ctx_kv_call(ctx.reshape(b * n_ctx, d), mod_ctx, g_mix, w_in_b[:, K0:V0], w_in_b[:, V0:GLU0].T)

    att = _attn_call(qt, k, vt, kc, vct, _bias_table(rpb[0]), seq, n_ctx)

    wr = jnp.zeros((d, LANES), F32).at[:, :N_GROUPS].set(w_router_group[0])
    wr = wr.at[:, ROUTER_E0:ROUTER_E0 + N_EXPERTS].set(w_router_expert[0])
    wr_hi = wr.astype(BF16)
    wr = jnp.concatenate([wr_hi, (wr - wr_hi.astype(F32)).astype(BF16)], axis=1)
    br = jnp.zeros((1, LANES), F32).at[0, :N_GROUPS].set(b_router_group[0])
    br = br.at[0, ROUTER_E0:ROUTER_E0 + N_EXPERTS].set(b_router_expert[0])
    x_mid, hp, info, counts, cleared = _merge_call(
        att, yn, ga, gb, x2, mod_lat, w_att_out[0].astype(BF16), w_conv_out[0].astype(BF16), w_o[0].astype(BF16), norm_ffn[0][None, :],
        wr, br, seq)

    dest, blk_ea, blk_eb, blk_rows = _routing_plan(info, counts, t)
    dest_row = dest * ROW_TILE
    hs = _dispatch_call(dest_row, hp, cleared)
    y = _moe_call(blk_ea, blk_eb, blk_rows, hs,
                  w_exp_gate[0].astype(BF16), w_exp_up[0].astype(BF16), w_exp_down[0].astype(BF16))
    out = _final_call(dest_row, x_mid, mod_lat, final_norm[None, :], y, seq)
    return out.reshape(b, seq, d)
```

```python
import functools

import numpy as np
import jax
import jax.numpy as jnp
from jax import lax
from jax.experimental import pallas as pl
from jax.experimental.pallas import tpu as pltpu

F32 = jnp.float32
BF16 = jnp.bfloat16
U32 = jnp.uint32

GRID_W = 64
N_HEADS = 8
HEAD_DIM = 64
D_ATT = N_HEADS * HEAD_DIM
WIN_ROWS = 8
WIN_COLS = 16
D_CONV = 512
CONV_WIDTH = 31
N_GROUPS = 4
EXPERTS_PER_GROUP = 8
N_EXPERTS = N_GROUPS * EXPERTS_PER_GROUP
PAIRS_PER_GROUP = EXPERTS_PER_GROUP * (EXPERTS_PER_GROUP - 1) // 2
N_CLASSES = N_GROUPS * PAIRS_PER_GROUP
ROUTER_E0 = 8
NORM_EPS = 1e-6
NEG_INF = -1e30

LANES = 128
SUBLANES = 8
ROW_TILE = SUBLANES
HALO = 16
VMEM_LIMIT = 56 * 1024 * 1024

TM_PROJ = 1024
TM_MERGE = 512
MERGE_PARTS = 1
TM_ROWS = 1024
TM_DISPATCH = 2048
TM_CONV = 1024
MOE_BLOCK = 256
MOE_BLOCKS_PER_STEP = 2
WEIGHT_SLOTS = MOE_BLOCKS_PER_STEP + 2
CONV_CHUNK = 64
DMA_UNROLL = 8

HIGHEST = lax.Precision.HIGHEST


def _norm_mod(x, g, shift, scale):
    ms = jnp.mean(x * x, axis=-1, keepdims=True)
    y = x * lax.rsqrt(ms + NORM_EPS) * g
    return y * (1.0 + scale) + shift


def _sigmoid(x):
    return jax.nn.sigmoid(x)


def _mod_kernel(c_ref, w_ref, b_ref, o_ref):
    c = c_ref[...]
    s = c * _sigmoid(c)
    o_ref[...] = jnp.dot(s, w_ref[...], precision=HIGHEST, preferred_element_type=F32) + b_ref[...]


def _mod_call(cc, w_mod, b_mod):
    rows, d = cc.shape
    n = w_mod.shape[1]
    tn = 1024
    return pl.pallas_call(
        _mod_kernel,
        grid=(n // tn,),
        in_specs=[
            pl.BlockSpec((rows, d), lambda j: (0, 0)),
            pl.BlockSpec((d, tn), lambda j: (0, j)),
            pl.BlockSpec((1, tn), lambda j: (0, j)),
        ],
        out_specs=pl.BlockSpec((rows, tn), lambda j: (0, j)),
        out_shape=jax.ShapeDtypeStruct((rows, n), F32),
        compiler_params=pltpu.CompilerParams(dimension_semantics=("arbitrary",), vmem_limit_bytes=VMEM_LIMIT),
        name="mod",
    )(cc, w_mod, b_mod)


Q0, K0, V0, GLU0 = 0, D_ATT, 2 * D_ATT, 3 * D_ATT
GA0 = GLU0 + 2 * D_CONV


NT_DIMS = (((1,), (1,)), ((), ()))
LOG2_E = 1.4426950408889634
SCORE_SCALE = HEAD_DIM ** -0.5 * LOG2_E


def _store_transposed(res_t, ref, scale=None):
    for j in range(ref.shape[0]):
        blk = res_t[:, j * LANES:(j + 1) * LANES]
        if scale is not None:
            blk = blk * scale
        ref[j] = blk.astype(ref.dtype)


def _conv_kernel(u_ref, up_ref, un_ref, cw_ref, cb_ref, lg_ref, lb_ref, yn_ref, ubuf, shifted,
                 *, tiles_per_seq):
    i = pl.program_id(0)
    tm = u_ref.shape[0]
    first = (i % tiles_per_seq) == 0
    last = (i % tiles_per_seq) == tiles_per_seq - 1
    ubuf[0:HALO, :] = jnp.where(first, 0.0, up_ref[...].astype(F32))
    for r0 in range(0, tm, CONV_CHUNK):
        ubuf[HALO + r0:HALO + r0 + CONV_CHUNK, :] = u_ref[r0:r0 + CONV_CHUNK, :].astype(F32)
    ubuf[HALO + tm:, :] = jnp.where(last, 0.0, un_ref[...].astype(F32))

    span = shifted.shape[1]
    for ph in range(SUBLANES):
        for r0 in range(0, span, CONV_CHUNK):
            n = min(CONV_CHUNK, span - r0)
            shifted[ph, r0:r0 + n, :] = ubuf[ph + r0:ph + r0 + n, :]

    base = HALO - CONV_WIDTH // 2
    inv_c = 1.0 / D_CONV
    for tc in range(tm // CONV_CHUNK):
        accs = []
        for lc in range(D_CONV // LANES):
            ls = slice(lc * LANES, (lc + 1) * LANES)
            acc = jnp.zeros((CONV_CHUNK, LANES), F32)
            for j in range(CONV_WIDTH):
                tiles, ph = divmod(base + j, SUBLANES)
                lo = tc * CONV_CHUNK + tiles * SUBLANES
                acc = acc + shifted[ph, lo:lo + CONV_CHUNK, ls] * cw_ref[j:j + 1, ls]
            accs.append(acc + cb_ref[:, ls])
        mu = sum(jnp.sum(a, axis=-1, keepdims=True) for a in accs) * inv_c
        cen = [a - mu for a in accs]
        var = sum(jnp.sum(c * c, axis=-1, keepdims=True) for c in cen) * inv_c
        rstd = lax.rsqrt(var + NORM_EPS)
        for lc, c in enumerate(cen):
            ls = slice(lc * LANES, (lc + 1) * LANES)
            yn = c * rstd * lg_ref[:, ls] + lb_ref[:, ls]
            yn_ref[tc * CONV_CHUNK:(tc + 1) * CONV_CHUNK, ls] = (yn * _sigmoid(yn)).astype(yn_ref.dtype)


def _conv_call(u, cw, cb, lg, lb, seq):
    t = u.shape[0]
    tm = TM_CONV
    tpb = seq // tm
    hb = tm // HALO
    n_halo = t // HALO
    row = lambda i: (i, 0)
    const = lambda i: (0, 0)
    return pl.pallas_call(
        functools.partial(_conv_kernel, tiles_per_seq=tpb),
        grid=(t // tm,),
        in_specs=[
            pl.BlockSpec((tm, D_CONV), row),
            pl.BlockSpec((HALO, D_CONV), lambda i: (jnp.maximum(i * hb - 1, 0), 0)),
            pl.BlockSpec((HALO, D_CONV), lambda i: (jnp.minimum((i + 1) * hb, n_halo - 1), 0)),
            pl.BlockSpec(cw.shape, const),
            pl.BlockSpec((1, D_CONV), const),
            pl.BlockSpec((1, D_CONV), const),
            pl.BlockSpec((1, D_CONV), const),
        ],
        out_specs=pl.BlockSpec((tm, D_CONV), row),
        out_shape=jax.ShapeDtypeStruct((t, D_CONV), BF16),
        scratch_shapes=[
            pltpu.VMEM((tm + 2 * HALO, D_CONV), F32),
            pltpu.VMEM((SUBLANES, tm + 2 * HALO - SUBLANES, D_CONV), F32),
        ],
        compiler_params=pltpu.CompilerParams(dimension_semantics=("arbitrary",), vmem_limit_bytes=VMEM_LIMIT),
        name="conv",
    )(u, u, u, cw, cb, lg, lb)


def _proj_kernel(x_ref, mod_ref, g_ref, wqv_ref, w_ref, qt_ref, k_ref, vt_ref, u_ref, ga_ref, gb_ref):
    d = x_ref.shape[1]
    mod = mod_ref[0]
    h = _norm_mod(x_ref[...], g_ref[...], mod[0:1], mod[1:2]).astype(BF16)

    qv_t = lax.dot_general(wqv_ref[...], h, NT_DIMS, preferred_element_type=F32)
    _store_transposed(qv_t[:D_ATT], qt_ref, SCORE_SCALE)
    _store_transposed(qv_t[D_ATT:], vt_ref)

    def seg(lo, hi):
        return jnp.dot(h, w_ref[:, lo:hi], preferred_element_type=F32)

    k_ref[...] = seg(0, D_ATT).astype(BF16)
    a = seg(D_ATT, D_ATT + D_CONV)
    g = seg(D_ATT + D_CONV, D_ATT + 2 * D_CONV)
    u_ref[...] = (a * _sigmoid(g)).astype(BF16)
    g0 = D_ATT + 2 * D_CONV
    ga_ref[...] = _sigmoid(seg(g0, g0 + d)).astype(BF16)
    gb_ref[...] = _sigmoid(seg(g0 + d, g0 + 2 * d)).astype(BF16)


def _proj_call(x2, mod3, g, wqv_t, w_rest, seq):
    t, d = x2.shape
    tm = TM_PROJ
    tpb = seq // tm
    nblk = tm // LANES
    outs = [
        jax.ShapeDtypeStruct((t // LANES, D_ATT, LANES), BF16),
        jax.ShapeDtypeStruct((t, D_ATT), BF16),
        jax.ShapeDtypeStruct((t // LANES, D_ATT, LANES), BF16),
        jax.ShapeDtypeStruct((t, D_CONV), BF16),
        jax.ShapeDtypeStruct((t, d), BF16),
        jax.ShapeDtypeStruct((t, d), BF16),
    ]
    row = lambda i: (i, 0)
    blk3 = lambda i: (i, 0, 0)
    return pl.pallas_call(
        _proj_kernel,
        grid=(t // tm,),
        in_specs=[
            pl.BlockSpec((tm, d), row),
            pl.BlockSpec((1, 6, d), lambda i: (i // tpb, 0, 0)),
            pl.BlockSpec((1, d), lambda i: (0, 0)),
            pl.BlockSpec(wqv_t.shape, lambda i: (0, 0), pipeline_mode=pl.Buffered(1)),
            pl.BlockSpec(w_rest.shape, lambda i: (0, 0), pipeline_mode=pl.Buffered(1)),
        ],
        out_specs=[
            pl.BlockSpec((nblk, D_ATT, LANES), blk3),
            pl.BlockSpec((tm, D_ATT), row),
            pl.BlockSpec((nblk, D_ATT, LANES), blk3),
            pl.BlockSpec((tm, D_CONV), row),
            pl.BlockSpec((tm, d), row),
            pl.BlockSpec((tm, d), row),
        ],
        out_shape=outs,
        compiler_params=pltpu.CompilerParams(dimension_semantics=("arbitrary",), vmem_limit_bytes=VMEM_LIMIT),
        name="proj",
    )(x2, mod3, g, wqv_t, w_rest)


def _ctx_kv_kernel(x_ref, mod_ref, g_ref, wk_ref, wvt_ref, k_ref, vt_ref):
    mod = mod_ref[0]
    h = _norm_mod(x_ref[...], g_ref[...], mod[0:1], mod[1:2]).astype(BF16)
    k_ref[...] = jnp.dot(h, wk_ref[...], preferred_element_type=F32).astype(BF16)
    _store_transposed(lax.dot_general(wvt_ref[...], h, NT_DIMS, preferred_element_type=F32), vt_ref)


def _ctx_kv_call(c2, mod3, g, w_k, w_vt):
    t, d = c2.shape
    tm = 512
    nblk = tm // LANES
    row = lambda i: (i, 0)
    return pl.pallas_call(
        _ctx_kv_kernel,
        grid=(t // tm,),
        in_specs=[
            pl.BlockSpec((tm, d), row),
            pl.BlockSpec((1, 6, d), lambda i: (0, 0, 0)),
            pl.BlockSpec((1, d), lambda i: (0, 0)),
            pl.BlockSpec(w_k.shape, lambda i: (0, 0)),
            pl.BlockSpec(w_vt.shape, lambda i: (0, 0)),
        ],
        out_specs=[pl.BlockSpec((tm, D_ATT), row), pl.BlockSpec((nblk, D_ATT, LANES), lambda i: (i, 0, 0))],
        out_shape=[jax.ShapeDtypeStruct((t, D_ATT), BF16), jax.ShapeDtypeStruct((t // LANES, D_ATT, LANES), BF16)],
        compiler_params=pltpu.CompilerParams(dimension_semantics=("arbitrary",), vmem_limit_bytes=VMEM_LIMIT),
        name="ctx_kv",
    )(c2, mod3, g, w_k, w_vt)


HEADS_PER_GROUP = 4
GROUP_W = HEADS_PER_GROUP * HEAD_DIM
N_HEAD_GROUPS = N_HEADS // HEADS_PER_GROUP


def _attn_kernel(qt_ref, k_ref, vt_ref, kc_ref, vct_ref, bias_ref, o_ref, vboth):
    nblk = qt_ref.shape[0]
    rows = 2 * nblk
    n_loc = WIN_ROWS * GRID_W
    n_ctx_blk = vct_ref.shape[0]
    half = GRID_W

    for j in range(nblk):
        vboth[0, j] = vt_ref[j]
    for j in range(nblk - 1):
        vboth[1, j] = jnp.concatenate([vt_ref[j][:, half:], vt_ref[j + 1][:, :half]], axis=1)
    vboth[1, nblk - 1] = jnp.zeros_like(vt_ref[0])

    lane = lax.broadcasted_iota(jnp.int32, (D_ATT, LANES), 1)
    low = lane < half
    rblk = lax.broadcasted_iota(jnp.int32, (GROUP_W, GROUP_W), 0) // HEAD_DIM
    cblk = lax.broadcasted_iota(jnp.int32, (GROUP_W, GROUP_W), 1) // HEAD_DIM
    diag = rblk == cblk
    low64 = lax.broadcasted_iota(jnp.int32, (HEAD_DIM, LANES), 1) < half
    kc = kc_ref[...]

    def one_row(row, tiled, side):
        r_start = jnp.clip(row - WIN_ROWS // 2, 0, rows - WIN_ROWS)
        b_off = pl.multiple_of((WIN_ROWS - 1 - (row - r_start)) * GRID_W, GRID_W)
        kw = k_ref[pl.ds(pl.multiple_of(r_start * GRID_W, GRID_W), n_loc), :]
        vwin = vboth[r_start % 2, pl.ds(r_start // 2, n_loc // LANES)]
        parts = []
        for g in range(N_HEAD_GROUPS):
            fs = slice(g * GROUP_W, (g + 1) * GROUP_W)
            tg = tiled[fs, :]
            w = jnp.where(diag, jnp.concatenate([tg, tg], axis=1), jnp.zeros((), BF16))
            s = jnp.dot(jnp.concatenate([kw[:, fs], kc[:, fs]], axis=0), w, preferred_element_type=F32)
            s_loc = s[:n_loc] + bias_ref[g, pl.ds(b_off, n_loc), :]
            s_ctx = s[n_loc:]
            m = jnp.maximum(jnp.max(s_loc, axis=0, keepdims=True), jnp.max(s_ctx, axis=0, keepdims=True))
            p_loc = jnp.exp2(s_loc - m)
            p_ctx = jnp.exp2(s_ctx - m)
            l = jnp.sum(p_loc, axis=0, keepdims=True) + jnp.sum(p_ctx, axis=0, keepdims=True)
            p = jnp.concatenate([p_loc.astype(BF16), p_ctx.astype(BF16)], axis=0)
            vt_g = jnp.concatenate([vwin[j][fs, :] for j in range(n_loc // LANES)]
                                   + [vct_ref[j][fs, :] for j in range(n_ctx_blk)], axis=1)
            o_t = jnp.dot(vt_g, p, preferred_element_type=F32) * (1.0 / l)
            for j in range(HEADS_PER_GROUP):
                blk = o_t[j * HEAD_DIM:(j + 1) * HEAD_DIM, (j // 2) * LANES:(j // 2 + 1) * LANES]
                if j % 2 != side:
                    blk = pltpu.roll(blk, half, axis=1)
                parts.append(blk)
        return parts

    def body(i, carry):
        xq = qt_ref[i]
        rolled = jnp.concatenate([xq[:, half:], xq[:, :half]], axis=1)
        parts_a = one_row(2 * i, jnp.where(low, xq, rolled), 0)
        parts_b = one_row(2 * i + 1, jnp.where(low, rolled, xq), 1)
        out = jnp.concatenate([jnp.where(low64, a, b) for a, b in zip(parts_a, parts_b)], axis=0)
        o_ref[i] = out.astype(o_ref.dtype)
        return carry

    lax.fori_loop(0, nblk, body, 0, unroll=8)


def _attn_call(qt, k, vt, kc, vct, bias, seq, n_ctx):
    t = k.shape[0]
    b = t // seq
    nblk = seq // LANES
    lat3 = pl.BlockSpec((nblk, D_ATT, LANES), lambda i: (i, 0, 0))
    return pl.pallas_call(
        _attn_kernel,
        grid=(b,),
        in_specs=[lat3,
                  pl.BlockSpec((seq, D_ATT), lambda i: (i, 0)),
                  lat3,
                  pl.BlockSpec((n_ctx, D_ATT), lambda i: (i, 0)),
                  pl.BlockSpec((n_ctx // LANES, D_ATT, LANES), lambda i: (i, 0, 0)),
                  pl.BlockSpec(bias.shape, lambda i: (0, 0, 0))],
        out_specs=lat3,
        out_shape=jax.ShapeDtypeStruct((t // LANES, D_ATT, LANES), BF16),
        scratch_shapes=[pltpu.VMEM((2, nblk, D_ATT, LANES), BF16)],
        compiler_params=pltpu.CompilerParams(dimension_semantics=("arbitrary",), vmem_limit_bytes=VMEM_LIMIT),
        name="attn",
    )(qt, k, vt, kc, vct, bias)


def _bias_table(rpb):
    cq = np.arange(GRID_W)[:, None]
    ck = np.arange(GRID_W)[None, :]
    c_start = np.clip(cq - WIN_COLS // 2, 0, GRID_W - WIN_COLS)
    col_mask = (ck >= c_start) & (ck < c_start + WIN_COLS)
    n_dc = rpb.shape[2]
    lead = GRID_W - WIN_COLS
    w = jnp.pad(rpb, ((0, 0), (0, 0), (lead, 2 * GRID_W - lead - n_dc)))
    skew = jnp.tile(w, (1, 1, GRID_W + 1))[:, :, :GRID_W * (2 * GRID_W + 1)]
    tab = skew.reshape(rpb.shape[0], rpb.shape[1], GRID_W, 2 * GRID_W + 1)[:, :, ::-1, :GRID_W]
    tab = jnp.where(col_mask[None, None], tab, NEG_INF)
    n_dr = tab.shape[1]
    tab = tab.reshape(N_HEAD_GROUPS, HEADS_PER_GROUP, n_dr, GRID_W, GRID_W)
    tab = tab.transpose(0, 2, 4, 1, 3)
    return (tab.reshape(N_HEAD_GROUPS, n_dr * GRID_W, GROUP_W) * LOG2_E).astype(F32)


def _merge_rows(r0, n, att_ref, yn_ref, ga_ref, gb_ref, x_ref, mod,
                wa_ref, wc_ref, wo_ref, nf_ref, wr_ref, br_ref, xmid_ref, hp_ref):
    d = x_ref.shape[1]
    rs = slice(r0, r0 + n)
    y_conv = jnp.dot(yn_ref[rs, :], wc_ref[...], preferred_element_type=F32)

    att = jnp.concatenate([att_ref[j].astype(F32).T for j in range(r0 // LANES, (r0 + n) // LANES)], axis=0)
    y_att = jnp.dot(att.astype(BF16), wa_ref[...], preferred_element_type=F32)
    mix = ga_ref[rs, :].astype(F32) * y_att + gb_ref[rs, :].astype(F32) * y_conv
    y = jnp.dot(mix.astype(BF16), wo_ref[...], preferred_element_type=F32)
    x_mid = x_ref[rs, :] + mod[2:3] * y
    xmid_ref[rs, :] = x_mid
    h2 = _norm_mod(x_mid, nf_ref[...], mod[3:4], mod[4:5])

    h2_hi = h2.astype(BF16)
    h2_lo = (h2 - h2_hi.astype(F32)).astype(BF16)
    t_hi = jnp.dot(h2_hi, wr_ref[...], preferred_element_type=F32)
    t_lo = jnp.dot(h2_lo, wr_ref[:, :LANES], preferred_element_type=F32)
    logits = t_hi[:, :LANES] + t_hi[:, LANES:] + t_lo + br_ref[...]
    lt = logits.T
    sub = lax.broadcasted_iota(jnp.int32, (SUBLANES, n), 0).astype(F32)
    big = float(LANES)
    is_g = sub < N_GROUPS
    gl = jnp.where(is_g, lt[0:SUBLANES], -jnp.inf)
    gmax = jnp.max(gl, axis=0, keepdims=True)
    g_idx = jnp.min(jnp.where(gl == gmax, sub, big), axis=0, keepdims=True)
    p_group = 1.0 / jnp.sum(jnp.where(is_g, jnp.exp(gl - gmax), 0.0), axis=0, keepdims=True)
    el = lt[ROUTER_E0:ROUTER_E0 + EXPERTS_PER_GROUP]
    for g in range(1, N_GROUPS):
        lo_g = ROUTER_E0 + g * EXPERTS_PER_GROUP
        el = jnp.where(g_idx == float(g), lt[lo_g:lo_g + EXPERTS_PER_GROUP], el)
    v1 = jnp.max(el, axis=0, keepdims=True)
    j1 = jnp.min(jnp.where(el == v1, sub, big), axis=0, keepdims=True)
    el2 = jnp.where(sub == j1, -jnp.inf, el)
    v2 = jnp.max(el2, axis=0, keepdims=True)
    j2 = jnp.min(jnp.where(el2 == v2, sub, big), axis=0, keepdims=True)
    e2 = jnp.exp(v2 - v1)
    gate1 = p_group / (1.0 + e2)
    gate2 = p_group * e2 / (1.0 + e2)
    ja = jnp.minimum(j1, j2)
    jb = jnp.maximum(j1, j2)
    gate_a = jnp.where(j1 < j2, gate1, gate2)
    gate_b = jnp.where(j1 < j2, gate2, gate1)
    pair = ja * (2 * EXPERTS_PER_GROUP - 1 - ja) * 0.5 + (jb - ja - 1.0)
    cls = g_idx * PAIRS_PER_GROUP + pair
    gate_rows = jnp.where(sub == 0.0, gate_a, jnp.where(sub == 1.0, gate_b, 0.0))
    gates = jnp.concatenate([gate_rows, jnp.zeros((LANES - SUBLANES, n), F32)], axis=0).T

    half = d // 2
    hi = lax.bitcast_convert_type(h2[:, :half].astype(BF16).astype(F32), U32)
    lo = lax.bitcast_convert_type(h2[:, half:].astype(BF16).astype(F32), U32)
    words = hi | (lo >> 16)
    n_words = half // LANES
    base = r0 * ROW_TILE
    for c in range(n_words):
        hp_ref[pl.ds(base + c, n, stride=ROW_TILE), :] = words[:, c * LANES:(c + 1) * LANES]
    hp_ref[pl.ds(base + n_words, n, stride=ROW_TILE), :] = lax.bitcast_convert_type(gates, U32)
    zero = jnp.zeros((n, LANES), U32)
    for c in range(n_words + 1, ROW_TILE):
        hp_ref[pl.ds(base + c, n, stride=ROW_TILE), :] = zero
    return cls


def _merge_kernel(att_ref, yn_ref, ga_ref, gb_ref, x_ref, mod_ref,
                  wa_ref, wc_ref, wo_ref, nf_ref, wr_ref, br_ref, tri_ref,
                  xmid_ref, hp_ref, info_ref, cnt_ref, hs_hbm, run_ref, zbuf, zsem, *, n_steps):
    i = pl.program_id(0)
    tm = x_ref.shape[0]

    @pl.when(i == 0)
    def _():
        run_ref[...] = jnp.zeros_like(run_ref)
        zbuf[...] = jnp.zeros_like(zbuf)

    mod = mod_ref[0]
    n = tm // MERGE_PARTS
    cls = jnp.concatenate(
        [_merge_rows(p * n, n, att_ref, yn_ref, ga_ref, gb_ref, x_ref, mod, wa_ref, wc_ref, wo_ref, nf_ref,
                     wr_ref, br_ref, xmid_ref, hp_ref) for p in range(MERGE_PARTS)], axis=1)

    crow = lax.broadcasted_iota(jnp.int32, (LANES, tm), 0).astype(F32)
    oh_f = jnp.where(crow == cls, 1.0, 0.0)
    before = jnp.dot(oh_f.astype(BF16), tri_ref[...], preferred_element_type=F32)
    run = run_ref[:, 0:1]
    rank = jnp.sum(oh_f * (before + run), axis=0, keepdims=True)
    new_run = run + jnp.sum(oh_f, axis=1, keepdims=True)
    run_ref[...] = jnp.broadcast_to(new_run, run_ref.shape)
    cnt_ref[...] = jnp.broadcast_to(new_run, cnt_ref.shape)

    sub = lax.broadcasted_iota(jnp.int32, (SUBLANES, tm), 0)
    info_ref[...] = jnp.where(sub == 0, cls, jnp.where(sub == 1, rank, 0.0))

    blk_rows = zbuf.shape[0]
    n_blocks = hs_hbm.shape[0] // blk_rows
    per_step = -(-n_blocks // n_steps)

    def clear_copies(step, fn):
        for k in range(per_step):
            blk = step * per_step + k

            @pl.when(blk < n_blocks)
            def _():
                start = pl.multiple_of(blk * blk_rows, blk_rows)
                fn(pltpu.make_async_copy(zbuf, hs_hbm.at[pl.ds(start, blk_rows)], zsem))

    @pl.when(i > 0)
    def _():
        clear_copies(i - 1, lambda cp: cp.wait())

    clear_copies(i, lambda cp: cp.start())

    @pl.when(i == n_steps - 1)
    def _():
        clear_copies(i, lambda cp: cp.wait())


def _merge_call(att, yn, ga, gb, x2, mod3, wa, wc, wo, nf, wr, br, seq):
    t, d = x2.shape
    tm = TM_MERGE
    tpb = seq // tm
    tri = jnp.asarray(np.triu(np.ones((tm, tm), np.float32), 1), BF16)
    row = lambda i: (i, 0)
    const = lambda i: (0, 0)
    n_steps = t // tm
    block_rows = MOE_BLOCK * ROW_TILE
    return pl.pallas_call(
        functools.partial(_merge_kernel, n_steps=n_steps),
        grid=(n_steps,),
        in_specs=[
            pl.BlockSpec((tm // LANES, D_ATT, LANES), lambda i: (i, 0, 0)),
            pl.BlockSpec((tm, D_CONV), row),
            pl.BlockSpec((tm, d), row),
            pl.BlockSpec((tm, d), row),
            pl.BlockSpec((tm, d), row),
            pl.BlockSpec((1, 6, d), lambda i: (i // tpb, 0, 0)),
            pl.BlockSpec(wa.shape, const),
            pl.BlockSpec(wc.shape, const),
            pl.BlockSpec(wo.shape, const),
            pl.BlockSpec((1, d), const),
            pl.BlockSpec(wr.shape, const),
            pl.BlockSpec((1, LANES), const),
            pl.BlockSpec((tm, tm), const),
        ],
        out_specs=[
            pl.BlockSpec((tm, d), row),
            pl.BlockSpec((tm * ROW_TILE, LANES), row),
            pl.BlockSpec((SUBLANES, tm), lambda i: (0, i)),
            pl.BlockSpec((LANES, LANES), const),
            pl.BlockSpec(memory_space=pl.ANY),
        ],
        out_shape=[
            jax.ShapeDtypeStruct((t, d), F32),
            jax.ShapeDtypeStruct((t * ROW_TILE, LANES), U32),
            jax.ShapeDtypeStruct((SUBLANES, t), F32),
            jax.ShapeDtypeStruct((LANES, LANES), F32),
            jax.ShapeDtypeStruct((_num_moe_blocks(t) * block_rows, LANES), U32),
        ],
        scratch_shapes=[pltpu.VMEM((LANES, LANES), F32), pltpu.VMEM((block_rows, LANES), U32),
                        pltpu.SemaphoreType.DMA(())],
        compiler_params=pltpu.CompilerParams(dimension_semantics=("arbitrary",), vmem_limit_bytes=VMEM_LIMIT),
        name="merge",
    )(att, yn, ga, gb, x2, mod3, wa, wc, wo, nf, wr, br, tri)


def _dispatch_kernel(dest_ref, hp_ref, cleared_hbm, hs_hbm, sem):
    del cleared_hbm
    n = dest_ref.shape[0]

    def body(o, carry):
        for k in range(DMA_UNROLL):
            t = o * DMA_UNROLL + k
            src = pl.multiple_of(t * ROW_TILE, ROW_TILE)
            dst = pl.multiple_of(dest_ref[t], ROW_TILE)
            pltpu.make_async_copy(hp_ref.at[pl.ds(src, ROW_TILE)], hs_hbm.at[pl.ds(dst, ROW_TILE)],
                                  sem).start(priority=k % 2)
        return carry

    lax.fori_loop(0, n // DMA_UNROLL, body, 0)
    pltpu.make_async_copy(hp_ref, hs_hbm.at[pl.ds(0, n * ROW_TILE)], sem).wait()


def _dispatch_call(dest, hp, cleared):
    t = dest.shape[0]
    return pl.pallas_call(
        _dispatch_kernel,
        grid=(t // TM_DISPATCH,),
        in_specs=[
            pl.BlockSpec((TM_DISPATCH,), lambda i: (i,), memory_space=pltpu.SMEM),
            pl.BlockSpec((TM_DISPATCH * ROW_TILE, LANES), lambda i: (i, 0)),
            pl.BlockSpec(memory_space=pl.ANY),
        ],
        out_specs=pl.BlockSpec(memory_space=pl.ANY),
        out_shape=jax.ShapeDtypeStruct(cleared.shape, cleared.dtype),
        scratch_shapes=[pltpu.SemaphoreType.DMA(())],
        input_output_aliases={2: 0},
        compiler_params=pltpu.CompilerParams(dimension_semantics=("arbitrary",), has_side_effects=True,
                                             vmem_limit_bytes=VMEM_LIMIT),
        name="dispatch",
    )(dest, hp, cleared)


def _moe_kernel(rows_ref, seg_ref, first_ref, fetch_ref, pre_ref, hs_ref, wg_hbm, wu_hbm, wd_hbm, y_ref,
                wg_buf, wu_buf, wd_buf, sems):
    base = pl.program_id(0) * MOE_BLOCKS_PER_STEP
    rows = hs_ref.shape[0] // MOE_BLOCKS_PER_STEP
    blk = rows // ROW_TILE
    d = wg_buf.shape[2]
    n_words = d // 2 // LANES

    def copies(side, expert, slot):
        return [pltpu.make_async_copy(src.at[expert], buf.at[side, slot], sems.at[side, slot])
                for src, buf in ((wg_hbm, wg_buf), (wu_hbm, wu_buf), (wd_hbm, wd_buf))]

    for sub in range(MOE_BLOCKS_PER_STEP):
        b = base + sub
        for side in range(2):
            @pl.when(b == 0)
            def _():
                for s in range(2):
                    @pl.when(pre_ref[side, s] >= 0)
                    def _():
                        for cp in copies(side, pre_ref[side, s], s):
                            cp.start()

            @pl.when(fetch_ref[side, b] >= 0)
            def _():
                for cp in copies(side, fetch_ref[side, b], (seg_ref[side, b] + 2) % WEIGHT_SLOTS):
                    cp.start()

        for side in range(2):
            @pl.when(first_ref[side, b] == 1)
            def _():
                for cp in copies(side, 0, seg_ref[side, b] % WEIGHT_SLOTS):
                    cp.wait()

    def run(sub):
        b = base + sub
        off = sub * rows
        slot_a = seg_ref[0, b] % WEIGHT_SLOTS
        slot_b = seg_ref[1, b] % WEIGHT_SLOTS
        his, los = [], []
        for c in range(n_words):
            w = hs_ref[pl.ds(off + c, blk, stride=ROW_TILE), :]
            his.append(lax.bitcast_convert_type(w & jnp.uint32(0xFFFF0000), F32).astype(BF16))
            los.append(lax.bitcast_convert_type(w << 16, F32).astype(BF16))
        x = jnp.concatenate(his + los, axis=-1)
        gates = lax.bitcast_convert_type(hs_ref[pl.ds(off + n_words, blk, stride=ROW_TILE), :], F32)

        def mlp(side, slot):
            g = jnp.dot(x, wg_buf[side, slot], preferred_element_type=F32)
            u = jnp.dot(x, wu_buf[side, slot], preferred_element_type=F32)
            a = (g * _sigmoid(g) * u).astype(BF16)
            return jnp.dot(a, wd_buf[side, slot], preferred_element_type=F32)

        y = gates[:, 0:1] * mlp(0, slot_a) + gates[:, 1:2] * mlp(1, slot_b)
        for c in range(d // LANES):
            y_ref[pl.ds(off + c, blk, stride=ROW_TILE), :] = y[:, c * LANES:(c + 1) * LANES]

    used = [rows_ref[base + sub] > 0 for sub in range(MOE_BLOCKS_PER_STEP)]
    all_used = functools.reduce(jnp.logical_and, used)

    @pl.when(all_used)
    def _():
        for sub in range(MOE_BLOCKS_PER_STEP):
            run(sub)

    for sub in range(MOE_BLOCKS_PER_STEP):
        @pl.when(jnp.logical_not(all_used) & used[sub])
        def _():
            run(sub)

        @pl.when(jnp.logical_not(used[sub]))
        def _():
            y_ref[sub * rows:(sub + 1) * rows, :] = jnp.zeros((rows, LANES), y_ref.dtype)


def _weight_schedule(blk_e, blk_rows):
    nb = blk_e.shape[0]
    valid = blk_rows > 0
    prev = jnp.concatenate([jnp.full((1,), -1, jnp.int32), blk_e[:-1]])
    first = valid & (blk_e != prev)
    seg = jnp.cumsum(first.astype(jnp.int32)) - 1
    n_seg = jnp.sum(first.astype(jnp.int32))
    seg_e = jnp.full((nb + 3,), -1, jnp.int32).at[jnp.where(first, seg, nb + 2)].set(blk_e)
    seg_e = seg_e.at[nb + 2].set(-1)
    fetch = jnp.where(first & (seg + 2 < n_seg), seg_e[jnp.clip(seg + 2, 0, nb + 1)], -1)
    return jnp.maximum(seg, 0), first.astype(jnp.int32), fetch.astype(jnp.int32), seg_e[:2]


def _moe_call(blk_ea, blk_eb, blk_rows, hs, wg, wu, wd):
    nb = blk_ea.shape[0]
    d = wg.shape[1]
    de = wg.shape[2]
    rows = MOE_BLOCKS_PER_STEP * MOE_BLOCK * ROW_TILE
    sched = [_weight_schedule(e, blk_rows) for e in (blk_ea, blk_eb)]
    seg, first, fetch, pre = (jnp.stack([s[k] for s in sched]) for k in range(4))
    tok = pl.BlockSpec((rows, LANES), lambda i, *_: (i, 0))
    hbm = pl.BlockSpec(memory_space=pl.ANY)
    grid_spec = pltpu.PrefetchScalarGridSpec(
        num_scalar_prefetch=5,
        grid=(nb // MOE_BLOCKS_PER_STEP,),
        in_specs=[tok, hbm, hbm, hbm],
        out_specs=tok,
        scratch_shapes=[
            pltpu.VMEM((2, WEIGHT_SLOTS, d, de), BF16),
            pltpu.VMEM((2, WEIGHT_SLOTS, d, de), BF16),
            pltpu.VMEM((2, WEIGHT_SLOTS, de, d), BF16),
            pltpu.SemaphoreType.DMA((2, WEIGHT_SLOTS)),
        ],
    )
    return pl.pallas_call(
        _moe_kernel,
        grid_spec=grid_spec,
        out_shape=jax.ShapeDtypeStruct((nb * MOE_BLOCK * ROW_TILE, LANES), F32),
        compiler_params=pltpu.CompilerParams(dimension_semantics=("arbitrary",), vmem_limit_bytes=VMEM_LIMIT),
        name="moe",
    )(blk_rows, seg, first, fetch, pre, hs, wg, wu, wd)


def _final_kernel(dest_ref, dest_next_ref, xmid_ref, mod_ref, fn_ref, y_hbm, o_ref, fbuf, sems):
    i = pl.program_id(0)
    n_steps = pl.num_programs(0)
    n = dest_ref.shape[0]
    d = xmid_ref.shape[1]

    def issue(idx_ref, slot):
        def body(o, carry):
            for k in range(DMA_UNROLL):
                t = o * DMA_UNROLL + k
                src = pl.multiple_of(idx_ref[t], ROW_TILE)
                dst = pl.multiple_of(t * ROW_TILE, ROW_TILE)
                pltpu.make_async_copy(y_hbm.at[pl.ds(src, ROW_TILE)], fbuf.at[slot, pl.ds(dst, ROW_TILE)],
                                      sems.at[slot]).start(priority=k % 2)
            return carry

        lax.fori_loop(0, n // DMA_UNROLL, body, 0)

    slot = i % 2

    @pl.when(i == 0)
    def _():
        issue(dest_ref, 0)

    for s in range(2):
        @pl.when((i + 1 < n_steps) & (slot == s))
        def _():
            issue(dest_next_ref, 1 - s)

    pltpu.make_async_copy(y_hbm.at[pl.ds(0, n * ROW_TILE)], fbuf.at[slot], sems.at[slot]).wait()
    f = jnp.concatenate([fbuf[slot, pl.ds(c, n, stride=ROW_TILE), :] for c in range(d // LANES)], axis=-1)
    x = xmid_ref[...] + mod_ref[0][5:6] * f
    ms = jnp.mean(x * x, axis=-1, keepdims=True)
    o_ref[...] = x * lax.rsqrt(ms + NORM_EPS) * fn_ref[...]


def _final_call(dest, xmid, mod3, fn, y, seq):
    t, d = xmid.shape
    tm = TM_ROWS
    tpb = seq // tm
    n_steps = t // tm
    return pl.pallas_call(
        _final_kernel,
        grid=(n_steps,),
        in_specs=[
            pl.BlockSpec((tm,), lambda i: (i,), memory_space=pltpu.SMEM),
            pl.BlockSpec((tm,), lambda i: (jnp.minimum(i + 1, n_steps - 1),), memory_space=pltpu.SMEM),
            pl.BlockSpec((tm, d), lambda i: (i, 0)),
            pl.BlockSpec((1, 6, d), lambda i: (i // tpb, 0, 0)),
            pl.BlockSpec((1, d), lambda i: (0, 0)),
            pl.BlockSpec(memory_space=pl.ANY),
        ],
        out_specs=pl.BlockSpec((tm, d), lambda i: (i, 0)),
        out_shape=jax.ShapeDtypeStruct((t, d), F32),
        scratch_shapes=[pltpu.VMEM((2, tm * ROW_TILE, LANES), F32), pltpu.SemaphoreType.DMA((2,))],
        compiler_params=pltpu.CompilerParams(dimension_semantics=("arbitrary",), vmem_limit_bytes=VMEM_LIMIT),
        name="final",
    )(dest, dest, xmid, mod3, fn, y)


def _pair_tables():
    ea, eb = [], []
    for g in range(N_GROUPS):
        for a in range(EXPERTS_PER_GROUP):
            for b in range(a + 1, EXPERTS_PER_GROUP):
                ea.append(g * EXPERTS_PER_GROUP + a)
                eb.append(g * EXPERTS_PER_GROUP + b)
    return np.asarray(ea, np.int32), np.asarray(eb, np.int32)


def _num_moe_blocks(t):
    return -(-(t // MOE_BLOCK + N_CLASSES) // MOE_BLOCKS_PER_STEP) * MOE_BLOCKS_PER_STEP


def _routing_plan(info, counts, t):
    cls = info[0].astype(jnp.int32)
    rank = info[1].astype(jnp.int32)
    cnt = counts[:N_CLASSES, 0].astype(jnp.int32)
    padded = (cnt + MOE_BLOCK - 1) // MOE_BLOCK * MOE_BLOCK
    pad_end = jnp.cumsum(padded)
    pad_start = pad_end - padded
    cls_ids = jnp.arange(N_CLASSES, dtype=jnp.int32)
    dest = rank + jnp.sum(jnp.where(cls[:, None] == cls_ids[None, :], pad_start[None, :], 0), axis=1)
    nb = _num_moe_blocks(t)
    nb_used = pad_end[-1] // MOE_BLOCK
    blk = jnp.arange(nb, dtype=jnp.int32)
    blk_cls = jnp.sum(pad_end[None, :] <= (blk * MOE_BLOCK)[:, None], axis=1)
    blk_cls = jnp.clip(blk_cls, 0, N_CLASSES - 1)
    valid = blk < nb_used
    last_cls = blk_cls[jnp.maximum(nb_used - 1, 0)]
    blk_cls = jnp.where(valid, blk_cls, last_cls).astype(jnp.int32)
    tab_a, tab_b = _pair_tables()
    blk_ea = jnp.asarray(tab_a)[blk_cls]
    blk_eb = jnp.asarray(tab_b)[blk_cls]
    blk_rows = jnp.clip(cnt[blk_cls] - (blk * MOE_BLOCK - pad_start[blk_cls]), 0, MOE_BLOCK)
    blk_rows = jnp.where(valid, blk_rows, 0).astype(jnp.int32)
    return dest, blk_ea, blk_eb, blk_rows


def kernel(x, c, ctx, c_ctx, w_mod, b_mod, norm_mix, w_in, rpb, w_att_out, conv_w, conv_b, conv_ln_g, conv_ln_b,
           w_conv_out, w_o, norm_ffn, w_router_group, b_router_group, w_router_expert, b_router_expert,
           w_exp_gate, w_exp_up, w_exp_down, final_norm):
    b, seq, d = x.shape
    n_ctx = ctx.shape[1]
    t = b * seq
    assert w_mod.shape[0] == 1, "single layer"
    assert seq % TM_ROWS == 0 and t % TM_DISPATCH == 0 and seq % TM_CONV == 0 and seq // GRID_W >= WIN_ROWS and n_ctx % LANES == 0 and (b * n_ctx) % 512 == 0
    assert d == 1024

    mod_rows = -(-(b + 1) // SUBLANES) * SUBLANES
    cc = jnp.zeros((mod_rows, d), F32).at[:b].set(c).at[b].set(c_ctx)
    m_all = _mod_call(cc, w_mod[0], b_mod[0][None, :])
    mod_lat = m_all[:b].reshape(b, 6, d)
    mod_ctx = m_all[b:b + 1].reshape(1, 6, d)

    x2 = x.reshape(t, d)
    g_mix = norm_mix[0][None, :]
    w_in_b = w_in[0].astype(BF16)
    wqv_t = jnp.concatenate([w_in_b[:, Q0:K0], w_in_b[:, V0:GLU0]], axis=1).T
    w_rest = jnp.concatenate([w_in_b[:, K0:V0], w_in_b[:, GLU0:]], axis=1)
    cw = jnp.zeros((32, D_CONV), F32).at[:CONV_WIDTH].set(conv_w[0])
    qt, k, vt, u, ga, gb = _proj_call(x2, mod_lat, g_mix, wqv_t, w_rest, seq)
    yn = _conv_call(u, cw, conv_b[0][None, :], conv_ln_g[0][None, :], conv_ln_b[0][None, :], seq)
    kc, vct = _ctx_kv_call(ctx.reshape(b * n_ctx, d), mod_ctx, g_mix, w_in_b[:, K0:V0], w_in_b[:, V0:GLU0].T)

    att = _attn_call(qt, k, vt, kc, vct, _bias_table(rpb[0]), seq, n_ctx)

    wr = jnp.zeros((d, LANES), F32).at[:, :N_GROUPS].set(w_router_group[0])
    wr = wr.at[:, ROUTER_E0:ROUTER_E0 + N_EXPERTS].set(w_router_expert[0])
    wr_hi = wr.astype(BF16)
    wr = jnp.concatenate([wr_hi, (wr - wr_hi.astype(F32)).astype(BF16)], axis=1)
    br = jnp.zeros((1, LANES), F32).at[0, :N_GROUPS].set(b_router_group[0])
    br = br.at[0, ROUTER_E0:ROUTER_E0 + N_EXPERTS].set(b_router_expert[0])
    x_mid, hp, info, counts, cleared = _merge_call(
        att, yn, ga, gb, x2, mod_lat, w_att_out[0].astype(BF16), w_conv_out[0].astype(BF16), w_o[0].astype(BF16), norm_ffn[0][None, :],
        wr, br, seq)

    dest, blk_ea, blk_eb, blk_rows = _routing_plan(info, counts, t)
    dest_row = dest * ROW_TILE
    hs = _dispatch_call(dest_row, hp, cleared)
    y = _moe_call(blk_ea, blk_eb, blk_rows, hs,
                  w_exp_gate[0].astype(BF16), w_exp_up[0].astype(BF16), w_exp_down[0].astype(BF16))
    out = _final_call(dest_row, x_mid, mod_lat, final_norm[None, :], y, seq)
    return out.reshape(b, seq, d)
```
